```python
import math
import jax
import jax.numpy as jnp
from jax import lax
import numpy as np

D_MODEL = 2048
BATCH = 4
SEQ = 2048
DEPTH = 4

GRID_W = 64
CTX_LEN = 256
N_MIXERS = 2
EPS = 1e-6

GLA_HEADS = 4
GLA_DK = D_MODEL // 2
GLA_DV = D_MODEL
GLA_HK = GLA_DK // GLA_HEADS
GLA_HV = GLA_DV // GLA_HEADS
GLA_GATE_RANK = 16
GLA_GATE_TAU = 16.0
GLA_CHUNK = 64
GLA_SPLITS = (GLA_DK, 2 * GLA_DK, 2 * GLA_DK + GLA_DV, 2 * GLA_DK + 2 * GLA_DV)
GLA_IN = 2 * GLA_DK + 2 * GLA_DV + 2 * GLA_GATE_RANK

S5_GROUP = 16
S5_GROUPS = D_MODEL // S5_GROUP
S5_STATE = 64
S5_DT_MIN = 1e-3
S5_DT_MAX = 1e-1

D_FF = 5632
N_EXPERTS = 8
TOP_K = 2

kernel_name = "hybrid_gla_s5_moe_prefix_dit"


def rmsnorm(x, g):
    xf = x.astype(jnp.float32)
    y = xf * lax.rsqrt(jnp.mean(xf * xf, axis=-1, keepdims=True) + EPS)
    return (y * g.astype(jnp.float32)).astype(x.dtype)


def gla_chunked(q, k, v, log_a, s0):
    bsz, nh, length, _ = q.shape
    n_chunks = length // GLA_CHUNK

    def to_chunks(t):
        return jnp.moveaxis(t.reshape(bsz, nh, n_chunks, GLA_CHUNK, t.shape[-1]), 2, 0)

    lower_tri = jnp.tril(jnp.ones((GLA_CHUNK, GLA_CHUNK), dtype=bool))

    def step(s, inp):
        qi, ki, vi, gi = inp
        b = jnp.cumsum(gi, axis=-2)
        b_last = b[:, :, -1:, :]
        q_dec = qi * jnp.exp(b)
        k_dec = ki * jnp.exp(-b)
        scores = jnp.where(lower_tri, jnp.einsum('bhik,bhjk->bhij', q_dec, k_dec), 0.0)
        o = jnp.einsum('bhij,bhjv->bhiv', scores, vi) + jnp.einsum('bhik,bhkv->bhiv', q_dec, s)
        k_tail = ki * jnp.exp(b_last - b)
        s_new = jnp.swapaxes(jnp.exp(b_last), -1, -2) * s + jnp.einsum('bhjk,bhjv->bhkv', k_tail, vi)
        return s_new, o

    s_fin, o = lax.scan(step, s0, (to_chunks(q), to_chunks(k), to_chunks(v), to_chunks(log_a)))
    return jnp.moveaxis(o, 0, 2).reshape(bsz, nh, length, v.shape[-1]), s_fin


def gla_mixer(uc, ux, w_in, w_gate_up, b_gate, g_head, w_out, with_ctx_out):
    f32 = jnp.float32
    bsz, lc, _ = uc.shape
    u = jnp.concatenate([uc, ux], axis=1)
    length = u.shape[1]
    proj = u @ w_in
    q, k, v, r, gl = jnp.split(proj, list(GLA_SPLITS), axis=-1)
    gl = gl.reshape(bsz, length, 2, GLA_GATE_RANK)
    log_a = jax.nn.log_sigmoid((jnp.einsum('blzr,zrk->blzk', gl, w_gate_up) + b_gate).astype(f32)) / GLA_GATE_TAU

    def heads(t, hd):
        return t.reshape(bsz, length, GLA_HEADS, hd).transpose(0, 2, 1, 3).astype(f32)

    qh = heads(q, GLA_HK) * (GLA_HK ** -0.5)
    kh = heads(k, GLA_HK)
    vh = heads(v, GLA_HV)
    outs_c, outs_x = [], []
    for d in range(2):
        gh = heads(log_a[:, :, d], GLA_HK)
        ctx_p = [t[:, :, :lc] for t in (qh, kh, vh, gh)]
        lat_p = [t[:, :, lc:] for t in (qh, kh, vh, gh)]
        if d == 1:
            ctx_p = [jnp.flip(t, axis=2) for t in ctx_p]
            lat_p = [jnp.flip(t, axis=2) for t in lat_p]
        s0 = jnp.zeros((bsz, GLA_HEADS, GLA_HK, GLA_HV), f32)
        o_c, s_c = gla_chunked(*ctx_p, s0)
        o_x, _ = gla_chunked(*lat_p, s_c)
        if d == 1:
            o_c = jnp.flip(o_c, axis=2)
            o_x = jnp.flip(o_x, axis=2)
        outs_c.append(o_c)
        outs_x.append(o_x)

    def finish(o, r_part):
        o = rmsnorm(o, g_head).transpose(0, 2, 1, 3).reshape(bsz, -1, GLA_DV).astype(uc.dtype)
        return (o * jax.nn.silu(r_part)) @ w_out

    y_x = finish(outs_x[0] + outs_x[1], r[:, lc:])
    y_c = finish(outs_c[0] + outs_c[1], r[:, :lc]) if with_ctx_out else None
    return y_c, y_x


def s5_discretize(a_re, a_im, log_dt, b_re, b_im):
    f32 = jnp.float32
    a_re, a_im, b_re, b_im = (t.astype(f32) for t in (a_re, a_im, b_re, b_im))
    dt = jnp.exp(log_dt.astype(f32))[:, None]
    mag = jnp.exp(a_re * dt)
    ab_re = mag * jnp.cos(a_im * dt)
    ab_im = mag * jnp.sin(a_im * dt)
    den = a_re * a_re + a_im * a_im
    f_re = ((ab_re - 1.0) * a_re + ab_im * a_im) / den
    f_im = (ab_im * a_re - (ab_re - 1.0) * a_im) / den
    bb_re = f_re[..., None] * b_re - f_im[..., None] * b_im
    bb_im = f_re[..., None] * b_im + f_im[..., None] * b_re
    return ab_re, ab_im, bb_re, bb_im


def s5_scan(ab_re, ab_im, bu_re, bu_im, h0_re, h0_im):
    if h0_re is not None:
        bu_re = bu_re.at[0].add(ab_re * h0_re - ab_im * h0_im)
        bu_im = bu_im.at[0].add(ab_re * h0_im + ab_im * h0_re)
    length = bu_re.shape[0]
    a_re = jnp.broadcast_to(ab_re, (length, 1) + ab_re.shape)
    a_im = jnp.broadcast_to(ab_im, (length, 1) + ab_im.shape)

    def combine(e1, e2):
        a1r, a1i, b1r, b1i = e1
        a2r, a2i, b2r, b2i = e2
        return (a2r * a1r - a2i * a1i,
                a2r * a1i + a2i * a1r,
                a2r * b1r - a2i * b1i + b2r,
                a2r * b1i + a2i * b1r + b2i)

    _, _, x_re, x_im = lax.associative_scan(combine, (a_re, a_im, bu_re, bu_im), axis=0)
    return x_re, x_im


def s5_mixer(uc, ux, a_re, a_im, log_dt, b_re, b_im, c_re, c_im, d_skip, w_glu, b_glu, with_ctx_out):
    f32 = jnp.float32
    bsz, lc, dm = uc.shape

    def groups(u):
        return jnp.moveaxis(u.astype(f32).reshape(bsz, u.shape[1], S5_GROUPS, S5_GROUP), 1, 0)

    gc, gx = groups(uc), groups(ux)
    y_x = ux.astype(f32) * d_skip.astype(f32)
    y_c = uc.astype(f32) * d_skip.astype(f32) if with_ctx_out else None
    for d in range(2):
        ab_re, ab_im, bb_re, bb_im = s5_discretize(a_re[d], a_im[d], log_dt[d], b_re[d], b_im[d])
        cr, ci = c_re[d].astype(f32), c_im[d].astype(f32)
        uc_d, ux_d = (gc, gx) if d == 0 else (gc[::-1], gx[::-1])

        def drive(u):
            return (jnp.einsum('lbgc,gpc->lbgp', u, bb_re), jnp.einsum('lbgc,gpc->lbgp', u, bb_im))

        def readout(xr, xi):
            y = jnp.einsum('lbgp,gcp->lbgc', xr, cr) - jnp.einsum('lbgp,gcp->lbgc', xi, ci)
            if d == 1:
                y = y[::-1]
            return jnp.moveaxis(y, 0, 1).reshape(bsz, -1, dm)

        xc_re, xc_im = s5_scan(ab_re, ab_im, *drive(uc_d), None, None)
        xx_re, xx_im = s5_scan(ab_re, ab_im, *drive(ux_d), xc_re[-1], xc_im[-1])
        y_x = y_x + readout(xx_re, xx_im)
        if with_ctx_out:
            y_c = y_c + readout(xc_re, xc_im)

    def glu(y):
        z = jax.nn.gelu(y).astype(ux.dtype) @ w_glu + b_glu
        z1, z2 = jnp.split(z, 2, axis=-1)
        return z1 * jax.nn.sigmoid(z2)

    return (glu(y_c) if with_ctx_out else None), glu(y_x)


def swiglu(h, w13, w2):
    a, b = jnp.split(h @ w13, 2, axis=-1)
    return (jax.nn.silu(a) * b) @ w2


def moe_swiglu(h, w_router, w13, w2):
    logits = (h @ w_router).astype(jnp.float32)
    top_v, top_i = lax.top_k(logits, TOP_K)
    gates = jax.nn.softmax(top_v, axis=-1)
    comb = jnp.sum(jax.nn.one_hot(top_i, N_EXPERTS, dtype=jnp.float32) * gates[..., None], axis=1).astype(h.dtype)
    out = jnp.zeros_like(h)
    for e in range(N_EXPERTS):
        out = out + comb[:, e:e + 1] * swiglu(h, w13[e], w2[e])
    return out


def setup_inputs(seed: int = 0) -> dict:
    key = jax.random.key(seed)
    ks = iter(jax.random.split(key, 40))
    f32 = jnp.float32
    D = D_MODEL
    n_gla = (DEPTH + 1) // 2
    n_s5 = DEPTH // 2

    def nrm(shape, scale):
        return jax.random.normal(next(ks), shape, f32) * scale

    n_idx = jnp.arange(S5_STATE, dtype=f32) * math.pi
    return {
        'x': nrm((BATCH, SEQ, D), 1.0),
        'c': nrm((BATCH, D), 1.0),
        'ctx': nrm((BATCH, CTX_LEN, D), 1.0),
        'c_ctx': nrm((D,), 1.0),
        'w_mod': nrm((DEPTH, D, 6 * D), 0.5 * D ** -0.5),
        'b_mod': nrm((DEPTH, 6 * D), 0.02),
        'norm1': 1.0 + nrm((DEPTH, D), 0.02),
        'norm2': 1.0 + nrm((DEPTH, D), 0.02),
        'norm_f': 1.0 + nrm((D,), 0.02),
        'gla_w_in': nrm((n_gla, D, GLA_IN), D ** -0.5),
        'gla_w_gate_up': nrm((n_gla, 2, GLA_GATE_RANK, GLA_DK), GLA_GATE_RANK ** -0.5),
        'gla_b_gate': nrm((n_gla, 2, GLA_DK), 0.1),
        'gla_g_head': 1.0 + nrm((n_gla, GLA_HV), 0.02),
        'gla_w_out': nrm((n_gla, GLA_DV, D), GLA_DV ** -0.5),
        'ffn_w13': nrm((n_gla, D, 2 * D_FF), D ** -0.5),
        'ffn_w2': nrm((n_gla, D_FF, D), D_FF ** -0.5),
        's5_a_re': -0.5 + nrm((n_s5, 2, S5_GROUPS, S5_STATE), 0.01),
        's5_a_im': n_idx + nrm((n_s5, 2, S5_GROUPS, S5_STATE), 0.01),
        's5_log_dt': jax.random.uniform(next(ks), (n_s5, 2, S5_GROUPS), f32,
                                        minval=math.log(S5_DT_MIN), maxval=math.log(S5_DT_MAX)),
        's5_b_re': nrm((n_s5, 2, S5_GROUPS, S5_STATE, S5_GROUP), (2 * S5_GROUP) ** -0.5),
        's5_b_im': nrm((n_s5, 2, S5_GROUPS, S5_STATE, S5_GROUP), (2 * S5_GROUP) ** -0.5),
        's5_c_re': nrm((n_s5, 2, S5_GROUPS, S5_GROUP, S5_STATE), S5_STATE ** -0.5),
        's5_c_im': nrm((n_s5, 2, S5_GROUPS, S5_GROUP, S5_STATE), S5_STATE ** -0.5),
        's5_d': nrm((n_s5, D), 1.0),
        's5_w_glu': nrm((n_s5, D, 2 * D), D ** -0.5),
        's5_b_glu': nrm((n_s5, 2 * D), 0.02),
        'moe_w_router': nrm((n_s5, D, N_EXPERTS), D ** -0.5),
        'moe_w13': nrm((n_s5, N_EXPERTS, D, 2 * D_FF), D ** -0.5),
        'moe_w2': nrm((n_s5, N_EXPERTS, D_FF, D), D_FF ** -0.5),
    }


def reference(x, c, ctx, c_ctx, w_mod, b_mod, norm1, norm2, norm_f,
              gla_w_in, gla_w_gate_up, gla_b_gate, gla_g_head, gla_w_out, ffn_w13, ffn_w2,
              s5_a_re, s5_a_im, s5_log_dt, s5_b_re, s5_b_im, s5_c_re, s5_c_im, s5_d,
              s5_w_glu, s5_b_glu, moe_w_router, moe_w13, moe_w2):
    bsz, seq, dm = x.shape
    lc = ctx.shape[1]
    rows = seq // GRID_W

    def to_colmajor(t):
        return t.reshape(bsz, rows, GRID_W, dm).transpose(0, 2, 1, 3).reshape(bsz, seq, dm)

    def from_colmajor(t):
        return t.reshape(bsz, GRID_W, rows, dm).transpose(0, 2, 1, 3).reshape(bsz, seq, dm)

    hx, hc = x, ctx
    silu_c = jax.nn.silu(c)
    silu_cc = jax.nn.silu(c_ctx)
    for i in range(DEPTH):
        last = i == DEPTH - 1
        j = i // 2
        sh1, sc1, g1, sh2, sc2, g2 = jnp.split(silu_c @ w_mod[i] + b_mod[i], 6, axis=-1)
        csh1, csc1, cg1, csh2, csc2, cg2 = jnp.split(silu_cc @ w_mod[i] + b_mod[i], 6, axis=-1)

        ux = rmsnorm(hx, norm1[i]) * (1.0 + sc1[:, None]) + sh1[:, None]
        uc = rmsnorm(hc, norm1[i]) * (1.0 + csc1) + csh1
        if i % N_MIXERS == 0:
            yc, yx = gla_mixer(uc, ux, gla_w_in[j], gla_w_gate_up[j], gla_b_gate[j], gla_g_head[j],
                               gla_w_out[j], not last)
        else:
            yc, yx = s5_mixer(uc, to_colmajor(ux), s5_a_re[j], s5_a_im[j], s5_log_dt[j], s5_b_re[j],
                              s5_b_im[j], s5_c_re[j], s5_c_im[j], s5_d[j], s5_w_glu[j], s5_b_glu[j], not last)
            yx = from_colmajor(yx)
        hx = hx + g1[:, None] * yx
        if not last:
            hc = hc + cg1 * yc

        vx = rmsnorm(hx, norm2[i]) * (1.0 + sc2[:, None]) + sh2[:, None]
        if not last:
            vc = rmsnorm(hc, norm2[i]) * (1.0 + csc2) + csh2
            v = jnp.concatenate([vc, vx], axis=1)
        else:
            v = vx
        flat = v.reshape(-1, dm)
        if i % 2 == 0:
            f = swiglu(flat, ffn_w13[j], ffn_w2[j])
        else:
            f = moe_swiglu(flat, moe_w_router[j], moe_w13[j], moe_w2[j])
        f = f.reshape(bsz, -1, dm)
        if not last:
            hc = hc + cg2 * f[:, :lc]
            hx = hx + g2[:, None] * f[:, lc:]
        else:
            hx = hx + g2[:, None] * f
    return rmsnorm(hx, norm_f)
```

```python
import functools

import jax
import jax.numpy as jnp
from jax import lax
from jax.experimental import pallas as pl
from jax.experimental.pallas import tpu as pltpu

F32 = jnp.float32
BF16 = jnp.bfloat16

EPS = 1e-6
GRID_W = 64

GLA_HEADS = 4
GLA_GATE_RANK = 16
GLA_GATE_TAU = 16.0
GLA_CHUNK = 64

S5_GROUP = 16
S5_CHUNK = 32
S5_GROUP_BLOCK = 4

N_EXPERTS = 8
TOP_K = 2
ROUTER_LANES = 128

ROW_TILE = 256
MOE_TILE = 256
VMEM_LIMIT = 56 * 1024 * 1024

SH1, SC1, G1, SH2, SC2, G2 = range(6)


def _cp(sem, vmem=None):
    return pltpu.CompilerParams(dimension_semantics=sem, vmem_limit_bytes=vmem)


def _dot(a, b):
    return jnp.dot(a, b, preferred_element_type=F32)


def _split(x):
    hi = x.astype(BF16)
    lo = (x - hi.astype(F32)).astype(BF16)
    return hi, lo


def _dot3(a, b):
    ah, al = _split(a)
    bh, bl = _split(b)
    return _dot(ah, bh) + _dot(ah, bl) + _dot(al, bh)


def _sigmoid(x):
    return 1.0 / (1.0 + jnp.exp(-x))


def _silu(x):
    return x * _sigmoid(x)


def _mod_kernel(c_ref, w_ref, b_ref, o_ref):
    x = _silu(c_ref[...]).astype(BF16)
    o_ref[...] = _dot(x, w_ref[...].astype(BF16)) + b_ref[...]


def _modulation(c8, w_mod, b_mod):
    depth, d, n = w_mod.shape
    tn = 1024
    return pl.pallas_call(
        _mod_kernel,
        out_shape=jax.ShapeDtypeStruct((depth, 8, n), F32),
        grid=(depth, n // tn),
        in_specs=[
            pl.BlockSpec((8, d), lambda i, j: (0, 0)),
            pl.BlockSpec((None, d, tn), lambda i, j: (i, 0, j)),
            pl.BlockSpec((None, 1, tn), lambda i, j: (i, 0, j)),
        ],
        out_specs=pl.BlockSpec((None, 8, tn), lambda i, j: (i, 0, j)),
        compiler_params=_cp(("arbitrary", "arbitrary"), VMEM_LIMIT),
        name="modulation",
    )(c8, w_mod, b_mod.reshape(depth, 1, n))


def _normmod_body(x_ref, nw_ref, mod_ref, sh_row, sc_row):
    x = x_ref[...]
    ms = jnp.mean(x * x, axis=-1, keepdims=True)
    y = x * lax.rsqrt(ms + EPS) * nw_ref[...]
    return y * (1.0 + mod_ref[sc_row:sc_row + 1, :]) + mod_ref[sh_row:sh_row + 1, :]


def _normmod_kernel(x_ref, nw_ref, mod_ref, o_ref, *, sh_row, sc_row):
    o_ref[...] = _normmod_body(x_ref, nw_ref, mod_ref, sh_row, sc_row).astype(o_ref.dtype)


def _normmod_router_kernel(x_ref, nw_ref, mod_ref, wr_ref, o_ref, idx_ref, gate_ref, *, sh_row, sc_row):
    y = _normmod_body(x_ref, nw_ref, mod_ref, sh_row, sc_row)
    o_ref[...] = y
    logits = _dot3(y, wr_ref[...])
    lane = lax.broadcasted_iota(jnp.int32, logits.shape, 1)
    lane_f = lane.astype(F32)
    neg = jnp.float32(-jnp.inf)
    l1 = jnp.where(lane < N_EXPERTS, logits, neg)
    m1 = jnp.max(l1, axis=-1, keepdims=True)
    i1 = jnp.min(jnp.where(l1 == m1, lane_f, float(ROUTER_LANES)), axis=-1, keepdims=True)
    l2 = jnp.where(lane_f == i1, neg, l1)
    m2 = jnp.max(l2, axis=-1, keepdims=True)
    i2 = jnp.min(jnp.where(l2 == m2, lane_f, float(ROUTER_LANES)), axis=-1, keepdims=True)
    e = jnp.exp(m2 - m1)
    g1 = 1.0 / (1.0 + e)
    g2 = e * g1
    idx_ref[...] = jnp.where(lane == 0, i1, jnp.where(lane == 1, i2, 0.0)).astype(jnp.int32)
    gate_ref[...] = jnp.where(lane == 0, g1, jnp.where(lane == 1, g2, 0.0))


def _row_specs(d, nct):
    x_spec = pl.BlockSpec((None, ROW_TILE, d), lambda b, t: (b, t, 0))
    mod_spec = pl.BlockSpec((None, None, 6, d), lambda b, t: (b, jnp.where(t >= nct, 1, 0), 0, 0))
    return x_spec, mod_spec


def _normmod(h, nw, modtab, sh_row, sc_row, nct, out_dtype):
    bsz, length, d = h.shape
    x_spec, mod_spec = _row_specs(d, nct)
    return pl.pallas_call(
        functools.partial(_normmod_kernel, sh_row=sh_row, sc_row=sc_row),
        out_shape=jax.ShapeDtypeStruct(h.shape, out_dtype),
        grid=(bsz, length // ROW_TILE),
        in_specs=[x_spec, pl.BlockSpec((1, d), lambda b, t: (0, 0)), mod_spec],
        out_specs=x_spec,
        compiler_params=_cp(("arbitrary", "arbitrary")),
        name="normmod",
    )(h, nw.reshape(1, d), modtab)


def _normmod_router(h, nw, modtab, w_router, sh_row, sc_row, nct):
    bsz, length, d = h.shape
    x_spec, mod_spec = _row_specs(d, nct)
    wr = jnp.pad(w_router, ((0, 0), (0, ROUTER_LANES - w_router.shape[1])))
    lane_spec = pl.BlockSpec((None, ROW_TILE, ROUTER_LANES), lambda b, t: (b, t, 0))
    return pl.pallas_call(
        functools.partial(_normmod_router_kernel, sh_row=sh_row, sc_row=sc_row),
        out_shape=(
            jax.ShapeDtypeStruct(h.shape, F32),
            jax.ShapeDtypeStruct((bsz, length, ROUTER_LANES), jnp.int32),
            jax.ShapeDtypeStruct((bsz, length, ROUTER_LANES), F32),
        ),
        grid=(bsz, length // ROW_TILE),
        in_specs=[x_spec, pl.BlockSpec((1, d), lambda b, t: (0, 0)), mod_spec,
                  pl.BlockSpec((d, ROUTER_LANES), lambda b, t: (0, 0))],
        out_specs=(x_spec, lane_spec, lane_spec),
        compiler_params=_cp(("arbitrary", "arbitrary")),
        name="normmod_router",
    )(h, nw.reshape(1, d), modtab, wr)


def _rms_kernel(x_ref, nw_ref, o_ref):
    x = x_ref[...]
    ms = jnp.mean(x * x, axis=-1, keepdims=True)
    o_ref[...] = x * lax.rsqrt(ms + EPS) * nw_ref[...]


def _final_norm(h, nw):
    bsz, length, d = h.shape
    x_spec = pl.BlockSpec((None, ROW_TILE, d), lambda b, t: (b, t, 0))
    return pl.pallas_call(
        _rms_kernel,
        out_shape=jax.ShapeDtypeStruct(h.shape, F32),
        grid=(bsz, length // ROW_TILE),
        in_specs=[x_spec, pl.BlockSpec((1, d), lambda b, t: (0, 0))],
        out_specs=x_spec,
        compiler_params=_cp(("arbitrary", "arbitrary")),
        name="final_norm",
    )(h, nw.reshape(1, d))


def _cast_weights(pairs):
    @pl.when(pl.program_id(1) == 0)
    def _():
        for w_ref, wbf_ref in pairs:
            wbf_ref[...] = w_ref[...].astype(BF16)


def _mm_plain_kernel(x_ref, w_ref, o_ref, wbf_ref):
    _cast_weights([(w_ref, wbf_ref)])
    o_ref[...] = _dot(x_ref[...], wbf_ref[...]).astype(o_ref.dtype)


def _mm_swiglu_kernel(x_ref, wa_ref, wb_ref, o_ref, wabf_ref, wbbf_ref):
    _cast_weights([(wa_ref, wabf_ref), (wb_ref, wbbf_ref)])
    x = x_ref[...]
    a = _dot(x, wabf_ref[...])
    b = _dot(x, wbbf_ref[...])
    o_ref[...] = (_silu(a) * b).astype(o_ref.dtype)


def _mm_res_kernel(x_ref, w_ref, res_ref, mod_ref, o_ref, wbf_ref, *, g_row):
    _cast_weights([(w_ref, wbf_ref)])
    acc = _dot(x_ref[...], wbf_ref[...])
    o_ref[...] = res_ref[...] + mod_ref[g_row:g_row + 1, :] * acc


def _mm_glu_res_kernel(x_ref, wa_ref, wb_ref, ba_ref, bb_ref, res_ref, mod_ref, o_ref, wabf_ref, wbbf_ref,
                       *, g_row):
    _cast_weights([(wa_ref, wabf_ref), (wb_ref, wbbf_ref)])
    x = x_ref[...]
    z1 = _dot(x, wabf_ref[...]) + ba_ref[...]
    z2 = _dot(x, wbbf_ref[...]) + bb_ref[...]
    o_ref[...] = res_ref[...] + mod_ref[g_row:g_row + 1, :] * (z1 * _sigmoid(z2))


def _mm_plain(x, w, n, tm, tn, out_dtype, name):
    r, k = x.shape
    return pl.pallas_call(
        _mm_plain_kernel,
        out_shape=jax.ShapeDtypeStruct((r, n), out_dtype),
        grid=(n // tn, r // tm),
        in_specs=[pl.BlockSpec((tm, k), lambda j, m: (m, 0)),
                  pl.BlockSpec((k, tn), lambda j, m: (0, j))],
        out_specs=pl.BlockSpec((tm, tn), lambda j, m: (m, j)),
        scratch_shapes=[pltpu.VMEM((k, tn), BF16)],
        compiler_params=_cp(("arbitrary", "arbitrary"), VMEM_LIMIT),
        name=name,
    )(x, w)


def _mm_swiglu(x, w13, tm, tn, name):
    r, k = x.shape
    half = w13.shape[1] // 2
    nb = half // tn
    return pl.pallas_call(
        _mm_swiglu_kernel,
        out_shape=jax.ShapeDtypeStruct((r, half), BF16),
        grid=(nb, r // tm),
        in_specs=[pl.BlockSpec((tm, k), lambda j, m: (m, 0)),
                  pl.BlockSpec((k, tn), lambda j, m: (0, j)),
                  pl.BlockSpec((k, tn), lambda j, m: (0, nb + j))],
        out_specs=pl.BlockSpec((tm, tn), lambda j, m: (m, j)),
        scratch_shapes=[pltpu.VMEM((k, tn), BF16), pltpu.VMEM((k, tn), BF16)],
        compiler_params=_cp(("arbitrary", "arbitrary"), VMEM_LIMIT),
        name=name,
    )(x, w13, w13)


def _res_specs(tn, tpb, nct):
    res_spec = pl.BlockSpec((ROW_TILE, tn), lambda j, m: (m, j))
    mod_spec = pl.BlockSpec((None, None, 6, tn),
                            lambda j, m: (m // tpb, jnp.where(m % tpb >= nct, 1, 0), 0, j))
    return res_spec, mod_spec


def _mm_res(x, w, res, modtab, g_row, tpb, nct, tn, name):
    r, k = x.shape
    n = w.shape[1]
    tm = ROW_TILE
    res_spec, mod_spec = _res_specs(tn, tpb, nct)
    return pl.pallas_call(
        functools.partial(_mm_res_kernel, g_row=g_row),
        out_shape=jax.ShapeDtypeStruct((r, n), F32),
        grid=(n // tn, r // tm),
        in_specs=[pl.BlockSpec((tm, k), lambda j, m: (m, 0)),
                  pl.BlockSpec((k, tn), lambda j, m: (0, j)),
                  res_spec, mod_spec],
        out_specs=pl.BlockSpec((tm, tn), lambda j, m: (m, j)),
        scratch_shapes=[pltpu.VMEM((k, tn), BF16)],
        compiler_params=_cp(("arbitrary", "arbitrary"), VMEM_LIMIT),
        name=name,
    )(x, w, res, modtab)


def _mm_glu_res(x, w, bias, res, modtab, g_row, tpb, nct, tn, name):
    r, k = x.shape
    half = w.shape[1] // 2
    nb = half // tn
    tm = ROW_TILE
    res_spec, mod_spec = _res_specs(tn, tpb, nct)
    bias2 = bias.reshape(1, 2 * half)
    return pl.pallas_call(
        functools.partial(_mm_glu_res_kernel, g_row=g_row),
        out_shape=jax.ShapeDtypeStruct((r, half), F32),
        grid=(nb, r // tm),
        in_specs=[pl.BlockSpec((tm, k), lambda j, m: (m, 0)),
                  pl.BlockSpec((k, tn), lambda j, m: (0, j)),
                  pl.BlockSpec((k, tn), lambda j, m: (0, nb + j)),
                  pl.BlockSpec((1, tn), lambda j, m: (0, j)),
                  pl.BlockSpec((1, tn), lambda j, m: (0, nb + j)),
                  res_spec, mod_spec],
        out_specs=pl.BlockSpec((tm, tn), lambda j, m: (m, j)),
        scratch_shapes=[pltpu.VMEM((k, tn), BF16), pltpu.VMEM((k, tn), BF16)],
        compiler_params=_cp(("arbitrary", "arbitrary"), VMEM_LIMIT),
        name=name,
    )(x, w, w, bias2, bias2, res, modtab)


def _gla_gate_kernel(x_ref, wgl_ref, wup_ref, b_ref, o_ref):
    wh, wl = _split(wgl_ref[...])
    x = x_ref[...]
    gl = _dot(x, wh) + _dot(x, wl)
    z = _dot3(gl, wup_ref[...]) + b_ref[...]
    log_sig = jnp.minimum(z, 0.0) - jnp.log(1.0 + jnp.exp(-jnp.abs(z)))
    o_ref[...] = log_sig * (1.0 / GLA_GATE_TAU)


def _gla_gate(u, w_gl, w_up_bd, b_gate):
    r, d = u.shape
    n = w_up_bd.shape[1]
    tm = 512
    return pl.pallas_call(
        _gla_gate_kernel,
        out_shape=jax.ShapeDtypeStruct((r, n), F32),
        grid=(r // tm,),
        in_specs=[pl.BlockSpec((tm, d), lambda m: (m, 0)),
                  pl.BlockSpec(w_gl.shape, lambda m: (0, 0)),
                  pl.BlockSpec(w_up_bd.shape, lambda m: (0, 0)),
                  pl.BlockSpec((1, n), lambda m: (0, 0))],
        out_specs=pl.BlockSpec((tm, n), lambda m: (m, 0)),
        compiler_params=_cp(("arbitrary",)),
        name="gla_gate",
    )(u, w_gl, w_up_bd, b_gate.reshape(1, n))


def _gla_scan_kernel(tri_ref, q_ref, k_ref, v_ref, g_ref, o_ref, s_ref, *, hk, hv):
    @pl.when(pl.program_id(2) == 0)
    def _():
        s_ref[...] = jnp.zeros_like(s_ref)

    tri = tri_ref[...]
    ones = jnp.ones((GLA_CHUNK, 128), BF16)
    scale = hk ** -0.5
    tn_dims = (((0,), (0,)), ((), ()))
    nt_dims = (((1,), (1,)), ((), ()))
    for h in range(GLA_HEADS):
        q = q_ref[:, h * hk:(h + 1) * hk]
        k = k_ref[:, h * hk:(h + 1) * hk]
        v = v_ref[:, h * hv:(h + 1) * hv].astype(BF16)
        g = g_ref[:, h * hk:(h + 1) * hk]
        gh, gl = _split(g)
        b = _dot(tri, gh) + _dot(tri, gl)
        tot = jnp.sum(g, axis=0, keepdims=True)
        tot_col = (lax.dot_general(gh, ones, tn_dims, preferred_element_type=F32)
                   + lax.dot_general(gl, ones, tn_dims, preferred_element_type=F32))
        qd = (q * scale * jnp.exp(b)).astype(BF16)
        kd = (k * jnp.exp(-b)).astype(BF16)
        kt = (k * jnp.exp(tot - b)).astype(BF16)
        scores = jnp.where(tri > 0, lax.dot_general(qd, kd, nt_dims, preferred_element_type=F32), 0.0)
        s = s_ref[h]
        o = _dot(scores.astype(BF16), v) + _dot(qd, s.astype(BF16))
        o_ref[:, h * hv:(h + 1) * hv] = o
        decay = jnp.exp(tot_col)
        decay = jnp.concatenate([decay] * (hv // 128), axis=1)
        s_ref[h] = decay * s + lax.dot_general(kt, v, tn_dims, preferred_element_type=F32)


def _gla_scan(proj, log_a, nctx_chunks, dk, dv):
    bsz, length, _ = proj.shape
    nchunks = length // GLA_CHUNK
    hk, hv = dk // GLA_HEADS, dv // GLA_HEADS
    idx = jnp.arange(GLA_CHUNK)
    tri = jnp.stack([idx[:, None] >= idx[None, :], idx[:, None] <= idx[None, :]]).astype(BF16)

    def cidx(d, s):
        back = jnp.where(s < nctx_chunks, nctx_chunks - 1 - s, nchunks - 1 - (s - nctx_chunks))
        return jnp.where(d == 0, s, back)

    return pl.pallas_call(
        functools.partial(_gla_scan_kernel, hk=hk, hv=hv),
        out_shape=jax.ShapeDtypeStruct((2, bsz, length, dv), F32),
        grid=(bsz, 2, nchunks),
        in_specs=[
            pl.BlockSpec((None, GLA_CHUNK, GLA_CHUNK), lambda b, d, s: (d, 0, 0)),
            pl.BlockSpec((None, GLA_CHUNK, dk), lambda b, d, s: (b, cidx(d, s), 0)),
            pl.BlockSpec((None, GLA_CHUNK, dk), lambda b, d, s: (b, cidx(d, s), 1)),
            pl.BlockSpec((None, GLA_CHUNK, dv), lambda b, d, s: (b, cidx(d, s), (2 * dk) // dv)),
            pl.BlockSpec((None, GLA_CHUNK, dk), lambda b, d, s: (b, cidx(d, s), d)),
        ],
        out_specs=pl.BlockSpec((None, None, GLA_CHUNK, dv), lambda b, d, s: (d, b, cidx(d, s), 0)),
        scratch_shapes=[pltpu.VMEM((GLA_HEADS, hk, hv), F32)],
        compiler_params=_cp(("arbitrary", "arbitrary", "arbitrary"), VMEM_LIMIT),
        name="gla_scan",
    )(tri, proj, proj, proj, log_a)


def _gla_finish_kernel(of_ref, ob_ref, r_ref, gh_ref, o_ref, *, hv):
    for h in range(GLA_HEADS):
        sl = slice(h * hv, (h + 1) * hv)
        o = of_ref[:, sl] + ob_ref[:, sl]
        ms = jnp.mean(o * o, axis=-1, keepdims=True)
        y = o * lax.rsqrt(ms + EPS) * gh_ref[...]
        o_ref[:, sl] = (y * _silu(r_ref[:, sl])).astype(o_ref.dtype)


def _gla_finish(o2, proj, g_head, dk, dv):
    _, bsz, length, _ = o2.shape
    hv = dv // GLA_HEADS
    tm = ROW_TILE
    return pl.pallas_call(
        functools.partial(_gla_finish_kernel, hv=hv),
        out_shape=jax.ShapeDtypeStruct((bsz, length, dv), BF16),
        grid=(bsz, length // tm),
        in_specs=[
            pl.BlockSpec((None, None, tm, dv), lambda b, t: (0, b, t, 0)),
            pl.BlockSpec((None, None, tm, dv), lambda b, t: (1, b, t, 0)),
            pl.BlockSpec((None, tm, dv), lambda b, t: (b, t, (2 * dk + dv) // dv)),
            pl.BlockSpec((1, hv), lambda b, t: (0, 0)),
        ],
        out_specs=pl.BlockSpec((None, tm, dv), lambda b, t: (b, t, 0)),
        compiler_params=_cp(("arbitrary", "arbitrary")),
        name="gla_finish",
    )(o2, o2, proj, g_head.reshape(1, hv))


def _gla_layer(h, modtab, nw, w_in, w_gate_up, b_gate, g_head, w_out, nctx):
    bsz, length, d = h.shape
    dk = w_gate_up.shape[2]
    dv = w_out.shape[0]
    nmain = 2 * dk + 2 * dv
    tpb = length // ROW_TILE
    nct = nctx // ROW_TILE
    u = _normmod(h, nw, modtab, SH1, SC1, nct, BF16).reshape(bsz * length, d)
    proj = _mm_plain(u, w_in, nmain, 512, 512, F32, "gla_in_proj")
    rank = GLA_GATE_RANK
    w_up_bd = jnp.zeros((2 * rank, 2 * dk), F32)
    w_up_bd = w_up_bd.at[:rank, :dk].set(w_gate_up[0]).at[rank:, dk:].set(w_gate_up[1])
    log_a = _gla_gate(u, w_in[:, nmain:], w_up_bd, b_gate.reshape(2 * dk))
    proj3 = proj.reshape(bsz, length, nmain)
    o2 = _gla_scan(proj3, log_a.reshape(bsz, length, 2 * dk), nctx // GLA_CHUNK, dk, dv)
    gated = _gla_finish(o2, proj3, g_head, dk, dv).reshape(bsz * length, dv)
    out = _mm_res(gated, w_out, h.reshape(bsz * length, d), modtab, G1, tpb, nct, 512, "gla_out_proj")
    return out.reshape(bsz, length, d)


def _s5_kernel(u_ref, l4_ref, r2_ref, pq_ref, y_ref, sv_ref, ss_ref, w_ref, *, bsz, nchunks, np2):
    gb, _, k = u_ref.shape
    row = lax.broadcasted_iota(jnp.int32, (k, k), 0)
    col = lax.broadcasted_iota(jnp.int32, (k, k), 1)
    causal = (row // S5_GROUP) <= (col // S5_GROUP)
    for g in range(gb):
        u = u_ref[g]
        l4 = l4_ref[g]
        r2 = r2_ref[g]
        pt = pq_ref[g, 0:1, :]
        qt = pq_ref[g, 1:2, :]
        qts = pq_ref[g, 2:3, :]
        toep = jnp.where(causal, _dot3(l4[:, :np2], r2), 0.0).astype(BF16)
        y_ref[g] = _dot(u, toep)
        v2 = _dot(u, l4.astype(BF16))
        v, vs = v2[:, :np2], v2[:, np2:]
        sv_ref[:, g * np2:(g + 1) * np2] = pt * v + qt * vs
        ss_ref[:, g * np2:(g + 1) * np2] = pt * vs + qts * v
    lanes = gb * np2
    pt = jnp.broadcast_to(jnp.concatenate([pq_ref[g, 0:1, :] for g in range(gb)], axis=1), (bsz, lanes))
    qt = jnp.broadcast_to(jnp.concatenate([pq_ref[g, 1:2, :] for g in range(gb)], axis=1), (bsz, lanes))
    qts = jnp.broadcast_to(jnp.concatenate([pq_ref[g, 2:3, :] for g in range(gb)], axis=1), (bsz, lanes))
    w = jnp.zeros((bsz, lanes), F32)
    ws = jnp.zeros((bsz, lanes), F32)
    for j in range(nchunks):
        rows = slice(j * bsz, (j + 1) * bsz)
        w_ref[rows, :] = w
        if j + 1 < nchunks:
            w_next = pt * w + qt * ws + sv_ref[rows, :]
            ws = pt * ws + qts * w + ss_ref[rows, :]
            w = w_next
    for g in range(gb):
        carry = _dot(w_ref[:, g * np2:(g + 1) * np2].astype(BF16), r2_ref[g].astype(BF16))
        y_ref[g] = y_ref[g] + carry


def _s5_operators(a_re, a_im, log_dt, b_re, b_im, c_re, c_im):
    t_len = S5_CHUNK
    dt = jnp.exp(log_dt)[..., None]
    mag = jnp.exp(a_re * dt)
    ab_re = mag * jnp.cos(a_im * dt)
    ab_im = mag * jnp.sin(a_im * dt)
    den = a_re * a_re + a_im * a_im
    f_re = ((ab_re - 1.0) * a_re + ab_im * a_im) / den
    f_im = (ab_im * a_re - (ab_re - 1.0) * a_im) / den
    bb_re = f_re[..., None] * b_re - f_im[..., None] * b_im
    bb_im = f_re[..., None] * b_im + f_im[..., None] * b_re

    def cpow(n):
        n = jnp.asarray(n, F32)
        n = n.reshape(n.shape + (1, 1, 1))
        m = jnp.exp(a_re * dt * n)
        th = a_im * dt * n
        return m * jnp.cos(th), m * jnp.sin(th)

    steps = jnp.arange(t_len)
    an_re, an_im = cpow(-steps)
    ap_re, ap_im = cpow(steps)
    at_re, at_im = cpow(jnp.asarray(t_len))
    l_re = jnp.einsum('sdgp,dgpc->dgscp', an_re, bb_re) - jnp.einsum('sdgp,dgpc->dgscp', an_im, bb_im)
    l_im = jnp.einsum('sdgp,dgpc->dgscp', an_re, bb_im) + jnp.einsum('sdgp,dgpc->dgscp', an_im, bb_re)
    r_re = jnp.einsum('tdgp,dgcp->dgptc', ap_re, c_re) - jnp.einsum('tdgp,dgcp->dgptc', ap_im, c_im)
    r_im = jnp.einsum('tdgp,dgcp->dgptc', ap_re, c_im) + jnp.einsum('tdgp,dgcp->dgptc', ap_im, c_re)
    ngroups, nstate = a_re.shape[1], a_re.shape[2]
    k = t_len * S5_GROUP
    l_re = l_re.reshape(2, ngroups, k, nstate)
    l_im = l_im.reshape(2, ngroups, k, nstate)
    l4 = jnp.concatenate([l_re, l_im, l_im, l_re], axis=-1)
    r2 = jnp.concatenate([r_re.reshape(2, ngroups, nstate, k), -r_im.reshape(2, ngroups, nstate, k)], axis=2)
    zero = jnp.zeros_like(at_re)
    pq = jnp.stack([jnp.concatenate([at_re, at_re], -1),
                    jnp.concatenate([-at_im, at_im], -1),
                    jnp.concatenate([at_im, -at_im], -1)] + [jnp.concatenate([zero, zero], -1)] * 5, axis=2)
    return l4, r2, pq


def _s5_scan(useq, l4, r2, pq, bsz):
    _, ngroups, rows, k = useq.shape
    np2 = r2.shape[2]
    gb = S5_GROUP_BLOCK
    nchunks = rows // bsz

    def spec(s1, s2):
        return pl.BlockSpec((None, gb, s1, s2), lambda d, g: (d, g, 0, 0))

    return pl.pallas_call(
        functools.partial(_s5_kernel, bsz=bsz, nchunks=nchunks, np2=np2),
        out_shape=jax.ShapeDtypeStruct(useq.shape, F32),
        grid=(2, ngroups // gb),
        in_specs=[spec(rows, k), spec(k, 2 * np2), spec(np2, k), spec(8, np2)],
        out_specs=spec(rows, k),
        scratch_shapes=[pltpu.VMEM((rows, gb * np2), F32)] * 3,
        compiler_params=_cp(("arbitrary", "arbitrary"), VMEM_LIMIT),
        name="s5_scan",
    )(useq, l4, r2, pq)


def _s5_finish_kernel(ya_ref, yb_ref, u_ref, d_ref, o_ref):
    y = ya_ref[...] + yb_ref[...] + u_ref[...] * d_ref[...]
    c0 = 0.7978845608028654
    gelu = 0.5 * y * (1.0 + jnp.tanh(c0 * (y + 0.044715 * (y * y * y))))
    o_ref[...] = gelu.astype(o_ref.dtype)


def _s5_finish(ya, yb, u, d_skip):
    bsz, length, d = ya.shape
    x_spec = pl.BlockSpec((None, ROW_TILE, d), lambda b, t: (b, t, 0))
    return pl.pallas_call(
        _s5_finish_kernel,
        out_shape=jax.ShapeDtypeStruct(ya.shape, BF16),
        grid=(bsz, length // ROW_TILE),
        in_specs=[x_spec, x_spec, x_spec, pl.BlockSpec((1, d), lambda b, t: (0, 0))],
        out_specs=x_spec,
        compiler_params=_cp(("arbitrary", "arbitrary")),
        name="s5_finish",
    )(ya, yb, u, d_skip.reshape(1, d))


def _to_colmajor(t, rows):
    bsz, seq, d = t.shape
    return t.reshape(bsz, rows, GRID_W, d).transpose(0, 2, 1, 3).reshape(bsz, seq, d)


def _from_colmajor(t, rows):
    bsz, seq, d = t.shape
    return t.reshape(bsz, GRID_W, rows, d).transpose(0, 2, 1, 3).reshape(bsz, seq, d)


def _s5_layer(h, modtab, nw, a_re, a_im, log_dt, b_re, b_im, c_re, c_im, d_skip, w_glu, b_glu, nctx, last):
    bsz, length, d = h.shape
    seq = length - nctx
    grid_rows = seq // GRID_W
    ngroups = d // S5_GROUP
    nct = nctx // ROW_TILE
    t_len = S5_CHUNK
    nchunks = length // t_len
    u = _normmod(h, nw, modtab, SH1, SC1, nct, F32)
    uc, uxc = u[:, :nctx], _to_colmajor(u[:, nctx:], grid_rows)
    z = jnp.stack([jnp.concatenate([uc, uxc], axis=1),
                   jnp.concatenate([jnp.flip(uc, 1), jnp.flip(uxc, 1)], axis=1)]).astype(BF16)
    useq = z.reshape(2, bsz, nchunks, t_len, ngroups, S5_GROUP).transpose(0, 4, 2, 1, 3, 5)
    useq = useq.reshape(2, ngroups, nchunks * bsz, t_len * S5_GROUP)
    l4, r2, pq = _s5_operators(a_re, a_im, log_dt, b_re, b_im, c_re, c_im)
    yseq = _s5_scan(useq, l4, r2, pq, bsz)
    y = yseq.reshape(2, ngroups, nchunks, bsz, t_len, S5_GROUP).transpose(0, 3, 2, 4, 1, 5)
    y = y.reshape(2, bsz, length, d)
    yb = jnp.concatenate([jnp.flip(y[1][:, :nctx], 1), jnp.flip(y[1][:, nctx:], 1)], axis=1)
    ya = y[0]
    if last:
        ya = _from_colmajor(ya[:, nctx:], grid_rows)
        yb = _from_colmajor(yb[:, nctx:], grid_rows)
        u, h, nct = u[:, nctx:], h[:, nctx:], 0
    else:
        ya = jnp.concatenate([ya[:, :nctx], _from_colmajor(ya[:, nctx:], grid_rows)], axis=1)
        yb = jnp.concatenate([yb[:, :nctx], _from_colmajor(yb[:, nctx:], grid_rows)], axis=1)
    rows_out = h.shape[0] * h.shape[1]
    act = _s5_finish(ya, yb, u, d_skip).reshape(rows_out, d)
    out = _mm_glu_res(act, w_glu, b_glu, h.reshape(rows_out, d), modtab, G1, h.shape[1] // ROW_TILE, nct, 512,
                      "s5_glu")
    return out.reshape(h.shape)


def _dense_ffn(h, modtab, nw, w13, w2, nct):
    bsz, length, d = h.shape
    rows = bsz * length
    v = _normmod(h, nw, modtab, SH2, SC2, nct, BF16).reshape(rows, d)
    act = _mm_swiglu(v, w13, 512, 512, "ffn_w13")
    out = _mm_res(act, w2, h.reshape(rows, d), modtab, G2, length // ROW_TILE, nct, 512, "ffn_w2")
    return out.reshape(h.shape)


def _moe_gather_kernel(src_ref, v_hbm, o_ref, buf_ref, sem):
    def row_copy(r, tok):
        return pltpu.make_async_copy(v_hbm.at[pl.ds(tok, 1)], buf_ref.at[pl.ds(r, 1)], sem)

    def issue(r, carry):
        row_copy(r, src_ref[0, r]).start()
        return carry

    def drain(r, carry):
        row_copy(r, 0).wait()
        return carry

    lax.fori_loop(0, MOE_TILE, issue, 0)
    lax.fori_loop(0, MOE_TILE, drain, 0)
    o_ref[...] = buf_ref[...].astype(o_ref.dtype)


def _moe_gather(src, v):
    ntiles = src.shape[0]
    d = v.shape[1]
    return pl.pallas_call(
        _moe_gather_kernel,
        out_shape=jax.ShapeDtypeStruct((ntiles * MOE_TILE, d), BF16),
        grid=(ntiles,),
        in_specs=[pl.BlockSpec((None, 1, MOE_TILE), lambda t: (t, 0, 0), memory_space=pltpu.SMEM),
                  pl.BlockSpec(memory_space=pl.ANY)],
        out_specs=pl.BlockSpec((MOE_TILE, d), lambda t: (t, 0)),
        scratch_shapes=[pltpu.VMEM((MOE_TILE, d), F32), pltpu.SemaphoreType.DMA],
        compiler_params=_cp(("arbitrary",)),
        name="moe_gather",
    )(src, v)


def _moe_cast_weights(te_ref, pairs):
    t = pl.program_id(1)
    changed = jnp.logical_or(t == 0, te_ref[t] != te_ref[jnp.maximum(t - 1, 0)])

    @pl.when(changed)
    def _():
        for w_ref, wbf_ref in pairs:
            wbf_ref[...] = w_ref[...].astype(BF16)


def _moe_swiglu_kernel(te_ref, nv_ref, x_ref, wa_ref, wb_ref, o_ref, wabf_ref, wbbf_ref):
    _moe_cast_weights(te_ref, [(wa_ref, wabf_ref), (wb_ref, wbbf_ref)])

    in_use = pl.program_id(1) < nv_ref[0]

    @pl.when(in_use)
    def _():
        x = x_ref[...]
        a = _dot(x, wabf_ref[...])
        b = _dot(x, wbbf_ref[...])
        o_ref[...] = (_silu(a) * b).astype(o_ref.dtype)

    @pl.when(jnp.logical_not(in_use))
    def _():
        o_ref[...] = jnp.zeros_like(o_ref)


def _moe_mm_kernel(te_ref, nv_ref, x_ref, w_ref, o_ref, wbf_ref):
    _moe_cast_weights(te_ref, [(w_ref, wbf_ref)])

    in_use = pl.program_id(1) < nv_ref[0]

    @pl.when(in_use)
    def _():
        o_ref[...] = _dot(x_ref[...], wbf_ref[...])

    @pl.when(jnp.logical_not(in_use))
    def _():
        o_ref[...] = jnp.zeros_like(o_ref)


def _moe_swiglu(te, nv, xs, w13, tn):
    rows, k = xs.shape
    half = w13.shape[2] // 2
    nb = half // tn
    tm = MOE_TILE
    return pl.pallas_call(
        _moe_swiglu_kernel,
        out_shape=jax.ShapeDtypeStruct((rows, half), BF16),
        grid_spec=pltpu.PrefetchScalarGridSpec(
            num_scalar_prefetch=2,
            grid=(nb, rows // tm),
            in_specs=[pl.BlockSpec((tm, k), lambda j, t, te, nv: (t, 0)),
                      pl.BlockSpec((None, k, tn), lambda j, t, te, nv: (te[t], 0, j)),
                      pl.BlockSpec((None, k, tn), lambda j, t, te, nv: (te[t], 0, nb + j))],
            out_specs=pl.BlockSpec((tm, tn), lambda j, t, te, nv: (t, j)),
            scratch_shapes=[pltpu.VMEM((k, tn), BF16), pltpu.VMEM((k, tn), BF16)]),
        compiler_params=_cp(("arbitrary", "arbitrary"), VMEM_LIMIT),
        name="moe_w13",
    )(te, nv, xs, w13, w13)


def _moe_mm(te, nv, xs, w2, tn):
    rows, k = xs.shape
    n = w2.shape[2]
    tm = MOE_TILE
    return pl.pallas_call(
        _moe_mm_kernel,
        out_shape=jax.ShapeDtypeStruct((rows, n), F32),
        grid_spec=pltpu.PrefetchScalarGridSpec(
            num_scalar_prefetch=2,
            grid=(n // tn, rows // tm),
            in_specs=[pl.BlockSpec((tm, k), lambda j, t, te, nv: (t, 0)),
                      pl.BlockSpec((None, k, tn), lambda j, t, te, nv: (te[t], 0, j))],
            out_specs=pl.BlockSpec((tm, tn), lambda j, t, te, nv: (t, j)),
            scratch_shapes=[pltpu.VMEM((k, tn), BF16)]),
        compiler_params=_cp(("arbitrary", "arbitrary"), VMEM_LIMIT),
        name="moe_w2",
    )(te, nv, xs, w2)


def _moe_combine_kernel(pos_ref, gate_ref, y_hbm, res_ref, mod_ref, o_ref, buf_ref, sem, *, g_row):
    def row_copy(k, r, p):
        return pltpu.make_async_copy(y_hbm.at[pl.ds(p, 1)], buf_ref.at[k, pl.ds(r, 1)], sem)

    for k in range(TOP_K):
        def issue(r, carry, k=k):
            row_copy(k, r, pos_ref[k, r]).start()
            return carry
        lax.fori_loop(0, ROW_TILE, issue, 0)
    for k in range(TOP_K):
        def drain(r, carry, k=k):
            row_copy(k, r, 0).wait()
            return carry
        lax.fori_loop(0, ROW_TILE, drain, 0)
    p = gate_ref[...]
    mix = p[:, 0:1] * buf_ref[0] + p[:, 1:2] * buf_ref[1]
    o_ref[...] = res_ref[...] + mod_ref[g_row:g_row + 1, :] * mix


def _moe_combine(pos, gates, ys, res, modtab, g_row, tpb, nct):
    n, d = res.shape
    tm = ROW_TILE
    return pl.pallas_call(
        functools.partial(_moe_combine_kernel, g_row=g_row),
        out_shape=jax.ShapeDtypeStruct((n, d), F32),
        grid=(n // tm,),
        in_specs=[pl.BlockSpec((None, TOP_K, tm), lambda m: (m, 0, 0), memory_space=pltpu.SMEM),
                  pl.BlockSpec((tm, ROUTER_LANES), lambda m: (m, 0)),
                  pl.BlockSpec(memory_space=pl.ANY),
                  pl.BlockSpec((tm, d), lambda m: (m, 0)),
                  pl.BlockSpec((None, None, 6, d), lambda m: (m // tpb, jnp.where(m % tpb >= nct, 1, 0), 0, 0))],
        out_specs=pl.BlockSpec((tm, d), lambda m: (m, 0)),
        scratch_shapes=[pltpu.VMEM((TOP_K, tm, d), F32), pltpu.SemaphoreType.DMA],
        compiler_params=_cp(("arbitrary",), VMEM_LIMIT),
        name="moe_combine",
    )(pos, gates, ys, res, modtab)


def _moe_route(idx2):
    n = idx2.shape[0]
    flat_e = idx2.reshape(-1)
    onehot = (flat_e[:, None] == jnp.arange(N_EXPERTS, dtype=jnp.int32)[None, :]).astype(jnp.int32)
    csum = jnp.cumsum(onehot, axis=0)
    rank = jnp.sum((csum - onehot) * onehot, axis=1)
    counts = csum[-1]
    ntile = (counts + MOE_TILE - 1) // MOE_TILE
    tile_end = jnp.cumsum(ntile)
    tile_start = tile_end - ntile
    pos = tile_start[flat_e] * MOE_TILE + rank
    tiles = (TOP_K * n) // MOE_TILE + N_EXPERTS
    src = jnp.zeros((tiles * MOE_TILE,), jnp.int32).at[pos].set(jnp.arange(TOP_K * n, dtype=jnp.int32) // TOP_K)
    nv = tile_end[-1:]
    tids = jnp.minimum(jnp.arange(tiles, dtype=jnp.int32), nv[0] - 1)
    te = jnp.searchsorted(tile_end, tids, side='right').astype(jnp.int32)
    return pos.astype(jnp.int32), src.reshape(tiles, 1, MOE_TILE), te, nv.astype(jnp.int32)


def _moe_ffn(h, modtab, nw, w_router, w13, w2, nct):
    bsz, length, d = h.shape
    n = bsz * length
    v, idx, gates = _normmod_router(h, nw, modtab, w_router, SH2, SC2, nct)
    idx2 = idx.reshape(n, ROUTER_LANES)[:, :TOP_K]
    pos, src, te, nv = _moe_route(idx2)
    xs = _moe_gather(src, v.reshape(n, d))
    act = _moe_swiglu(te, nv, xs, w13, 512)
    ys = _moe_mm(te, nv, act, w2, 512)
    pos3 = pos.reshape(n // ROW_TILE, ROW_TILE, TOP_K).transpose(0, 2, 1)
    out = _moe_combine(pos3, gates.reshape(n, ROUTER_LANES), ys, h.reshape(n, d), modtab, G2,
                       length // ROW_TILE, nct)
    return out.reshape(h.shape)


def kernel(x, c, ctx, c_ctx, w_mod, b_mod, norm1, norm2, norm_f, gla_w_in, gla_w_gate_up, gla_b_gate,
           gla_g_head, gla_w_out, ffn_w13, ffn_w2, s5_a_re, s5_a_im, s5_log_dt, s5_b_re, s5_b_im, s5_c_re,
           s5_c_im, s5_d, s5_w_glu, s5_b_glu, moe_w_router, moe_w13, moe_w2):
    bsz, seq, d = x.shape
    nctx = ctx.shape[1]
    depth = w_mod.shape[0]
    assert bsz + 1 <= 8 and seq % ROW_TILE == 0 and nctx % ROW_TILE == 0 and seq % GRID_W == 0
    nct = nctx // ROW_TILE

    c8 = jnp.concatenate([c, c_ctx[None, :], jnp.zeros((8 - bsz - 1, d), F32)], axis=0)
    mods = _modulation(c8, w_mod, b_mod).reshape(depth, 8, 6, d)
    h = jnp.concatenate([ctx, x], axis=1)
    for i in range(depth):
        last = i == depth - 1
        j = i // 2
        modtab = jnp.stack([jnp.broadcast_to(mods[i, bsz], (bsz, 6, d)), mods[i, :bsz]], axis=1)
        if i % 2 == 0:
            h = _gla_layer(h, modtab, norm1[i], gla_w_in[j], gla_w_gate_up[j], gla_b_gate[j], gla_g_head[j],
                           gla_w_out[j], nctx)
        else:
            h = _s5_layer(h, modtab, norm1[i], s5_a_re[j], s5_a_im[j], s5_log_dt[j], s5_b_re[j], s5_b_im[j],
                          s5_c_re[j], s5_c_im[j], s5_d[j], s5_w_glu[j], s5_b_glu[j], nctx, last)
        ffn_nct = 0 if last else nct
        if i % 2 == 0:
            h = _dense_ffn(h, modtab, norm2[i], ffn_w13[j], ffn_w2[j], ffn_nct)
        else:
            h = _moe_ffn(h, modtab, norm2[i], moe_w_router[j], moe_w13[j], moe_w2[j], ffn_nct)
    if h.shape[1] != seq:
        h = h[:, nctx:]
    return _final_norm(h, norm_f)
```

```python
import functools

import jax
import jax.numpy as jnp
from jax import lax
from jax.experimental import pallas as pl
from jax.experimental.pallas import tpu as pltpu

F32 = jnp.float32
BF16 = jnp.bfloat16

EPS = 1e-6
GRID_W = 64

GLA_HEADS = 4
GLA_GATE_RANK = 16
GLA_GATE_TAU = 16.0
GLA_CHUNK = 64

S5_GROUP = 16
S5_CHUNK = 32
S5_GROUP_BLOCK = 4

N_EXPERTS = 8
TOP_K = 2
ROUTER_LANES = 128

ROW_TILE = 256
MOE_TILE = 256
VMEM_LIMIT = 56 * 1024 * 1024

SH1, SC1, G1, SH2, SC2, G2 = range(6)


def _cp(sem, vmem=None):
    return pltpu.CompilerParams(dimension_semantics=sem, vmem_limit_bytes=vmem)


def _dot(a, b):
    return jnp.dot(a, b, preferred_element_type=F32)


def _split(x):
    hi = x.astype(BF16)
    lo = (x - hi.astype(F32)).astype(BF16)
    return hi, lo


def _dot3(a, b):
    ah, al = _split(a)
    bh, bl = _split(b)
    return _dot(ah, bh) + _dot(ah, bl) + _dot(al, bh)


def _sigmoid(x):
    return 1.0 / (1.0 + jnp.exp(-x))


def _silu(x):
    return x * _sigmoid(x)


def _mod_kernel(c_ref, w_ref, b_ref, o_ref):
    x = _silu(c_ref[...]).astype(BF16)
    o_ref[...] = _dot(x, w_ref[...].astype(BF16)) + b_ref[...]


def _modulation(c8, w_mod, b_mod):
    depth, d, n = w_mod.shape
    tn = 1024
    return pl.pallas_call(
        _mod_kernel,
        out_shape=jax.ShapeDtypeStruct((depth, 8, n), F32),
        grid=(depth, n // tn),
        in_specs=[
            pl.BlockSpec((8, d), lambda i, j: (0, 0)),
            pl.BlockSpec((None, d, tn), lambda i, j: (i, 0, j)),
            pl.BlockSpec((None, 1, tn), lambda i, j: (i, 0, j)),
        ],
        out_specs=pl.BlockSpec((None, 8, tn), lambda i, j: (i, 0, j)),
        compiler_params=_cp(("arbitrary", "arbitrary"), VMEM_LIMIT),
        name="modulation",
    )(c8, w_mod, b_mod.reshape(depth, 1, n))


def _normmod_rows(x, nw, scale, shift):
    ms = jnp.mean(x * x, axis=-1, keepdims=True)
    return x * lax.rsqrt(ms + EPS) * nw * (1.0 + scale) + shift


def _normmod_body(x_ref, nw_ref, mod_ref, sh_row, sc_row):
    return _normmod_rows(x_ref[...], nw_ref[...], mod_ref[sc_row:sc_row + 1, :], mod_ref[sh_row:sh_row + 1, :])


def _normmod_kernel(x_ref, nw_ref, mod_ref, o_ref, *, sh_row, sc_row):
    o_ref[...] = _normmod_body(x_ref, nw_ref, mod_ref, sh_row, sc_row).astype(o_ref.dtype)


def _normmod_router_kernel(x_ref, nw_ref, mod_ref, wr_ref, o_ref, idx_ref, gate_ref, *, sh_row, sc_row):
    y = _normmod_body(x_ref, nw_ref, mod_ref, sh_row, sc_row)
    o_ref[...] = y
    logits = _dot3(y, wr_ref[...])
    lane = lax.broadcasted_iota(jnp.int32, logits.shape, 1)
    lane_f = lane.astype(F32)
    neg = jnp.float32(-jnp.inf)
    l1 = jnp.where(lane < N_EXPERTS, logits, neg)
    m1 = jnp.max(l1, axis=-1, keepdims=True)
    i1 = jnp.min(jnp.where(l1 == m1, lane_f, float(ROUTER_LANES)), axis=-1, keepdims=True)
    l2 = jnp.where(lane_f == i1, neg, l1)
    m2 = jnp.max(l2, axis=-1, keepdims=True)
    i2 = jnp.min(jnp.where(l2 == m2, lane_f, float(ROUTER_LANES)), axis=-1, keepdims=True)
    e = jnp.exp(m2 - m1)
    g1 = 1.0 / (1.0 + e)
    g2 = e * g1
    idx_ref[...] = jnp.where(lane == 0, i1, jnp.where(lane == 1, i2, 0.0)).astype(jnp.int32)
    gate_ref[...] = jnp.where(lane == 0, g1, jnp.where(lane == 1, g2, 0.0))


def _row_specs(d, nct):
    x_spec = pl.BlockSpec((None, ROW_TILE, d), lambda b, t: (b, t, 0))
    mod_spec = pl.BlockSpec((None, None, 6, d), lambda b, t: (b, jnp.where(t >= nct, 1, 0), 0, 0))
    return x_spec, mod_spec


def _normmod(h, nw, modtab, sh_row, sc_row, nct, out_dtype):
    bsz, length, d = h.shape
    x_spec, mod_spec = _row_specs(d, nct)
    return pl.pallas_call(
        functools.partial(_normmod_kernel, sh_row=sh_row, sc_row=sc_row),
        out_shape=jax.ShapeDtypeStruct(h.shape, out_dtype),
        grid=(bsz, length // ROW_TILE),
        in_specs=[x_spec, pl.BlockSpec((1, d), lambda b, t: (0, 0)), mod_spec],
        out_specs=x_spec,
        compiler_params=_cp(("arbitrary", "arbitrary")),
        name="normmod",
    )(h, nw.reshape(1, d), modtab)


def _normmod_router(h, nw, modtab, w_router, sh_row, sc_row, nct):
    bsz, length, d = h.shape
    x_spec, mod_spec = _row_specs(d, nct)
    wr = jnp.pad(w_router, ((0, 0), (0, ROUTER_LANES - w_router.shape[1])))
    lane_spec = pl.BlockSpec((None, ROW_TILE, ROUTER_LANES), lambda b, t: (b, t, 0))
    return pl.pallas_call(
        functools.partial(_normmod_router_kernel, sh_row=sh_row, sc_row=sc_row),
        out_shape=(
            jax.ShapeDtypeStruct(h.shape, F32),
            jax.ShapeDtypeStruct((bsz, length, ROUTER_LANES), jnp.int32),
            jax.ShapeDtypeStruct((bsz, length, ROUTER_LANES), F32),
        ),
        grid=(bsz, length // ROW_TILE),
        in_specs=[x_spec, pl.BlockSpec((1, d), lambda b, t: (0, 0)), mod_spec,
                  pl.BlockSpec((d, ROUTER_LANES), lambda b, t: (0, 0))],
        out_specs=(x_spec, lane_spec, lane_spec),
        compiler_params=_cp(("arbitrary", "arbitrary")),
        name="normmod_router",
    )(h, nw.reshape(1, d), modtab, wr)


def _rms_kernel(x_ref, nw_ref, o_ref):
    x = x_ref[...]
    ms = jnp.mean(x * x, axis=-1, keepdims=True)
    o_ref[...] = x * lax.rsqrt(ms + EPS) * nw_ref[...]


def _final_norm(h, nw):
    bsz, length, d = h.shape
    x_spec = pl.BlockSpec((None, ROW_TILE, d), lambda b, t: (b, t, 0))
    return pl.pallas_call(
        _rms_kernel,
        out_shape=jax.ShapeDtypeStruct(h.shape, F32),
        grid=(bsz, length // ROW_TILE),
        in_specs=[x_spec, pl.BlockSpec((1, d), lambda b, t: (0, 0))],
        out_specs=x_spec,
        compiler_params=_cp(("arbitrary", "arbitrary")),
        name="final_norm",
    )(h, nw.reshape(1, d))


def _cast_weights(pairs):
    @pl.when(pl.program_id(1) == 0)
    def _():
        for w_ref, wbf_ref in pairs:
            wbf_ref[...] = w_ref[...].astype(BF16)


def _mm_plain_kernel(x_ref, w_ref, o_ref, wbf_ref):
    _cast_weights([(w_ref, wbf_ref)])
    o_ref[...] = _dot(x_ref[...], wbf_ref[...]).astype(o_ref.dtype)


def _mm_swiglu_kernel(x_ref, wa_ref, wb_ref, o_ref, wabf_ref, wbbf_ref):
    _cast_weights([(wa_ref, wabf_ref), (wb_ref, wbbf_ref)])
    x = x_ref[...]
    a = _dot(x, wabf_ref[...])
    b = _dot(x, wbbf_ref[...])
    o_ref[...] = (_silu(a) * b).astype(o_ref.dtype)


def _mm_res_kernel(x_ref, w_ref, res_ref, mod_ref, o_ref, wbf_ref, *, g_row):
    _cast_weights([(w_ref, wbf_ref)])
    acc = _dot(x_ref[...], wbf_ref[...])
    o_ref[...] = res_ref[...] + mod_ref[g_row:g_row + 1, :] * acc


def _mm_glu_res_kernel(x_ref, wa_ref, wb_ref, ba_ref, bb_ref, res_ref, mod_ref, o_ref, wabf_ref, wbbf_ref,
                       *, g_row):
    _cast_weights([(wa_ref, wabf_ref), (wb_ref, wbbf_ref)])
    x = x_ref[...]
    z1 = _dot(x, wabf_ref[...]) + ba_ref[...]
    z2 = _dot(x, wbbf_ref[...]) + bb_ref[...]
    o_ref[...] = res_ref[...] + mod_ref[g_row:g_row + 1, :] * (z1 * _sigmoid(z2))


def _mm_plain(x, w, n, tm, tn, out_dtype, name):
    r, k = x.shape
    return pl.pallas_call(
        _mm_plain_kernel,
        out_shape=jax.ShapeDtypeStruct((r, n), out_dtype),
        grid=(n // tn, r // tm),
        in_specs=[pl.BlockSpec((tm, k), lambda j, m: (m, 0)),
                  pl.BlockSpec((k, tn), lambda j, m: (0, j))],
        out_specs=pl.BlockSpec((tm, tn), lambda j, m: (m, j)),
        scratch_shapes=[pltpu.VMEM((k, tn), BF16)],
        compiler_params=_cp(("arbitrary", "arbitrary"), VMEM_LIMIT),
        name=name,
    )(x, w)


def _mm_swiglu(x, w13, tm, tn, name):
    r, k = x.shape
    half = w13.shape[1] // 2
    nb = half // tn
    return pl.pallas_call(
        _mm_swiglu_kernel,
        out_shape=jax.ShapeDtypeStruct((r, half), BF16),
        grid=(nb, r // tm),
        in_specs=[pl.BlockSpec((tm, k), lambda j, m: (m, 0)),
                  pl.BlockSpec((k, tn), lambda j, m: (0, j)),
                  pl.BlockSpec((k, tn), lambda j, m: (0, nb + j))],
        out_specs=pl.BlockSpec((tm, tn), lambda j, m: (m, j)),
        scratch_shapes=[pltpu.VMEM((k, tn), BF16), pltpu.VMEM((k, tn), BF16)],
        compiler_params=_cp(("arbitrary", "arbitrary"), VMEM_LIMIT),
        name=name,
    )(x, w13, w13)


def _res_specs(tn, tpb, nct):
    res_spec = pl.BlockSpec((ROW_TILE, tn), lambda j, m: (m, j))
    mod_spec = pl.BlockSpec((None, None, 6, tn),
                            lambda j, m: (m // tpb, jnp.where(m % tpb >= nct, 1, 0), 0, j))
    return res_spec, mod_spec


def _mm_res(x, w, res, modtab, g_row, tpb, nct, tn, name):
    r, k = x.shape
    n = w.shape[1]
    tm = ROW_TILE
    res_spec, mod_spec = _res_specs(tn, tpb, nct)
    return pl.pallas_call(
        functools.partial(_mm_res_kernel, g_row=g_row),
        out_shape=jax.ShapeDtypeStruct((r, n), F32),
        grid=(n // tn, r // tm),
        in_specs=[pl.BlockSpec((tm, k), lambda j, m: (m, 0)),
                  pl.BlockSpec((k, tn), lambda j, m: (0, j)),
                  res_spec, mod_spec],
        out_specs=pl.BlockSpec((tm, tn), lambda j, m: (m, j)),
        scratch_shapes=[pltpu.VMEM((k, tn), BF16)],
        compiler_params=_cp(("arbitrary", "arbitrary"), VMEM_LIMIT),
        name=name,
    )(x, w, res, modtab)


def _mm_glu_res(x, w, bias, res, modtab, g_row, tpb, nct, tn, name):
    r, k = x.shape
    half = w.shape[1] // 2
    nb = half // tn
    tm = ROW_TILE
    res_spec, mod_spec = _res_specs(tn, tpb, nct)
    bias2 = bias.reshape(1, 2 * half)
    return pl.pallas_call(
        functools.partial(_mm_glu_res_kernel, g_row=g_row),
        out_shape=jax.ShapeDtypeStruct((r, half), F32),
        grid=(nb, r // tm),
        in_specs=[pl.BlockSpec((tm, k), lambda j, m: (m, 0)),
                  pl.BlockSpec((k, tn), lambda j, m: (0, j)),
                  pl.BlockSpec((k, tn), lambda j, m: (0, nb + j)),
                  pl.BlockSpec((1, tn), lambda j, m: (0, j)),
                  pl.BlockSpec((1, tn), lambda j, m: (0, nb + j)),
                  res_spec, mod_spec],
        out_specs=pl.BlockSpec((tm, tn), lambda j, m: (m, j)),
        scratch_shapes=[pltpu.VMEM((k, tn), BF16), pltpu.VMEM((k, tn), BF16)],
        compiler_params=_cp(("arbitrary", "arbitrary"), VMEM_LIMIT),
        name=name,
    )(x, w, w, bias2, bias2, res, modtab)


def _gla_gate_kernel(x_ref, wgl_ref, wup_ref, b_ref, o_ref):
    wh, wl = _split(wgl_ref[...])
    x = x_ref[...]
    gl = _dot(x, wh) + _dot(x, wl)
    z = _dot3(gl, wup_ref[...]) + b_ref[...]
    log_sig = jnp.minimum(z, 0.0) - jnp.log(1.0 + jnp.exp(-jnp.abs(z)))
    o_ref[...] = log_sig * (1.0 / GLA_GATE_TAU)


def _gla_gate(u, w_gl, w_up_bd, b_gate):
    r, d = u.shape
    n = w_up_bd.shape[1]
    tm = 512
    return pl.pallas_call(
        _gla_gate_kernel,
        out_shape=jax.ShapeDtypeStruct((r, n), F32),
        grid=(r // tm,),
        in_specs=[pl.BlockSpec((tm, d), lambda m: (m, 0)),
                  pl.BlockSpec(w_gl.shape, lambda m: (0, 0)),
                  pl.BlockSpec(w_up_bd.shape, lambda m: (0, 0)),
                  pl.BlockSpec((1, n), lambda m: (0, 0))],
        out_specs=pl.BlockSpec((tm, n), lambda m: (m, 0)),
        compiler_params=_cp(("arbitrary",)),
        name="gla_gate",
    )(u, w_gl, w_up_bd, b_gate.reshape(1, n))


def _gla_scan_kernel(tri_ref, q_ref, k_ref, v_ref, g_ref, o_ref, s_ref, *, hk, hv):
    @pl.when(pl.program_id(2) == 0)
    def _():
        s_ref[...] = jnp.zeros_like(s_ref)

    tri = tri_ref[...]
    ones = jnp.ones((GLA_CHUNK, 128), BF16)
    scale = hk ** -0.5
    tn_dims = (((0,), (0,)), ((), ()))
    nt_dims = (((1,), (1,)), ((), ()))
    for h in range(GLA_HEADS):
        q = q_ref[:, h * hk:(h + 1) * hk]
        k = k_ref[:, h * hk:(h + 1) * hk]
        v = v_ref[:, h * hv:(h + 1) * hv].astype(BF16)
        g = g_ref[:, h * hk:(h + 1) * hk]
        gh, gl = _split(g)
        b = _dot(tri, gh) + _dot(tri, gl)
        tot = jnp.sum(g, axis=0, keepdims=True)
        tot_col = (lax.dot_general(gh, ones, tn_dims, preferred_element_type=F32)
                   + lax.dot_general(gl, ones, tn_dims, preferred_element_type=F32))
        qd = (q * scale * jnp.exp(b)).astype(BF16)
        kd = (k * jnp.exp(-b)).astype(BF16)
        kt = (k * jnp.exp(tot - b)).astype(BF16)
        scores = jnp.where(tri > 0, lax.dot_general(qd, kd, nt_dims, preferred_element_type=F32), 0.0)
        s = s_ref[h]
        o = _dot(scores.astype(BF16), v) + _dot(qd, s.astype(BF16))
        o_ref[:, h * hv:(h + 1) * hv] = o
        decay = jnp.exp(tot_col)
        decay = jnp.concatenate([decay] * (hv // 128), axis=1)
        s_ref[h] = decay * s + lax.dot_general(kt, v, tn_dims, preferred_element_type=F32)


def _gla_scan(proj, log_a, nctx_chunks, dk, dv):
    bsz, length, _ = proj.shape
    nchunks = length // GLA_CHUNK
    hk, hv = dk // GLA_HEADS, dv // GLA_HEADS
    idx = jnp.arange(GLA_CHUNK)
    tri = jnp.stack([idx[:, None] >= idx[None, :], idx[:, None] <= idx[None, :]]).astype(BF16)

    def cidx(d, s):
        back = jnp.where(s < nctx_chunks, nctx_chunks - 1 - s, nchunks - 1 - (s - nctx_chunks))
        return jnp.where(d == 0, s, back)

    return pl.pallas_call(
        functools.partial(_gla_scan_kernel, hk=hk, hv=hv),
        out_shape=jax.ShapeDtypeStruct((2, bsz, length, dv), F32),
        grid=(bsz, 2, nchunks),
        in_specs=[
            pl.BlockSpec((None, GLA_CHUNK, GLA_CHUNK), lambda b, d, s: (d, 0, 0)),
            pl.BlockSpec((None, GLA_CHUNK, dk), lambda b, d, s: (b, cidx(d, s), 0)),
            pl.BlockSpec((None, GLA_CHUNK, dk), lambda b, d, s: (b, cidx(d, s), 1)),
            pl.BlockSpec((None, GLA_CHUNK, dv), lambda b, d, s: (b, cidx(d, s), (2 * dk) // dv)),
            pl.BlockSpec((None, GLA_CHUNK, dk), lambda b, d, s: (b, cidx(d, s), d)),
        ],
        out_specs=pl.BlockSpec((None, None, GLA_CHUNK, dv), lambda b, d, s: (d, b, cidx(d, s), 0)),
        scratch_shapes=[pltpu.VMEM((GLA_HEADS, hk, hv), F32)],
        compiler_params=_cp(("arbitrary", "arbitrary", "arbitrary"), VMEM_LIMIT),
        name="gla_scan",
    )(tri, proj, proj, proj, log_a)


def _gla_finish_kernel(of_ref, ob_ref, r_ref, gh_ref, o_ref, *, hv):
    for h in range(GLA_HEADS):
        sl = slice(h * hv, (h + 1) * hv)
        o = of_ref[:, sl] + ob_ref[:, sl]
        ms = jnp.mean(o * o, axis=-1, keepdims=True)
        y = o * lax.rsqrt(ms + EPS) * gh_ref[...]
        o_ref[:, sl] = (y * _silu(r_ref[:, sl])).astype(o_ref.dtype)


def _gla_finish(o2, proj, g_head, dk, dv):
    _, bsz, length, _ = o2.shape
    hv = dv // GLA_HEADS
    tm = ROW_TILE
    return pl.pallas_call(
        functools.partial(_gla_finish_kernel, hv=hv),
        out_shape=jax.ShapeDtypeStruct((bsz, length, dv), BF16),
        grid=(bsz, length // tm),
        in_specs=[
            pl.BlockSpec((None, None, tm, dv), lambda b, t: (0, b, t, 0)),
            pl.BlockSpec((None, None, tm, dv), lambda b, t: (1, b, t, 0)),
            pl.BlockSpec((None, tm, dv), lambda b, t: (b, t, (2 * dk + dv) // dv)),
            pl.BlockSpec((1, hv), lambda b, t: (0, 0)),
        ],
        out_specs=pl.BlockSpec((None, tm, dv), lambda b, t: (b, t, 0)),
        compiler_params=_cp(("arbitrary", "arbitrary")),
        name="gla_finish",
    )(o2, o2, proj, g_head.reshape(1, hv))


def _gla_layer(h, modtab, nw, w_in, w_gate_up, b_gate, g_head, w_out, nctx):
    bsz, length, d = h.shape
    dk = w_gate_up.shape[2]
    dv = w_out.shape[0]
    nmain = 2 * dk + 2 * dv
    tpb = length // ROW_TILE
    nct = nctx // ROW_TILE
    u = _normmod(h, nw, modtab, SH1, SC1, nct, BF16).reshape(bsz * length, d)
    proj = _mm_plain(u, w_in, nmain, 512, 512, F32, "gla_in_proj")
    rank = GLA_GATE_RANK
    w_up_bd = jnp.zeros((2 * rank, 2 * dk), F32)
    w_up_bd = w_up_bd.at[:rank, :dk].set(w_gate_up[0]).at[rank:, dk:].set(w_gate_up[1])
    log_a = _gla_gate(u, w_in[:, nmain:], w_up_bd, b_gate.reshape(2 * dk))
    proj3 = proj.reshape(bsz, length, nmain)
    o2 = _gla_scan(proj3, log_a.reshape(bsz, length, 2 * dk), nctx // GLA_CHUNK, dk, dv)
    gated = _gla_finish(o2, proj3, g_head, dk, dv).reshape(bsz * length, dv)
    out = _mm_res(gated, w_out, h.reshape(bsz * length, d), modtab, G1, tpb, nct, 512, "gla_out_proj")
    return out.reshape(bsz, length, d)


S5_LANES_PER_BATCH = 96
_AN1, _AN2, _AP1, _AP2, _BB1, _BB2, _CC1, _CC2, _PQ, _PK_ROWS = 0, 32, 64, 96, 128, 144, 160, 176, 192, 200


def _s5_params(a_re, a_im, log_dt, b_re, b_im, c_re, c_im):
    t_len = S5_CHUNK
    dt = jnp.exp(log_dt)[..., None]
    mag = jnp.exp(a_re * dt)
    ab_re = mag * jnp.cos(a_im * dt)
    ab_im = mag * jnp.sin(a_im * dt)
    den = a_re * a_re + a_im * a_im
    f_re = ((ab_re - 1.0) * a_re + ab_im * a_im) / den
    f_im = (ab_im * a_re - (ab_re - 1.0) * a_im) / den
    bb_re = f_re[..., None] * b_re - f_im[..., None] * b_im
    bb_im = f_re[..., None] * b_im + f_im[..., None] * b_re

    def cpow(n):
        nn = n.astype(F32)[:, None, :, None]
        m = jnp.exp((a_re * dt)[:, :, None, :] * nn)
        th = (a_im * dt)[:, :, None, :] * nn
        return m * jnp.cos(th), m * jnp.sin(th)

    steps = jnp.arange(t_len)
    order = jnp.stack([steps, t_len - 1 - steps])
    anr, ani = cpow(-order)
    apr, api = cpow(order)
    atr, ati = cpow(jnp.full((2, 1), t_len))
    cat = lambda u, v: jnp.concatenate([u, v], axis=-1)
    bbr_t, bbi_t = bb_re.transpose(0, 1, 3, 2), bb_im.transpose(0, 1, 3, 2)
    zero = jnp.zeros((2, a_re.shape[1], 5, 2 * a_re.shape[2]), F32)
    return jnp.concatenate([
        cat(anr, anr), cat(-ani, ani),
        cat(apr, api), cat(api, apr),
        cat(bbr_t, bbi_t), cat(bbi_t, bbr_t),
        cat(c_re, -c_re), cat(-c_im, -c_im),
        cat(atr, atr), cat(-ati, ati), cat(ati, -ati), zero], axis=2)


def _s5_in_kernel(hl_ref, hc_ref, nw_ref, mod_ref, o_ref, *, bsz, qb):
    nw = nw_ref[...]
    blocks = []
    for b in range(bsz):
        lat = _normmod_rows(hl_ref[b], nw, mod_ref[b, 1, SC1:SC1 + 1, :], mod_ref[b, 1, SH1:SH1 + 1, :])
        ctx = _normmod_rows(hc_ref[b], nw, mod_ref[b, 0, SC1:SC1 + 1, :], mod_ref[b, 0, SH1:SH1 + 1, :])
        pad = jnp.zeros((qb - lat.shape[0] - ctx.shape[0], lat.shape[1]), F32)
        blocks += [lat, ctx, pad]
    o_ref[...] = jnp.concatenate(blocks, axis=0).T.astype(o_ref.dtype)


def _s5_in(h, nw, modtab, nctx):
    bsz, length, d = h.shape
    t_len = S5_CHUNK
    qb = S5_LANES_PER_BATCH
    nctx_chunks = nctx // t_len
    return pl.pallas_call(
        functools.partial(_s5_in_kernel, bsz=bsz, qb=qb),
        out_shape=jax.ShapeDtypeStruct((d, t_len * bsz * qb), BF16),
        grid=(t_len,),
        in_specs=[
            pl.BlockSpec((bsz, None, GRID_W, d), lambda s: (0, nctx // GRID_W + s, 0, 0)),
            pl.BlockSpec((bsz, nctx_chunks, d), lambda s: (0, 0, s)),
            pl.BlockSpec((1, d), lambda s: (0, 0)),
            pl.BlockSpec(modtab.shape, lambda s: (0, 0, 0, 0)),
        ],
        out_specs=pl.BlockSpec((d, bsz * qb), lambda s: (0, s)),
        compiler_params=_cp(("arbitrary",), VMEM_LIMIT),
        name="s5_in",
    )(h.reshape(bsz, length // GRID_W, GRID_W, d), h.reshape(bsz, length // t_len, t_len * d),
      nw.reshape(1, d), modtab)


def _tile_rows(x, reps):
    return jnp.concatenate([x] * reps, axis=0)


def _repeat_rows(x, reps):
    return jnp.concatenate([jnp.broadcast_to(x[i:i + 1, :], (reps, x.shape[1])) for i in range(x.shape[0])],
                           axis=0)


def _s5_kernel(ut_ref, pk_ref, y_ref, sv_ref, ss_ref, w_ref, r2_ref, acc_ref, *, bsz, qb, orders):
    gb = ut_ref.shape[0]
    t_len = S5_CHUNK
    k = S5_GROUP * t_len
    q = ut_ref.shape[3]
    row_s = lax.broadcasted_iota(jnp.int32, (k, k), 0) % t_len
    col_t = lax.broadcasted_iota(jnp.int32, (k, k), 1) % t_len
    masks = (row_s <= col_t, row_s >= col_t)
    w_ref[...] = jnp.zeros_like(w_ref)
    for g in range(gb):
        u = ut_ref[g].reshape(k, q).T
        for d in range(2):
            i = 2 * g + d
            pk = pk_ref[d, g]
            l2 = (_tile_rows(pk[_AN1:_AN1 + t_len], S5_GROUP) * _repeat_rows(pk[_BB1:_BB1 + S5_GROUP], t_len)
                  + _tile_rows(pk[_AN2:_AN2 + t_len], S5_GROUP) * _repeat_rows(pk[_BB2:_BB2 + S5_GROUP], t_len))
            r2t = (_tile_rows(pk[_AP1:_AP1 + t_len], S5_GROUP) * _repeat_rows(pk[_CC1:_CC1 + S5_GROUP], t_len)
                   + _tile_rows(pk[_AP2:_AP2 + t_len], S5_GROUP) * _repeat_rows(pk[_CC2:_CC2 + S5_GROUP], t_len))
            r2 = r2t.T
            r2_ref[i] = r2.astype(BF16)
            toep = jnp.where(masks[d], _dot3(l2, r2), 0.0).astype(BF16)
            y_local = _dot(u, toep)
            if d == 0:
                acc_ref[g] = y_local
            else:
                acc_ref[g] = acc_ref[g] + y_local
            v = _dot(u, l2.astype(BF16))
            vs = pltpu.roll(v, v.shape[1] // 2, axis=1)
            pt, qt, qts = pk[_PQ:_PQ + 1], pk[_PQ + 1:_PQ + 2], pk[_PQ + 2:_PQ + 3]
            sv_ref[i] = pt * v + qt * vs
            ss_ref[i] = pt * vs + qts * v
    coef = []
    for g in range(gb):
        for d in range(2):
            pk = pk_ref[d, g]
            coef.append(tuple(jnp.broadcast_to(pk[_PQ + r:_PQ + r + 1], (bsz, pk.shape[1])) for r in range(3)))
    zero = jnp.zeros((bsz, pk_ref.shape[3]), F32)
    state = [(zero, zero)] * (2 * gb)
    nsteps = len(orders[0])
    for step in range(nsteps):
        for i in range(2 * gb):
            rows = pl.ds(orders[i % 2][step], bsz, stride=qb)
            w, ws = state[i]
            w_ref[i, rows, :] = w
            if step + 1 < nsteps:
                pt, qt, qts = coef[i]
                state[i] = (pt * w + qt * ws + sv_ref[i, rows, :], pt * ws + qts * w + ss_ref[i, rows, :])
    for g in range(gb):
        y = acc_ref[g]
        for d in range(2):
            i = 2 * g + d
            y = y + _dot(w_ref[i].astype(BF16), r2_ref[i])
        y_ref[g] = y.T.reshape(S5_GROUP, t_len, q)


def _s5_scan(ut, pk, bsz, nctx_chunks, nlat_chunks):
    ngroups, _, t_len, q = ut.shape
    np2 = pk.shape[3]
    gb = S5_GROUP_BLOCK
    qb = S5_LANES_PER_BATCH
    lat = list(range(nlat_chunks))
    ctx = list(range(nlat_chunks, nlat_chunks + nctx_chunks))
    orders = (ctx + lat, ctx[::-1] + lat[::-1])
    blk = (gb, S5_GROUP, t_len, q)
    return pl.pallas_call(
        functools.partial(_s5_kernel, bsz=bsz, qb=qb, orders=orders),
        out_shape=jax.ShapeDtypeStruct(ut.shape, F32),
        grid=(ngroups // gb,),
        in_specs=[pl.BlockSpec(blk, lambda g: (g, 0, 0, 0)),
                  pl.BlockSpec((2, gb, _PK_ROWS, np2), lambda g: (0, g, 0, 0))],
        out_specs=pl.BlockSpec(blk, lambda g: (g, 0, 0, 0)),
        scratch_shapes=[pltpu.VMEM((2 * gb, q, np2), F32), pltpu.VMEM((2 * gb, q, np2), F32),
                        pltpu.VMEM((2 * gb, q, np2), F32), pltpu.VMEM((2 * gb, np2, S5_GROUP * t_len), BF16),
                        pltpu.VMEM((gb, q, S5_GROUP * t_len), F32)],
        compiler_params=_cp(("arbitrary",), VMEM_LIMIT),
        name="s5_scan",
    )(ut, pk)


def _s5_out_kernel(yt_ref, hl_ref, hc_ref, nw_ref, mod_ref, d_ref, ol_ref, oc_ref, *, bsz, qb):
    y = yt_ref[...].T
    nw = nw_ref[...]
    nlat, nc = hl_ref.shape[1], hc_ref.shape[1]
    c0 = 0.7978845608028654

    def act(yv, hv, region, b):
        u = _normmod_rows(hv, nw, mod_ref[b, region, SC1:SC1 + 1, :], mod_ref[b, region, SH1:SH1 + 1, :])
        z = yv + u * d_ref[...]
        return (0.5 * z * (1.0 + jnp.tanh(c0 * (z + 0.044715 * (z * z * z))))).astype(ol_ref.dtype)

    for b in range(bsz):
        ol_ref[b] = act(y[b * qb:b * qb + nlat], hl_ref[b], 1, b)
        oc_ref[b] = act(y[b * qb + nlat:b * qb + nlat + nc], hc_ref[b], 0, b)


def _s5_out(yt, h, nw, modtab, d_skip, nctx):
    bsz, length, d = h.shape
    t_len = S5_CHUNK
    qb = S5_LANES_PER_BATCH
    nctx_chunks = nctx // t_len
    seq = length - nctx
    lat_spec = pl.BlockSpec((bsz, None, GRID_W, d), lambda s: (0, nctx // GRID_W + s, 0, 0))
    ctx_spec = pl.BlockSpec((bsz, nctx_chunks, d), lambda s: (0, 0, s))
    act_l, act_c = pl.pallas_call(
        functools.partial(_s5_out_kernel, bsz=bsz, qb=qb),
        out_shape=(jax.ShapeDtypeStruct((bsz, seq // GRID_W, GRID_W, d), BF16),
                   jax.ShapeDtypeStruct((bsz, nctx_chunks, t_len * d), BF16)),
        grid=(t_len,),
        in_specs=[pl.BlockSpec((d, bsz * qb), lambda s: (0, s)),
                  lat_spec, ctx_spec,
                  pl.BlockSpec((1, d), lambda s: (0, 0)),
                  pl.BlockSpec(modtab.shape, lambda s: (0, 0, 0, 0)),
                  pl.BlockSpec((1, d), lambda s: (0, 0))],
        out_specs=(pl.BlockSpec((bsz, None, GRID_W, d), lambda s: (0, s, 0, 0)), ctx_spec),
        compiler_params=_cp(("arbitrary",), VMEM_LIMIT),
        name="s5_out",
    )(yt, h.reshape(bsz, length // GRID_W, GRID_W, d), h.reshape(bsz, length // t_len, t_len * d),
      nw.reshape(1, d), modtab, d_skip.reshape(1, d))
    return act_l.reshape(bsz, seq, d), act_c.reshape(bsz, nctx, d)


def _s5_layer(h, modtab, nw, a_re, a_im, log_dt, b_re, b_im, c_re, c_im, d_skip, w_glu, b_glu, nctx, last):
    bsz, length, d = h.shape
    seq = length - nctx
    t_len = S5_CHUNK
    assert seq // GRID_W == t_len and nctx % t_len == 0 and nctx % GRID_W == 0
    assert seq // t_len + nctx // t_len <= S5_LANES_PER_BATCH
    ngroups = d // S5_GROUP
    ut = _s5_in(h, nw, modtab, nctx)
    pk = _s5_params(a_re, a_im, log_dt, b_re, b_im, c_re, c_im)
    lanes = ut.shape[1] // t_len
    yt = _s5_scan(ut.reshape(ngroups, S5_GROUP, t_len, lanes), pk, bsz, nctx // t_len, GRID_W)
    act_l, act_c = _s5_out(yt.reshape(d, t_len * lanes), h, nw, modtab, d_skip, nctx)
    if last:
        act, h, nct = act_l, h[:, nctx:], 0
    else:
        act, nct = jnp.concatenate([act_c, act_l], axis=1), nctx // ROW_TILE
    rows = h.shape[0] * h.shape[1]
    out = _mm_glu_res(act.reshape(rows, d), w_glu, b_glu, h.reshape(rows, d), modtab, G1,
                      h.shape[1] // ROW_TILE, nct, 512, "s5_glu")
    return out.reshape(h.shape)


def _dense_ffn(h, modtab, nw, w13, w2, nct):
    bsz, length, d = h.shape
    rows = bsz * length
    v = _normmod(h, nw, modtab, SH2, SC2, nct, BF16).reshape(rows, d)
    act = _mm_swiglu(v, w13, 512, 512, "ffn_w13")
    out = _mm_res(act, w2, h.reshape(rows, d), modtab, G2, length // ROW_TILE, nct, 512, "ffn_w2")
    return out.reshape(h.shape)


def _moe_gather_kernel(src_ref, v_hbm, o_ref, buf_ref, sem):
    def row_copy(r, tok):
        return pltpu.make_async_copy(v_hbm.at[pl.ds(tok, 1)], buf_ref.at[pl.ds(r, 1)], sem)

    def issue(r, carry):
        row_copy(r, src_ref[0, r]).start()
        return carry

    def drain(r, carry):
        row_copy(r, 0).wait()
        return carry

    lax.fori_loop(0, MOE_TILE, issue, 0)
    lax.fori_loop(0, MOE_TILE, drain, 0)
    o_ref[...] = buf_ref[...].astype(o_ref.dtype)


def _moe_gather(src, v):
    ntiles = src.shape[0]
    d = v.shape[1]
    return pl.pallas_call(
        _moe_gather_kernel,
        out_shape=jax.ShapeDtypeStruct((ntiles * MOE_TILE, d), BF16),
        grid=(ntiles,),
        in_specs=[pl.BlockSpec((None, 1, MOE_TILE), lambda t: (t, 0, 0), memory_space=pltpu.SMEM),
                  pl.BlockSpec(memory_space=pl.ANY)],
        out_specs=pl.BlockSpec((MOE_TILE, d), lambda t: (t, 0)),
        scratch_shapes=[pltpu.VMEM((MOE_TILE, d), F32), pltpu.SemaphoreType.DMA],
        compiler_params=_cp(("arbitrary",)),
        name="moe_gather",
    )(src, v)


def _moe_cast_weights(te_ref, pairs):
    t = pl.program_id(1)
    changed = jnp.logical_or(t == 0, te_ref[t] != te_ref[jnp.maximum(t - 1, 0)])

    @pl.when(changed)
    def _():
        for w_ref, wbf_ref in pairs:
            wbf_ref[...] = w_ref[...].astype(BF16)


def _moe_swiglu_kernel(te_ref, nv_ref, x_ref, wa_ref, wb_ref, o_ref, wabf_ref, wbbf_ref):
    _moe_cast_weights(te_ref, [(wa_ref, wabf_ref), (wb_ref, wbbf_ref)])

    in_use = pl.program_id(1) < nv_ref[0]

    @pl.when(in_use)
    def _():
        x = x_ref[...]
        a = _dot(x, wabf_ref[...])
        b = _dot(x, wbbf_ref[...])
        o_ref[...] = (_silu(a) * b).astype(o_ref.dtype)

    @pl.when(jnp.logical_not(in_use))
    def _():
        o_ref[...] = jnp.zeros_like(o_ref)


def _moe_mm_kernel(te_ref, nv_ref, x_ref, w_ref, o_ref, wbf_ref):
    _moe_cast_weights(te_ref, [(w_ref, wbf_ref)])

    in_use = pl.program_id(1) < nv_ref[0]

    @pl.when(in_use)
    def _():
        o_ref[...] = _dot(x_ref[...], wbf_ref[...])

    @pl.when(jnp.logical_not(in_use))
    def _():
        o_ref[...] = jnp.zeros_like(o_ref)


def _moe_swiglu(te, nv, xs, w13, tn):
    rows, k = xs.shape
    half = w13.shape[2] // 2
    nb = half // tn
    tm = MOE_TILE
    return pl.pallas_call(
        _moe_swiglu_kernel,
        out_shape=jax.ShapeDtypeStruct((rows, half), BF16),
        grid_spec=pltpu.PrefetchScalarGridSpec(
            num_scalar_prefetch=2,
            grid=(nb, rows // tm),
            in_specs=[pl.BlockSpec((tm, k), lambda j, t, te, nv: (t, 0)),
                      pl.BlockSpec((None, k, tn), lambda j, t, te, nv: (te[t], 0, j)),
                      pl.BlockSpec((None, k, tn), lambda j, t, te, nv: (te[t], 0, nb + j))],
            out_specs=pl.BlockSpec((tm, tn), lambda j, t, te, nv: (t, j)),
            scratch_shapes=[pltpu.VMEM((k, tn), BF16), pltpu.VMEM((k, tn), BF16)]),
        compiler_params=_cp(("arbitrary", "arbitrary"), VMEM_LIMIT),
        name="moe_w13",
    )(te, nv, xs, w13, w13)


def _moe_mm(te, nv, xs, w2, tn):
    rows, k = xs.shape
    n = w2.shape[2]
    tm = MOE_TILE
    return pl.pallas_call(
        _moe_mm_kernel,
        out_shape=jax.ShapeDtypeStruct((rows, n), F32),
        grid_spec=pltpu.PrefetchScalarGridSpec(
            num_scalar_prefetch=2,
            grid=(n // tn, rows // tm),
            in_specs=[pl.BlockSpec((tm, k), lambda j, t, te, nv: (t, 0)),
                      pl.BlockSpec((None, k, tn), lambda j, t, te, nv: (te[t], 0, j))],
            out_specs=pl.BlockSpec((tm, tn), lambda j, t, te, nv: (t, j)),
            scratch_shapes=[pltpu.VMEM((k, tn), BF16)]),
        compiler_params=_cp(("arbitrary", "arbitrary"), VMEM_LIMIT),
        name="moe_w2",
    )(te, nv, xs, w2)


def _moe_combine_kernel(pos_ref, gate_ref, y_hbm, res_ref, mod_ref, o_ref, buf_ref, sem, *, g_row):
    def row_copy(k, r, p):
        return pltpu.make_async_copy(y_hbm.at[pl.ds(p, 1)], buf_ref.at[k, pl.ds(r, 1)], sem)

    for k in range(TOP_K):
        def issue(r, carry, k=k):
            row_copy(k, r, pos_ref[k, r]).start()
            return carry
        lax.fori_loop(0, ROW_TILE, issue, 0)
    for k in range(TOP_K):
        def drain(r, carry, k=k):
            row_copy(k, r, 0).wait()
            return carry
        lax.fori_loop(0, ROW_TILE, drain, 0)
    p = gate_ref[...]
    mix = p[:, 0:1] * buf_ref[0] + p[:, 1:2] * buf_ref[1]
    o_ref[...] = res_ref[...] + mod_ref[g_row:g_row + 1, :] * mix


def _moe_combine(pos, gates, ys, res, modtab, g_row, tpb, nct):
    n, d = res.shape
    tm = ROW_TILE
    return pl.pallas_call(
        functools.partial(_moe_combine_kernel, g_row=g_row),
        out_shape=jax.ShapeDtypeStruct((n, d), F32),
        grid=(n // tm,),
        in_specs=[pl.BlockSpec((None, TOP_K, tm), lambda m: (m, 0, 0), memory_space=pltpu.SMEM),
                  pl.BlockSpec((tm, ROUTER_LANES), lambda m: (m, 0)),
                  pl.BlockSpec(memory_space=pl.ANY),
                  pl.BlockSpec((tm, d), lambda m: (m, 0)),
                  pl.BlockSpec((None, None, 6, d), lambda m: (m // tpb, jnp.where(m % tpb >= nct, 1, 0), 0, 0))],
        out_specs=pl.BlockSpec((tm, d), lambda m: (m, 0)),
        scratch_shapes=[pltpu.VMEM((TOP_K, tm, d), F32), pltpu.SemaphoreType.DMA],
        compiler_params=_cp(("arbitrary",), VMEM_LIMIT),
        name="moe_combine",
    )(pos, gates, ys, res, modtab)


def _moe_route(idx2):
    n = idx2.shape[0]
    flat_e = idx2.reshape(-1)
    onehot = (flat_e[:, None] == jnp.arange(N_EXPERTS, dtype=jnp.int32)[None, :]).astype(jnp.int32)
    csum = jnp.cumsum(onehot, axis=0)
    rank = jnp.sum((csum - onehot) * onehot, axis=1)
    counts = csum[-1]
    ntile = (counts + MOE_TILE - 1) // MOE_TILE
    tile_end = jnp.cumsum(ntile)
    tile_start = tile_end - ntile
    pos = tile_start[flat_e] * MOE_TILE + rank
    tiles = (TOP_K * n) // MOE_TILE + N_EXPERTS
    src = jnp.zeros((tiles * MOE_TILE,), jnp.int32).at[pos].set(jnp.arange(TOP_K * n, dtype=jnp.int32) // TOP_K)
    nv = tile_end[-1:]
    tids = jnp.minimum(jnp.arange(tiles, dtype=jnp.int32), nv[0] - 1)
    te = jnp.sum((tile_end[None, :] <= tids[:, None]).astype(jnp.int32), axis=1)
    return pos.astype(jnp.int32), src.reshape(tiles, 1, MOE_TILE), te, nv.astype(jnp.int32)


def _moe_ffn(h, modtab, nw, w_router, w13, w2, nct):
    bsz, length, d = h.shape
    n = bsz * length
    v, idx, gates = _normmod_router(h, nw, modtab, w_router, SH2, SC2, nct)
    idx2 = idx.reshape(n, ROUTER_LANES)[:, :TOP_K]
    pos, src, te, nv = _moe_route(idx2)
    xs = _moe_gather(src, v.reshape(n, d))
    act = _moe_swiglu(te, nv, xs, w13, 512)
    ys = _moe_mm(te, nv, act, w2, 512)
    pos3 = pos.reshape(n // ROW_TILE, ROW_TILE, TOP_K).transpose(0, 2, 1)
    out = _moe_combine(pos3, gates.reshape(n, ROUTER_LANES), ys, h.reshape(n, d), modtab, G2,
                       length // ROW_TILE, nct)
    return out.reshape(h.shape)


def kernel(x, c, ctx, c_ctx, w_mod, b_mod, norm1, norm2, norm_f, gla_w_in, gla_w_gate_up, gla_b_gate,
           gla_g_head, gla_w_out, ffn_w13, ffn_w2, s5_a_re, s5_a_im, s5_log_dt, s5_b_re, s5_b_im, s5_c_re,
           s5_c_im, s5_d, s5_w_glu, s5_b_glu, moe_w_router, moe_w13, moe_w2):
    bsz, seq, d = x.shape
    nctx = ctx.shape[1]
    depth = w_mod.shape[0]
    assert bsz + 1 <= 8 and seq % ROW_TILE == 0 and nctx % ROW_TILE == 0 and seq % GRID_W == 0
    nct = nctx // ROW_TILE

    c8 = jnp.concatenate([c, c_ctx[None, :], jnp.zeros((8 - bsz - 1, d), F32)], axis=0)
    mods = _modulation(c8, w_mod, b_mod).reshape(depth, 8, 6, d)
    h = jnp.concatenate([ctx, x], axis=1)
    for i in range(depth):
        last = i == depth - 1
        j = i // 2
        modtab = jnp.stack([jnp.broadcast_to(mods[i, bsz], (bsz, 6, d)), mods[i, :bsz]], axis=1)
        if i % 2 == 0:
            h = _gla_layer(h, modtab, norm1[i], gla_w_in[j], gla_w_gate_up[j], gla_b_gate[j], gla_g_head[j],
                           gla_w_out[j], nctx)
        else:
            h = _s5_layer(h, modtab, norm1[i], s5_a_re[j], s5_a_im[j], s5_log_dt[j], s5_b_re[j], s5_b_im[j],
                          s5_c_re[j], s5_c_im[j], s5_d[j], s5_w_glu[j], s5_b_glu[j], nctx, last)
        ffn_nct = 0 if last else nct
        if i % 2 == 0:
            h = _dense_ffn(h, modtab, norm2[i], ffn_w13[j], ffn_w2[j], ffn_nct)
        else:
            h = _moe_ffn(h, modtab, norm2[i], moe_w_router[j], moe_w13[j], moe_w2[j], ffn_nct)
    if h.shape[1] != seq:
        h = h[:, nctx:]
    return _final_norm(h, norm_f)
```

```python
import functools

import jax
import jax.numpy as jnp
from jax import lax
from jax.experimental import pallas as pl
from jax.experimental.pallas import tpu as pltpu

F32 = jnp.float32
BF16 = jnp.bfloat16

EPS = 1e-6
GRID_W = 64

GLA_HEADS = 4
GLA_GATE_RANK = 16
GLA_GATE_TAU = 16.0
GLA_CHUNK = 64

S5_GROUP = 16
S5_CHUNK = 32
S5_GROUP_BLOCK = 4

N_EXPERTS = 8
TOP_K = 2
ROUTER_LANES = 128

ROW_TILE = 256
MOE_TILE = 256
VMEM_LIMIT = 56 * 1024 * 1024

SH1, SC1, G1, SH2, SC2, G2 = range(6)


def _cp(sem, vmem=None):
    return pltpu.CompilerParams(dimension_semantics=sem, vmem_limit_bytes=vmem)


def _dot(a, b):
    return jnp.dot(a, b, preferred_element_type=F32)


def _split(x):
    hi = x.astype(BF16)
    lo = (x - hi.astype(F32)).astype(BF16)
    return hi, lo


def _dot3(a, b):
    ah, al = _split(a)
    bh, bl = _split(b)
    return _dot(ah, bh) + _dot(ah, bl) + _dot(al, bh)


def _sigmoid(x):
    return 1.0 / (1.0 + jnp.exp(-x))


def _silu(x):
    return x * _sigmoid(x)


def _mod_kernel(c_ref, w_ref, b_ref, o_ref):
    x = _silu(c_ref[...]).astype(BF16)
    o_ref[...] = _dot(x, w_ref[...].astype(BF16)) + b_ref[...]


def _modulation(c8, w_mod, b_mod):
    depth, d, n = w_mod.shape
    tn = 1024
    return pl.pallas_call(
        _mod_kernel,
        out_shape=jax.ShapeDtypeStruct((depth, 8, n), F32),
        grid=(depth, n // tn),
        in_specs=[
            pl.BlockSpec((8, d), lambda i, j: (0, 0)),
            pl.BlockSpec((None, d, tn), lambda i, j: (i, 0, j)),
            pl.BlockSpec((None, 1, tn), lambda i, j: (i, 0, j)),
        ],
        out_specs=pl.BlockSpec((None, 8, tn), lambda i, j: (i, 0, j)),
        compiler_params=_cp(("arbitrary", "arbitrary"), VMEM_LIMIT),
        name="modulation",
    )(c8, w_mod, b_mod.reshape(depth, 1, n))


def _normmod_rows(x, nw, scale, shift):
    ms = jnp.mean(x * x, axis=-1, keepdims=True)
    return x * lax.rsqrt(ms + EPS) * nw * (1.0 + scale) + shift


def _normmod_body(x_ref, nw_ref, mod_ref, sh_row, sc_row):
    return _normmod_rows(x_ref[...], nw_ref[...], mod_ref[sc_row:sc_row + 1, :], mod_ref[sh_row:sh_row + 1, :])


def _normmod_kernel(x_ref, nw_ref, mod_ref, o_ref, *, sh_row, sc_row):
    o_ref[...] = _normmod_body(x_ref, nw_ref, mod_ref, sh_row, sc_row).astype(o_ref.dtype)


def _normmod_router_kernel(x_ref, nw_ref, mod_ref, wr_ref, o_ref, idx_ref, gate_ref, *, sh_row, sc_row):
    y = _normmod_body(x_ref, nw_ref, mod_ref, sh_row, sc_row)
    o_ref[...] = y
    logits = _dot3(y, wr_ref[...])
    lane = lax.broadcasted_iota(jnp.int32, logits.shape, 1)
    lane_f = lane.astype(F32)
    neg = jnp.float32(-jnp.inf)
    l1 = jnp.where(lane < N_EXPERTS, logits, neg)
    m1 = jnp.max(l1, axis=-1, keepdims=True)
    i1 = jnp.min(jnp.where(l1 == m1, lane_f, float(ROUTER_LANES)), axis=-1, keepdims=True)
    l2 = jnp.where(lane_f == i1, neg, l1)
    m2 = jnp.max(l2, axis=-1, keepdims=True)
    i2 = jnp.min(jnp.where(l2 == m2, lane_f, float(ROUTER_LANES)), axis=-1, keepdims=True)
    e = jnp.exp(m2 - m1)
    g1 = 1.0 / (1.0 + e)
    g2 = e * g1
    idx_ref[...] = jnp.where(lane == 0, i1, jnp.where(lane == 1, i2, 0.0)).astype(jnp.int32)
    gate_ref[...] = jnp.where(lane == 0, g1, jnp.where(lane == 1, g2, 0.0))


def _row_specs(d, nct):
    x_spec = pl.BlockSpec((None, ROW_TILE, d), lambda b, t: (b, t, 0))
    mod_spec = pl.BlockSpec((None, None, 6, d), lambda b, t: (b, jnp.where(t >= nct, 1, 0), 0, 0))
    return x_spec, mod_spec


def _normmod(h, nw, modtab, sh_row, sc_row, nct, out_dtype):
    bsz, length, d = h.shape
    x_spec, mod_spec = _row_specs(d, nct)
    return pl.pallas_call(
        functools.partial(_normmod_kernel, sh_row=sh_row, sc_row=sc_row),
        out_shape=jax.ShapeDtypeStruct(h.shape, out_dtype),
        grid=(bsz, length // ROW_TILE),
        in_specs=[x_spec, pl.BlockSpec((1, d), lambda b, t: (0, 0)), mod_spec],
        out_specs=x_spec,
        compiler_params=_cp(("arbitrary", "arbitrary")),
        name="normmod",
    )(h, nw.reshape(1, d), modtab)


def _normmod_router(h, nw, modtab, w_router, sh_row, sc_row, nct):
    bsz, length, d = h.shape
    x_spec, mod_spec = _row_specs(d, nct)
    wr = jnp.pad(w_router, ((0, 0), (0, ROUTER_LANES - w_router.shape[1])))
    lane_spec = pl.BlockSpec((None, ROW_TILE, ROUTER_LANES), lambda b, t: (b, t, 0))
    return pl.pallas_call(
        functools.partial(_normmod_router_kernel, sh_row=sh_row, sc_row=sc_row),
        out_shape=(
            jax.ShapeDtypeStruct(h.shape, F32),
            jax.ShapeDtypeStruct((bsz, length, ROUTER_LANES), jnp.int32),
            jax.ShapeDtypeStruct((bsz, length, ROUTER_LANES), F32),
        ),
        grid=(bsz, length // ROW_TILE),
        in_specs=[x_spec, pl.BlockSpec((1, d), lambda b, t: (0, 0)), mod_spec,
                  pl.BlockSpec((d, ROUTER_LANES), lambda b, t: (0, 0))],
        out_specs=(x_spec, lane_spec, lane_spec),
        compiler_params=_cp(("arbitrary", "arbitrary")),
        name="normmod_router",
    )(h, nw.reshape(1, d), modtab, wr)


def _rms_kernel(x_ref, nw_ref, o_ref):
    x = x_ref[...]
    ms = jnp.mean(x * x, axis=-1, keepdims=True)
    o_ref[...] = x * lax.rsqrt(ms + EPS) * nw_ref[...]


def _final_norm(h, nw):
    bsz, length, d = h.shape
    x_spec = pl.BlockSpec((None, ROW_TILE, d), lambda b, t: (b, t, 0))
    return pl.pallas_call(
        _rms_kernel,
        out_shape=jax.ShapeDtypeStruct(h.shape, F32),
        grid=(bsz, length // ROW_TILE),
        in_specs=[x_spec, pl.BlockSpec((1, d), lambda b, t: (0, 0))],
        out_specs=x_spec,
        compiler_params=_cp(("arbitrary", "arbitrary")),
        name="final_norm",
    )(h, nw.reshape(1, d))


def _cast_weights(pairs):
    @pl.when(pl.program_id(1) == 0)
    def _():
        for w_ref, wbf_ref in pairs:
            wbf_ref[...] = w_ref[...].astype(BF16)


def _mm_plain_kernel(x_ref, w_ref, o_ref, wbf_ref):
    _cast_weights([(w_ref, wbf_ref)])
    o_ref[...] = _dot(x_ref[...], wbf_ref[...]).astype(o_ref.dtype)


def _mm_swiglu_kernel(x_ref, wa_ref, wb_ref, o_ref, wabf_ref, wbbf_ref):
    _cast_weights([(wa_ref, wabf_ref), (wb_ref, wbbf_ref)])
    x = x_ref[...]
    a = _dot(x, wabf_ref[...])
    b = _dot(x, wbbf_ref[...])
    o_ref[...] = (_silu(a) * b).astype(o_ref.dtype)


def _mm_res_kernel(x_ref, w_ref, res_ref, mod_ref, o_ref, wbf_ref, *, g_row):
    _cast_weights([(w_ref, wbf_ref)])
    acc = _dot(x_ref[...], wbf_ref[...])
    o_ref[...] = res_ref[...] + mod_ref[g_row:g_row + 1, :] * acc


def _mm_glu_res_kernel(x_ref, wa_ref, wb_ref, ba_ref, bb_ref, res_ref, mod_ref, o_ref, wabf_ref, wbbf_ref,
                       *, g_row):
    _cast_weights([(wa_ref, wabf_ref), (wb_ref, wbbf_ref)])
    x = x_ref[...]
    z1 = _dot(x, wabf_ref[...]) + ba_ref[...]
    z2 = _dot(x, wbbf_ref[...]) + bb_ref[...]
    o_ref[...] = res_ref[...] + mod_ref[g_row:g_row + 1, :] * (z1 * _sigmoid(z2))


def _mm_plain(x, w, layer, n, tm, tn, out_dtype, name):
    r, k = x.shape
    assert r % tm == 0 and n % tn == 0
    return pl.pallas_call(
        _mm_plain_kernel,
        out_shape=jax.ShapeDtypeStruct((r, n), out_dtype),
        grid=(n // tn, r // tm),
        in_specs=[pl.BlockSpec((tm, k), lambda j, m: (m, 0)),
                  pl.BlockSpec((None, k, tn), lambda j, m: (layer, 0, j))],
        out_specs=pl.BlockSpec((tm, tn), lambda j, m: (m, j)),
        scratch_shapes=[pltpu.VMEM((k, tn), BF16)],
        compiler_params=_cp(("arbitrary", "arbitrary"), VMEM_LIMIT),
        name=name,
    )(x, w)


def _mm_swiglu(x, w13, layer, tm, tn, name):
    r, k = x.shape
    half = w13.shape[2] // 2
    nb = half // tn
    assert r % tm == 0 and half % tn == 0
    return pl.pallas_call(
        _mm_swiglu_kernel,
        out_shape=jax.ShapeDtypeStruct((r, half), BF16),
        grid=(nb, r // tm),
        in_specs=[pl.BlockSpec((tm, k), lambda j, m: (m, 0)),
                  pl.BlockSpec((None, k, tn), lambda j, m: (layer, 0, j)),
                  pl.BlockSpec((None, k, tn), lambda j, m: (layer, 0, nb + j))],
        out_specs=pl.BlockSpec((tm, tn), lambda j, m: (m, j)),
        scratch_shapes=[pltpu.VMEM((k, tn), BF16), pltpu.VMEM((k, tn), BF16)],
        compiler_params=_cp(("arbitrary", "arbitrary"), VMEM_LIMIT),
        name=name,
    )(x, w13, w13)


def _res_specs(tn, tpb, nct):
    res_spec = pl.BlockSpec((ROW_TILE, tn), lambda j, m: (m, j))
    mod_spec = pl.BlockSpec((None, None, 6, tn),
                            lambda j, m: (m // tpb, jnp.where(m % tpb >= nct, 1, 0), 0, j))
    return res_spec, mod_spec


def _mm_res(x, w, layer, res, modtab, g_row, tpb, nct, tn, name):
    r, k = x.shape
    n = w.shape[2]
    tm = ROW_TILE
    res_spec, mod_spec = _res_specs(tn, tpb, nct)
    return pl.pallas_call(
        functools.partial(_mm_res_kernel, g_row=g_row),
        out_shape=jax.ShapeDtypeStruct((r, n), F32),
        grid=(n // tn, r // tm),
        in_specs=[pl.BlockSpec((tm, k), lambda j, m: (m, 0)),
                  pl.BlockSpec((None, k, tn), lambda j, m: (layer, 0, j)),
                  res_spec, mod_spec],
        out_specs=pl.BlockSpec((tm, tn), lambda j, m: (m, j)),
        scratch_shapes=[pltpu.VMEM((k, tn), BF16)],
        compiler_params=_cp(("arbitrary", "arbitrary"), VMEM_LIMIT),
        name=name,
    )(x, w, res, modtab)


def _mm_glu_res(x, w, bias, layer, res, modtab, g_row, tpb, nct, tn, name):
    r, k = x.shape
    half = w.shape[2] // 2
    nb = half // tn
    tm = ROW_TILE
    res_spec, mod_spec = _res_specs(tn, tpb, nct)
    bias2 = bias.reshape(bias.shape[0], 1, 2 * half)
    return pl.pallas_call(
        functools.partial(_mm_glu_res_kernel, g_row=g_row),
        out_shape=jax.ShapeDtypeStruct((r, half), F32),
        grid=(nb, r // tm),
        in_specs=[pl.BlockSpec((tm, k), lambda j, m: (m, 0)),
                  pl.BlockSpec((None, k, tn), lambda j, m: (layer, 0, j)),
                  pl.BlockSpec((None, k, tn), lambda j, m: (layer, 0, nb + j)),
                  pl.BlockSpec((None, 1, tn), lambda j, m: (layer, 0, j)),
                  pl.BlockSpec((None, 1, tn), lambda j, m: (layer, 0, nb + j)),
                  res_spec, mod_spec],
        out_specs=pl.BlockSpec((tm, tn), lambda j, m: (m, j)),
        scratch_shapes=[pltpu.VMEM((k, tn), BF16), pltpu.VMEM((k, tn), BF16)],
        compiler_params=_cp(("arbitrary", "arbitrary"), VMEM_LIMIT),
        name=name,
    )(x, w, w, bias2, bias2, res, modtab)


def _gla_gate_kernel(x_ref, wgl_ref, wup_ref, b_ref, o_ref):
    wh, wl = _split(wgl_ref[...])
    x = x_ref[...]
    gl = _dot(x, wh) + _dot(x, wl)
    z = _dot3(gl, wup_ref[...]) + b_ref[...]
    log_sig = jnp.minimum(z, 0.0) - jnp.log(1.0 + jnp.exp(-jnp.abs(z)))
    o_ref[...] = log_sig * (1.0 / GLA_GATE_TAU)


def _gla_gate(u, w_gl, w_up_bd, b_gate):
    r, d = u.shape
    n = w_up_bd.shape[1]
    tm = 512
    return pl.pallas_call(
        _gla_gate_kernel,
        out_shape=jax.ShapeDtypeStruct((r, n), F32),
        grid=(r // tm,),
        in_specs=[pl.BlockSpec((tm, d), lambda m: (m, 0)),
                  pl.BlockSpec(w_gl.shape, lambda m: (0, 0)),
                  pl.BlockSpec(w_up_bd.shape, lambda m: (0, 0)),
                  pl.BlockSpec((1, n), lambda m: (0, 0))],
        out_specs=pl.BlockSpec((tm, n), lambda m: (m, 0)),
        compiler_params=_cp(("arbitrary",)),
        name="gla_gate",
    )(u, w_gl, w_up_bd, b_gate.reshape(1, n))


def _gla_scan_kernel(tri_ref, q_ref, k_ref, v_ref, g_ref, o_ref, s_ref, *, hk, hv):
    @pl.when(pl.program_id(2) == 0)
    def _():
        s_ref[...] = jnp.zeros_like(s_ref)

    tri = tri_ref[...]
    ones = jnp.ones((GLA_CHUNK, 128), BF16)
    scale = hk ** -0.5
    tn_dims = (((0,), (0,)), ((), ()))
    nt_dims = (((1,), (1,)), ((), ()))
    for h in range(GLA_HEADS):
        q = q_ref[:, h * hk:(h + 1) * hk]
        k = k_ref[:, h * hk:(h + 1) * hk]
        v = v_ref[:, h * hv:(h + 1) * hv].astype(BF16)
        g = g_ref[:, h * hk:(h + 1) * hk]
        gh, gl = _split(g)
        b = _dot(tri, gh) + _dot(tri, gl)
        tot = jnp.sum(g, axis=0, keepdims=True)
        tot_col = (lax.dot_general(gh, ones, tn_dims, preferred_element_type=F32)
                   + lax.dot_general(gl, ones, tn_dims, preferred_element_type=F32))
        qd = (q * scale * jnp.exp(b)).astype(BF16)
        kd = (k * jnp.exp(-b)).astype(BF16)
        kt = (k * jnp.exp(tot - b)).astype(BF16)
        scores = jnp.where(tri > 0, lax.dot_general(qd, kd, nt_dims, preferred_element_type=F32), 0.0)
        s = s_ref[h]
        o = _dot(scores.astype(BF16), v) + _dot(qd, s.astype(BF16))
        o_ref[:, h * hv:(h + 1) * hv] = o
        decay = jnp.exp(tot_col)
        decay = jnp.concatenate([decay] * (hv // 128), axis=1)
        s_ref[h] = decay * s + lax.dot_general(kt, v, tn_dims, preferred_element_type=F32)


def _gla_scan(proj, log_a, nctx_chunks, dk, dv):
    bsz, length, _ = proj.shape
    nchunks = length // GLA_CHUNK
    hk, hv = dk // GLA_HEADS, dv // GLA_HEADS
    idx = jnp.arange(GLA_CHUNK)
    tri = jnp.stack([idx[:, None] >= idx[None, :], idx[:, None] <= idx[None, :]]).astype(BF16)

    def cidx(d, s):
        back = jnp.where(s < nctx_chunks, nctx_chunks - 1 - s, nchunks - 1 - (s - nctx_chunks))
        return jnp.where(d == 0, s, back)

    return pl.pallas_call(
        functools.partial(_gla_scan_kernel, hk=hk, hv=hv),
        out_shape=jax.ShapeDtypeStruct((2, bsz, length, dv), F32),
        grid=(bsz, 2, nchunks),
        in_specs=[
            pl.BlockSpec((None, GLA_CHUNK, GLA_CHUNK), lambda b, d, s: (d, 0, 0)),
            pl.BlockSpec((None, GLA_CHUNK, dk), lambda b, d, s: (b, cidx(d, s), 0)),
            pl.BlockSpec((None, GLA_CHUNK, dk), lambda b, d, s: (b, cidx(d, s), 1)),
            pl.BlockSpec((None, GLA_CHUNK, dv), lambda b, d, s: (b, cidx(d, s), (2 * dk) // dv)),
            pl.BlockSpec((None, GLA_CHUNK, dk), lambda b, d, s: (b, cidx(d, s), d)),
        ],
        out_specs=pl.BlockSpec((None, None, GLA_CHUNK, dv), lambda b, d, s: (d, b, cidx(d, s), 0)),
        scratch_shapes=[pltpu.VMEM((GLA_HEADS, hk, hv), F32)],
        compiler_params=_cp(("arbitrary", "arbitrary", "arbitrary"), VMEM_LIMIT),
        name="gla_scan",
    )(tri, proj, proj, proj, log_a)


def _gla_finish_kernel(of_ref, ob_ref, r_ref, gh_ref, o_ref, *, hv):
    for h in range(GLA_HEADS):
        sl = slice(h * hv, (h + 1) * hv)
        o = of_ref[:, sl] + ob_ref[:, sl]
        ms = jnp.mean(o * o, axis=-1, keepdims=True)
        y = o * lax.rsqrt(ms + EPS) * gh_ref[...]
        o_ref[:, sl] = (y * _silu(r_ref[:, sl])).astype(o_ref.dtype)


def _gla_finish(o2, proj, g_head, dk, dv):
    _, bsz, length, _ = o2.shape
    hv = dv // GLA_HEADS
    tm = ROW_TILE
    return pl.pallas_call(
        functools.partial(_gla_finish_kernel, hv=hv),
        out_shape=jax.ShapeDtypeStruct((bsz, length, dv), BF16),
        grid=(bsz, length // tm),
        in_specs=[
            pl.BlockSpec((None, None, tm, dv), lambda b, t: (0, b, t, 0)),
            pl.BlockSpec((None, None, tm, dv), lambda b, t: (1, b, t, 0)),
            pl.BlockSpec((None, tm, dv), lambda b, t: (b, t, (2 * dk + dv) // dv)),
            pl.BlockSpec((1, hv), lambda b, t: (0, 0)),
        ],
        out_specs=pl.BlockSpec((None, tm, dv), lambda b, t: (b, t, 0)),
        compiler_params=_cp(("arbitrary", "arbitrary")),
        name="gla_finish",
    )(o2, o2, proj, g_head.reshape(1, hv))


def _gla_layer(h, modtab, nw, w_in, layer, w_gate_up, b_gate, g_head, w_out, nctx):
    bsz, length, d = h.shape
    dk = w_gate_up.shape[2]
    dv = w_out.shape[1]
    nmain = 2 * dk + 2 * dv
    tpb = length // ROW_TILE
    nct = nctx // ROW_TILE
    u = _normmod(h, nw, modtab, SH1, SC1, nct, BF16).reshape(bsz * length, d)
    proj = _mm_plain(u, w_in, layer, nmain, 512, 512, F32, "gla_in_proj")
    rank = GLA_GATE_RANK
    w_up_bd = jnp.zeros((2 * rank, 2 * dk), F32)
    w_up_bd = w_up_bd.at[:rank, :dk].set(w_gate_up[0]).at[rank:, dk:].set(w_gate_up[1])
    log_a = _gla_gate(u, w_in[layer, :, nmain:], w_up_bd, b_gate.reshape(2 * dk))
    proj3 = proj.reshape(bsz, length, nmain)
    o2 = _gla_scan(proj3, log_a.reshape(bsz, length, 2 * dk), nctx // GLA_CHUNK, dk, dv)
    gated = _gla_finish(o2, proj3, g_head, dk, dv).reshape(bsz * length, dv)
    out = _mm_res(gated, w_out, layer, h.reshape(bsz * length, d), modtab, G1, tpb, nct, 512, "gla_out_proj")
    return out.reshape(bsz, length, d)


S5_LANES_PER_BATCH = 96
_AN1, _AN2, _AP1, _AP2, _BB1, _BB2, _CC1, _CC2, _PQ, _PK_ROWS = 0, 32, 64, 96, 128, 144, 160, 176, 192, 200


def _s5_params(a_re, a_im, log_dt, b_re, b_im, c_re, c_im):
    t_len = S5_CHUNK
    dt = jnp.exp(log_dt)[..., None]
    mag = jnp.exp(a_re * dt)
    ab_re = mag * jnp.cos(a_im * dt)
    ab_im = mag * jnp.sin(a_im * dt)
    den = a_re * a_re + a_im * a_im
    f_re = ((ab_re - 1.0) * a_re + ab_im * a_im) / den
    f_im = (ab_im * a_re - (ab_re - 1.0) * a_im) / den
    bb_re = f_re[..., None] * b_re - f_im[..., None] * b_im
    bb_im = f_re[..., None] * b_im + f_im[..., None] * b_re

    def cpow(n):
        nn = n.astype(F32)[:, None, :, None]
        m = jnp.exp((a_re * dt)[:, :, None, :] * nn)
        th = (a_im * dt)[:, :, None, :] * nn
        return m * jnp.cos(th), m * jnp.sin(th)

    steps = jnp.arange(t_len)
    order = jnp.stack([steps, t_len - 1 - steps])
    anr, ani = cpow(-order)
    apr, api = cpow(order)
    atr, ati = cpow(jnp.full((2, 1), t_len))
    cat = lambda u, v: jnp.concatenate([u, v], axis=-1)
    bbr_t, bbi_t = bb_re.transpose(0, 1, 3, 2), bb_im.transpose(0, 1, 3, 2)
    zero = jnp.zeros((2, a_re.shape[1], 5, 2 * a_re.shape[2]), F32)
    return jnp.concatenate([
        cat(anr, anr), cat(-ani, ani),
        cat(apr, api), cat(api, apr),
        cat(bbr_t, bbi_t), cat(bbi_t, bbr_t),
        cat(c_re, -c_re), cat(-c_im, -c_im),
        cat(atr, atr), cat(-ati, ati), cat(ati, -ati), zero], axis=2)


def _s5_in_kernel(hl_ref, hc_ref, nw_ref, mod_ref, o_ref, *, bsz, qb):
    nw = nw_ref[...]
    blocks = []
    for b in range(bsz):
        lat = _normmod_rows(hl_ref[b], nw, mod_ref[b, 1, SC1:SC1 + 1, :], mod_ref[b, 1, SH1:SH1 + 1, :])
        ctx = _normmod_rows(hc_ref[b], nw, mod_ref[b, 0, SC1:SC1 + 1, :], mod_ref[b, 0, SH1:SH1 + 1, :])
        pad = jnp.zeros((qb - lat.shape[0] - ctx.shape[0], lat.shape[1]), F32)
        blocks += [lat, ctx, pad]
    o_ref[...] = jnp.concatenate(blocks, axis=0).T.astype(o_ref.dtype)


def _s5_in(h, nw, modtab, nctx):
    bsz, length, d = h.shape
    t_len = S5_CHUNK
    qb = S5_LANES_PER_BATCH
    nctx_chunks = nctx // t_len
    return pl.pallas_call(
        functools.partial(_s5_in_kernel, bsz=bsz, qb=qb),
        out_shape=jax.ShapeDtypeStruct((d, t_len * bsz * qb), BF16),
        grid=(t_len,),
        in_specs=[
            pl.BlockSpec((bsz, None, GRID_W, d), lambda s: (0, nctx // GRID_W + s, 0, 0)),
            pl.BlockSpec((bsz, nctx_chunks, d), lambda s: (0, 0, s)),
            pl.BlockSpec((1, d), lambda s: (0, 0)),
            pl.BlockSpec(modtab.shape, lambda s: (0, 0, 0, 0)),
        ],
        out_specs=pl.BlockSpec((d, bsz * qb), lambda s: (0, s)),
        compiler_params=_cp(("arbitrary",), VMEM_LIMIT),
        name="s5_in",
    )(h.reshape(bsz, length // GRID_W, GRID_W, d), h.reshape(bsz, length // t_len, t_len * d),
      nw.reshape(1, d), modtab)


def _tile_rows(x, reps):
    return jnp.concatenate([x] * reps, axis=0)


def _repeat_rows(x, reps):
    return jnp.concatenate([jnp.broadcast_to(x[i:i + 1, :], (reps, x.shape[1])) for i in range(x.shape[0])],
                           axis=0)


def _s5_kernel(ut_ref, pk_ref, y_ref, sv_ref, ss_ref, w_ref, r2_ref, acc_ref, *, bsz, qb, orders):
    gb = ut_ref.shape[0]
    t_len = S5_CHUNK
    k = S5_GROUP * t_len
    q = ut_ref.shape[3]
    row_s = lax.broadcasted_iota(jnp.int32, (k, k), 0) % t_len
    col_t = lax.broadcasted_iota(jnp.int32, (k, k), 1) % t_len
    masks = (row_s <= col_t, row_s >= col_t)
    w_ref[...] = jnp.zeros_like(w_ref)
    for g in range(gb):
        u = ut_ref[g].reshape(k, q).T
        for d in range(2):
            i = 2 * g + d
            pk = pk_ref[d, g]
            l2 = (_tile_rows(pk[_AN1:_AN1 + t_len], S5_GROUP) * _repeat_rows(pk[_BB1:_BB1 + S5_GROUP], t_len)
                  + _tile_rows(pk[_AN2:_AN2 + t_len], S5_GROUP) * _repeat_rows(pk[_BB2:_BB2 + S5_GROUP], t_len))
            r2t = (_tile_rows(pk[_AP1:_AP1 + t_len], S5_GROUP) * _repeat_rows(pk[_CC1:_CC1 + S5_GROUP], t_len)
                   + _tile_rows(pk[_AP2:_AP2 + t_len], S5_GROUP) * _repeat_rows(pk[_CC2:_CC2 + S5_GROUP], t_len))
            r2 = r2t.T
            r2_ref[i] = r2.astype(BF16)
            toep = jnp.where(masks[d], _dot3(l2, r2), 0.0).astype(BF16)
            y_local = _dot(u, toep)
            if d == 0:
                acc_ref[g] = y_local
            else:
                acc_ref[g] = acc_ref[g] + y_local
            v = _dot(u, l2.astype(BF16))
            vs = pltpu.roll(v, v.shape[1] // 2, axis=1)
            pt, qt, qts = pk[_PQ:_PQ + 1], pk[_PQ + 1:_PQ + 2], pk[_PQ + 2:_PQ + 3]
            sv_ref[i] = pt * v + qt * vs
            ss_ref[i] = pt * vs + qts * v
    coef = []
    for g in range(gb):
        for d in range(2):
            pk = pk_ref[d, g]
            coef.append(tuple(jnp.broadcast_to(pk[_PQ + r:_PQ + r + 1], (bsz, pk.shape[1])) for r in range(3)))
    zero = jnp.zeros((bsz, pk_ref.shape[3]), F32)
    state = [(zero, zero)] * (2 * gb)
    nsteps = len(orders[0])
    for step in range(nsteps):
        for i in range(2 * gb):
            rows = pl.ds(orders[i % 2][step], bsz, stride=qb)
            w, ws = state[i]
            w_ref[i, rows, :] = w
            if step + 1 < nsteps:
                pt, qt, qts = coef[i]
                state[i] = (pt * w + qt * ws + sv_ref[i, rows, :], pt * ws + qts * w + ss_ref[i, rows, :])
    for g in range(gb):
        y = acc_ref[g]
        for d in range(2):
            i = 2 * g + d
            y = y + _dot(w_ref[i].astype(BF16), r2_ref[i])
        y_ref[g] = y.T.reshape(S5_GROUP, t_len, q)


def _s5_scan(ut, pk, bsz, nctx_chunks, nlat_chunks):
    ngroups, _, t_len, q = ut.shape
    np2 = pk.shape[3]
    gb = S5_GROUP_BLOCK
    qb = S5_LANES_PER_BATCH
    lat = list(range(nlat_chunks))
    ctx = list(range(nlat_chunks, nlat_chunks + nctx_chunks))
    orders = (ctx + lat, ctx[::-1] + lat[::-1])
    blk = (gb, S5_GROUP, t_len, q)
    return pl.pallas_call(
        functools.partial(_s5_kernel, bsz=bsz, qb=qb, orders=orders),
        out_shape=jax.ShapeDtypeStruct(ut.shape, F32),
        grid=(ngroups // gb,),
        in_specs=[pl.BlockSpec(blk, lambda g: (g, 0, 0, 0)),
                  pl.BlockSpec((2, gb, _PK_ROWS, np2), lambda g: (0, g, 0, 0))],
        out_specs=pl.BlockSpec(blk, lambda g: (g, 0, 0, 0)),
        scratch_shapes=[pltpu.VMEM((2 * gb, q, np2), F32), pltpu.VMEM((2 * gb, q, np2), F32),
                        pltpu.VMEM((2 * gb, q, np2), F32), pltpu.VMEM((2 * gb, np2, S5_GROUP * t_len), BF16),
                        pltpu.VMEM((gb, q, S5_GROUP * t_len), F32)],
        compiler_params=_cp(("arbitrary",), VMEM_LIMIT),
        name="s5_scan",
    )(ut, pk)


def _s5_out_kernel(yt_ref, hl_ref, hc_ref, nw_ref, mod_ref, d_ref, ol_ref, oc_ref, *, bsz, qb):
    y = yt_ref[...].T
    nw = nw_ref[...]
    nlat, nc = hl_ref.shape[1], hc_ref.shape[1]
    c0 = 0.7978845608028654

    def act(yv, hv, region, b):
        u = _normmod_rows(hv, nw, mod_ref[b, region, SC1:SC1 + 1, :], mod_ref[b, region, SH1:SH1 + 1, :])
        z = yv + u * d_ref[...]
        return (0.5 * z * (1.0 + jnp.tanh(c0 * (z + 0.044715 * (z * z * z))))).astype(ol_ref.dtype)

    for b in range(bsz):
        ol_ref[b] = act(y[b * qb:b * qb + nlat], hl_ref[b], 1, b)
        oc_ref[b] = act(y[b * qb + nlat:b * qb + nlat + nc], hc_ref[b], 0, b)


def _s5_out(yt, h, nw, modtab, d_skip, nctx):
    bsz, length, d = h.shape
    t_len = S5_CHUNK
    qb = S5_LANES_PER_BATCH
    nctx_chunks = nctx // t_len
    seq = length - nctx
    lat_spec = pl.BlockSpec((bsz, None, GRID_W, d), lambda s: (0, nctx // GRID_W + s, 0, 0))
    ctx_spec = pl.BlockSpec((bsz, nctx_chunks, d), lambda s: (0, 0, s))
    act_l, act_c = pl.pallas_call(
        functools.partial(_s5_out_kernel, bsz=bsz, qb=qb),
        out_shape=(jax.ShapeDtypeStruct((bsz, seq // GRID_W, GRID_W, d), BF16),
                   jax.ShapeDtypeStruct((bsz, nctx_chunks, t_len * d), BF16)),
        grid=(t_len,),
        in_specs=[pl.BlockSpec((d, bsz * qb), lambda s: (0, s)),
                  lat_spec, ctx_spec,
                  pl.BlockSpec((1, d), lambda s: (0, 0)),
                  pl.BlockSpec(modtab.shape, lambda s: (0, 0, 0, 0)),
                  pl.BlockSpec((1, d), lambda s: (0, 0))],
        out_specs=(pl.BlockSpec((bsz, None, GRID_W, d), lambda s: (0, s, 0, 0)), ctx_spec),
        compiler_params=_cp(("arbitrary",), VMEM_LIMIT),
        name="s5_out",
    )(yt, h.reshape(bsz, length // GRID_W, GRID_W, d), h.reshape(bsz, length // t_len, t_len * d),
      nw.reshape(1, d), modtab, d_skip.reshape(1, d))
    return act_l.reshape(bsz, seq, d), act_c.reshape(bsz, nctx, d)


def _s5_layer(h, modtab, nw, a_re, a_im, log_dt, b_re, b_im, c_re, c_im, d_skip, w_glu, b_glu, layer, nctx,
              last):
    bsz, length, d = h.shape
    seq = length - nctx
    t_len = S5_CHUNK
    assert seq // GRID_W == t_len and nctx % t_len == 0 and nctx % GRID_W == 0
    assert seq // t_len + nctx // t_len <= S5_LANES_PER_BATCH
    ngroups = d // S5_GROUP
    ut = _s5_in(h, nw, modtab, nctx)
    pk = _s5_params(a_re, a_im, log_dt, b_re, b_im, c_re, c_im)
    lanes = ut.shape[1] // t_len
    yt = _s5_scan(ut.reshape(ngroups, S5_GROUP, t_len, lanes), pk, bsz, nctx // t_len, GRID_W)
    act_l, act_c = _s5_out(yt.reshape(d, t_len * lanes), h, nw, modtab, d_skip, nctx)
    if last:
        act, h, nct = act_l, h[:, nctx:], 0
    else:
        act, nct = jnp.concatenate([act_c, act_l], axis=1), nctx // ROW_TILE
    rows = h.shape[0] * h.shape[1]
    out = _mm_glu_res(act.reshape(rows, d), w_glu, b_glu, layer, h.reshape(rows, d), modtab, G1,
                      h.shape[1] // ROW_TILE, nct, 512, "s5_glu")
    return out.reshape(h.shape)


def _dense_ffn(h, modtab, nw, w13, w2, layer, nct):
    bsz, length, d = h.shape
    rows = bsz * length
    v = _normmod(h, nw, modtab, SH2, SC2, nct, BF16).reshape(rows, d)
    act = _mm_swiglu(v, w13, layer, 512, 512, "ffn_w13")
    out = _mm_res(act, w2, layer, h.reshape(rows, d), modtab, G2, length // ROW_TILE, nct, 512, "ffn_w2")
    return out.reshape(h.shape)


def _moe_gather_kernel(src_ref, nxt_ref, v_hbm, o_ref, buf_ref, sem):
    t = pl.program_id(0)
    slot = t % 2

    def request(idx_ref, s):
        def issue(r, carry):
            pltpu.make_async_copy(v_hbm.at[pl.ds(idx_ref[0, r], 1)], buf_ref.at[s, pl.ds(r, 1)],
                                  sem.at[s]).start()
            return carry
        lax.fori_loop(0, MOE_TILE, issue, 0, unroll=8)

    @pl.when(t == 0)
    def _():
        request(src_ref, 0)

    @pl.when(t + 1 < pl.num_programs(0))
    def _():
        request(nxt_ref, 1 - slot)

    pltpu.make_async_copy(v_hbm.at[pl.ds(0, MOE_TILE)], buf_ref.at[slot], sem.at[slot]).wait()
    o_ref[...] = buf_ref[slot].astype(o_ref.dtype)


def _moe_gather(src, v):
    ntiles = src.shape[0]
    d = v.shape[1]
    idx_block = (None, 1, MOE_TILE)
    return pl.pallas_call(
        _moe_gather_kernel,
        out_shape=jax.ShapeDtypeStruct((ntiles * MOE_TILE, d), BF16),
        grid=(ntiles,),
        in_specs=[pl.BlockSpec(idx_block, lambda t: (t, 0, 0), memory_space=pltpu.SMEM),
                  pl.BlockSpec(idx_block, lambda t: (jnp.minimum(t + 1, ntiles - 1), 0, 0),
                               memory_space=pltpu.SMEM),
                  pl.BlockSpec(memory_space=pl.ANY)],
        out_specs=pl.BlockSpec((MOE_TILE, d), lambda t: (t, 0)),
        scratch_shapes=[pltpu.VMEM((2, MOE_TILE, d), F32), pltpu.SemaphoreType.DMA((2,))],
        compiler_params=_cp(("arbitrary",)),
        name="moe_gather",
    )(src, src, v)


def _moe_cast_weights(te_ref, pairs):
    t = pl.program_id(1)
    changed = jnp.logical_or(t == 0, te_ref[t] != te_ref[jnp.maximum(t - 1, 0)])

    @pl.when(changed)
    def _():
        for w_ref, wbf_ref in pairs:
            wbf_ref[...] = w_ref[...].astype(BF16)


def _moe_swiglu_kernel(te_ref, nv_ref, x_ref, wa_ref, wb_ref, o_ref, wabf_ref, wbbf_ref):
    _moe_cast_weights(te_ref, [(wa_ref, wabf_ref), (wb_ref, wbbf_ref)])

    in_use = pl.program_id(1) < nv_ref[0]

    @pl.when(in_use)
    def _():
        x = x_ref[...]
        a = _dot(x, wabf_ref[...])
        b = _dot(x, wbbf_ref[...])
        o_ref[...] = (_silu(a) * b).astype(o_ref.dtype)

    @pl.when(jnp.logical_not(in_use))
    def _():
        o_ref[...] = jnp.zeros_like(o_ref)


def _moe_mm_kernel(te_ref, nv_ref, x_ref, w_ref, o_ref, wbf_ref):
    _moe_cast_weights(te_ref, [(w_ref, wbf_ref)])

    in_use = pl.program_id(1) < nv_ref[0]

    @pl.when(in_use)
    def _():
        o_ref[...] = _dot(x_ref[...], wbf_ref[...])

    @pl.when(jnp.logical_not(in_use))
    def _():
        o_ref[...] = jnp.zeros_like(o_ref)


def _moe_swiglu(te, nv, xs, w13, layer, tn):
    rows, k = xs.shape
    half = w13.shape[3] // 2
    nb = half // tn
    tm = MOE_TILE
    return pl.pallas_call(
        _moe_swiglu_kernel,
        out_shape=jax.ShapeDtypeStruct((rows, half), BF16),
        grid_spec=pltpu.PrefetchScalarGridSpec(
            num_scalar_prefetch=2,
            grid=(nb, rows // tm),
            in_specs=[pl.BlockSpec((tm, k), lambda j, t, te, nv: (t, 0)),
                      pl.BlockSpec((None, None, k, tn), lambda j, t, te, nv: (layer, te[t], 0, j)),
                      pl.BlockSpec((None, None, k, tn), lambda j, t, te, nv: (layer, te[t], 0, nb + j))],
            out_specs=pl.BlockSpec((tm, tn), lambda j, t, te, nv: (t, j)),
            scratch_shapes=[pltpu.VMEM((k, tn), BF16), pltpu.VMEM((k, tn), BF16)]),
        compiler_params=_cp(("arbitrary", "arbitrary"), VMEM_LIMIT),
        name="moe_w13",
    )(te, nv, xs, w13, w13)


def _moe_mm(te, nv, xs, w2, layer, tn):
    rows, k = xs.shape
    n = w2.shape[3]
    tm = MOE_TILE
    return pl.pallas_call(
        _moe_mm_kernel,
        out_shape=jax.ShapeDtypeStruct((rows, n), F32),
        grid_spec=pltpu.PrefetchScalarGridSpec(
            num_scalar_prefetch=2,
            grid=(n // tn, rows // tm),
            in_specs=[pl.BlockSpec((tm, k), lambda j, t, te, nv: (t, 0)),
                      pl.BlockSpec((None, None, k, tn), lambda j, t, te, nv: (layer, te[t], 0, j))],
            out_specs=pl.BlockSpec((tm, tn), lambda j, t, te, nv: (t, j)),
            scratch_shapes=[pltpu.VMEM((k, tn), BF16)]),
        compiler_params=_cp(("arbitrary", "arbitrary"), VMEM_LIMIT),
        name="moe_w2",
    )(te, nv, xs, w2)


def _moe_combine_kernel(pos_ref, nxt_ref, gate_ref, y_hbm, res_ref, mod_ref, o_ref, buf_ref, sem, *, g_row):
    m = pl.program_id(0)
    slot = m % 2

    def request(idx_ref, s):
        for k in range(TOP_K):
            def issue(r, carry, k=k):
                pltpu.make_async_copy(y_hbm.at[pl.ds(idx_ref[k, r], 1)], buf_ref.at[s, k, pl.ds(r, 1)],
                                      sem.at[s]).start()
                return carry
            lax.fori_loop(0, ROW_TILE, issue, 0, unroll=8)

    @pl.when(m == 0)
    def _():
        request(pos_ref, 0)

    @pl.when(m + 1 < pl.num_programs(0))
    def _():
        request(nxt_ref, 1 - slot)

    for k in range(TOP_K):
        pltpu.make_async_copy(y_hbm.at[pl.ds(0, ROW_TILE)], buf_ref.at[slot, k], sem.at[slot]).wait()
    p = gate_ref[...]
    mix = p[:, 0:1] * buf_ref[slot, 0] + p[:, 1:2] * buf_ref[slot, 1]
    o_ref[...] = res_ref[...] + mod_ref[g_row:g_row + 1, :] * mix


def _moe_combine(pos, gates, ys, res, modtab, g_row, tpb, nct):
    n, d = res.shape
    tm = ROW_TILE
    ntiles = n // tm
    return pl.pallas_call(
        functools.partial(_moe_combine_kernel, g_row=g_row),
        out_shape=jax.ShapeDtypeStruct((n, d), F32),
        grid=(ntiles,),
        in_specs=[pl.BlockSpec((None, TOP_K, tm), lambda m: (m, 0, 0), memory_space=pltpu.SMEM),
                  pl.BlockSpec((None, TOP_K, tm), lambda m: (jnp.minimum(m + 1, ntiles - 1), 0, 0),
                               memory_space=pltpu.SMEM),
                  pl.BlockSpec((tm, ROUTER_LANES), lambda m: (m, 0)),
                  pl.BlockSpec(memory_space=pl.ANY),
                  pl.BlockSpec((tm, d), lambda m: (m, 0)),
                  pl.BlockSpec((None, None, 6, d), lambda m: (m // tpb, jnp.where(m % tpb >= nct, 1, 0), 0, 0))],
        out_specs=pl.BlockSpec((tm, d), lambda m: (m, 0)),
        scratch_shapes=[pltpu.VMEM((2, TOP_K, tm, d), F32), pltpu.SemaphoreType.DMA((2,))],
        compiler_params=_cp(("arbitrary",), VMEM_LIMIT),
        name="moe_combine",
    )(pos, pos, gates, ys, res, modtab)


def _moe_route(idx2):
    n = idx2.shape[0]
    flat_e = idx2.reshape(-1)
    onehot = (flat_e[:, None] == jnp.arange(N_EXPERTS, dtype=jnp.int32)[None, :]).astype(jnp.int32)
    csum = jnp.cumsum(onehot, axis=0)
    rank = jnp.sum((csum - onehot) * onehot, axis=1)
    counts = csum[-1]
    ntile = (counts + MOE_TILE - 1) // MOE_TILE
    tile_end = jnp.cumsum(ntile)
    tile_start = tile_end - ntile
    pos = tile_start[flat_e] * MOE_TILE + rank
    tiles = (TOP_K * n) // MOE_TILE + N_EXPERTS
    src = jnp.zeros((tiles * MOE_TILE,), jnp.int32).at[pos].set(jnp.arange(TOP_K * n, dtype=jnp.int32) // TOP_K)
    nv = tile_end[-1:]
    tids = jnp.minimum(jnp.arange(tiles, dtype=jnp.int32), nv[0] - 1)
    te = jnp.sum((tile_end[None, :] <= tids[:, None]).astype(jnp.int32), axis=1)
    return pos.astype(jnp.int32), src.reshape(tiles, 1, MOE_TILE), te, nv.astype(jnp.int32)


def _moe_ffn(h, modtab, nw, w_router, w13, w2, layer, nct):
    bsz, length, d = h.shape
    n = bsz * length
    v, idx, gates = _normmod_router(h, nw, modtab, w_router, SH2, SC2, nct)
    idx2 = idx.reshape(n, ROUTER_LANES)[:, :TOP_K]
    pos, src, te, nv = _moe_route(idx2)
    xs = _moe_gather(src, v.reshape(n, d))
    act = _moe_swiglu(te, nv, xs, w13, layer, 512)
    ys = _moe_mm(te, nv, act, w2, layer, 512)
    pos3 = pos.reshape(n // ROW_TILE, ROW_TILE, TOP_K).transpose(0, 2, 1)
    out = _moe_combine(pos3, gates.reshape(n, ROUTER_LANES), ys, h.reshape(n, d), modtab, G2,
                       length // ROW_TILE, nct)
    return out.reshape(h.shape)


def kernel(x, c, ctx, c_ctx, w_mod, b_mod, norm1, norm2, norm_f, gla_w_in, gla_w_gate_up, gla_b_gate,
           gla_g_head, gla_w_out, ffn_w13, ffn_w2, s5_a_re, s5_a_im, s5_log_dt, s5_b_re, s5_b_im, s5_c_re,
           s5_c_im, s5_d, s5_w_glu, s5_b_glu, moe_w_router, moe_w13, moe_w2):
    bsz, seq, d = x.shape
    nctx = ctx.shape[1]
    depth = w_mod.shape[0]
    assert bsz + 1 <= 8 and seq % ROW_TILE == 0 and nctx % ROW_TILE == 0 and seq % GRID_W == 0
    nct = nctx // ROW_TILE

    c8 = jnp.concatenate([c, c_ctx[None, :], jnp.zeros((8 - bsz - 1, d), F32)], axis=0)
    mods = _modulation(c8, w_mod, b_mod).reshape(depth, 8, 6, d)
    h = jnp.concatenate([ctx, x], axis=1)
    for i in range(depth):
        last = i == depth - 1
        j = i // 2
        modtab = jnp.stack([jnp.broadcast_to(mods[i, bsz], (bsz, 6, d)), mods[i, :bsz]], axis=1)
        if i % 2 == 0:
            h = _gla_layer(h, modtab, norm1[i], gla_w_in, j, gla_w_gate_up[j], gla_b_gate[j], gla_g_head[j],
                           gla_w_out, nctx)
        else:
            h = _s5_layer(h, modtab, norm1[i], s5_a_re[j], s5_a_im[j], s5_log_dt[j], s5_b_re[j], s5_b_im[j],
                          s5_c_re[j], s5_c_im[j], s5_d[j], s5_w_glu, s5_b_glu, j, nctx, last)
        ffn_nct = 0 if last else nct
        if i % 2 == 0:
            h = _dense_ffn(h, modtab, norm2[i], ffn_w13, ffn_w2, j, ffn_nct)
        else:
            h = _moe_ffn(h, modtab, norm2[i], moe_w_router[j], moe_w13, moe_w2, j, ffn_nct)
    if h.shape[1] != seq:
        h = h[:, nctx:]
    return _final_norm(h, norm_f)
```

```python
import functools

import jax
import jax.numpy as jnp
from jax import lax
from jax.experimental import pallas as pl
from jax.experimental.pallas import tpu as pltpu

F32 = jnp.float32
BF16 = jnp.bfloat16

EPS = 1e-6
GRID_W = 64

GLA_HEADS = 4
GLA_GATE_RANK = 16
GLA_GATE_TAU = 16.0
GLA_CHUNK = 64

S5_GROUP = 16
S5_CHUNK = 32
S5_GROUP_BLOCK = 4

N_EXPERTS = 8
TOP_K = 2
ROUTER_LANES = 128

ROW_TILE = 256
MOE_TILE = 256
VMEM_LIMIT = 56 * 1024 * 1024

SH1, SC1, G1, SH2, SC2, G2 = range(6)


def _cp(sem, vmem=None):
    return pltpu.CompilerParams(dimension_semantics=sem, vmem_limit_bytes=vmem)


def _dot(a, b):
    return jnp.dot(a, b, preferred_element_type=F32)


def _split(x):
    hi = x.astype(BF16)
    lo = (x - hi.astype(F32)).astype(BF16)
    return hi, lo


def _dot3(a, b):
    ah, al = _split(a)
    bh, bl = _split(b)
    return _dot(ah, bh) + _dot(ah, bl) + _dot(al, bh)


def _sigmoid(x):
    return 1.0 / (1.0 + jnp.exp(-x))


def _silu(x):
    return x * _sigmoid(x)


def _mod_kernel(c_ref, w_ref, b_ref, o_ref):
    x = _silu(c_ref[...]).astype(BF16)
    o_ref[...] = _dot(x, w_ref[...].astype(BF16)) + b_ref[...]


def _modulation(c8, w_mod, b_mod):
    depth, d, n = w_mod.shape
    tn = 1024
    return pl.pallas_call(
        _mod_kernel,
        out_shape=jax.ShapeDtypeStruct((depth, 8, n), F32),
        grid=(depth, n // tn),
        in_specs=[
            pl.BlockSpec((8, d), lambda i, j: (0, 0)),
            pl.BlockSpec((None, d, tn), lambda i, j: (i, 0, j)),
            pl.BlockSpec((None, 1, tn), lambda i, j: (i, 0, j)),
        ],
        out_specs=pl.BlockSpec((None, 8, tn), lambda i, j: (i, 0, j)),
        compiler_params=_cp(("arbitrary", "arbitrary"), VMEM_LIMIT),
        name="modulation",
    )(c8, w_mod, b_mod.reshape(depth, 1, n))


def _normmod_rows(x, nw, scale, shift):
    ms = jnp.mean(x * x, axis=-1, keepdims=True)
    return x * lax.rsqrt(ms + EPS) * nw * (1.0 + scale) + shift


def _normmod_body(x_ref, nw_ref, mod_ref, sh_row, sc_row):
    return _normmod_rows(x_ref[...], nw_ref[...], mod_ref[sc_row:sc_row + 1, :], mod_ref[sh_row:sh_row + 1, :])


def _normmod_kernel(x_ref, nw_ref, mod_ref, o_ref, *, sh_row, sc_row):
    o_ref[...] = _normmod_body(x_ref, nw_ref, mod_ref, sh_row, sc_row).astype(o_ref.dtype)


def _normmod_router_kernel(x_ref, nw_ref, mod_ref, wr_ref, o_ref, idx_ref, gate_ref, *, sh_row, sc_row):
    y = _normmod_body(x_ref, nw_ref, mod_ref, sh_row, sc_row)
    o_ref[...] = y
    logits = _dot3(y, wr_ref[...])
    lane = lax.broadcasted_iota(jnp.int32, logits.shape, 1)
    lane_f = lane.astype(F32)
    neg = jnp.float32(-jnp.inf)
    l1 = jnp.where(lane < N_EXPERTS, logits, neg)
    m1 = jnp.max(l1, axis=-1, keepdims=True)
    i1 = jnp.min(jnp.where(l1 == m1, lane_f, float(ROUTER_LANES)), axis=-1, keepdims=True)
    l2 = jnp.where(lane_f == i1, neg, l1)
    m2 = jnp.max(l2, axis=-1, keepdims=True)
    i2 = jnp.min(jnp.where(l2 == m2, lane_f, float(ROUTER_LANES)), axis=-1, keepdims=True)
    e = jnp.exp(m2 - m1)
    g1 = 1.0 / (1.0 + e)
    g2 = e * g1
    idx_ref[...] = jnp.where(lane == 0, i1, jnp.where(lane == 1, i2, 0.0)).astype(jnp.int32)
    gate_ref[...] = jnp.where(lane == 0, g1, jnp.where(lane == 1, g2, 0.0))


def _row_specs(d, nct):
    x_spec = pl.BlockSpec((None, ROW_TILE, d), lambda b, t: (b, t, 0))
    mod_spec = pl.BlockSpec((None, None, 6, d), lambda b, t: (b, jnp.where(t >= nct, 1, 0), 0, 0))
    return x_spec, mod_spec


def _normmod(h, nw, modtab, sh_row, sc_row, nct, out_dtype):
    bsz, length, d = h.shape
    x_spec, mod_spec = _row_specs(d, nct)
    return pl.pallas_call(
        functools.partial(_normmod_kernel, sh_row=sh_row, sc_row=sc_row),
        out_shape=jax.ShapeDtypeStruct(h.shape, out_dtype),
        grid=(bsz, length // ROW_TILE),
        in_specs=[x_spec, pl.BlockSpec((1, d), lambda b, t: (0, 0)), mod_spec],
        out_specs=x_spec,
        compiler_params=_cp(("arbitrary", "arbitrary")),
        name="normmod",
    )(h, nw.reshape(1, d), modtab)


def _normmod_router(h, nw, modtab, w_router, sh_row, sc_row, nct):
    bsz, length, d = h.shape
    x_spec, mod_spec = _row_specs(d, nct)
    wr = jnp.pad(w_router, ((0, 0), (0, ROUTER_LANES - w_router.shape[1])))
    lane_spec = pl.BlockSpec((None, ROW_TILE, ROUTER_LANES), lambda b, t: (b, t, 0))
    return pl.pallas_call(
        functools.partial(_normmod_router_kernel, sh_row=sh_row, sc_row=sc_row),
        out_shape=(
            jax.ShapeDtypeStruct(h.shape, F32),
            jax.ShapeDtypeStruct((bsz, length, ROUTER_LANES), jnp.int32),
            jax.ShapeDtypeStruct((bsz, length, ROUTER_LANES), F32),
        ),
        grid=(bsz, length // ROW_TILE),
        in_specs=[x_spec, pl.BlockSpec((1, d), lambda b, t: (0, 0)), mod_spec,
                  pl.BlockSpec((d, ROUTER_LANES), lambda b, t: (0, 0))],
        out_specs=(x_spec, lane_spec, lane_spec),
        compiler_params=_cp(("arbitrary", "arbitrary")),
        name="normmod_router",
    )(h, nw.reshape(1, d), modtab, wr)


def _rms_kernel(x_ref, nw_ref, o_ref):
    x = x_ref[...]
    ms = jnp.mean(x * x, axis=-1, keepdims=True)
    o_ref[...] = x * lax.rsqrt(ms + EPS) * nw_ref[...]


def _final_norm(h, nw):
    bsz, length, d = h.shape
    x_spec = pl.BlockSpec((None, ROW_TILE, d), lambda b, t: (b, t, 0))
    return pl.pallas_call(
        _rms_kernel,
        out_shape=jax.ShapeDtypeStruct(h.shape, F32),
        grid=(bsz, length // ROW_TILE),
        in_specs=[x_spec, pl.BlockSpec((1, d), lambda b, t: (0, 0))],
        out_specs=x_spec,
        compiler_params=_cp(("arbitrary", "arbitrary")),
        name="final_norm",
    )(h, nw.reshape(1, d))


def _cast_weights(pairs):
    @pl.when(pl.program_id(1) == 0)
    def _():
        for w_ref, wbf_ref in pairs:
            wbf_ref[...] = w_ref[...].astype(BF16)


def _mm_plain_kernel(x_ref, w_ref, o_ref, wbf_ref):
    _cast_weights([(w_ref, wbf_ref)])
    o_ref[...] = _dot(x_ref[...], wbf_ref[...]).astype(o_ref.dtype)


def _mm_swiglu_kernel(x_ref, wa_ref, wb_ref, o_ref, wabf_ref, wbbf_ref):
    _cast_weights([(wa_ref, wabf_ref), (wb_ref, wbbf_ref)])
    x = x_ref[...]
    a = _dot(x, wabf_ref[...])
    b = _dot(x, wbbf_ref[...])
    o_ref[...] = (_silu(a) * b).astype(o_ref.dtype)


def _mm_res_kernel(x_ref, w_ref, res_ref, mod_ref, o_ref, wbf_ref, *, g_row):
    _cast_weights([(w_ref, wbf_ref)])
    acc = _dot(x_ref[...], wbf_ref[...])
    o_ref[...] = res_ref[...] + mod_ref[g_row:g_row + 1, :] * acc


def _mm_glu_res_kernel(x_ref, wa_ref, wb_ref, ba_ref, bb_ref, res_ref, mod_ref, o_ref, wabf_ref, wbbf_ref,
                       *, g_row):
    _cast_weights([(wa_ref, wabf_ref), (wb_ref, wbbf_ref)])
    x = x_ref[...]
    z1 = _dot(x, wabf_ref[...]) + ba_ref[...]
    z2 = _dot(x, wbbf_ref[...]) + bb_ref[...]
    o_ref[...] = res_ref[...] + mod_ref[g_row:g_row + 1, :] * (z1 * _sigmoid(z2))


def _mm_plain(x, w, layer, n, tm, tn, out_dtype, name):
    r, k = x.shape
    assert r % tm == 0 and n % tn == 0
    return pl.pallas_call(
        _mm_plain_kernel,
        out_shape=jax.ShapeDtypeStruct((r, n), out_dtype),
        grid=(n // tn, r // tm),
        in_specs=[pl.BlockSpec((tm, k), lambda j, m: (m, 0)),
                  pl.BlockSpec((None, k, tn), lambda j, m: (layer, 0, j))],
        out_specs=pl.BlockSpec((tm, tn), lambda j, m: (m, j)),
        scratch_shapes=[pltpu.VMEM((k, tn), BF16)],
        compiler_params=_cp(("arbitrary", "arbitrary"), VMEM_LIMIT),
        name=name,
    )(x, w)


def _mm_swiglu(x, w13, layer, tm, tn, name):
    r, k = x.shape
    half = w13.shape[2] // 2
    nb = half // tn
    assert r % tm == 0 and half % tn == 0
    return pl.pallas_call(
        _mm_swiglu_kernel,
        out_shape=jax.ShapeDtypeStruct((r, half), BF16),
        grid=(nb, r // tm),
        in_specs=[pl.BlockSpec((tm, k), lambda j, m: (m, 0)),
                  pl.BlockSpec((None, k, tn), lambda j, m: (layer, 0, j)),
                  pl.BlockSpec((None, k, tn), lambda j, m: (layer, 0, nb + j))],
        out_specs=pl.BlockSpec((tm, tn), lambda j, m: (m, j)),
        scratch_shapes=[pltpu.VMEM((k, tn), BF16), pltpu.VMEM((k, tn), BF16)],
        compiler_params=_cp(("arbitrary", "arbitrary"), VMEM_LIMIT),
        name=name,
    )(x, w13, w13)


def _res_specs(tn, tpb, nct):
    res_spec = pl.BlockSpec((ROW_TILE, tn), lambda j, m: (m, j))
    mod_spec = pl.BlockSpec((None, None, 6, tn),
                            lambda j, m: (m // tpb, jnp.where(m % tpb >= nct, 1, 0), 0, j))
    return res_spec, mod_spec


def _mm_res(x, w, layer, res, modtab, g_row, tpb, nct, tn, name):
    r, k = x.shape
    n = w.shape[2]
    tm = ROW_TILE
    res_spec, mod_spec = _res_specs(tn, tpb, nct)
    return pl.pallas_call(
        functools.partial(_mm_res_kernel, g_row=g_row),
        out_shape=jax.ShapeDtypeStruct((r, n), F32),
        grid=(n // tn, r // tm),
        in_specs=[pl.BlockSpec((tm, k), lambda j, m: (m, 0)),
                  pl.BlockSpec((None, k, tn), lambda j, m: (layer, 0, j)),
                  res_spec, mod_spec],
        out_specs=pl.BlockSpec((tm, tn), lambda j, m: (m, j)),
        scratch_shapes=[pltpu.VMEM((k, tn), BF16)],
        compiler_params=_cp(("arbitrary", "arbitrary"), VMEM_LIMIT),
        name=name,
    )(x, w, res, modtab)


def _mm_glu_res(x, w, bias, layer, res, modtab, g_row, tpb, nct, tn, name):
    r, k = x.shape
    half = w.shape[2] // 2
    nb = half // tn
    tm = ROW_TILE
    res_spec, mod_spec = _res_specs(tn, tpb, nct)
    bias2 = bias.reshape(bias.shape[0], 1, 2 * half)
    return pl.pallas_call(
        functools.partial(_mm_glu_res_kernel, g_row=g_row),
        out_shape=jax.ShapeDtypeStruct((r, half), F32),
        grid=(nb, r // tm),
        in_specs=[pl.BlockSpec((tm, k), lambda j, m: (m, 0)),
                  pl.BlockSpec((None, k, tn), lambda j, m: (layer, 0, j)),
                  pl.BlockSpec((None, k, tn), lambda j, m: (layer, 0, nb + j)),
                  pl.BlockSpec((None, 1, tn), lambda j, m: (layer, 0, j)),
                  pl.BlockSpec((None, 1, tn), lambda j, m: (layer, 0, nb + j)),
                  res_spec, mod_spec],
        out_specs=pl.BlockSpec((tm, tn), lambda j, m: (m, j)),
        scratch_shapes=[pltpu.VMEM((k, tn), BF16), pltpu.VMEM((k, tn), BF16)],
        compiler_params=_cp(("arbitrary", "arbitrary"), VMEM_LIMIT),
        name=name,
    )(x, w, w, bias2, bias2, res, modtab)


def _gla_gate_kernel(x_ref, wgl_ref, wup_ref, b_ref, o_ref):
    wh, wl = _split(wgl_ref[...])
    x = x_ref[...]
    gl = _dot(x, wh) + _dot(x, wl)
    z = _dot3(gl, wup_ref[...]) + b_ref[...]
    log_sig = jnp.minimum(z, 0.0) - jnp.log(1.0 + jnp.exp(-jnp.abs(z)))
    o_ref[...] = log_sig * (1.0 / GLA_GATE_TAU)


def _gla_gate(u, w_gl, w_up_bd, b_gate):
    r, d = u.shape
    n = w_up_bd.shape[1]
    tm = 512
    return pl.pallas_call(
        _gla_gate_kernel,
        out_shape=jax.ShapeDtypeStruct((r, n), F32),
        grid=(r // tm,),
        in_specs=[pl.BlockSpec((tm, d), lambda m: (m, 0)),
                  pl.BlockSpec(w_gl.shape, lambda m: (0, 0)),
                  pl.BlockSpec(w_up_bd.shape, lambda m: (0, 0)),
                  pl.BlockSpec((1, n), lambda m: (0, 0))],
        out_specs=pl.BlockSpec((tm, n), lambda m: (m, 0)),
        compiler_params=_cp(("arbitrary",)),
        name="gla_gate",
    )(u, w_gl, w_up_bd, b_gate.reshape(1, n))


def _gla_scan_kernel(tri_ref, q_ref, k_ref, v_ref, g_ref, o_ref, s_ref, *, hk, hv):
    @pl.when(pl.program_id(2) == 0)
    def _():
        s_ref[...] = jnp.zeros_like(s_ref)

    tri = tri_ref[...]
    ones = jnp.ones((GLA_CHUNK, 128), BF16)
    scale = hk ** -0.5
    tn_dims = (((0,), (0,)), ((), ()))
    nt_dims = (((1,), (1,)), ((), ()))
    for h in range(GLA_HEADS):
        q = q_ref[:, h * hk:(h + 1) * hk]
        k = k_ref[:, h * hk:(h + 1) * hk]
        v = v_ref[:, h * hv:(h + 1) * hv].astype(BF16)
        g = g_ref[:, h * hk:(h + 1) * hk]
        gh, gl = _split(g)
        b = _dot(tri, gh) + _dot(tri, gl)
        tot = jnp.sum(g, axis=0, keepdims=True)
        tot_col = (lax.dot_general(gh, ones, tn_dims, preferred_element_type=F32)
                   + lax.dot_general(gl, ones, tn_dims, preferred_element_type=F32))
        qd = (q * scale * jnp.exp(b)).astype(BF16)
        kd = (k * jnp.exp(-b)).astype(BF16)
        kt = (k * jnp.exp(tot - b)).astype(BF16)
        scores = jnp.where(tri > 0, lax.dot_general(qd, kd, nt_dims, preferred_element_type=F32), 0.0)
        s = s_ref[h]
        o = _dot(scores.astype(BF16), v) + _dot(qd, s.astype(BF16))
        o_ref[:, h * hv:(h + 1) * hv] = o
        decay = jnp.exp(tot_col)
        decay = jnp.concatenate([decay] * (hv // 128), axis=1)
        s_ref[h] = decay * s + lax.dot_general(kt, v, tn_dims, preferred_element_type=F32)


def _gla_scan(proj, log_a, nctx_chunks, dk, dv):
    bsz, length, _ = proj.shape
    nchunks = length // GLA_CHUNK
    hk, hv = dk // GLA_HEADS, dv // GLA_HEADS
    idx = jnp.arange(GLA_CHUNK)
    tri = jnp.stack([idx[:, None] >= idx[None, :], idx[:, None] <= idx[None, :]]).astype(BF16)

    def cidx(d, s):
        back = jnp.where(s < nctx_chunks, nctx_chunks - 1 - s, nchunks - 1 - (s - nctx_chunks))
        return jnp.where(d == 0, s, back)

    return pl.pallas_call(
        functools.partial(_gla_scan_kernel, hk=hk, hv=hv),
        out_shape=jax.ShapeDtypeStruct((2, bsz, length, dv), F32),
        grid=(bsz, 2, nchunks),
        in_specs=[
            pl.BlockSpec((None, GLA_CHUNK, GLA_CHUNK), lambda b, d, s: (d, 0, 0)),
            pl.BlockSpec((None, GLA_CHUNK, dk), lambda b, d, s: (b, cidx(d, s), 0)),
            pl.BlockSpec((None, GLA_CHUNK, dk), lambda b, d, s: (b, cidx(d, s), 1)),
            pl.BlockSpec((None, GLA_CHUNK, dv), lambda b, d, s: (b, cidx(d, s), (2 * dk) // dv)),
            pl.BlockSpec((None, GLA_CHUNK, dk), lambda b, d, s: (b, cidx(d, s), d)),
        ],
        out_specs=pl.BlockSpec((None, None, GLA_CHUNK, dv), lambda b, d, s: (d, b, cidx(d, s), 0)),
        scratch_shapes=[pltpu.VMEM((GLA_HEADS, hk, hv), F32)],
        compiler_params=_cp(("arbitrary", "arbitrary", "arbitrary"), VMEM_LIMIT),
        name="gla_scan",
    )(tri, proj, proj, proj, log_a)


def _gla_finish_kernel(of_ref, ob_ref, r_ref, gh_ref, o_ref, *, hv):
    for h in range(GLA_HEADS):
        sl = slice(h * hv, (h + 1) * hv)
        o = of_ref[:, sl] + ob_ref[:, sl]
        ms = jnp.mean(o * o, axis=-1, keepdims=True)
        y = o * lax.rsqrt(ms + EPS) * gh_ref[...]
        o_ref[:, sl] = (y * _silu(r_ref[:, sl])).astype(o_ref.dtype)


def _gla_finish(o2, proj, g_head, dk, dv):
    _, bsz, length, _ = o2.shape
    hv = dv // GLA_HEADS
    tm = ROW_TILE
    return pl.pallas_call(
        functools.partial(_gla_finish_kernel, hv=hv),
        out_shape=jax.ShapeDtypeStruct((bsz, length, dv), BF16),
        grid=(bsz, length // tm),
        in_specs=[
            pl.BlockSpec((None, None, tm, dv), lambda b, t: (0, b, t, 0)),
            pl.BlockSpec((None, None, tm, dv), lambda b, t: (1, b, t, 0)),
            pl.BlockSpec((None, tm, dv), lambda b, t: (b, t, (2 * dk + dv) // dv)),
            pl.BlockSpec((1, hv), lambda b, t: (0, 0)),
        ],
        out_specs=pl.BlockSpec((None, tm, dv), lambda b, t: (b, t, 0)),
        compiler_params=_cp(("arbitrary", "arbitrary")),
        name="gla_finish",
    )(o2, o2, proj, g_head.reshape(1, hv))


def _gla_layer(h, modtab, nw, w_in, layer, w_gate_up, b_gate, g_head, w_out, nctx):
    bsz, length, d = h.shape
    dk = w_gate_up.shape[2]
    dv = w_out.shape[1]
    nmain = 2 * dk + 2 * dv
    tpb = length // ROW_TILE
    nct = nctx // ROW_TILE
    u = _normmod(h, nw, modtab, SH1, SC1, nct, BF16).reshape(bsz * length, d)
    proj = _mm_plain(u, w_in, layer, nmain, 512, 1024, F32, "gla_in_proj")
    rank = GLA_GATE_RANK
    w_up_bd = jnp.zeros((2 * rank, 2 * dk), F32)
    w_up_bd = w_up_bd.at[:rank, :dk].set(w_gate_up[0]).at[rank:, dk:].set(w_gate_up[1])
    log_a = _gla_gate(u, w_in[layer, :, nmain:], w_up_bd, b_gate.reshape(2 * dk))
    proj3 = proj.reshape(bsz, length, nmain)
    o2 = _gla_scan(proj3, log_a.reshape(bsz, length, 2 * dk), nctx // GLA_CHUNK, dk, dv)
    gated = _gla_finish(o2, proj3, g_head, dk, dv).reshape(bsz * length, dv)
    out = _mm_res(gated, w_out, layer, h.reshape(bsz * length, d), modtab, G1, tpb, nct, 512, "gla_out_proj")
    return out.reshape(bsz, length, d)


S5_LANES_PER_BATCH = 96
_AN1, _AN2, _AP1, _AP2, _BB1, _BB2, _CC1, _CC2, _PQ, _PK_ROWS = 0, 32, 64, 96, 128, 144, 160, 176, 192, 200


def _s5_params(a_re, a_im, log_dt, b_re, b_im, c_re, c_im):
    t_len = S5_CHUNK
    dt = jnp.exp(log_dt)[..., None]
    mag = jnp.exp(a_re * dt)
    ab_re = mag * jnp.cos(a_im * dt)
    ab_im = mag * jnp.sin(a_im * dt)
    den = a_re * a_re + a_im * a_im
    f_re = ((ab_re - 1.0) * a_re + ab_im * a_im) / den
    f_im = (ab_im * a_re - (ab_re - 1.0) * a_im) / den
    bb_re = f_re[..., None] * b_re - f_im[..., None] * b_im
    bb_im = f_re[..., None] * b_im + f_im[..., None] * b_re

    def cpow(n):
        nn = n.astype(F32)[:, None, :, None]
        m = jnp.exp((a_re * dt)[:, :, None, :] * nn)
        th = (a_im * dt)[:, :, None, :] * nn
        return m * jnp.cos(th), m * jnp.sin(th)

    steps = jnp.arange(t_len)
    order = jnp.stack([steps, t_len - 1 - steps])
    anr, ani = cpow(-order)
    apr, api = cpow(order)
    atr, ati = cpow(jnp.full((2, 1), t_len))
    cat = lambda u, v: jnp.concatenate([u, v], axis=-1)
    bbr_t, bbi_t = bb_re.transpose(0, 1, 3, 2), bb_im.transpose(0, 1, 3, 2)
    zero = jnp.zeros((2, a_re.shape[1], 5, 2 * a_re.shape[2]), F32)
    return jnp.concatenate([
        cat(anr, anr), cat(-ani, ani),
        cat(apr, api), cat(api, apr),
        cat(bbr_t, bbi_t), cat(bbi_t, bbr_t),
        cat(c_re, -c_re), cat(-c_im, -c_im),
        cat(atr, atr), cat(-ati, ati), cat(ati, -ati), zero], axis=2)


def _s5_in_kernel(hl_ref, hc_ref, nw_ref, mod_ref, o_ref, *, bsz, qb):
    nw = nw_ref[...]
    blocks = []
    for b in range(bsz):
        lat = _normmod_rows(hl_ref[b], nw, mod_ref[b, 1, SC1:SC1 + 1, :], mod_ref[b, 1, SH1:SH1 + 1, :])
        ctx = _normmod_rows(hc_ref[b], nw, mod_ref[b, 0, SC1:SC1 + 1, :], mod_ref[b, 0, SH1:SH1 + 1, :])
        pad = jnp.zeros((qb - lat.shape[0] - ctx.shape[0], lat.shape[1]), F32)
        blocks += [lat, ctx, pad]
    o_ref[...] = jnp.concatenate(blocks, axis=0).T.astype(o_ref.dtype)


def _s5_in(h, nw, modtab, nctx):
    bsz, length, d = h.shape
    t_len = S5_CHUNK
    qb = S5_LANES_PER_BATCH
    nctx_chunks = nctx // t_len
    return pl.pallas_call(
        functools.partial(_s5_in_kernel, bsz=bsz, qb=qb),
        out_shape=jax.ShapeDtypeStruct((d, t_len * bsz * qb), BF16),
        grid=(t_len,),
        in_specs=[
            pl.BlockSpec((bsz, None, GRID_W, d), lambda s: (0, nctx // GRID_W + s, 0, 0)),
            pl.BlockSpec((bsz, nctx_chunks, d), lambda s: (0, 0, s)),
            pl.BlockSpec((1, d), lambda s: (0, 0)),
            pl.BlockSpec(modtab.shape, lambda s: (0, 0, 0, 0)),
        ],
        out_specs=pl.BlockSpec((d, bsz * qb), lambda s: (0, s)),
        compiler_params=_cp(("arbitrary",), VMEM_LIMIT),
        name="s5_in",
    )(h.reshape(bsz, length // GRID_W, GRID_W, d), h.reshape(bsz, length // t_len, t_len * d),
      nw.reshape(1, d), modtab)


def _tile_rows(x, reps):
    return jnp.concatenate([x] * reps, axis=0)


def _repeat_rows(x, reps):
    return jnp.concatenate([jnp.broadcast_to(x[i:i + 1, :], (reps, x.shape[1])) for i in range(x.shape[0])],
                           axis=0)


def _s5_kernel(ut_ref, pk_ref, y_ref, sv_ref, ss_ref, w_ref, r2_ref, acc_ref, *, bsz, qb, orders):
    gb = ut_ref.shape[0]
    t_len = S5_CHUNK
    k = S5_GROUP * t_len
    q = ut_ref.shape[3]
    row_s = lax.broadcasted_iota(jnp.int32, (k, k), 0) % t_len
    col_t = lax.broadcasted_iota(jnp.int32, (k, k), 1) % t_len
    masks = (row_s <= col_t, row_s >= col_t)
    w_ref[...] = jnp.zeros_like(w_ref)
    for g in range(gb):
        u = ut_ref[g].reshape(k, q).T
        for d in range(2):
            i = 2 * g + d
            pk = pk_ref[d, g]
            l2 = (_tile_rows(pk[_AN1:_AN1 + t_len], S5_GROUP) * _repeat_rows(pk[_BB1:_BB1 + S5_GROUP], t_len)
                  + _tile_rows(pk[_AN2:_AN2 + t_len], S5_GROUP) * _repeat_rows(pk[_BB2:_BB2 + S5_GROUP], t_len))
            r2t = (_tile_rows(pk[_AP1:_AP1 + t_len], S5_GROUP) * _repeat_rows(pk[_CC1:_CC1 + S5_GROUP], t_len)
                   + _tile_rows(pk[_AP2:_AP2 + t_len], S5_GROUP) * _repeat_rows(pk[_CC2:_CC2 + S5_GROUP], t_len))
            r2 = r2t.T
            r2_ref[i] = r2.astype(BF16)
            toep = jnp.where(masks[d], _dot3(l2, r2), 0.0).astype(BF16)
            y_local = _dot(u, toep)
            if d == 0:
                acc_ref[g] = y_local
            else:
                acc_ref[g] = acc_ref[g] + y_local
            v = _dot(u, l2.astype(BF16))
            vs = pltpu.roll(v, v.shape[1] // 2, axis=1)
            pt, qt, qts = pk[_PQ:_PQ + 1], pk[_PQ + 1:_PQ + 2], pk[_PQ + 2:_PQ + 3]
            sv_ref[i] = pt * v + qt * vs
            ss_ref[i] = pt * vs + qts * v
    coef = []
    for g in range(gb):
        for d in range(2):
            pk = pk_ref[d, g]
            coef.append(tuple(jnp.broadcast_to(pk[_PQ + r:_PQ + r + 1], (bsz, pk.shape[1])) for r in range(3)))
    zero = jnp.zeros((bsz, pk_ref.shape[3]), F32)
    state = [(zero, zero)] * (2 * gb)
    nsteps = len(orders[0])
    for step in range(nsteps):
        for i in range(2 * gb):
            rows = pl.ds(orders[i % 2][step], bsz, stride=qb)
            w, ws = state[i]
            w_ref[i, rows, :] = w
            if step + 1 < nsteps:
                pt, qt, qts = coef[i]
                state[i] = (pt * w + qt * ws + sv_ref[i, rows, :], pt * ws + qts * w + ss_ref[i, rows, :])
    for g in range(gb):
        y = acc_ref[g]
        for d in range(2):
            i = 2 * g + d
            y = y + _dot(w_ref[i].astype(BF16), r2_ref[i])
        y_ref[g] = y.T.reshape(S5_GROUP, t_len, q)


def _s5_scan(ut, pk, bsz, nctx_chunks, nlat_chunks):
    ngroups, _, t_len, q = ut.shape
    np2 = pk.shape[3]
    gb = S5_GROUP_BLOCK
    qb = S5_LANES_PER_BATCH
    lat = list(range(nlat_chunks))
    ctx = list(range(nlat_chunks, nlat_chunks + nctx_chunks))
    orders = (ctx + lat, ctx[::-1] + lat[::-1])
    blk = (gb, S5_GROUP, t_len, q)
    return pl.pallas_call(
        functools.partial(_s5_kernel, bsz=bsz, qb=qb, orders=orders),
        out_shape=jax.ShapeDtypeStruct(ut.shape, F32),
        grid=(ngroups // gb,),
        in_specs=[pl.BlockSpec(blk, lambda g: (g, 0, 0, 0)),
                  pl.BlockSpec((2, gb, _PK_ROWS, np2), lambda g: (0, g, 0, 0))],
        out_specs=pl.BlockSpec(blk, lambda g: (g, 0, 0, 0)),
        scratch_shapes=[pltpu.VMEM((2 * gb, q, np2), F32), pltpu.VMEM((2 * gb, q, np2), F32),
                        pltpu.VMEM((2 * gb, q, np2), F32), pltpu.VMEM((2 * gb, np2, S5_GROUP * t_len), BF16),
                        pltpu.VMEM((gb, q, S5_GROUP * t_len), F32)],
        compiler_params=_cp(("arbitrary",), VMEM_LIMIT),
        name="s5_scan",
    )(ut, pk)


def _s5_out_kernel(yt_ref, hl_ref, hc_ref, nw_ref, mod_ref, d_ref, ol_ref, oc_ref, *, bsz, qb):
    y = yt_ref[...].T
    nw = nw_ref[...]
    nlat, nc = hl_ref.shape[1], hc_ref.shape[1]
    c0 = 0.7978845608028654

    def act(yv, hv, region, b):
        u = _normmod_rows(hv, nw, mod_ref[b, region, SC1:SC1 + 1, :], mod_ref[b, region, SH1:SH1 + 1, :])
        z = yv + u * d_ref[...]
        return (0.5 * z * (1.0 + jnp.tanh(c0 * (z + 0.044715 * (z * z * z))))).astype(ol_ref.dtype)

    for b in range(bsz):
        ol_ref[b] = act(y[b * qb:b * qb + nlat], hl_ref[b], 1, b)
        oc_ref[b] = act(y[b * qb + nlat:b * qb + nlat + nc], hc_ref[b], 0, b)


def _s5_out(yt, h, nw, modtab, d_skip, nctx):
    bsz, length, d = h.shape
    t_len = S5_CHUNK
    qb = S5_LANES_PER_BATCH
    nctx_chunks = nctx // t_len
    seq = length - nctx
    lat_spec = pl.BlockSpec((bsz, None, GRID_W, d), lambda s: (0, nctx // GRID_W + s, 0, 0))
    ctx_spec = pl.BlockSpec((bsz, nctx_chunks, d), lambda s: (0, 0, s))
    act_l, act_c = pl.pallas_call(
        functools.partial(_s5_out_kernel, bsz=bsz, qb=qb),
        out_shape=(jax.ShapeDtypeStruct((bsz, seq // GRID_W, GRID_W, d), BF16),
                   jax.ShapeDtypeStruct((bsz, nctx_chunks, t_len * d), BF16)),
        grid=(t_len,),
        in_specs=[pl.BlockSpec((d, bsz * qb), lambda s: (0, s)),
                  lat_spec, ctx_spec,
                  pl.BlockSpec((1, d), lambda s: (0, 0)),
                  pl.BlockSpec(modtab.shape, lambda s: (0, 0, 0, 0)),
                  pl.BlockSpec((1, d), lambda s: (0, 0))],
        out_specs=(pl.BlockSpec((bsz, None, GRID_W, d), lambda s: (0, s, 0, 0)), ctx_spec),
        compiler_params=_cp(("arbitrary",), VMEM_LIMIT),
        name="s5_out",
    )(yt, h.reshape(bsz, length // GRID_W, GRID_W, d), h.reshape(bsz, length // t_len, t_len * d),
      nw.reshape(1, d), modtab, d_skip.reshape(1, d))
    return act_l.reshape(bsz, seq, d), act_c.reshape(bsz, nctx, d)


def _s5_layer(h, modtab, nw, a_re, a_im, log_dt, b_re, b_im, c_re, c_im, d_skip, w_glu, b_glu, layer, nctx,
              last):
    bsz, length, d = h.shape
    seq = length - nctx
    t_len = S5_CHUNK
    assert seq // GRID_W == t_len and nctx % t_len == 0 and nctx % GRID_W == 0
    assert seq // t_len + nctx // t_len <= S5_LANES_PER_BATCH
    ngroups = d // S5_GROUP
    ut = _s5_in(h, nw, modtab, nctx)
    pk = _s5_params(a_re, a_im, log_dt, b_re, b_im, c_re, c_im)
    lanes = ut.shape[1] // t_len
    yt = _s5_scan(ut.reshape(ngroups, S5_GROUP, t_len, lanes), pk, bsz, nctx // t_len, GRID_W)
    act_l, act_c = _s5_out(yt.reshape(d, t_len * lanes), h, nw, modtab, d_skip, nctx)
    if last:
        act, h, nct = act_l, h[:, nctx:], 0
    else:
        act, nct = jnp.concatenate([act_c, act_l], axis=1), nctx // ROW_TILE
    rows = h.shape[0] * h.shape[1]
    out = _mm_glu_res(act.reshape(rows, d), w_glu, b_glu, layer, h.reshape(rows, d), modtab, G1,
                      h.shape[1] // ROW_TILE, nct, 512, "s5_glu")
    return out.reshape(h.shape)


def _dense_ffn(h, modtab, nw, w13, w2, layer, nct):
    bsz, length, d = h.shape
    rows = bsz * length
    v = _normmod(h, nw, modtab, SH2, SC2, nct, BF16).reshape(rows, d)
    act = _mm_swiglu(v, w13, layer, 512, 512, "ffn_w13")
    out = _mm_res(act, w2, layer, h.reshape(rows, d), modtab, G2, length // ROW_TILE, nct, 512, "ffn_w2")
    return out.reshape(h.shape)


def _moe_gather_kernel(src_ref, nxt_ref, v_hbm, o_ref, buf_ref, sem):
    t = pl.program_id(0)
    slot = t % 2

    def request(idx_ref, s):
        def issue(r, carry):
            pltpu.make_async_copy(v_hbm.at[pl.ds(idx_ref[0, r], 1)], buf_ref.at[s, pl.ds(r, 1)],
                                  sem.at[s]).start()
            return carry
        lax.fori_loop(0, MOE_TILE, issue, 0, unroll=8)

    @pl.when(t == 0)
    def _():
        request(src_ref, 0)

    @pl.when(t + 1 < pl.num_programs(0))
    def _():
        request(nxt_ref, 1 - slot)

    pltpu.make_async_copy(v_hbm.at[pl.ds(0, MOE_TILE)], buf_ref.at[slot], sem.at[slot]).wait()
    o_ref[...] = buf_ref[slot].astype(o_ref.dtype)


def _moe_gather(src, v):
    ntiles = src.shape[0]
    d = v.shape[1]
    idx_block = (None, 1, MOE_TILE)
    return pl.pallas_call(
        _moe_gather_kernel,
        out_shape=jax.ShapeDtypeStruct((ntiles * MOE_TILE, d), BF16),
        grid=(ntiles,),
        in_specs=[pl.BlockSpec(idx_block, lambda t: (t, 0, 0), memory_space=pltpu.SMEM),
                  pl.BlockSpec(idx_block, lambda t: (jnp.minimum(t + 1, ntiles - 1), 0, 0),
                               memory_space=pltpu.SMEM),
                  pl.BlockSpec(memory_space=pl.ANY)],
        out_specs=pl.BlockSpec((MOE_TILE, d), lambda t: (t, 0)),
        scratch_shapes=[pltpu.VMEM((2, MOE_TILE, d), F32), pltpu.SemaphoreType.DMA((2,))],
        compiler_params=_cp(("arbitrary",)),
        name="moe_gather",
    )(src, src, v)


def _moe_fetch_weights(te_ref, first_ref, nxt_ref, w_hbm, stage_ref, sem, run_ref, wbf_refs, col_blocks, layer):
    j, t = pl.program_id(0), pl.program_id(1)
    nsweeps = pl.num_programs(0)
    tn = wbf_refs[0].shape[1]

    def copies(e, sweep, slot):
        return [pltpu.make_async_copy(w_hbm.at[layer, e, :, pl.ds(pl.multiple_of(cb(sweep) * tn, tn), tn)],
                                      stage_ref.at[slot, i], sem.at[slot, i])
                for i, cb in enumerate(col_blocks)]

    @pl.when(jnp.logical_and(j == 0, t == 0))
    def _():
        run_ref[0] = 0
        for c in copies(te_ref[0], 0, 0):
            c.start()

    @pl.when(first_ref[t] == 1)
    def _():
        run = run_ref[0]
        slot = run % 2
        for c in copies(te_ref[t], j, slot):
            c.wait()
        for i, wbf_ref in enumerate(wbf_refs):
            wbf_ref[...] = stage_ref[slot, i].astype(BF16)
        sweep_done = nxt_ref[t] < 0
        next_e = jnp.where(sweep_done, te_ref[0], nxt_ref[t])
        next_sweep = jnp.where(sweep_done, j + 1, j)

        @pl.when(next_sweep < nsweeps)
        def _():
            for c in copies(next_e, next_sweep, 1 - slot):
                c.start()
        run_ref[0] = run + 1


def _moe_swiglu_kernel(te_ref, nv_ref, first_ref, nxt_ref, x_ref, w_hbm, o_ref, wabf_ref, wbbf_ref, stage_ref, sem,
                       run_ref, *, layer):
    nb = pl.num_programs(0)
    _moe_fetch_weights(te_ref, first_ref, nxt_ref, w_hbm, stage_ref, sem, run_ref, [wabf_ref, wbbf_ref],
                       [lambda sweep: sweep, lambda sweep: nb + sweep], layer)

    in_use = pl.program_id(1) < nv_ref[0]

    @pl.when(in_use)
    def _():
        x = x_ref[...]
        a = _dot(x, wabf_ref[...])
        b = _dot(x, wbbf_ref[...])
        o_ref[...] = (_silu(a) * b).astype(o_ref.dtype)

    @pl.when(jnp.logical_not(in_use))
    def _():
        o_ref[...] = jnp.zeros_like(o_ref)


def _moe_mm_kernel(te_ref, nv_ref, first_ref, nxt_ref, x_ref, w_hbm, o_ref, wbf_ref, stage_ref, sem, run_ref, *,
                   layer):
    _moe_fetch_weights(te_ref, first_ref, nxt_ref, w_hbm, stage_ref, sem, run_ref, [wbf_ref],
                       [lambda sweep: sweep], layer)

    in_use = pl.program_id(1) < nv_ref[0]

    @pl.when(in_use)
    def _():
        o_ref[...] = _dot(x_ref[...], wbf_ref[...])

    @pl.when(jnp.logical_not(in_use))
    def _():
        o_ref[...] = jnp.zeros_like(o_ref)


def _moe_swiglu(route, xs, w13, layer, tn):
    rows, k = xs.shape
    half = w13.shape[3] // 2
    nb = half // tn
    tm = MOE_TILE
    return pl.pallas_call(
        functools.partial(_moe_swiglu_kernel, layer=layer),
        out_shape=jax.ShapeDtypeStruct((rows, half), BF16),
        grid_spec=pltpu.PrefetchScalarGridSpec(
            num_scalar_prefetch=4,
            grid=(nb, rows // tm),
            in_specs=[pl.BlockSpec((tm, k), lambda j, t, *_: (t, 0)),
                      pl.BlockSpec(memory_space=pl.ANY)],
            out_specs=pl.BlockSpec((tm, tn), lambda j, t, *_: (t, j)),
            scratch_shapes=[pltpu.VMEM((k, tn), BF16), pltpu.VMEM((k, tn), BF16),
                            pltpu.VMEM((2, 2, k, tn), F32), pltpu.SemaphoreType.DMA((2, 2)),
                            pltpu.SMEM((1,), jnp.int32)]),
        compiler_params=_cp(("arbitrary", "arbitrary"), VMEM_LIMIT),
        name="moe_w13",
    )(*route, xs, w13)


def _moe_mm(route, xs, w2, layer, tn):
    rows, k = xs.shape
    n = w2.shape[3]
    tm = MOE_TILE
    return pl.pallas_call(
        functools.partial(_moe_mm_kernel, layer=layer),
        out_shape=jax.ShapeDtypeStruct((rows, n), F32),
        grid_spec=pltpu.PrefetchScalarGridSpec(
            num_scalar_prefetch=4,
            grid=(n // tn, rows // tm),
            in_specs=[pl.BlockSpec((tm, k), lambda j, t, *_: (t, 0)),
                      pl.BlockSpec(memory_space=pl.ANY)],
            out_specs=pl.BlockSpec((tm, tn), lambda j, t, *_: (t, j)),
            scratch_shapes=[pltpu.VMEM((k, tn), BF16), pltpu.VMEM((2, 1, k, tn), F32),
                            pltpu.SemaphoreType.DMA((2, 1)), pltpu.SMEM((1,), jnp.int32)]),
        compiler_params=_cp(("arbitrary", "arbitrary"), VMEM_LIMIT),
        name="moe_w2",
    )(*route, xs, w2)


def _moe_combine_kernel(pos_ref, nxt_ref, gate_ref, y_hbm, res_ref, mod_ref, o_ref, buf_ref, sem, *, g_row):
    m = pl.program_id(0)
    slot = m % 2

    def request(idx_ref, s):
        for k in range(TOP_K):
            def issue(r, carry, k=k):
                pltpu.make_async_copy(y_hbm.at[pl.ds(idx_ref[k, r], 1)], buf_ref.at[s, k, pl.ds(r, 1)],
                                      sem.at[s]).start()
                return carry
            lax.fori_loop(0, ROW_TILE, issue, 0, unroll=8)

    @pl.when(m == 0)
    def _():
        request(pos_ref, 0)

    @pl.when(m + 1 < pl.num_programs(0))
    def _():
        request(nxt_ref, 1 - slot)

    for k in range(TOP_K):
        pltpu.make_async_copy(y_hbm.at[pl.ds(0, ROW_TILE)], buf_ref.at[slot, k], sem.at[slot]).wait()
    p = gate_ref[...]
    mix = p[:, 0:1] * buf_ref[slot, 0] + p[:, 1:2] * buf_ref[slot, 1]
    o_ref[...] = res_ref[...] + mod_ref[g_row:g_row + 1, :] * mix


def _moe_combine(pos, gates, ys, res, modtab, g_row, tpb, nct):
    n, d = res.shape
    tm = ROW_TILE
    ntiles = n // tm
    return pl.pallas_call(
        functools.partial(_moe_combine_kernel, g_row=g_row),
        out_shape=jax.ShapeDtypeStruct((n, d), F32),
        grid=(ntiles,),
        in_specs=[pl.BlockSpec((None, TOP_K, tm), lambda m: (m, 0, 0), memory_space=pltpu.SMEM),
                  pl.BlockSpec((None, TOP_K, tm), lambda m: (jnp.minimum(m + 1, ntiles - 1), 0, 0),
                               memory_space=pltpu.SMEM),
                  pl.BlockSpec((tm, ROUTER_LANES), lambda m: (m, 0)),
                  pl.BlockSpec(memory_space=pl.ANY),
                  pl.BlockSpec((tm, d), lambda m: (m, 0)),
                  pl.BlockSpec((None, None, 6, d), lambda m: (m // tpb, jnp.where(m % tpb >= nct, 1, 0), 0, 0))],
        out_specs=pl.BlockSpec((tm, d), lambda m: (m, 0)),
        scratch_shapes=[pltpu.VMEM((2, TOP_K, tm, d), F32), pltpu.SemaphoreType.DMA((2,))],
        compiler_params=_cp(("arbitrary",), VMEM_LIMIT),
        name="moe_combine",
    )(pos, pos, gates, ys, res, modtab)


def _moe_route(idx2):
    n = idx2.shape[0]
    flat_e = idx2.reshape(-1)
    onehot = (flat_e[:, None] == jnp.arange(N_EXPERTS, dtype=jnp.int32)[None, :]).astype(jnp.int32)
    csum = jnp.cumsum(onehot, axis=0)
    rank = jnp.sum((csum - onehot) * onehot, axis=1)
    counts = csum[-1]
    ntile = (counts + MOE_TILE - 1) // MOE_TILE
    tile_end = jnp.cumsum(ntile)
    tile_start = tile_end - ntile
    pos = tile_start[flat_e] * MOE_TILE + rank
    tiles = (TOP_K * n) // MOE_TILE + N_EXPERTS
    src = jnp.zeros((tiles * MOE_TILE,), jnp.int32).at[pos].set(jnp.arange(TOP_K * n, dtype=jnp.int32) // TOP_K)
    nv = tile_end[-1:]
    tile_ids = jnp.arange(tiles, dtype=jnp.int32)
    tids = jnp.minimum(tile_ids, nv[0] - 1)
    te = jnp.sum((tile_end[None, :] <= tids[:, None]).astype(jnp.int32), axis=1)
    first = jnp.logical_and(tile_ids < nv[0], jnp.logical_or(tile_ids == 0, te != jnp.roll(te, 1)))
    run_end = tile_end[te]
    nxt = jnp.where(run_end < nv[0], te[jnp.minimum(run_end, tiles - 1)], -1)
    route = (te, nv.astype(jnp.int32), first.astype(jnp.int32), nxt.astype(jnp.int32))
    return pos.astype(jnp.int32), src.reshape(tiles, 1, MOE_TILE), route


def _moe_ffn(h, modtab, nw, w_router, w13, w2, layer, nct):
    bsz, length, d = h.shape
    n = bsz * length
    v, idx, gates = _normmod_router(h, nw, modtab, w_router, SH2, SC2, nct)
    idx2 = idx.reshape(n, ROUTER_LANES)[:, :TOP_K]
    pos, src, route = _moe_route(idx2)
    xs = _moe_gather(src, v.reshape(n, d))
    act = _moe_swiglu(route, xs, w13, layer, 512)
    ys = _moe_mm(route, act, w2, layer, 512)
    pos3 = pos.reshape(n // ROW_TILE, ROW_TILE, TOP_K).transpose(0, 2, 1)
    out = _moe_combine(pos3, gates.reshape(n, ROUTER_LANES), ys, h.reshape(n, d), modtab, G2,
                       length // ROW_TILE, nct)
    return out.reshape(h.shape)


def kernel(x, c, ctx, c_ctx, w_mod, b_mod, norm1, norm2, norm_f, gla_w_in, gla_w_gate_up, gla_b_gate,
           gla_g_head, gla_w_out, ffn_w13, ffn_w2, s5_a_re, s5_a_im, s5_log_dt, s5_b_re, s5_b_im, s5_c_re,
           s5_c_im, s5_d, s5_w_glu, s5_b_glu, moe_w_router, moe_w13, moe_w2):
    bsz, seq, d = x.shape
    nctx = ctx.shape[1]
    depth = w_mod.shape[0]
    assert bsz + 1 <= 8 and seq % ROW_TILE == 0 and nctx % ROW_TILE == 0 and seq % GRID_W == 0
    nct = nctx // ROW_TILE

    c8 = jnp.concatenate([c, c_ctx[None, :], jnp.zeros((8 - bsz - 1, d), F32)], axis=0)
    mods = _modulation(c8, w_mod, b_mod).reshape(depth, 8, 6, d)
    h = jnp.concatenate([ctx, x], axis=1)
    for i in range(depth):
        last = i == depth - 1
        j = i // 2
        modtab = jnp.stack([jnp.broadcast_to(mods[i, bsz], (bsz, 6, d)), mods[i, :bsz]], axis=1)
        if i % 2 == 0:
            h = _gla_layer(h, modtab, norm1[i], gla_w_in, j, gla_w_gate_up[j], gla_b_gate[j], gla_g_head[j],
                           gla_w_out, nctx)
        else:
            h = _s5_layer(h, modtab, norm1[i], s5_a_re[j], s5_a_im[j], s5_log_dt[j], s5_b_re[j], s5_b_im[j],
                          s5_c_re[j], s5_c_im[j], s5_d[j], s5_w_glu, s5_b_glu, j, nctx, last)
        ffn_nct = 0 if last else nct
        if i % 2 == 0:
            h = _dense_ffn(h, modtab, norm2[i], ffn_w13, ffn_w2, j, ffn_nct)
        else:
            h = _moe_ffn(h, modtab, norm2[i], moe_w_router[j], moe_w13, moe_w2, j, ffn_nct)
    if h.shape[1] != seq:
        h = h[:, nctx:]
    return _final_norm(h, norm_f)
```

```python
import functools

import jax
import jax.numpy as jnp
from jax import lax
from jax.experimental import pallas as pl
from jax.experimental.pallas import tpu as pltpu

F32 = jnp.float32
BF16 = jnp.bfloat16

EPS = 1e-6
GRID_W = 64

GLA_HEADS = 4
GLA_GATE_RANK = 16
GLA_GATE_TAU = 16.0
GLA_CHUNK = 64

S5_GROUP = 16
S5_CHUNK = 32
S5_GROUP_BLOCK = 4

N_EXPERTS = 8
TOP_K = 2
ROUTER_LANES = 128

ROW_TILE = 256
MOE_TILE = 256
VMEM_LIMIT = 56 * 1024 * 1024

SH1, SC1, G1, SH2, SC2, G2 = range(6)


def _cp(sem, vmem=None):
    return pltpu.CompilerParams(dimension_semantics=sem, vmem_limit_bytes=vmem)


def _dot(a, b):
    return jnp.dot(a, b, preferred_element_type=F32)


def _split(x):
    hi = x.astype(BF16)
    lo = (x - hi.astype(F32)).astype(BF16)
    return hi, lo


def _dot3(a, b):
    ah, al = _split(a)
    bh, bl = _split(b)
    return _dot(ah, bh) + _dot(ah, bl) + _dot(al, bh)


def _sigmoid(x):
    return 1.0 / (1.0 + jnp.exp(-x))


def _silu(x):
    return x * _sigmoid(x)


def _mod_kernel(c_ref, w_ref, b_ref, o_ref):
    x = _silu(c_ref[...]).astype(BF16)
    o_ref[...] = _dot(x, w_ref[...].astype(BF16)) + b_ref[...]


def _modulation(c8, w_mod, b_mod):
    depth, d, n = w_mod.shape
    tn = 1024
    return pl.pallas_call(
        _mod_kernel,
        out_shape=jax.ShapeDtypeStruct((depth, 8, n), F32),
        grid=(depth, n // tn),
        in_specs=[
            pl.BlockSpec((8, d), lambda i, j: (0, 0)),
            pl.BlockSpec((None, d, tn), lambda i, j: (i, 0, j)),
            pl.BlockSpec((None, 1, tn), lambda i, j: (i, 0, j)),
        ],
        out_specs=pl.BlockSpec((None, 8, tn), lambda i, j: (i, 0, j)),
        compiler_params=_cp(("arbitrary", "arbitrary"), VMEM_LIMIT),
        name="modulation",
    )(c8, w_mod, b_mod.reshape(depth, 1, n))


def _normmod_rows(x, nw, scale, shift):
    ms = jnp.mean(x * x, axis=-1, keepdims=True)
    return x * lax.rsqrt(ms + EPS) * nw * (1.0 + scale) + shift


def _normmod_body(x_ref, nw_ref, mod_ref, sh_row, sc_row):
    return _normmod_rows(x_ref[...], nw_ref[...], mod_ref[sc_row:sc_row + 1, :], mod_ref[sh_row:sh_row + 1, :])


def _normmod_kernel(x_ref, nw_ref, mod_ref, o_ref, *, sh_row, sc_row):
    o_ref[...] = _normmod_body(x_ref, nw_ref, mod_ref, sh_row, sc_row).astype(o_ref.dtype)


def _normmod_router_kernel(x_ref, nw_ref, mod_ref, wr_ref, o_ref, idx_ref, gate_ref, *, sh_row, sc_row):
    y = _normmod_body(x_ref, nw_ref, mod_ref, sh_row, sc_row)
    o_ref[...] = y
    logits = _dot3(y, wr_ref[...])
    lane = lax.broadcasted_iota(jnp.int32, logits.shape, 1)
    lane_f = lane.astype(F32)
    neg = jnp.float32(-jnp.inf)
    l1 = jnp.where(lane < N_EXPERTS, logits, neg)
    m1 = jnp.max(l1, axis=-1, keepdims=True)
    i1 = jnp.min(jnp.where(l1 == m1, lane_f, float(ROUTER_LANES)), axis=-1, keepdims=True)
    l2 = jnp.where(lane_f == i1, neg, l1)
    m2 = jnp.max(l2, axis=-1, keepdims=True)
    i2 = jnp.min(jnp.where(l2 == m2, lane_f, float(ROUTER_LANES)), axis=-1, keepdims=True)
    e = jnp.exp(m2 - m1)
    g1 = 1.0 / (1.0 + e)
    g2 = e * g1
    idx_ref[...] = jnp.where(lane == 0, i1, jnp.where(lane == 1, i2, 0.0)).astype(jnp.int32)
    gate_ref[...] = jnp.where(lane == 0, g1, jnp.where(lane == 1, g2, 0.0))


def _row_specs(d, nct):
    x_spec = pl.BlockSpec((None, ROW_TILE, d), lambda b, t: (b, t, 0))
    mod_spec = pl.BlockSpec((None, None, 6, d), lambda b, t: (b, jnp.where(t >= nct, 1, 0), 0, 0))
    return x_spec, mod_spec


def _normmod(h, nw, modtab, sh_row, sc_row, nct, out_dtype):
    bsz, length, d = h.shape
    x_spec, mod_spec = _row_specs(d, nct)
    return pl.pallas_call(
        functools.partial(_normmod_kernel, sh_row=sh_row, sc_row=sc_row),
        out_shape=jax.ShapeDtypeStruct(h.shape, out_dtype),
        grid=(bsz, length // ROW_TILE),
        in_specs=[x_spec, pl.BlockSpec((1, d), lambda b, t: (0, 0)), mod_spec],
        out_specs=x_spec,
        compiler_params=_cp(("arbitrary", "arbitrary")),
        name="normmod",
    )(h, nw.reshape(1, d), modtab)


def _normmod_router(h, nw, modtab, w_router, sh_row, sc_row, nct):
    bsz, length, d = h.shape
    x_spec, mod_spec = _row_specs(d, nct)
    wr = jnp.pad(w_router, ((0, 0), (0, ROUTER_LANES - w_router.shape[1])))
    lane_spec = pl.BlockSpec((None, ROW_TILE, ROUTER_LANES), lambda b, t: (b, t, 0))
    return pl.pallas_call(
        functools.partial(_normmod_router_kernel, sh_row=sh_row, sc_row=sc_row),
        out_shape=(
            jax.ShapeDtypeStruct(h.shape, F32),
            jax.ShapeDtypeStruct((bsz, length, ROUTER_LANES), jnp.int32),
            jax.ShapeDtypeStruct((bsz, length, ROUTER_LANES), F32),
        ),
        grid=(bsz, length // ROW_TILE),
        in_specs=[x_spec, pl.BlockSpec((1, d), lambda b, t: (0, 0)), mod_spec,
                  pl.BlockSpec((d, ROUTER_LANES), lambda b, t: (0, 0))],
        out_specs=(x_spec, lane_spec, lane_spec),
        compiler_params=_cp(("arbitrary", "arbitrary")),
        name="normmod_router",
    )(h, nw.reshape(1, d), modtab, wr)


def _rms_kernel(x_ref, nw_ref, o_ref):
    x = x_ref[...]
    ms = jnp.mean(x * x, axis=-1, keepdims=True)
    o_ref[...] = x * lax.rsqrt(ms + EPS) * nw_ref[...]


def _final_norm(h, nw):
    bsz, length, d = h.shape
    x_spec = pl.BlockSpec((None, ROW_TILE, d), lambda b, t: (b, t, 0))
    return pl.pallas_call(
        _rms_kernel,
        out_shape=jax.ShapeDtypeStruct(h.shape, F32),
        grid=(bsz, length // ROW_TILE),
        in_specs=[x_spec, pl.BlockSpec((1, d), lambda b, t: (0, 0))],
        out_specs=x_spec,
        compiler_params=_cp(("arbitrary", "arbitrary")),
        name="final_norm",
    )(h, nw.reshape(1, d))


def _cast_weights(pairs):
    @pl.when(pl.program_id(1) == 0)
    def _():
        for w_ref, wbf_ref in pairs:
            wbf_ref[...] = w_ref[...].astype(BF16)


def _mm_plain_kernel(x_ref, w_ref, o_ref, wbf_ref):
    _cast_weights([(w_ref, wbf_ref)])
    o_ref[...] = _dot(x_ref[...], wbf_ref[...]).astype(o_ref.dtype)


def _mm_swiglu_kernel(x_ref, wa_ref, wb_ref, o_ref, wabf_ref, wbbf_ref):
    _cast_weights([(wa_ref, wabf_ref), (wb_ref, wbbf_ref)])
    x = x_ref[...]
    a = _dot(x, wabf_ref[...])
    b = _dot(x, wbbf_ref[...])
    o_ref[...] = (_silu(a) * b).astype(o_ref.dtype)


def _mm_res_kernel(x_ref, w_ref, res_ref, mod_ref, o_ref, wbf_ref, *, g_row):
    _cast_weights([(w_ref, wbf_ref)])
    acc = _dot(x_ref[...], wbf_ref[...])
    o_ref[...] = res_ref[...] + mod_ref[g_row:g_row + 1, :] * acc


def _mm_glu_res_kernel(x_ref, wa_ref, wb_ref, ba_ref, bb_ref, res_ref, mod_ref, o_ref, wabf_ref, wbbf_ref,
                       *, g_row):
    _cast_weights([(wa_ref, wabf_ref), (wb_ref, wbbf_ref)])
    x = x_ref[...]
    z1 = _dot(x, wabf_ref[...]) + ba_ref[...]
    z2 = _dot(x, wbbf_ref[...]) + bb_ref[...]
    o_ref[...] = res_ref[...] + mod_ref[g_row:g_row + 1, :] * (z1 * _sigmoid(z2))


def _mm_plain(x, w, layer, n, tm, tn, out_dtype, name):
    r, k = x.shape
    assert r % tm == 0 and n % tn == 0
    return pl.pallas_call(
        _mm_plain_kernel,
        out_shape=jax.ShapeDtypeStruct((r, n), out_dtype),
        grid=(n // tn, r // tm),
        in_specs=[pl.BlockSpec((tm, k), lambda j, m: (m, 0)),
                  pl.BlockSpec((None, k, tn), lambda j, m: (layer, 0, j))],
        out_specs=pl.BlockSpec((tm, tn), lambda j, m: (m, j)),
        scratch_shapes=[pltpu.VMEM((k, tn), BF16)],
        compiler_params=_cp(("arbitrary", "arbitrary"), VMEM_LIMIT),
        name=name,
    )(x, w)


def _mm_swiglu(x, w13, layer, tm, tn, name):
    r, k = x.shape
    half = w13.shape[2] // 2
    nb = half // tn
    assert r % tm == 0 and half % tn == 0
    return pl.pallas_call(
        _mm_swiglu_kernel,
        out_shape=jax.ShapeDtypeStruct((r, half), BF16),
        grid=(nb, r // tm),
        in_specs=[pl.BlockSpec((tm, k), lambda j, m: (m, 0)),
                  pl.BlockSpec((None, k, tn), lambda j, m: (layer, 0, j)),
                  pl.BlockSpec((None, k, tn), lambda j, m: (layer, 0, nb + j))],
        out_specs=pl.BlockSpec((tm, tn), lambda j, m: (m, j)),
        scratch_shapes=[pltpu.VMEM((k, tn), BF16), pltpu.VMEM((k, tn), BF16)],
        compiler_params=_cp(("arbitrary", "arbitrary"), VMEM_LIMIT),
        name=name,
    )(x, w13, w13)


def _res_specs(tn, tpb, nct):
    res_spec = pl.BlockSpec((ROW_TILE, tn), lambda j, m: (m, j))
    mod_spec = pl.BlockSpec((None, None, 6, tn),
                            lambda j, m: (m // tpb, jnp.where(m % tpb >= nct, 1, 0), 0, j))
    return res_spec, mod_spec


def _mm_res(x, w, layer, res, modtab, g_row, tpb, nct, tn, name):
    r, k = x.shape
    n = w.shape[2]
    tm = ROW_TILE
    res_spec, mod_spec = _res_specs(tn, tpb, nct)
    return pl.pallas_call(
        functools.partial(_mm_res_kernel, g_row=g_row),
        out_shape=jax.ShapeDtypeStruct((r, n), F32),
        grid=(n // tn, r // tm),
        in_specs=[pl.BlockSpec((tm, k), lambda j, m: (m, 0)),
                  pl.BlockSpec((None, k, tn), lambda j, m: (layer, 0, j)),
                  res_spec, mod_spec],
        out_specs=pl.BlockSpec((tm, tn), lambda j, m: (m, j)),
        scratch_shapes=[pltpu.VMEM((k, tn), BF16)],
        compiler_params=_cp(("arbitrary", "arbitrary"), VMEM_LIMIT),
        name=name,
    )(x, w, res, modtab)


def _mm_glu_res(x, w, bias, layer, res, modtab, g_row, tpb, nct, tn, name):
    r, k = x.shape
    half = w.shape[2] // 2
    nb = half // tn
    tm = ROW_TILE
    res_spec, mod_spec = _res_specs(tn, tpb, nct)
    bias2 = bias.reshape(bias.shape[0], 1, 2 * half)
    return pl.pallas_call(
        functools.partial(_mm_glu_res_kernel, g_row=g_row),
        out_shape=jax.ShapeDtypeStruct((r, half), F32),
        grid=(nb, r // tm),
        in_specs=[pl.BlockSpec((tm, k), lambda j, m: (m, 0)),
                  pl.BlockSpec((None, k, tn), lambda j, m: (layer, 0, j)),
                  pl.BlockSpec((None, k, tn), lambda j, m: (layer, 0, nb + j)),
                  pl.BlockSpec((None, 1, tn), lambda j, m: (layer, 0, j)),
                  pl.BlockSpec((None, 1, tn), lambda j, m: (layer, 0, nb + j)),
                  res_spec, mod_spec],
        out_specs=pl.BlockSpec((tm, tn), lambda j, m: (m, j)),
        scratch_shapes=[pltpu.VMEM((k, tn), BF16), pltpu.VMEM((k, tn), BF16)],
        compiler_params=_cp(("arbitrary", "arbitrary"), VMEM_LIMIT),
        name=name,
    )(x, w, w, bias2, bias2, res, modtab)


def _gla_gate_kernel(x_ref, wgl_ref, wup_ref, b_ref, o_ref):
    wh, wl = _split(wgl_ref[...])
    x = x_ref[...]
    gl = _dot(x, wh) + _dot(x, wl)
    z = _dot3(gl, wup_ref[...]) + b_ref[...]
    log_sig = jnp.minimum(z, 0.0) - jnp.log(1.0 + jnp.exp(-jnp.abs(z)))
    o_ref[...] = log_sig * (1.0 / GLA_GATE_TAU)


def _gla_gate(u, w_gl, w_up_bd, b_gate):
    r, d = u.shape
    n = w_up_bd.shape[1]
    tm = 512
    return pl.pallas_call(
        _gla_gate_kernel,
        out_shape=jax.ShapeDtypeStruct((r, n), F32),
        grid=(r // tm,),
        in_specs=[pl.BlockSpec((tm, d), lambda m: (m, 0)),
                  pl.BlockSpec(w_gl.shape, lambda m: (0, 0)),
                  pl.BlockSpec(w_up_bd.shape, lambda m: (0, 0)),
                  pl.BlockSpec((1, n), lambda m: (0, 0))],
        out_specs=pl.BlockSpec((tm, n), lambda m: (m, 0)),
        compiler_params=_cp(("arbitrary",)),
        name="gla_gate",
    )(u, w_gl, w_up_bd, b_gate.reshape(1, n))


def _gla_scan_kernel(tri_ref, qf_ref, kf_ref, vf_ref, gf_ref, qb_ref, kb_ref, vb_ref, gb_ref, of_ref, ob_ref, s_ref,
                     *, hk, hv):
    @pl.when(pl.program_id(1) == 0)
    def _():
        s_ref[...] = jnp.zeros_like(s_ref)

    ones = jnp.ones((GLA_CHUNK, 128), BF16)
    scale = hk ** -0.5
    tn_dims = (((0,), (0,)), ((), ()))
    nt_dims = (((1,), (1,)), ((), ()))
    streams = ((qf_ref, kf_ref, vf_ref, gf_ref, of_ref), (qb_ref, kb_ref, vb_ref, gb_ref, ob_ref))
    chains = [(d, h) for d in range(2) for h in range(GLA_HEADS)]
    decays = []
    for d, h in chains:
        g = streams[d][3][:, h * hk:(h + 1) * hk]
        gh, gl = _split(g)
        tri = tri_ref[d]
        b = _dot(tri, gh) + _dot(tri, gl)
        tot = jnp.sum(g, axis=0, keepdims=True)
        tot_col = (lax.dot_general(gh, ones, tn_dims, preferred_element_type=F32)
                   + lax.dot_general(gl, ones, tn_dims, preferred_element_type=F32))
        decays.append((b, tot, tot_col))
    operands = []
    for (d, h), (b, tot, tot_col) in zip(chains, decays):
        q = streams[d][0][:, h * hk:(h + 1) * hk]
        k = streams[d][1][:, h * hk:(h + 1) * hk]
        qd = (q * scale * jnp.exp(b)).astype(BF16)
        kd = (k * jnp.exp(-b)).astype(BF16)
        kt = (k * jnp.exp(tot - b)).astype(BF16)
        scores = jnp.where(tri_ref[d] > 0, lax.dot_general(qd, kd, nt_dims, preferred_element_type=F32), 0.0)
        operands.append((qd, kt, scores.astype(BF16)))
    for (d, h), (qd, kt, scores) in zip(chains, operands):
        v = streams[d][2][:, h * hv:(h + 1) * hv].astype(BF16)
        streams[d][4][:, h * hv:(h + 1) * hv] = _dot(scores, v) + _dot(qd, s_ref[d, h].astype(BF16))
    for (d, h), (b, tot, tot_col), (qd, kt, scores) in zip(chains, decays, operands):
        v = streams[d][2][:, h * hv:(h + 1) * hv].astype(BF16)
        decay = jnp.concatenate([jnp.exp(tot_col)] * (hv // 128), axis=1)
        s_ref[d, h] = decay * s_ref[d, h] + lax.dot_general(kt, v, tn_dims, preferred_element_type=F32)


def _gla_scan(proj, log_a, nctx_chunks, dk, dv):
    bsz, length, _ = proj.shape
    nchunks = length // GLA_CHUNK
    hk, hv = dk // GLA_HEADS, dv // GLA_HEADS
    idx = jnp.arange(GLA_CHUNK)
    tri = jnp.stack([idx[:, None] >= idx[None, :], idx[:, None] <= idx[None, :]]).astype(BF16)

    def back(s):
        return jnp.where(s < nctx_chunks, nctx_chunks - 1 - s, nchunks - 1 - (s - nctx_chunks))

    def specs(chunk, direction):
        return [pl.BlockSpec((None, GLA_CHUNK, dk), lambda b, s: (b, chunk(s), 0)),
                pl.BlockSpec((None, GLA_CHUNK, dk), lambda b, s: (b, chunk(s), 1)),
                pl.BlockSpec((None, GLA_CHUNK, dv), lambda b, s: (b, chunk(s), (2 * dk) // dv)),
                pl.BlockSpec((None, GLA_CHUNK, dk), lambda b, s: (b, chunk(s), direction))]

    out = jax.ShapeDtypeStruct((bsz, length, dv), F32)
    return pl.pallas_call(
        functools.partial(_gla_scan_kernel, hk=hk, hv=hv),
        out_shape=(out, out),
        grid=(bsz, nchunks),
        in_specs=[pl.BlockSpec((2, GLA_CHUNK, GLA_CHUNK), lambda b, s: (0, 0, 0))]
        + specs(lambda s: s, 0) + specs(back, 1),
        out_specs=(pl.BlockSpec((None, GLA_CHUNK, dv), lambda b, s: (b, s, 0)),
                   pl.BlockSpec((None, GLA_CHUNK, dv), lambda b, s: (b, back(s), 0))),
        scratch_shapes=[pltpu.VMEM((2, GLA_HEADS, hk, hv), F32)],
        compiler_params=_cp(("arbitrary", "arbitrary"), VMEM_LIMIT),
        name="gla_scan",
    )(tri, proj, proj, proj, log_a, proj, proj, proj, log_a)


def _gla_finish_kernel(of_ref, ob_ref, r_ref, gh_ref, o_ref, *, hv):
    for h in range(GLA_HEADS):
        sl = slice(h * hv, (h + 1) * hv)
        o = of_ref[:, sl] + ob_ref[:, sl]
        ms = jnp.mean(o * o, axis=-1, keepdims=True)
        y = o * lax.rsqrt(ms + EPS) * gh_ref[...]
        o_ref[:, sl] = (y * _silu(r_ref[:, sl])).astype(o_ref.dtype)


def _gla_finish(o_fwd, o_bwd, proj, g_head, dk, dv):
    bsz, length, _ = o_fwd.shape
    hv = dv // GLA_HEADS
    tm = ROW_TILE
    o_spec = pl.BlockSpec((None, tm, dv), lambda b, t: (b, t, 0))
    return pl.pallas_call(
        functools.partial(_gla_finish_kernel, hv=hv),
        out_shape=jax.ShapeDtypeStruct((bsz, length, dv), BF16),
        grid=(bsz, length // tm),
        in_specs=[
            o_spec, o_spec,
            pl.BlockSpec((None, tm, dv), lambda b, t: (b, t, (2 * dk + dv) // dv)),
            pl.BlockSpec((1, hv), lambda b, t: (0, 0)),
        ],
        out_specs=o_spec,
        compiler_params=_cp(("arbitrary", "arbitrary")),
        name="gla_finish",
    )(o_fwd, o_bwd, proj, g_head.reshape(1, hv))


def _gla_layer(h, modtab, nw, w_in, layer, w_gate_up, b_gate, g_head, w_out, nctx):
    bsz, length, d = h.shape
    dk = w_gate_up.shape[2]
    dv = w_out.shape[1]
    nmain = 2 * dk + 2 * dv
    tpb = length // ROW_TILE
    nct = nctx // ROW_TILE
    u = _normmod(h, nw, modtab, SH1, SC1, nct, BF16).reshape(bsz * length, d)
    proj = _mm_plain(u, w_in, layer, nmain, 512, 1024, F32, "gla_in_proj")
    rank = GLA_GATE_RANK
    w_up_bd = jnp.zeros((2 * rank, 2 * dk), F32)
    w_up_bd = w_up_bd.at[:rank, :dk].set(w_gate_up[0]).at[rank:, dk:].set(w_gate_up[1])
    log_a = _gla_gate(u, w_in[layer, :, nmain:], w_up_bd, b_gate.reshape(2 * dk))
    proj3 = proj.reshape(bsz, length, nmain)
    o_fwd, o_bwd = _gla_scan(proj3, log_a.reshape(bsz, length, 2 * dk), nctx // GLA_CHUNK, dk, dv)
    gated = _gla_finish(o_fwd, o_bwd, proj3, g_head, dk, dv).reshape(bsz * length, dv)
    out = _mm_res(gated, w_out, layer, h.reshape(bsz * length, d), modtab, G1, tpb, nct, 512, "gla_out_proj")
    return out.reshape(bsz, length, d)


S5_LANES_PER_BATCH = 96
_AN1, _AN2, _AP1, _AP2, _BB1, _BB2, _CC1, _CC2, _PQ, _PK_ROWS = 0, 32, 64, 96, 128, 144, 160, 176, 192, 200


def _s5_params(a_re, a_im, log_dt, b_re, b_im, c_re, c_im):
    t_len = S5_CHUNK
    dt = jnp.exp(log_dt)[..., None]
    mag = jnp.exp(a_re * dt)
    ab_re = mag * jnp.cos(a_im * dt)
    ab_im = mag * jnp.sin(a_im * dt)
    den = a_re * a_re + a_im * a_im
    f_re = ((ab_re - 1.0) * a_re + ab_im * a_im) / den
    f_im = (ab_im * a_re - (ab_re - 1.0) * a_im) / den
    bb_re = f_re[..., None] * b_re - f_im[..., None] * b_im
    bb_im = f_re[..., None] * b_im + f_im[..., None] * b_re

    def cpow(n):
        nn = n.astype(F32)[:, None, :, None]
        m = jnp.exp((a_re * dt)[:, :, None, :] * nn)
        th = (a_im * dt)[:, :, None, :] * nn
        return m * jnp.cos(th), m * jnp.sin(th)

    steps = jnp.arange(t_len)
    order = jnp.stack([steps, t_len - 1 - steps])
    anr, ani = cpow(-order)
    apr, api = cpow(order)
    atr, ati = cpow(jnp.full((2, 1), t_len))
    cat = lambda u, v: jnp.concatenate([u, v], axis=-1)
    bbr_t, bbi_t = bb_re.transpose(0, 1, 3, 2), bb_im.transpose(0, 1, 3, 2)
    zero = jnp.zeros((2, a_re.shape[1], 5, 2 * a_re.shape[2]), F32)
    return jnp.concatenate([
        cat(anr, anr), cat(-ani, ani),
        cat(apr, api), cat(api, apr),
        cat(bbr_t, bbi_t), cat(bbi_t, bbr_t),
        cat(c_re, -c_re), cat(-c_im, -c_im),
        cat(atr, atr), cat(-ati, ati), cat(ati, -ati), zero], axis=2)


def _s5_in_kernel(hl_ref, hc_ref, nw_ref, mod_ref, o_ref, *, bsz, qb):
    nw = nw_ref[...]
    blocks = []
    for b in range(bsz):
        lat = _normmod_rows(hl_ref[b], nw, mod_ref[b, 1, SC1:SC1 + 1, :], mod_ref[b, 1, SH1:SH1 + 1, :])
        ctx = _normmod_rows(hc_ref[b], nw, mod_ref[b, 0, SC1:SC1 + 1, :], mod_ref[b, 0, SH1:SH1 + 1, :])
        pad = jnp.zeros((qb - lat.shape[0] - ctx.shape[0], lat.shape[1]), F32)
        blocks += [lat, ctx, pad]
    o_ref[...] = jnp.concatenate(blocks, axis=0).T.astype(o_ref.dtype)


def _s5_in(h, nw, modtab, nctx):
    bsz, length, d = h.shape
    t_len = S5_CHUNK
    qb = S5_LANES_PER_BATCH
    nctx_chunks = nctx // t_len
    return pl.pallas_call(
        functools.partial(_s5_in_kernel, bsz=bsz, qb=qb),
        out_shape=jax.ShapeDtypeStruct((d, t_len * bsz * qb), BF16),
        grid=(t_len,),
        in_specs=[
            pl.BlockSpec((bsz, None, GRID_W, d), lambda s: (0, nctx // GRID_W + s, 0, 0)),
            pl.BlockSpec((bsz, nctx_chunks, d), lambda s: (0, 0, s)),
            pl.BlockSpec((1, d), lambda s: (0, 0)),
            pl.BlockSpec(modtab.shape, lambda s: (0, 0, 0, 0)),
        ],
        out_specs=pl.BlockSpec((d, bsz * qb), lambda s: (0, s)),
        compiler_params=_cp(("arbitrary",), VMEM_LIMIT),
        name="s5_in",
    )(h.reshape(bsz, length // GRID_W, GRID_W, d), h.reshape(bsz, length // t_len, t_len * d),
      nw.reshape(1, d), modtab)


def _tile_rows(x, reps):
    return jnp.concatenate([x] * reps, axis=0)


def _repeat_rows(x, reps):
    return jnp.concatenate([jnp.broadcast_to(x[i:i + 1, :], (reps, x.shape[1])) for i in range(x.shape[0])],
                           axis=0)


def _s5_kernel(ut_ref, pk_ref, y_ref, sv_ref, ss_ref, w_ref, r2_ref, acc_ref, *, bsz, qb, orders):
    gb = ut_ref.shape[0] // S5_GROUP
    t_len = S5_CHUNK
    k = S5_GROUP * t_len
    q = ut_ref.shape[1] // t_len
    row_s = lax.broadcasted_iota(jnp.int32, (k, k), 0) // S5_GROUP
    col_t = lax.broadcasted_iota(jnp.int32, (k, k), 1) // S5_GROUP
    masks = (row_s <= col_t, row_s >= col_t)
    w_ref[...] = jnp.zeros_like(w_ref)
    for g in range(gb):
        x = ut_ref[g * S5_GROUP:(g + 1) * S5_GROUP, :]
        u = jnp.concatenate([x[:, s * q:(s + 1) * q] for s in range(t_len)], axis=0).T
        for d in range(2):
            i = 2 * g + d
            pk = pk_ref[d, g]
            l2 = (_repeat_rows(pk[_AN1:_AN1 + t_len], S5_GROUP) * _tile_rows(pk[_BB1:_BB1 + S5_GROUP], t_len)
                  + _repeat_rows(pk[_AN2:_AN2 + t_len], S5_GROUP) * _tile_rows(pk[_BB2:_BB2 + S5_GROUP], t_len))
            r2t = (_repeat_rows(pk[_AP1:_AP1 + t_len], S5_GROUP) * _tile_rows(pk[_CC1:_CC1 + S5_GROUP], t_len)
                   + _repeat_rows(pk[_AP2:_AP2 + t_len], S5_GROUP) * _tile_rows(pk[_CC2:_CC2 + S5_GROUP], t_len))
            r2 = r2t.T
            r2_ref[i] = r2.astype(BF16)
            toep = jnp.where(masks[d], _dot3(l2, r2), 0.0).astype(BF16)
            y_local = _dot(u, toep)
            if d == 0:
                acc_ref[g] = y_local
            else:
                acc_ref[g] = acc_ref[g] + y_local
            v = _dot(u, l2.astype(BF16))
            vs = pltpu.roll(v, v.shape[1] // 2, axis=1)
            pt, qt, qts = pk[_PQ:_PQ + 1], pk[_PQ + 1:_PQ + 2], pk[_PQ + 2:_PQ + 3]
            sv_ref[i] = pt * v + qt * vs
            ss_ref[i] = pt * vs + qts * v
    coef = []
    for g in range(gb):
        for d in range(2):
            pk = pk_ref[d, g]
            coef.append(tuple(jnp.broadcast_to(pk[_PQ + r:_PQ + r + 1], (bsz, pk.shape[1])) for r in range(3)))
    zero = jnp.zeros((bsz, pk_ref.shape[3]), F32)
    state = [(zero, zero)] * (2 * gb)
    nsteps = len(orders[0])
    for step in range(nsteps):
        for i in range(2 * gb):
            rows = pl.ds(orders[i % 2][step], bsz, stride=qb)
            w, ws = state[i]
            w_ref[i, rows, :] = w
            if step + 1 < nsteps:
                pt, qt, qts = coef[i]
                state[i] = (pt * w + qt * ws + sv_ref[i, rows, :], pt * ws + qts * w + ss_ref[i, rows, :])
    for g in range(gb):
        y = acc_ref[g]
        for d in range(2):
            i = 2 * g + d
            y = y + _dot(w_ref[i].astype(BF16), r2_ref[i])
        yt = y.T
        for t in range(t_len):
            y_ref[g * S5_GROUP:(g + 1) * S5_GROUP, t * q:(t + 1) * q] = yt[t * S5_GROUP:(t + 1) * S5_GROUP, :]


def _s5_scan(ut, pk, bsz, nctx_chunks, nlat_chunks):
    t_len = S5_CHUNK
    q = ut.shape[1] // t_len
    ngroups = ut.shape[0] // S5_GROUP
    np2 = pk.shape[3]
    gb = S5_GROUP_BLOCK
    qb = S5_LANES_PER_BATCH
    lat = list(range(nlat_chunks))
    ctx = list(range(nlat_chunks, nlat_chunks + nctx_chunks))
    orders = (ctx + lat, ctx[::-1] + lat[::-1])
    blk = (gb * S5_GROUP, t_len * q)
    return pl.pallas_call(
        functools.partial(_s5_kernel, bsz=bsz, qb=qb, orders=orders),
        out_shape=jax.ShapeDtypeStruct(ut.shape, F32),
        grid=(ngroups // gb,),
        in_specs=[pl.BlockSpec(blk, lambda g: (g, 0)),
                  pl.BlockSpec((2, gb, _PK_ROWS, np2), lambda g: (0, g, 0, 0))],
        out_specs=pl.BlockSpec(blk, lambda g: (g, 0)),
        scratch_shapes=[pltpu.VMEM((2 * gb, q, np2), F32), pltpu.VMEM((2 * gb, q, np2), F32),
                        pltpu.VMEM((2 * gb, q, np2), F32), pltpu.VMEM((2 * gb, np2, S5_GROUP * t_len), BF16),
                        pltpu.VMEM((gb, q, S5_GROUP * t_len), F32)],
        compiler_params=_cp(("arbitrary",), VMEM_LIMIT),
        name="s5_scan",
    )(ut, pk)


def _s5_out_kernel(yt_ref, hl_ref, hc_ref, nw_ref, mod_ref, d_ref, ol_ref, oc_ref, *, bsz, qb):
    y = yt_ref[...].T
    nw = nw_ref[...]
    nlat, nc = hl_ref.shape[1], hc_ref.shape[1]
    c0 = 0.7978845608028654

    def act(yv, hv, region, b):
        u = _normmod_rows(hv, nw, mod_ref[b, region, SC1:SC1 + 1, :], mod_ref[b, region, SH1:SH1 + 1, :])
        z = yv + u * d_ref[...]
        return (0.5 * z * (1.0 + jnp.tanh(c0 * (z + 0.044715 * (z * z * z))))).astype(ol_ref.dtype)

    for b in range(bsz):
        ol_ref[b] = act(y[b * qb:b * qb + nlat], hl_ref[b], 1, b)
        oc_ref[b] = act(y[b * qb + nlat:b * qb + nlat + nc], hc_ref[b], 0, b)


def _s5_out(yt, h, nw, modtab, d_skip, nctx):
    bsz, length, d = h.shape
    t_len = S5_CHUNK
    qb = S5_LANES_PER_BATCH
    nctx_chunks = nctx // t_len
    seq = length - nctx
    lat_spec = pl.BlockSpec((bsz, None, GRID_W, d), lambda s: (0, nctx // GRID_W + s, 0, 0))
    ctx_spec = pl.BlockSpec((bsz, nctx_chunks, d), lambda s: (0, 0, s))
    act_l, act_c = pl.pallas_call(
        functools.partial(_s5_out_kernel, bsz=bsz, qb=qb),
        out_shape=(jax.ShapeDtypeStruct((bsz, seq // GRID_W, GRID_W, d), BF16),
                   jax.ShapeDtypeStruct((bsz, nctx_chunks, t_len * d), BF16)),
        grid=(t_len,),
        in_specs=[pl.BlockSpec((d, bsz * qb), lambda s: (0, s)),
                  lat_spec, ctx_spec,
                  pl.BlockSpec((1, d), lambda s: (0, 0)),
                  pl.BlockSpec(modtab.shape, lambda s: (0, 0, 0, 0)),
                  pl.BlockSpec((1, d), lambda s: (0, 0))],
        out_specs=(pl.BlockSpec((bsz, None, GRID_W, d), lambda s: (0, s, 0, 0)), ctx_spec),
        compiler_params=_cp(("arbitrary",), VMEM_LIMIT),
        name="s5_out",
    )(yt, h.reshape(bsz, length // GRID_W, GRID_W, d), h.reshape(bsz, length // t_len, t_len * d),
      nw.reshape(1, d), modtab, d_skip.reshape(1, d))
    return act_l.reshape(bsz, seq, d), act_c.reshape(bsz, nctx, d)


def _s5_layer(h, modtab, nw, a_re, a_im, log_dt, b_re, b_im, c_re, c_im, d_skip, w_glu, b_glu, layer, nctx,
              last):
    bsz, length, d = h.shape
    seq = length - nctx
    t_len = S5_CHUNK
    assert seq // GRID_W == t_len and nctx % t_len == 0 and nctx % GRID_W == 0
    assert seq // t_len + nctx // t_len <= S5_LANES_PER_BATCH
    ngroups = d // S5_GROUP
    ut = _s5_in(h, nw, modtab, nctx)
    pk = _s5_params(a_re, a_im, log_dt, b_re, b_im, c_re, c_im)
    yt = _s5_scan(ut, pk, bsz, nctx // t_len, GRID_W)
    act_l, act_c = _s5_out(yt, h, nw, modtab, d_skip, nctx)
    if last:
        act, h, nct = act_l, h[:, nctx:], 0
    else:
        act, nct = jnp.concatenate([act_c, act_l], axis=1), nctx // ROW_TILE
    rows = h.shape[0] * h.shape[1]
    out = _mm_glu_res(act.reshape(rows, d), w_glu, b_glu, layer, h.reshape(rows, d), modtab, G1,
                      h.shape[1] // ROW_TILE, nct, 512, "s5_glu")
    return out.reshape(h.shape)


def _dense_ffn(h, modtab, nw, w13, w2, layer, nct):
    bsz, length, d = h.shape
    rows = bsz * length
    v = _normmod(h, nw, modtab, SH2, SC2, nct, BF16).reshape(rows, d)
    act = _mm_swiglu(v, w13, layer, 512, 512, "ffn_w13")
    out = _mm_res(act, w2, layer, h.reshape(rows, d), modtab, G2, length // ROW_TILE, nct, 512, "ffn_w2")
    return out.reshape(h.shape)


def _moe_gather_kernel(src_ref, nxt_ref, v_hbm, o_ref, buf_ref, sem):
    t = pl.program_id(0)
    slot = t % 2

    def request(idx_ref, s):
        def issue(r, carry):
            pltpu.make_async_copy(v_hbm.at[pl.ds(idx_ref[0, r], 1)], buf_ref.at[s, pl.ds(r, 1)],
                                  sem.at[s]).start()
            return carry
        lax.fori_loop(0, MOE_TILE, issue, 0, unroll=8)

    @pl.when(t == 0)
    def _():
        request(src_ref, 0)

    @pl.when(t + 1 < pl.num_programs(0))
    def _():
        request(nxt_ref, 1 - slot)

    pltpu.make_async_copy(v_hbm.at[pl.ds(0, MOE_TILE)], buf_ref.at[slot], sem.at[slot]).wait()
    o_ref[...] = buf_ref[slot].astype(o_ref.dtype)


def _moe_gather(src, v):
    ntiles = src.shape[0]
    d = v.shape[1]
    idx_block = (None, 1, MOE_TILE)
    return pl.pallas_call(
        _moe_gather_kernel,
        out_shape=jax.ShapeDtypeStruct((ntiles * MOE_TILE, d), BF16),
        grid=(ntiles,),
        in_specs=[pl.BlockSpec(idx_block, lambda t: (t, 0, 0), memory_space=pltpu.SMEM),
                  pl.BlockSpec(idx_block, lambda t: (jnp.minimum(t + 1, ntiles - 1), 0, 0),
                               memory_space=pltpu.SMEM),
                  pl.BlockSpec(memory_space=pl.ANY)],
        out_specs=pl.BlockSpec((MOE_TILE, d), lambda t: (t, 0)),
        scratch_shapes=[pltpu.VMEM((2, MOE_TILE, d), F32), pltpu.SemaphoreType.DMA((2,))],
        compiler_params=_cp(("arbitrary",)),
        name="moe_gather",
    )(src, src, v)


def _moe_fetch_weights(te_ref, first_ref, nxt_ref, w_hbm, stage_ref, sem, run_ref, wbf_refs, col_blocks, layer):
    j, t = pl.program_id(0), pl.program_id(1)
    nsweeps = pl.num_programs(0)
    tn = wbf_refs[0].shape[1]

    def copies(e, sweep, slot):
        return [pltpu.make_async_copy(w_hbm.at[layer, e, :, pl.ds(pl.multiple_of(cb(sweep) * tn, tn), tn)],
                                      stage_ref.at[slot, i], sem.at[slot, i])
                for i, cb in enumerate(col_blocks)]

    @pl.when(jnp.logical_and(j == 0, t == 0))
    def _():
        run_ref[0] = 0
        for c in copies(te_ref[0], 0, 0):
            c.start()

    @pl.when(first_ref[t] == 1)
    def _():
        run = run_ref[0]
        slot = run % 2
        for c in copies(te_ref[t], j, slot):
            c.wait()
        for i, wbf_ref in enumerate(wbf_refs):
            wbf_ref[...] = stage_ref[slot, i].astype(BF16)
        sweep_done = nxt_ref[t] < 0
        next_e = jnp.where(sweep_done, te_ref[0], nxt_ref[t])
        next_sweep = jnp.where(sweep_done, j + 1, j)

        @pl.when(next_sweep < nsweeps)
        def _():
            for c in copies(next_e, next_sweep, 1 - slot):
                c.start()
        run_ref[0] = run + 1


def _moe_swiglu_kernel(te_ref, nv_ref, first_ref, nxt_ref, x_ref, w_hbm, o_ref, wabf_ref, wbbf_ref, stage_ref, sem,
                       run_ref, *, layer):
    nb = pl.num_programs(0)
    _moe_fetch_weights(te_ref, first_ref, nxt_ref, w_hbm, stage_ref, sem, run_ref, [wabf_ref, wbbf_ref],
                       [lambda sweep: sweep, lambda sweep: nb + sweep], layer)

    in_use = pl.program_id(1) < nv_ref[0]

    @pl.when(in_use)
    def _():
        x = x_ref[...]
        a = _dot(x, wabf_ref[...])
        b = _dot(x, wbbf_ref[...])
        o_ref[...] = (_silu(a) * b).astype(o_ref.dtype)

    @pl.when(jnp.logical_not(in_use))
    def _():
        o_ref[...] = jnp.zeros_like(o_ref)


def _moe_mm_kernel(te_ref, nv_ref, first_ref, nxt_ref, x_ref, w_hbm, o_ref, wbf_ref, stage_ref, sem, run_ref, *,
                   layer):
    _moe_fetch_weights(te_ref, first_ref, nxt_ref, w_hbm, stage_ref, sem, run_ref, [wbf_ref],
                       [lambda sweep: sweep], layer)

    in_use = pl.program_id(1) < nv_ref[0]

    @pl.when(in_use)
    def _():
        o_ref[...] = _dot(x_ref[...], wbf_ref[...])

    @pl.when(jnp.logical_not(in_use))
    def _():
        o_ref[...] = jnp.zeros_like(o_ref)


def _moe_swiglu(route, xs, w13, layer, tn):
    rows, k = xs.shape
    half = w13.shape[3] // 2
    nb = half // tn
    tm = MOE_TILE
    return pl.pallas_call(
        functools.partial(_moe_swiglu_kernel, layer=layer),
        out_shape=jax.ShapeDtypeStruct((rows, half), BF16),
        grid_spec=pltpu.PrefetchScalarGridSpec(
            num_scalar_prefetch=4,
            grid=(nb, rows // tm),
            in_specs=[pl.BlockSpec((tm, k), lambda j, t, *_: (t, 0)),
                      pl.BlockSpec(memory_space=pl.ANY)],
            out_specs=pl.BlockSpec((tm, tn), lambda j, t, *_: (t, j)),
            scratch_shapes=[pltpu.VMEM((k, tn), BF16), pltpu.VMEM((k, tn), BF16),
                            pltpu.VMEM((2, 2, k, tn), F32), pltpu.SemaphoreType.DMA((2, 2)),
                            pltpu.SMEM((1,), jnp.int32)]),
        compiler_params=_cp(("arbitrary", "arbitrary"), VMEM_LIMIT),
        name="moe_w13",
    )(*route, xs, w13)


def _moe_mm(route, xs, w2, layer, tn):
    rows, k = xs.shape
    n = w2.shape[3]
    tm = MOE_TILE
    return pl.pallas_call(
        functools.partial(_moe_mm_kernel, layer=layer),
        out_shape=jax.ShapeDtypeStruct((rows, n), F32),
        grid_spec=pltpu.PrefetchScalarGridSpec(
            num_scalar_prefetch=4,
            grid=(n // tn, rows // tm),
            in_specs=[pl.BlockSpec((tm, k), lambda j, t, *_: (t, 0)),
                      pl.BlockSpec(memory_space=pl.ANY)],
            out_specs=pl.BlockSpec((tm, tn), lambda j, t, *_: (t, j)),
            scratch_shapes=[pltpu.VMEM((k, tn), BF16), pltpu.VMEM((2, 1, k, tn), F32),
                            pltpu.SemaphoreType.DMA((2, 1)), pltpu.SMEM((1,), jnp.int32)]),
        compiler_params=_cp(("arbitrary", "arbitrary"), VMEM_LIMIT),
        name="moe_w2",
    )(*route, xs, w2)


def _moe_combine_kernel(pos_ref, nxt_ref, gate_ref, y_hbm, res_ref, mod_ref, o_ref, buf_ref, sem, *, g_row):
    m = pl.program_id(0)
    slot = m % 2

    def request(idx_ref, s):
        for k in range(TOP_K):
            def issue(r, carry, k=k):
                pltpu.make_async_copy(y_hbm.at[pl.ds(idx_ref[k, r], 1)], buf_ref.at[s, k, pl.ds(r, 1)],
                                      sem.at[s]).start()
                return carry
            lax.fori_loop(0, ROW_TILE, issue, 0, unroll=8)

    @pl.when(m == 0)
    def _():
        request(pos_ref, 0)

    @pl.when(m + 1 < pl.num_programs(0))
    def _():
        request(nxt_ref, 1 - slot)

    for k in range(TOP_K):
        pltpu.make_async_copy(y_hbm.at[pl.ds(0, ROW_TILE)], buf_ref.at[slot, k], sem.at[slot]).wait()
    p = gate_ref[...]
    mix = p[:, 0:1] * buf_ref[slot, 0] + p[:, 1:2] * buf_ref[slot, 1]
    o_ref[...] = res_ref[...] + mod_ref[g_row:g_row + 1, :] * mix


def _moe_combine(pos, gates, ys, res, modtab, g_row, tpb, nct):
    n, d = res.shape
    tm = ROW_TILE
    ntiles = n // tm
    return pl.pallas_call(
        functools.partial(_moe_combine_kernel, g_row=g_row),
        out_shape=jax.ShapeDtypeStruct((n, d), F32),
        grid=(ntiles,),
        in_specs=[pl.BlockSpec((None, TOP_K, tm), lambda m: (m, 0, 0), memory_space=pltpu.SMEM),
                  pl.BlockSpec((None, TOP_K, tm), lambda m: (jnp.minimum(m + 1, ntiles - 1), 0, 0),
                               memory_space=pltpu.SMEM),
                  pl.BlockSpec((tm, ROUTER_LANES), lambda m: (m, 0)),
                  pl.BlockSpec(memory_space=pl.ANY),
                  pl.BlockSpec((tm, d), lambda m: (m, 0)),
                  pl.BlockSpec((None, None, 6, d), lambda m: (m // tpb, jnp.where(m % tpb >= nct, 1, 0), 0, 0))],
        out_specs=pl.BlockSpec((tm, d), lambda m: (m, 0)),
        scratch_shapes=[pltpu.VMEM((2, TOP_K, tm, d), F32), pltpu.SemaphoreType.DMA((2,))],
        compiler_params=_cp(("arbitrary",), VMEM_LIMIT),
        name="moe_combine",
    )(pos, pos, gates, ys, res, modtab)


def _moe_route(idx2):
    n = idx2.shape[0]
    flat_e = idx2.reshape(-1)
    onehot = (flat_e[:, None] == jnp.arange(N_EXPERTS, dtype=jnp.int32)[None, :]).astype(jnp.int32)
    csum = jnp.cumsum(onehot, axis=0)
    rank = jnp.sum((csum - onehot) * onehot, axis=1)
    counts = csum[-1]
    ntile = (counts + MOE_TILE - 1) // MOE_TILE
    tile_end = jnp.cumsum(ntile)
    tile_start = tile_end - ntile
    pos = tile_start[flat_e] * MOE_TILE + rank
    tiles = (TOP_K * n) // MOE_TILE + N_EXPERTS
    src = jnp.zeros((tiles * MOE_TILE,), jnp.int32).at[pos].set(jnp.arange(TOP_K * n, dtype=jnp.int32) // TOP_K)
    nv = tile_end[-1:]
    tile_ids = jnp.arange(tiles, dtype=jnp.int32)
    tids = jnp.minimum(tile_ids, nv[0] - 1)
    te = jnp.sum((tile_end[None, :] <= tids[:, None]).astype(jnp.int32), axis=1)
    first = jnp.logical_and(tile_ids < nv[0], jnp.logical_or(tile_ids == 0, te != jnp.roll(te, 1)))
    run_end = tile_end[te]
    nxt = jnp.where(run_end < nv[0], te[jnp.minimum(run_end, tiles - 1)], -1)
    route = (te, nv.astype(jnp.int32), first.astype(jnp.int32), nxt.astype(jnp.int32))
    return pos.astype(jnp.int32), src.reshape(tiles, 1, MOE_TILE), route


def _moe_ffn(h, modtab, nw, w_router, w13, w2, layer, nct):
    bsz, length, d = h.shape
    n = bsz * length
    v, idx, gates = _normmod_router(h, nw, modtab, w_router, SH2, SC2, nct)
    idx2 = idx.reshape(n, ROUTER_LANES)[:, :TOP_K]
    pos, src, route = _moe_route(idx2)
    xs = _moe_gather(src, v.reshape(n, d))
    act = _moe_swiglu(route, xs, w13, layer, 512)
    ys = _moe_mm(route, act, w2, layer, 512)
    pos3 = pos.reshape(n // ROW_TILE, ROW_TILE, TOP_K).transpose(0, 2, 1)
    out = _moe_combine(pos3, gates.reshape(n, ROUTER_LANES), ys, h.reshape(n, d), modtab, G2,
                       length // ROW_TILE, nct)
    return out.reshape(h.shape)


def kernel(x, c, ctx, c_ctx, w_mod, b_mod, norm1, norm2, norm_f, gla_w_in, gla_w_gate_up, gla_b_gate,
           gla_g_head, gla_w_out, ffn_w13, ffn_w2, s5_a_re, s5_a_im, s5_log_dt, s5_b_re, s5_b_im, s5_c_re,
           s5_c_im, s5_d, s5_w_glu, s5_b_glu, moe_w_router, moe_w13, moe_w2):
    bsz, seq, d = x.shape
    nctx = ctx.shape[1]
    depth = w_mod.shape[0]
    assert bsz + 1 <= 8 and seq % ROW_TILE == 0 and nctx % ROW_TILE == 0 and seq % GRID_W == 0
    nct = nctx // ROW_TILE

    c8 = jnp.concatenate([c, c_ctx[None, :], jnp.zeros((8 - bsz - 1, d), F32)], axis=0)
    mods = _modulation(c8, w_mod, b_mod).reshape(depth, 8, 6, d)
    h = jnp.concatenate([ctx, x], axis=1)
    for i in range(depth):
        last = i == depth - 1
        j = i // 2
        modtab = jnp.stack([jnp.broadcast_to(mods[i, bsz], (bsz, 6, d)), mods[i, :bsz]], axis=1)
        if i % 2 == 0:
            h = _gla_layer(h, modtab, norm1[i], gla_w_in, j, gla_w_gate_up[j], gla_b_gate[j], gla_g_head[j],
                           gla_w_out, nctx)
        else:
            h = _s5_layer(h, modtab, norm1[i], s5_a_re[j], s5_a_im[j], s5_log_dt[j], s5_b_re[j], s5_b_im[j],
                          s5_c_re[j], s5_c_im[j], s5_d[j], s5_w_glu, s5_b_glu, j, nctx, last)
        ffn_nct = 0 if last else nct
        if i % 2 == 0:
            h = _dense_ffn(h, modtab, norm2[i], ffn_w13, ffn_w2, j, ffn_nct)
        else:
            h = _moe_ffn(h, modtab, norm2[i], moe_w_router[j], moe_w13, moe_w2, j, ffn_nct)
    if h.shape[1] != seq:
        h = h[:, nctx:]
    return _final_norm(h, norm_f)
```

```python
import functools

import jax
import jax.numpy as jnp
from jax import lax
from jax.experimental import pallas as pl
from jax.experimental.pallas import tpu as pltpu

F32 = jnp.float32
BF16 = jnp.bfloat16

EPS = 1e-6
GRID_W = 64

GLA_HEADS = 4
GLA_GATE_RANK = 16
GLA_GATE_TAU = 16.0
GLA_CHUNK = 64

S5_GROUP = 16
S5_CHUNK = 32
S5_GROUP_BLOCK = 4

N_EXPERTS = 8
TOP_K = 2
ROUTER_LANES = 128

ROW_TILE = 256
MOE_TILE = 256
DMA_ISSUE_UNROLL = 8
VMEM_LIMIT = 56 * 1024 * 1024

SH1, SC1, G1, SH2, SC2, G2 = range(6)


def _cp(sem, vmem=None):
    return pltpu.CompilerParams(dimension_semantics=sem, vmem_limit_bytes=vmem)


def _dot(a, b):
    return jnp.dot(a, b, preferred_element_type=F32)


def _split(x):
    hi = x.astype(BF16)
    lo = (x - hi.astype(F32)).astype(BF16)
    return hi, lo


def _dot3(a, b):
    ah, al = _split(a)
    bh, bl = _split(b)
    return _dot(ah, bh) + _dot(ah, bl) + _dot(al, bh)


def _sigmoid(x):
    return 1.0 / (1.0 + jnp.exp(-x))


def _silu(x):
    return x * _sigmoid(x)


def _mod_kernel(c_ref, w_ref, b_ref, o_ref):
    x = _silu(c_ref[...]).astype(BF16)
    o_ref[...] = _dot(x, w_ref[...].astype(BF16)) + b_ref[...]


def _modulation(c8, w_mod, b_mod):
    depth, d, n = w_mod.shape
    tn = 1024
    return pl.pallas_call(
        _mod_kernel,
        out_shape=jax.ShapeDtypeStruct((depth, 8, n), F32),
        grid=(depth, n // tn),
        in_specs=[
            pl.BlockSpec((8, d), lambda i, j: (0, 0)),
            pl.BlockSpec((None, d, tn), lambda i, j: (i, 0, j)),
            pl.BlockSpec((None, 1, tn), lambda i, j: (i, 0, j)),
        ],
        out_specs=pl.BlockSpec((None, 8, tn), lambda i, j: (i, 0, j)),
        compiler_params=_cp(("arbitrary", "arbitrary"), VMEM_LIMIT),
        name="modulation",
    )(c8, w_mod, b_mod.reshape(depth, 1, n))


def _normmod_rows(x, nw, scale, shift):
    ms = jnp.mean(x * x, axis=-1, keepdims=True)
    return x * lax.rsqrt(ms + EPS) * nw * (1.0 + scale) + shift


def _normmod_body(x_ref, nw_ref, mod_ref, sh_row, sc_row):
    return _normmod_rows(x_ref[...], nw_ref[...], mod_ref[sc_row:sc_row + 1, :], mod_ref[sh_row:sh_row + 1, :])


def _normmod_kernel(x_ref, nw_ref, mod_ref, o_ref, *, sh_row, sc_row):
    o_ref[...] = _normmod_body(x_ref, nw_ref, mod_ref, sh_row, sc_row).astype(o_ref.dtype)


def _normmod_router_kernel(x_ref, nw_ref, mod_ref, wr_ref, o_ref, idx_ref, gate_ref, *, sh_row, sc_row):
    y = _normmod_body(x_ref, nw_ref, mod_ref, sh_row, sc_row)
    o_ref[...] = y
    logits = _dot3(y, wr_ref[...])
    lane = lax.broadcasted_iota(jnp.int32, logits.shape, 1)
    lane_f = lane.astype(F32)
    neg = jnp.float32(-jnp.inf)
    l1 = jnp.where(lane < N_EXPERTS, logits, neg)
    m1 = jnp.max(l1, axis=-1, keepdims=True)
    i1 = jnp.min(jnp.where(l1 == m1, lane_f, float(ROUTER_LANES)), axis=-1, keepdims=True)
    l2 = jnp.where(lane_f == i1, neg, l1)
    m2 = jnp.max(l2, axis=-1, keepdims=True)
    i2 = jnp.min(jnp.where(l2 == m2, lane_f, float(ROUTER_LANES)), axis=-1, keepdims=True)
    e = jnp.exp(m2 - m1)
    g1 = 1.0 / (1.0 + e)
    g2 = e * g1
    idx_ref[...] = jnp.where(lane == 0, i1, jnp.where(lane == 1, i2, 0.0)).astype(jnp.int32)
    gate_ref[...] = jnp.where(lane == 0, g1, jnp.where(lane == 1, g2, 0.0))


def _row_specs(d, nct):
    x_spec = pl.BlockSpec((None, ROW_TILE, d), lambda b, t: (b, t, 0))
    mod_spec = pl.BlockSpec((None, None, 6, d), lambda b, t: (b, jnp.where(t >= nct, 1, 0), 0, 0))
    return x_spec, mod_spec


def _normmod(h, nw, modtab, sh_row, sc_row, nct, out_dtype):
    bsz, length, d = h.shape
    x_spec, mod_spec = _row_specs(d, nct)
    return pl.pallas_call(
        functools.partial(_normmod_kernel, sh_row=sh_row, sc_row=sc_row),
        out_shape=jax.ShapeDtypeStruct(h.shape, out_dtype),
        grid=(bsz, length // ROW_TILE),
        in_specs=[x_spec, pl.BlockSpec((1, d), lambda b, t: (0, 0)), mod_spec],
        out_specs=x_spec,
        compiler_params=_cp(("arbitrary", "arbitrary")),
        name="normmod",
    )(h, nw.reshape(1, d), modtab)


def _normmod_router(h, nw, modtab, w_router, sh_row, sc_row, nct):
    bsz, length, d = h.shape
    x_spec, mod_spec = _row_specs(d, nct)
    wr = jnp.pad(w_router, ((0, 0), (0, ROUTER_LANES - w_router.shape[1])))
    lane_spec = pl.BlockSpec((None, ROW_TILE, ROUTER_LANES), lambda b, t: (b, t, 0))
    return pl.pallas_call(
        functools.partial(_normmod_router_kernel, sh_row=sh_row, sc_row=sc_row),
        out_shape=(
            jax.ShapeDtypeStruct(h.shape, F32),
            jax.ShapeDtypeStruct((bsz, length, ROUTER_LANES), jnp.int32),
            jax.ShapeDtypeStruct((bsz, length, ROUTER_LANES), F32),
        ),
        grid=(bsz, length // ROW_TILE),
        in_specs=[x_spec, pl.BlockSpec((1, d), lambda b, t: (0, 0)), mod_spec,
                  pl.BlockSpec((d, ROUTER_LANES), lambda b, t: (0, 0))],
        out_specs=(x_spec, lane_spec, lane_spec),
        compiler_params=_cp(("arbitrary", "arbitrary")),
        name="normmod_router",
    )(h, nw.reshape(1, d), modtab, wr)


def _rms_kernel(x_ref, nw_ref, o_ref):
    x = x_ref[...]
    ms = jnp.mean(x * x, axis=-1, keepdims=True)
    o_ref[...] = x * lax.rsqrt(ms + EPS) * nw_ref[...]


def _final_norm(h, nw):
    bsz, length, d = h.shape
    x_spec = pl.BlockSpec((None, ROW_TILE, d), lambda b, t: (b, t, 0))
    return pl.pallas_call(
        _rms_kernel,
        out_shape=jax.ShapeDtypeStruct(h.shape, F32),
        grid=(bsz, length // ROW_TILE),
        in_specs=[x_spec, pl.BlockSpec((1, d), lambda b, t: (0, 0))],
        out_specs=x_spec,
        compiler_params=_cp(("arbitrary", "arbitrary")),
        name="final_norm",
    )(h, nw.reshape(1, d))


def _cast_weights(pairs):
    @pl.when(pl.program_id(1) == 0)
    def _():
        for w_ref, wbf_ref in pairs:
            wbf_ref[...] = w_ref[...].astype(BF16)


def _mm_plain_kernel(x_ref, w_ref, o_ref, wbf_ref):
    _cast_weights([(w_ref, wbf_ref)])
    o_ref[...] = _dot(x_ref[...], wbf_ref[...]).astype(o_ref.dtype)


def _mm_swiglu_kernel(x_ref, wa_ref, wb_ref, o_ref, wabf_ref, wbbf_ref):
    _cast_weights([(wa_ref, wabf_ref), (wb_ref, wbbf_ref)])
    x = x_ref[...]
    a = _dot(x, wabf_ref[...])
    b = _dot(x, wbbf_ref[...])
    o_ref[...] = (_silu(a) * b).astype(o_ref.dtype)


def _mm_res_kernel(x_ref, w_ref, res_ref, mod_ref, o_ref, wbf_ref, *, g_row):
    _cast_weights([(w_ref, wbf_ref)])
    acc = _dot(x_ref[...], wbf_ref[...])
    o_ref[...] = res_ref[...] + mod_ref[g_row:g_row + 1, :] * acc


def _mm_glu_res_kernel(x_ref, wa_ref, wb_ref, ba_ref, bb_ref, res_ref, mod_ref, o_ref, wabf_ref, wbbf_ref,
                       *, g_row):
    _cast_weights([(wa_ref, wabf_ref), (wb_ref, wbbf_ref)])
    x = x_ref[...]
    z1 = _dot(x, wabf_ref[...]) + ba_ref[...]
    z2 = _dot(x, wbbf_ref[...]) + bb_ref[...]
    o_ref[...] = res_ref[...] + mod_ref[g_row:g_row + 1, :] * (z1 * _sigmoid(z2))


def _mm_plain(x, w, layer, n, tm, tn, out_dtype, name):
    r, k = x.shape
    assert r % tm == 0 and n % tn == 0
    return pl.pallas_call(
        _mm_plain_kernel,
        out_shape=jax.ShapeDtypeStruct((r, n), out_dtype),
        grid=(n // tn, r // tm),
        in_specs=[pl.BlockSpec((tm, k), lambda j, m: (m, 0)),
                  pl.BlockSpec((None, k, tn), lambda j, m: (layer, 0, j))],
        out_specs=pl.BlockSpec((tm, tn), lambda j, m: (m, j)),
        scratch_shapes=[pltpu.VMEM((k, tn), BF16)],
        compiler_params=_cp(("arbitrary", "arbitrary"), VMEM_LIMIT),
        name=name,
    )(x, w)


def _mm_swiglu(x, w13, layer, tm, tn, name):
    r, k = x.shape
    half = w13.shape[2] // 2
    nb = half // tn
    assert r % tm == 0 and half % tn == 0
    return pl.pallas_call(
        _mm_swiglu_kernel,
        out_shape=jax.ShapeDtypeStruct((r, half), BF16),
        grid=(nb, r // tm),
        in_specs=[pl.BlockSpec((tm, k), lambda j, m: (m, 0)),
                  pl.BlockSpec((None, k, tn), lambda j, m: (layer, 0, j)),
                  pl.BlockSpec((None, k, tn), lambda j, m: (layer, 0, nb + j))],
        out_specs=pl.BlockSpec((tm, tn), lambda j, m: (m, j)),
        scratch_shapes=[pltpu.VMEM((k, tn), BF16), pltpu.VMEM((k, tn), BF16)],
        compiler_params=_cp(("arbitrary", "arbitrary"), VMEM_LIMIT),
        name=name,
    )(x, w13, w13)


def _res_specs(tn, tpb, nct):
    res_spec = pl.BlockSpec((ROW_TILE, tn), lambda j, m: (m, j))
    mod_spec = pl.BlockSpec((None, None, 6, tn),
                            lambda j, m: (m // tpb, jnp.where(m % tpb >= nct, 1, 0), 0, j))
    return res_spec, mod_spec


def _mm_res(x, w, layer, res, modtab, g_row, tpb, nct, tn, name):
    r, k = x.shape
    n = w.shape[2]
    tm = ROW_TILE
    res_spec, mod_spec = _res_specs(tn, tpb, nct)
    return pl.pallas_call(
        functools.partial(_mm_res_kernel, g_row=g_row),
        out_shape=jax.ShapeDtypeStruct((r, n), F32),
        grid=(n // tn, r // tm),
        in_specs=[pl.BlockSpec((tm, k), lambda j, m: (m, 0)),
                  pl.BlockSpec((None, k, tn), lambda j, m: (layer, 0, j)),
                  res_spec, mod_spec],
        out_specs=pl.BlockSpec((tm, tn), lambda j, m: (m, j)),
        scratch_shapes=[pltpu.VMEM((k, tn), BF16)],
        compiler_params=_cp(("arbitrary", "arbitrary"), VMEM_LIMIT),
        name=name,
    )(x, w, res, modtab)


def _mm_glu_res(x, w, bias, layer, res, modtab, g_row, tpb, nct, tn, name):
    r, k = x.shape
    half = w.shape[2] // 2
    nb = half // tn
    tm = ROW_TILE
    res_spec, mod_spec = _res_specs(tn, tpb, nct)
    bias2 = bias.reshape(bias.shape[0], 1, 2 * half)
    return pl.pallas_call(
        functools.partial(_mm_glu_res_kernel, g_row=g_row),
        out_shape=jax.ShapeDtypeStruct((r, half), F32),
        grid=(nb, r // tm),
        in_specs=[pl.BlockSpec((tm, k), lambda j, m: (m, 0)),
                  pl.BlockSpec((None, k, tn), lambda j, m: (layer, 0, j)),
                  pl.BlockSpec((None, k, tn), lambda j, m: (layer, 0, nb + j)),
                  pl.BlockSpec((None, 1, tn), lambda j, m: (layer, 0, j)),
                  pl.BlockSpec((None, 1, tn), lambda j, m: (layer, 0, nb + j)),
                  res_spec, mod_spec],
        out_specs=pl.BlockSpec((tm, tn), lambda j, m: (m, j)),
        scratch_shapes=[pltpu.VMEM((k, tn), BF16), pltpu.VMEM((k, tn), BF16)],
        compiler_params=_cp(("arbitrary", "arbitrary"), VMEM_LIMIT),
        name=name,
    )(x, w, w, bias2, bias2, res, modtab)


def _gla_gate_kernel(x_ref, wgl_ref, wup_ref, b_ref, o_ref):
    wh, wl = _split(wgl_ref[...])
    x = x_ref[...]
    gl = _dot(x, wh) + _dot(x, wl)
    z = _dot3(gl, wup_ref[...]) + b_ref[...]
    log_sig = jnp.minimum(z, 0.0) - jnp.log(1.0 + jnp.exp(-jnp.abs(z)))
    o_ref[...] = log_sig * (1.0 / GLA_GATE_TAU)


def _gla_gate(u, w_gl, w_up_bd, b_gate):
    r, d = u.shape
    n = w_up_bd.shape[1]
    tm = 512
    return pl.pallas_call(
        _gla_gate_kernel,
        out_shape=jax.ShapeDtypeStruct((r, n), F32),
        grid=(r // tm,),
        in_specs=[pl.BlockSpec((tm, d), lambda m: (m, 0)),
                  pl.BlockSpec(w_gl.shape, lambda m: (0, 0)),
                  pl.BlockSpec(w_up_bd.shape, lambda m: (0, 0)),
                  pl.BlockSpec((1, n), lambda m: (0, 0))],
        out_specs=pl.BlockSpec((tm, n), lambda m: (m, 0)),
        compiler_params=_cp(("arbitrary",)),
        name="gla_gate",
    )(u, w_gl, w_up_bd, b_gate.reshape(1, n))


def _gla_scan_kernel(tri_ref, qf_ref, kf_ref, vf_ref, gf_ref, qb_ref, kb_ref, vb_ref, gb_ref, of_ref, ob_ref, s_ref,
                     *, hk, hv):
    @pl.when(pl.program_id(1) == 0)
    def _():
        s_ref[...] = jnp.zeros_like(s_ref)

    ones = jnp.ones((GLA_CHUNK, 128), BF16)
    scale = hk ** -0.5
    tn_dims = (((0,), (0,)), ((), ()))
    nt_dims = (((1,), (1,)), ((), ()))
    streams = ((qf_ref, kf_ref, vf_ref, gf_ref, of_ref), (qb_ref, kb_ref, vb_ref, gb_ref, ob_ref))
    chains = [(d, h) for d in range(2) for h in range(GLA_HEADS)]
    decays = []
    for d, h in chains:
        g = streams[d][3][:, h * hk:(h + 1) * hk]
        gh, gl = _split(g)
        tri = tri_ref[d]
        b = _dot(tri, gh) + _dot(tri, gl)
        tot = jnp.sum(g, axis=0, keepdims=True)
        tot_col = (lax.dot_general(gh, ones, tn_dims, preferred_element_type=F32)
                   + lax.dot_general(gl, ones, tn_dims, preferred_element_type=F32))
        decays.append((b, tot, tot_col))
    operands = []
    for (d, h), (b, tot, tot_col) in zip(chains, decays):
        q = streams[d][0][:, h * hk:(h + 1) * hk]
        k = streams[d][1][:, h * hk:(h + 1) * hk]
        qd = (q * scale * jnp.exp(b)).astype(BF16)
        kd = (k * jnp.exp(-b)).astype(BF16)
        kt = (k * jnp.exp(tot - b)).astype(BF16)
        scores = jnp.where(tri_ref[d] > 0, lax.dot_general(qd, kd, nt_dims, preferred_element_type=F32), 0.0)
        operands.append((qd, kt, scores.astype(BF16)))
    for (d, h), (qd, kt, scores) in zip(chains, operands):
        v = streams[d][2][:, h * hv:(h + 1) * hv].astype(BF16)
        streams[d][4][:, h * hv:(h + 1) * hv] = _dot(scores, v) + _dot(qd, s_ref[d, h].astype(BF16))
    for (d, h), (b, tot, tot_col), (qd, kt, scores) in zip(chains, decays, operands):
        v = streams[d][2][:, h * hv:(h + 1) * hv].astype(BF16)
        decay = jnp.concatenate([jnp.exp(tot_col)] * (hv // 128), axis=1)
        s_ref[d, h] = decay * s_ref[d, h] + lax.dot_general(kt, v, tn_dims, preferred_element_type=F32)


def _gla_scan(proj, log_a, nctx_chunks, dk, dv):
    bsz, length, _ = proj.shape
    nchunks = length // GLA_CHUNK
    hk, hv = dk // GLA_HEADS, dv // GLA_HEADS
    idx = jnp.arange(GLA_CHUNK)
    tri = jnp.stack([idx[:, None] >= idx[None, :], idx[:, None] <= idx[None, :]]).astype(BF16)

    def back(s):
        return jnp.where(s < nctx_chunks, nctx_chunks - 1 - s, nchunks - 1 - (s - nctx_chunks))

    def specs(chunk, direction):
        return [pl.BlockSpec((None, GLA_CHUNK, dk), lambda b, s: (b, chunk(s), 0)),
                pl.BlockSpec((None, GLA_CHUNK, dk), lambda b, s: (b, chunk(s), 1)),
                pl.BlockSpec((None, GLA_CHUNK, dv), lambda b, s: (b, chunk(s), (2 * dk) // dv)),
                pl.BlockSpec((None, GLA_CHUNK, dk), lambda b, s: (b, chunk(s), direction))]

    out = jax.ShapeDtypeStruct((bsz, length, dv), F32)
    return pl.pallas_call(
        functools.partial(_gla_scan_kernel, hk=hk, hv=hv),
        out_shape=(out, out),
        grid=(bsz, nchunks),
        in_specs=[pl.BlockSpec((2, GLA_CHUNK, GLA_CHUNK), lambda b, s: (0, 0, 0))]
        + specs(lambda s: s, 0) + specs(back, 1),
        out_specs=(pl.BlockSpec((None, GLA_CHUNK, dv), lambda b, s: (b, s, 0)),
                   pl.BlockSpec((None, GLA_CHUNK, dv), lambda b, s: (b, back(s), 0))),
        scratch_shapes=[pltpu.VMEM((2, GLA_HEADS, hk, hv), F32)],
        compiler_params=_cp(("arbitrary", "arbitrary"), VMEM_LIMIT),
        name="gla_scan",
    )(tri, proj, proj, proj, log_a, proj, proj, proj, log_a)


def _gla_finish_kernel(of_ref, ob_ref, r_ref, gh_ref, o_ref, *, hv):
    for h in range(GLA_HEADS):
        sl = slice(h * hv, (h + 1) * hv)
        o = of_ref[:, sl] + ob_ref[:, sl]
        ms = jnp.mean(o * o, axis=-1, keepdims=True)
        y = o * lax.rsqrt(ms + EPS) * gh_ref[...]
        o_ref[:, sl] = (y * _silu(r_ref[:, sl])).astype(o_ref.dtype)


def _gla_finish(o_fwd, o_bwd, proj, g_head, dk, dv):
    bsz, length, _ = o_fwd.shape
    hv = dv // GLA_HEADS
    tm = ROW_TILE
    o_spec = pl.BlockSpec((None, tm, dv), lambda b, t: (b, t, 0))
    return pl.pallas_call(
        functools.partial(_gla_finish_kernel, hv=hv),
        out_shape=jax.ShapeDtypeStruct((bsz, length, dv), BF16),
        grid=(bsz, length // tm),
        in_specs=[
            o_spec, o_spec,
            pl.BlockSpec((None, tm, dv), lambda b, t: (b, t, (2 * dk + dv) // dv)),
            pl.BlockSpec((1, hv), lambda b, t: (0, 0)),
        ],
        out_specs=o_spec,
        compiler_params=_cp(("arbitrary", "arbitrary")),
        name="gla_finish",
    )(o_fwd, o_bwd, proj, g_head.reshape(1, hv))


def _gla_layer(h, modtab, nw, w_in, layer, w_gate_up, b_gate, g_head, w_out, nctx):
    bsz, length, d = h.shape
    dk = w_gate_up.shape[2]
    dv = w_out.shape[1]
    nmain = 2 * dk + 2 * dv
    tpb = length // ROW_TILE
    nct = nctx // ROW_TILE
    u = _normmod(h, nw, modtab, SH1, SC1, nct, BF16).reshape(bsz * length, d)
    proj = _mm_plain(u, w_in, layer, nmain, 512, 1024, F32, "gla_in_proj")
    rank = GLA_GATE_RANK
    w_up_bd = jnp.zeros((2 * rank, 2 * dk), F32)
    w_up_bd = w_up_bd.at[:rank, :dk].set(w_gate_up[0]).at[rank:, dk:].set(w_gate_up[1])
    log_a = _gla_gate(u, w_in[layer, :, nmain:], w_up_bd, b_gate.reshape(2 * dk))
    proj3 = proj.reshape(bsz, length, nmain)
    o_fwd, o_bwd = _gla_scan(proj3, log_a.reshape(bsz, length, 2 * dk), nctx // GLA_CHUNK, dk, dv)
    gated = _gla_finish(o_fwd, o_bwd, proj3, g_head, dk, dv).reshape(bsz * length, dv)
    out = _mm_res(gated, w_out, layer, h.reshape(bsz * length, d), modtab, G1, tpb, nct, 512, "gla_out_proj")
    return out.reshape(bsz, length, d)


S5_LANES_PER_BATCH = 96
_AN1, _AN2, _AP1, _AP2, _BB1, _BB2, _CC1, _CC2, _PQ, _PK_ROWS = 0, 32, 64, 96, 128, 144, 160, 176, 192, 200


def _s5_params(a_re, a_im, log_dt, b_re, b_im, c_re, c_im):
    t_len = S5_CHUNK
    dt = jnp.exp(log_dt)[..., None]
    mag = jnp.exp(a_re * dt)
    ab_re = mag * jnp.cos(a_im * dt)
    ab_im = mag * jnp.sin(a_im * dt)
    den = a_re * a_re + a_im * a_im
    f_re = ((ab_re - 1.0) * a_re + ab_im * a_im) / den
    f_im = (ab_im * a_re - (ab_re - 1.0) * a_im) / den
    bb_re = f_re[..., None] * b_re - f_im[..., None] * b_im
    bb_im = f_re[..., None] * b_im + f_im[..., None] * b_re

    def cpow(n):
        nn = n.astype(F32)[:, None, :, None]
        m = jnp.exp((a_re * dt)[:, :, None, :] * nn)
        th = (a_im * dt)[:, :, None, :] * nn
        return m * jnp.cos(th), m * jnp.sin(th)

    steps = jnp.arange(t_len)
    order = jnp.stack([steps, t_len - 1 - steps])
    anr, ani = cpow(-order)
    apr, api = cpow(order)
    atr, ati = cpow(jnp.full((2, 1), t_len))
    cat = lambda u, v: jnp.concatenate([u, v], axis=-1)
    bbr_t, bbi_t = bb_re.transpose(0, 1, 3, 2), bb_im.transpose(0, 1, 3, 2)
    zero = jnp.zeros((2, a_re.shape[1], 5, 2 * a_re.shape[2]), F32)
    return jnp.concatenate([
        cat(anr, anr), cat(-ani, ani),
        cat(apr, api), cat(api, apr),
        cat(bbr_t, bbi_t), cat(bbi_t, bbr_t),
        cat(c_re, -c_re), cat(-c_im, -c_im),
        cat(atr, atr), cat(-ati, ati), cat(ati, -ati), zero], axis=2)


def _s5_in_kernel(hl_ref, hc_ref, nw_ref, mod_ref, o_ref, *, bsz, qb):
    nw = nw_ref[...]
    blocks = []
    for b in range(bsz):
        lat = _normmod_rows(hl_ref[b], nw, mod_ref[b, 1, SC1:SC1 + 1, :], mod_ref[b, 1, SH1:SH1 + 1, :])
        ctx = _normmod_rows(hc_ref[b], nw, mod_ref[b, 0, SC1:SC1 + 1, :], mod_ref[b, 0, SH1:SH1 + 1, :])
        pad = jnp.zeros((qb - lat.shape[0] - ctx.shape[0], lat.shape[1]), F32)
        blocks += [lat, ctx, pad]
    o_ref[...] = jnp.concatenate(blocks, axis=0).T.astype(o_ref.dtype)


def _s5_in(h, nw, modtab, nctx):
    bsz, length, d = h.shape
    t_len = S5_CHUNK
    qb = S5_LANES_PER_BATCH
    nctx_chunks = nctx // t_len
    return pl.pallas_call(
        functools.partial(_s5_in_kernel, bsz=bsz, qb=qb),
        out_shape=jax.ShapeDtypeStruct((d, t_len * bsz * qb), BF16),
        grid=(t_len,),
        in_specs=[
            pl.BlockSpec((bsz, None, GRID_W, d), lambda s: (0, nctx // GRID_W + s, 0, 0)),
            pl.BlockSpec((bsz, nctx_chunks, d), lambda s: (0, 0, s)),
            pl.BlockSpec((1, d), lambda s: (0, 0)),
            pl.BlockSpec(modtab.shape, lambda s: (0, 0, 0, 0)),
        ],
        out_specs=pl.BlockSpec((d, bsz * qb), lambda s: (0, s)),
        compiler_params=_cp(("arbitrary",), VMEM_LIMIT),
        name="s5_in",
    )(h.reshape(bsz, length // GRID_W, GRID_W, d), h[:, :nctx].reshape(bsz, nctx_chunks, t_len * d),
      nw.reshape(1, d), modtab)


def _tile_rows(x, reps):
    return jnp.concatenate([x] * reps, axis=0)


def _repeat_rows(x, reps):
    return jnp.concatenate([jnp.broadcast_to(x[i:i + 1, :], (reps, x.shape[1])) for i in range(x.shape[0])],
                           axis=0)


def _s5_kernel(ut_ref, pk_ref, y_ref, sv_ref, ss_ref, w_ref, r2_ref, acc_ref, *, bsz, qb, orders):
    gb = ut_ref.shape[0] // S5_GROUP
    t_len = S5_CHUNK
    k = S5_GROUP * t_len
    q = ut_ref.shape[1] // t_len
    row_s = lax.broadcasted_iota(jnp.int32, (k, k), 0) // S5_GROUP
    col_t = lax.broadcasted_iota(jnp.int32, (k, k), 1) // S5_GROUP
    masks = (row_s <= col_t, row_s >= col_t)
    w_ref[...] = jnp.zeros_like(w_ref)
    us = []
    for g in range(gb):
        x = ut_ref[g * S5_GROUP:(g + 1) * S5_GROUP, :]
        us.append(jnp.concatenate([x[:, s * q:(s + 1) * q] for s in range(t_len)], axis=0).T)
    pairs = [(g, d) for g in range(gb) for d in range(2)]
    l2s, raw = [], []
    for g, d in pairs:
        pk = pk_ref[d, g]
        l2 = (_repeat_rows(pk[_AN1:_AN1 + t_len], S5_GROUP) * _tile_rows(pk[_BB1:_BB1 + S5_GROUP], t_len)
              + _repeat_rows(pk[_AN2:_AN2 + t_len], S5_GROUP) * _tile_rows(pk[_BB2:_BB2 + S5_GROUP], t_len))
        r2t = (_repeat_rows(pk[_AP1:_AP1 + t_len], S5_GROUP) * _tile_rows(pk[_CC1:_CC1 + S5_GROUP], t_len)
               + _repeat_rows(pk[_AP2:_AP2 + t_len], S5_GROUP) * _tile_rows(pk[_CC2:_CC2 + S5_GROUP], t_len))
        r2 = r2t.T
        r2_ref[2 * g + d] = r2.astype(BF16)
        l2s.append(l2.astype(BF16))
        raw.append(_dot3(l2, r2))
    for (g, d), l2, toep_raw in zip(pairs, l2s, raw):
        i = 2 * g + d
        y_local = _dot(us[g], jnp.where(masks[d], toep_raw, 0.0).astype(BF16))
        if d == 0:
            acc_ref[g] = y_local
        else:
            acc_ref[g] = acc_ref[g] + y_local
        v = _dot(us[g], l2)
        vs = pltpu.roll(v, v.shape[1] // 2, axis=1)
        pk = pk_ref[d, g]
        pt, qt, qts = pk[_PQ:_PQ + 1], pk[_PQ + 1:_PQ + 2], pk[_PQ + 2:_PQ + 3]
        sv_ref[i] = pt * v + qt * vs
        ss_ref[i] = pt * vs + qts * v
    coef = []
    for g in range(gb):
        for d in range(2):
            pk = pk_ref[d, g]
            coef.append(tuple(jnp.broadcast_to(pk[_PQ + r:_PQ + r + 1], (bsz, pk.shape[1])) for r in range(3)))
    zero = jnp.zeros((bsz, pk_ref.shape[3]), F32)
    state = [(zero, zero)] * (2 * gb)
    nsteps = len(orders[0])
    for step in range(nsteps):
        for i in range(2 * gb):
            rows = pl.ds(orders[i % 2][step], bsz, stride=qb)
            w, ws = state[i]
            w_ref[i, rows, :] = w
            if step + 1 < nsteps:
                pt, qt, qts = coef[i]
                state[i] = (pt * w + qt * ws + sv_ref[i, rows, :], pt * ws + qts * w + ss_ref[i, rows, :])
    for g in range(gb):
        y = acc_ref[g]
        for d in range(2):
            i = 2 * g + d
            y = y + _dot(w_ref[i].astype(BF16), r2_ref[i])
        yt = y.T
        for t in range(t_len):
            y_ref[g * S5_GROUP:(g + 1) * S5_GROUP, t * q:(t + 1) * q] = yt[t * S5_GROUP:(t + 1) * S5_GROUP, :]


def _s5_scan(ut, pk, bsz, nctx_chunks, nlat_chunks):
    t_len = S5_CHUNK
    q = ut.shape[1] // t_len
    ngroups = ut.shape[0] // S5_GROUP
    np2 = pk.shape[3]
    gb = S5_GROUP_BLOCK
    qb = S5_LANES_PER_BATCH
    lat = list(range(nlat_chunks))
    ctx = list(range(nlat_chunks, nlat_chunks + nctx_chunks))
    orders = (ctx + lat, ctx[::-1] + lat[::-1])
    blk = (gb * S5_GROUP, t_len * q)
    return pl.pallas_call(
        functools.partial(_s5_kernel, bsz=bsz, qb=qb, orders=orders),
        out_shape=jax.ShapeDtypeStruct(ut.shape, F32),
        grid=(ngroups // gb,),
        in_specs=[pl.BlockSpec(blk, lambda g: (g, 0)),
                  pl.BlockSpec((2, gb, _PK_ROWS, np2), lambda g: (0, g, 0, 0))],
        out_specs=pl.BlockSpec(blk, lambda g: (g, 0)),
        scratch_shapes=[pltpu.VMEM((2 * gb, q, np2), F32), pltpu.VMEM((2 * gb, q, np2), F32),
                        pltpu.VMEM((2 * gb, q, np2), F32), pltpu.VMEM((2 * gb, np2, S5_GROUP * t_len), BF16),
                        pltpu.VMEM((gb, q, S5_GROUP * t_len), F32)],
        compiler_params=_cp(("arbitrary",), VMEM_LIMIT),
        name="s5_scan",
    )(ut, pk)


def _s5_out_kernel(yt_ref, hl_ref, hc_ref, nw_ref, mod_ref, d_ref, ol_ref, oc_ref, *, bsz, qb):
    y = yt_ref[...].T
    nw = nw_ref[...]
    nlat, nc = hl_ref.shape[1], hc_ref.shape[1]
    c0 = 0.7978845608028654

    def act(yv, hv, region, b):
        u = _normmod_rows(hv, nw, mod_ref[b, region, SC1:SC1 + 1, :], mod_ref[b, region, SH1:SH1 + 1, :])
        z = yv + u * d_ref[...]
        return (0.5 * z * (1.0 + jnp.tanh(c0 * (z + 0.044715 * (z * z * z))))).astype(ol_ref.dtype)

    for b in range(bsz):
        ol_ref[b] = act(y[b * qb:b * qb + nlat], hl_ref[b], 1, b)
        oc_ref[b] = act(y[b * qb + nlat:b * qb + nlat + nc], hc_ref[b], 0, b)


def _s5_out(yt, h, nw, modtab, d_skip, nctx):
    bsz, length, d = h.shape
    t_len = S5_CHUNK
    qb = S5_LANES_PER_BATCH
    nctx_chunks = nctx // t_len
    seq = length - nctx
    lat_spec = pl.BlockSpec((bsz, None, GRID_W, d), lambda s: (0, nctx // GRID_W + s, 0, 0))
    ctx_spec = pl.BlockSpec((bsz, nctx_chunks, d), lambda s: (0, 0, s))
    act_l, act_c = pl.pallas_call(
        functools.partial(_s5_out_kernel, bsz=bsz, qb=qb),
        out_shape=(jax.ShapeDtypeStruct((bsz, seq // GRID_W, GRID_W, d), BF16),
                   jax.ShapeDtypeStruct((bsz, nctx_chunks, t_len * d), BF16)),
        grid=(t_len,),
        in_specs=[pl.BlockSpec((d, bsz * qb), lambda s: (0, s)),
                  lat_spec, ctx_spec,
                  pl.BlockSpec((1, d), lambda s: (0, 0)),
                  pl.BlockSpec(modtab.shape, lambda s: (0, 0, 0, 0)),
                  pl.BlockSpec((1, d), lambda s: (0, 0))],
        out_specs=(pl.BlockSpec((bsz, None, GRID_W, d), lambda s: (0, s, 0, 0)), ctx_spec),
        compiler_params=_cp(("arbitrary",), VMEM_LIMIT),
        name="s5_out",
    )(yt, h.reshape(bsz, length // GRID_W, GRID_W, d), h[:, :nctx].reshape(bsz, nctx_chunks, t_len * d),
      nw.reshape(1, d), modtab, d_skip.reshape(1, d))
    return act_l.reshape(bsz, seq, d), act_c.reshape(bsz, nctx, d)


def _s5_layer(h, modtab, nw, a_re, a_im, log_dt, b_re, b_im, c_re, c_im, d_skip, w_glu, b_glu, layer, nctx,
              last):
    bsz, length, d = h.shape
    seq = length - nctx
    t_len = S5_CHUNK
    assert seq // GRID_W == t_len and nctx % t_len == 0 and nctx % GRID_W == 0
    assert seq // t_len + nctx // t_len <= S5_LANES_PER_BATCH
    ngroups = d // S5_GROUP
    ut = _s5_in(h, nw, modtab, nctx)
    pk = _s5_params(a_re, a_im, log_dt, b_re, b_im, c_re, c_im)
    yt = _s5_scan(ut, pk, bsz, nctx // t_len, GRID_W)
    act_l, act_c = _s5_out(yt, h, nw, modtab, d_skip, nctx)
    if last:
        act, h, nct = act_l, h[:, nctx:], 0
    else:
        act, nct = jnp.concatenate([act_c, act_l], axis=1), nctx // ROW_TILE
    rows = h.shape[0] * h.shape[1]
    out = _mm_glu_res(act.reshape(rows, d), w_glu, b_glu, layer, h.reshape(rows, d), modtab, G1,
                      h.shape[1] // ROW_TILE, nct, 512, "s5_glu")
    return out.reshape(h.shape)


def _dense_ffn(h, modtab, nw, w13, w2, layer, nct):
    bsz, length, d = h.shape
    rows = bsz * length
    v = _normmod(h, nw, modtab, SH2, SC2, nct, BF16).reshape(rows, d)
    act = _mm_swiglu(v, w13, layer, 512, 512, "ffn_w13")
    out = _mm_res(act, w2, layer, h.reshape(rows, d), modtab, G2, length // ROW_TILE, nct, 512, "ffn_w2")
    return out.reshape(h.shape)


def _moe_gather_kernel(src_ref, nxt_ref, v_hbm, o_ref, buf_ref, sem):
    t = pl.program_id(0)
    slot = t % 2

    def request(idx_ref, s):
        def issue(r8, carry):
            for i in range(DMA_ISSUE_UNROLL):
                r = r8 * DMA_ISSUE_UNROLL + i
                pltpu.make_async_copy(v_hbm.at[pl.ds(idx_ref[0, r], 1)], buf_ref.at[s, pl.ds(r, 1)],
                                      sem.at[s]).start(priority=i % 2)
            return carry
        lax.fori_loop(0, MOE_TILE // DMA_ISSUE_UNROLL, issue, 0)

    @pl.when(t == 0)
    def _():
        request(src_ref, 0)

    @pl.when(t + 1 < pl.num_programs(0))
    def _():
        request(nxt_ref, 1 - slot)

    pltpu.make_async_copy(v_hbm.at[pl.ds(0, MOE_TILE)], buf_ref.at[slot], sem.at[slot]).wait()
    o_ref[...] = buf_ref[slot].astype(o_ref.dtype)


def _moe_gather(src, v):
    ntiles = src.shape[0]
    d = v.shape[1]
    idx_block = (None, 1, MOE_TILE)
    return pl.pallas_call(
        _moe_gather_kernel,
        out_shape=jax.ShapeDtypeStruct((ntiles * MOE_TILE, d), BF16),
        grid=(ntiles,),
        in_specs=[pl.BlockSpec(idx_block, lambda t: (t, 0, 0), memory_space=pltpu.SMEM),
                  pl.BlockSpec(idx_block, lambda t: (jnp.minimum(t + 1, ntiles - 1), 0, 0),
                               memory_space=pltpu.SMEM),
                  pl.BlockSpec(memory_space=pl.ANY)],
        out_specs=pl.BlockSpec((MOE_TILE, d), lambda t: (t, 0)),
        scratch_shapes=[pltpu.VMEM((2, MOE_TILE, d), F32), pltpu.SemaphoreType.DMA((2,))],
        compiler_params=_cp(("arbitrary",)),
        name="moe_gather",
    )(src, src, v)


def _moe_fetch_weights(te_ref, first_ref, nxt_ref, w_hbm, stage_ref, sem, run_ref, wbf_refs, col_blocks, layer):
    j, t = pl.program_id(0), pl.program_id(1)
    nsweeps = pl.num_programs(0)
    tn = wbf_refs[0].shape[1]

    def copies(e, sweep, slot):
        return [pltpu.make_async_copy(w_hbm.at[layer, e, :, pl.ds(pl.multiple_of(cb(sweep) * tn, tn), tn)],
                                      stage_ref.at[slot, i], sem.at[slot, i])
                for i, cb in enumerate(col_blocks)]

    @pl.when(jnp.logical_and(j == 0, t == 0))
    def _():
        run_ref[0] = 0
        for c in copies(te_ref[0], 0, 0):
            c.start()

    @pl.when(first_ref[t] == 1)
    def _():
        run = run_ref[0]
        slot = run % 2
        for c in copies(te_ref[t], j, slot):
            c.wait()
        for i, wbf_ref in enumerate(wbf_refs):
            wbf_ref[...] = stage_ref[slot, i].astype(BF16)
        sweep_done = nxt_ref[t] < 0
        next_e = jnp.where(sweep_done, te_ref[0], nxt_ref[t])
        next_sweep = jnp.where(sweep_done, j + 1, j)

        @pl.when(next_sweep < nsweeps)
        def _():
            for c in copies(next_e, next_sweep, 1 - slot):
                c.start()
        run_ref[0] = run + 1


def _moe_swiglu_kernel(te_ref, nv_ref, first_ref, nxt_ref, x_ref, w_hbm, o_ref, wabf_ref, wbbf_ref, stage_ref, sem,
                       run_ref, *, layer):
    nb = pl.num_programs(0)
    _moe_fetch_weights(te_ref, first_ref, nxt_ref, w_hbm, stage_ref, sem, run_ref, [wabf_ref, wbbf_ref],
                       [lambda sweep: sweep, lambda sweep: nb + sweep], layer)

    in_use = pl.program_id(1) < nv_ref[0]

    @pl.when(in_use)
    def _():
        x = x_ref[...]
        a = _dot(x, wabf_ref[...])
        b = _dot(x, wbbf_ref[...])
        o_ref[...] = (_silu(a) * b).astype(o_ref.dtype)

    @pl.when(jnp.logical_not(in_use))
    def _():
        o_ref[...] = jnp.zeros_like(o_ref)


def _moe_mm_kernel(te_ref, nv_ref, first_ref, nxt_ref, x_ref, w_hbm, o_ref, wbf_ref, stage_ref, sem, run_ref, *,
                   layer):
    _moe_fetch_weights(te_ref, first_ref, nxt_ref, w_hbm, stage_ref, sem, run_ref, [wbf_ref],
                       [lambda sweep: sweep], layer)

    in_use = pl.program_id(1) < nv_ref[0]

    @pl.when(in_use)
    def _():
        o_ref[...] = _dot(x_ref[...], wbf_ref[...])

    @pl.when(jnp.logical_not(in_use))
    def _():
        o_ref[...] = jnp.zeros_like(o_ref)


def _moe_swiglu(route, xs, w13, layer, tn):
    rows, k = xs.shape
    half = w13.shape[3] // 2
    nb = half // tn
    tm = MOE_TILE
    return pl.pallas_call(
        functools.partial(_moe_swiglu_kernel, layer=layer),
        out_shape=jax.ShapeDtypeStruct((rows, half), BF16),
        grid_spec=pltpu.PrefetchScalarGridSpec(
            num_scalar_prefetch=4,
            grid=(nb, rows // tm),
            in_specs=[pl.BlockSpec((tm, k), lambda j, t, *_: (t, 0)),
                      pl.BlockSpec(memory_space=pl.ANY)],
            out_specs=pl.BlockSpec((tm, tn), lambda j, t, *_: (t, j)),
            scratch_shapes=[pltpu.VMEM((k, tn), BF16), pltpu.VMEM((k, tn), BF16),
                            pltpu.VMEM((2, 2, k, tn), F32), pltpu.SemaphoreType.DMA((2, 2)),
                            pltpu.SMEM((1,), jnp.int32)]),
        compiler_params=_cp(("arbitrary", "arbitrary"), VMEM_LIMIT),
        name="moe_w13",
    )(*route, xs, w13)


def _moe_mm(route, xs, w2, layer, tn):
    rows, k = xs.shape
    n = w2.shape[3]
    tm = MOE_TILE
    return pl.pallas_call(
        functools.partial(_moe_mm_kernel, layer=layer),
        out_shape=jax.ShapeDtypeStruct((rows, n), F32),
        grid_spec=pltpu.PrefetchScalarGridSpec(
            num_scalar_prefetch=4,
            grid=(n // tn, rows // tm),
            in_specs=[pl.BlockSpec((tm, k), lambda j, t, *_: (t, 0)),
                      pl.BlockSpec(memory_space=pl.ANY)],
            out_specs=pl.BlockSpec((tm, tn), lambda j, t, *_: (t, j)),
            scratch_shapes=[pltpu.VMEM((k, tn), BF16), pltpu.VMEM((2, 1, k, tn), F32),
                            pltpu.SemaphoreType.DMA((2, 1)), pltpu.SMEM((1,), jnp.int32)]),
        compiler_params=_cp(("arbitrary", "arbitrary"), VMEM_LIMIT),
        name="moe_w2",
    )(*route, xs, w2)


def _moe_combine_kernel(pos_ref, nxt_ref, gate_ref, y_hbm, res_ref, mod_ref, o_ref, buf_ref, sem, *, g_row):
    m = pl.program_id(0)
    slot = m % 2

    def request(idx_ref, s):
        for k in range(TOP_K):
            def issue(r8, carry, k=k):
                for i in range(DMA_ISSUE_UNROLL):
                    r = r8 * DMA_ISSUE_UNROLL + i
                    pltpu.make_async_copy(y_hbm.at[pl.ds(idx_ref[k, r], 1)], buf_ref.at[s, k, pl.ds(r, 1)],
                                          sem.at[s]).start(priority=i % 2)
                return carry
            lax.fori_loop(0, ROW_TILE // DMA_ISSUE_UNROLL, issue, 0)

    @pl.when(m == 0)
    def _():
        request(pos_ref, 0)

    @pl.when(m + 1 < pl.num_programs(0))
    def _():
        request(nxt_ref, 1 - slot)

    for k in range(TOP_K):
        pltpu.make_async_copy(y_hbm.at[pl.ds(0, ROW_TILE)], buf_ref.at[slot, k], sem.at[slot]).wait()
    p = gate_ref[...]
    mix = p[:, 0:1] * buf_ref[slot, 0] + p[:, 1:2] * buf_ref[slot, 1]
    o_ref[...] = res_ref[...] + mod_ref[g_row:g_row + 1, :] * mix


def _moe_combine(pos, gates, ys, res, modtab, g_row, tpb, nct):
    n, d = res.shape
    tm = ROW_TILE
    ntiles = n // tm
    return pl.pallas_call(
        functools.partial(_moe_combine_kernel, g_row=g_row),
        out_shape=jax.ShapeDtypeStruct((n, d), F32),
        grid=(ntiles,),
        in_specs=[pl.BlockSpec((None, TOP_K, tm), lambda m: (m, 0, 0), memory_space=pltpu.SMEM),
                  pl.BlockSpec((None, TOP_K, tm), lambda m: (jnp.minimum(m + 1, ntiles - 1), 0, 0),
                               memory_space=pltpu.SMEM),
                  pl.BlockSpec((tm, ROUTER_LANES), lambda m: (m, 0)),
                  pl.BlockSpec(memory_space=pl.ANY),
                  pl.BlockSpec((tm, d), lambda m: (m, 0)),
                  pl.BlockSpec((None, None, 6, d), lambda m: (m // tpb, jnp.where(m % tpb >= nct, 1, 0), 0, 0))],
        out_specs=pl.BlockSpec((tm, d), lambda m: (m, 0)),
        scratch_shapes=[pltpu.VMEM((2, TOP_K, tm, d), F32), pltpu.SemaphoreType.DMA((2,))],
        compiler_params=_cp(("arbitrary",), VMEM_LIMIT),
        name="moe_combine",
    )(pos, pos, gates, ys, res, modtab)


def _moe_route(idx2):
    n = idx2.shape[0]
    flat_e = idx2.reshape(-1)
    onehot = (flat_e[:, None] == jnp.arange(N_EXPERTS, dtype=jnp.int32)[None, :]).astype(jnp.int32)
    csum = jnp.cumsum(onehot, axis=0)
    rank = jnp.sum((csum - onehot) * onehot, axis=1)
    counts = csum[-1]
    ntile = (counts + MOE_TILE - 1) // MOE_TILE
    tile_end = jnp.cumsum(ntile)
    tile_start = tile_end - ntile
    pos = tile_start[flat_e] * MOE_TILE + rank
    tiles = (TOP_K * n) // MOE_TILE + N_EXPERTS
    src = jnp.zeros((tiles * MOE_TILE,), jnp.int32).at[pos].set(jnp.arange(TOP_K * n, dtype=jnp.int32) // TOP_K)
    nv = tile_end[-1:]
    tile_ids = jnp.arange(tiles, dtype=jnp.int32)
    tids = jnp.minimum(tile_ids, nv[0] - 1)
    te = jnp.sum((tile_end[None, :] <= tids[:, None]).astype(jnp.int32), axis=1)
    first = jnp.logical_and(tile_ids < nv[0], jnp.logical_or(tile_ids == 0, te != jnp.roll(te, 1)))
    run_end = tile_end[te]
    nxt = jnp.where(run_end < nv[0], te[jnp.minimum(run_end, tiles - 1)], -1)
    route = (te, nv.astype(jnp.int32), first.astype(jnp.int32), nxt.astype(jnp.int32))
    return pos.astype(jnp.int32), src.reshape(tiles, 1, MOE_TILE), route


def _moe_ffn(h, modtab, nw, w_router, w13, w2, layer, nct):
    bsz, length, d = h.shape
    n = bsz * length
    v, idx, gates = _normmod_router(h, nw, modtab, w_router, SH2, SC2, nct)
    idx2 = idx.reshape(n, ROUTER_LANES)[:, :TOP_K]
    pos, src, route = _moe_route(idx2)
    xs = _moe_gather(src, v.reshape(n, d))
    act = _moe_swiglu(route, xs, w13, layer, 512)
    ys = _moe_mm(route, act, w2, layer, 512)
    pos3 = pos.reshape(n // ROW_TILE, ROW_TILE, TOP_K).transpose(0, 2, 1)
    out = _moe_combine(pos3, gates.reshape(n, ROUTER_LANES), ys, h.reshape(n, d), modtab, G2,
                       length // ROW_TILE, nct)
    return out.reshape(h.shape)


def kernel(x, c, ctx, c_ctx, w_mod, b_mod, norm1, norm2, norm_f, gla_w_in, gla_w_gate_up, gla_b_gate,
           gla_g_head, gla_w_out, ffn_w13, ffn_w2, s5_a_re, s5_a_im, s5_log_dt, s5_b_re, s5_b_im, s5_c_re,
           s5_c_im, s5_d, s5_w_glu, s5_b_glu, moe_w_router, moe_w13, moe_w2):
    bsz, seq, d = x.shape
    nctx = ctx.shape[1]
    depth = w_mod.shape[0]
    assert bsz + 1 <= 8 and seq % ROW_TILE == 0 and nctx % ROW_TILE == 0 and seq % GRID_W == 0
    nct = nctx // ROW_TILE

    c8 = jnp.concatenate([c, c_ctx[None, :], jnp.zeros((8 - bsz - 1, d), F32)], axis=0)
    mods = _modulation(c8, w_mod, b_mod).reshape(depth, 8, 6, d)
    h = jnp.concatenate([ctx, x], axis=1)
    for i in range(depth):
        last = i == depth - 1
        j = i // 2
        modtab = jnp.stack([jnp.broadcast_to(mods[i, bsz], (bsz, 6, d)), mods[i, :bsz]], axis=1)
        if i % 2 == 0:
            h = _gla_layer(h, modtab, norm1[i], gla_w_in, j, gla_w_gate_up[j], gla_b_gate[j], gla_g_head[j],
                           gla_w_out, nctx)
        else:
            h = _s5_layer(h, modtab, norm1[i], s5_a_re[j], s5_a_im[j], s5_log_dt[j], s5_b_re[j], s5_b_im[j],
                          s5_c_re[j], s5_c_im[j], s5_d[j], s5_w_glu, s5_b_glu, j, nctx, last)
        ffn_nct = 0 if last else nct
        if i % 2 == 0:
            h = _dense_ffn(h, modtab, norm2[i], ffn_w13, ffn_w2, j, ffn_nct)
        else:
            h = _moe_ffn(h, modtab, norm2[i], moe_w_router[j], moe_w13, moe_w2, j, ffn_nct)
    if h.shape[1] != seq:
        h = h[:, nctx:]
    return _final_norm(h, norm_f)
```

```python
import functools

import jax
import jax.numpy as jnp
from jax import lax
from jax.experimental import pallas as pl
from jax.experimental.pallas import tpu as pltpu

F32 = jnp.float32
BF16 = jnp.bfloat16

EPS = 1e-6
GRID_W = 64

GLA_HEADS = 4
GLA_GATE_RANK = 16
GLA_GATE_TAU = 16.0
GLA_CHUNK = 64

S5_GROUP = 16
S5_CHUNK = 32
S5_GROUP_BLOCK = 4

N_EXPERTS = 8
TOP_K = 2
ROUTER_LANES = 128

ROW_TILE = 256
MOE_TILE = 256
DMA_ISSUE_UNROLL = 8
MOE_W13_COLS = 1408
MOE_W2_COLS = 1024
VMEM_LIMIT = 56 * 1024 * 1024

SH1, SC1, G1, SH2, SC2, G2 = range(6)


def _cp(sem, vmem=None):
    return pltpu.CompilerParams(dimension_semantics=sem, vmem_limit_bytes=vmem)


def _dot(a, b):
    return jnp.dot(a, b, preferred_element_type=F32)


def _split(x):
    hi = x.astype(BF16)
    lo = (x - hi.astype(F32)).astype(BF16)
    return hi, lo


def _dot3(a, b):
    ah, al = _split(a)
    bh, bl = _split(b)
    return _dot(ah, bh) + _dot(ah, bl) + _dot(al, bh)


def _sigmoid(x):
    return 1.0 / (1.0 + jnp.exp(-x))


def _silu(x):
    return x * _sigmoid(x)


def _mod_kernel(c_ref, w_ref, b_ref, o_ref):
    x = _silu(c_ref[...]).astype(BF16)
    o_ref[...] = _dot(x, w_ref[...].astype(BF16)) + b_ref[...]


def _modulation(c8, w_mod, b_mod):
    depth, d, n = w_mod.shape
    tn = 1024
    return pl.pallas_call(
        _mod_kernel,
        out_shape=jax.ShapeDtypeStruct((depth, 8, n), F32),
        grid=(depth, n // tn),
        in_specs=[
            pl.BlockSpec((8, d), lambda i, j: (0, 0)),
            pl.BlockSpec((None, d, tn), lambda i, j: (i, 0, j)),
            pl.BlockSpec((None, 1, tn), lambda i, j: (i, 0, j)),
        ],
        out_specs=pl.BlockSpec((None, 8, tn), lambda i, j: (i, 0, j)),
        compiler_params=_cp(("arbitrary", "arbitrary"), VMEM_LIMIT),
        name="modulation",
    )(c8, w_mod, b_mod.reshape(depth, 1, n))


def _normmod_rows(x, nw, scale, shift):
    ms = jnp.mean(x * x, axis=-1, keepdims=True)
    return x * lax.rsqrt(ms + EPS) * nw * (1.0 + scale) + shift


def _normmod_body(x_ref, nw_ref, mod_ref, sh_row, sc_row):
    return _normmod_rows(x_ref[...], nw_ref[...], mod_ref[sc_row:sc_row + 1, :], mod_ref[sh_row:sh_row + 1, :])


def _normmod_kernel(x_ref, nw_ref, mod_ref, o_ref, *, sh_row, sc_row):
    o_ref[...] = _normmod_body(x_ref, nw_ref, mod_ref, sh_row, sc_row).astype(o_ref.dtype)


def _normmod_router_kernel(x_ref, nw_ref, mod_ref, wr_ref, o_ref, idx_ref, gate_ref, *, sh_row, sc_row):
    y = _normmod_body(x_ref, nw_ref, mod_ref, sh_row, sc_row)
    o_ref[...] = y
    logits = _dot3(y, wr_ref[...])
    lane = lax.broadcasted_iota(jnp.int32, logits.shape, 1)
    lane_f = lane.astype(F32)
    neg = jnp.float32(-jnp.inf)
    l1 = jnp.where(lane < N_EXPERTS, logits, neg)
    m1 = jnp.max(l1, axis=-1, keepdims=True)
    i1 = jnp.min(jnp.where(l1 == m1, lane_f, float(ROUTER_LANES)), axis=-1, keepdims=True)
    l2 = jnp.where(lane_f == i1, neg, l1)
    m2 = jnp.max(l2, axis=-1, keepdims=True)
    i2 = jnp.min(jnp.where(l2 == m2, lane_f, float(ROUTER_LANES)), axis=-1, keepdims=True)
    e = jnp.exp(m2 - m1)
    g1 = 1.0 / (1.0 + e)
    g2 = e * g1
    idx_ref[...] = jnp.where(lane == 0, i1, jnp.where(lane == 1, i2, 0.0)).astype(jnp.int32)
    gate_ref[...] = jnp.where(lane == 0, g1, jnp.where(lane == 1, g2, 0.0))


def _row_specs(d, nct):
    x_spec = pl.BlockSpec((None, ROW_TILE, d), lambda b, t: (b, t, 0))
    mod_spec = pl.BlockSpec((None, None, 6, d), lambda b, t: (b, jnp.where(t >= nct, 1, 0), 0, 0))
    return x_spec, mod_spec


def _normmod(h, nw, modtab, sh_row, sc_row, nct, out_dtype):
    bsz, length, d = h.shape
    x_spec, mod_spec = _row_specs(d, nct)
    return pl.pallas_call(
        functools.partial(_normmod_kernel, sh_row=sh_row, sc_row=sc_row),
        out_shape=jax.ShapeDtypeStruct(h.shape, out_dtype),
        grid=(bsz, length // ROW_TILE),
        in_specs=[x_spec, pl.BlockSpec((1, d), lambda b, t: (0, 0)), mod_spec],
        out_specs=x_spec,
        compiler_params=_cp(("arbitrary", "arbitrary")),
        name="normmod",
    )(h, nw.reshape(1, d), modtab)


def _normmod_router(h, nw, modtab, w_router, sh_row, sc_row, nct):
    bsz, length, d = h.shape
    x_spec, mod_spec = _row_specs(d, nct)
    wr = jnp.pad(w_router, ((0, 0), (0, ROUTER_LANES - w_router.shape[1])))
    lane_spec = pl.BlockSpec((None, ROW_TILE, ROUTER_LANES), lambda b, t: (b, t, 0))
    return pl.pallas_call(
        functools.partial(_normmod_router_kernel, sh_row=sh_row, sc_row=sc_row),
        out_shape=(
            jax.ShapeDtypeStruct(h.shape, F32),
            jax.ShapeDtypeStruct((bsz, length, ROUTER_LANES), jnp.int32),
            jax.ShapeDtypeStruct((bsz, length, ROUTER_LANES), F32),
        ),
        grid=(bsz, length // ROW_TILE),
        in_specs=[x_spec, pl.BlockSpec((1, d), lambda b, t: (0, 0)), mod_spec,
                  pl.BlockSpec((d, ROUTER_LANES), lambda b, t: (0, 0))],
        out_specs=(x_spec, lane_spec, lane_spec),
        compiler_params=_cp(("arbitrary", "arbitrary")),
        name="normmod_router",
    )(h, nw.reshape(1, d), modtab, wr)


def _rms_kernel(x_ref, nw_ref, o_ref):
    x = x_ref[...]
    ms = jnp.mean(x * x, axis=-1, keepdims=True)
    o_ref[...] = x * lax.rsqrt(ms + EPS) * nw_ref[...]


def _final_norm(h, nw):
    bsz, length, d = h.shape
    x_spec = pl.BlockSpec((None, ROW_TILE, d), lambda b, t: (b, t, 0))
    return pl.pallas_call(
        _rms_kernel,
        out_shape=jax.ShapeDtypeStruct(h.shape, F32),
        grid=(bsz, length // ROW_TILE),
        in_specs=[x_spec, pl.BlockSpec((1, d), lambda b, t: (0, 0))],
        out_specs=x_spec,
        compiler_params=_cp(("arbitrary", "arbitrary")),
        name="final_norm",
    )(h, nw.reshape(1, d))


def _cast_weights(pairs):
    @pl.when(pl.program_id(1) == 0)
    def _():
        for w_ref, wbf_ref in pairs:
            wbf_ref[...] = w_ref[...].astype(BF16)


def _mm_plain_kernel(x_ref, w_ref, o_ref, wbf_ref):
    _cast_weights([(w_ref, wbf_ref)])
    o_ref[...] = _dot(x_ref[...], wbf_ref[...]).astype(o_ref.dtype)


def _mm_swiglu_kernel(x_ref, wa_ref, wb_ref, o_ref, wabf_ref, wbbf_ref):
    _cast_weights([(wa_ref, wabf_ref), (wb_ref, wbbf_ref)])
    x = x_ref[...]
    a = _dot(x, wabf_ref[...])
    b = _dot(x, wbbf_ref[...])
    o_ref[...] = (_silu(a) * b).astype(o_ref.dtype)


def _mm_res_kernel(x_ref, w_ref, res_ref, mod_ref, o_ref, wbf_ref, *, g_row):
    _cast_weights([(w_ref, wbf_ref)])
    acc = _dot(x_ref[...], wbf_ref[...])
    o_ref[...] = res_ref[...] + mod_ref[g_row:g_row + 1, :] * acc


def _mm_glu_res_kernel(x_ref, wa_ref, wb_ref, ba_ref, bb_ref, res_ref, mod_ref, o_ref, wabf_ref, wbbf_ref,
                       *, g_row):
    _cast_weights([(wa_ref, wabf_ref), (wb_ref, wbbf_ref)])
    x = x_ref[...]
    z1 = _dot(x, wabf_ref[...]) + ba_ref[...]
    z2 = _dot(x, wbbf_ref[...]) + bb_ref[...]
    o_ref[...] = res_ref[...] + mod_ref[g_row:g_row + 1, :] * (z1 * _sigmoid(z2))


def _mm_plain(x, w, layer, n, tm, tn, out_dtype, name):
    r, k = x.shape
    assert r % tm == 0 and n % tn == 0
    return pl.pallas_call(
        _mm_plain_kernel,
        out_shape=jax.ShapeDtypeStruct((r, n), out_dtype),
        grid=(n // tn, r // tm),
        in_specs=[pl.BlockSpec((tm, k), lambda j, m: (m, 0)),
                  pl.BlockSpec((None, k, tn), lambda j, m: (layer, 0, j))],
        out_specs=pl.BlockSpec((tm, tn), lambda j, m: (m, j)),
        scratch_shapes=[pltpu.VMEM((k, tn), BF16)],
        compiler_params=_cp(("arbitrary", "arbitrary"), VMEM_LIMIT),
        name=name,
    )(x, w)


def _mm_swiglu(x, w13, layer, tm, tn, name):
    r, k = x.shape
    half = w13.shape[2] // 2
    nb = half // tn
    assert r % tm == 0 and half % tn == 0
    return pl.pallas_call(
        _mm_swiglu_kernel,
        out_shape=jax.ShapeDtypeStruct((r, half), BF16),
        grid=(nb, r // tm),
        in_specs=[pl.BlockSpec((tm, k), lambda j, m: (m, 0)),
                  pl.BlockSpec((None, k, tn), lambda j, m: (layer, 0, j)),
                  pl.BlockSpec((None, k, tn), lambda j, m: (layer, 0, nb + j))],
        out_specs=pl.BlockSpec((tm, tn), lambda j, m: (m, j)),
        scratch_shapes=[pltpu.VMEM((k, tn), BF16), pltpu.VMEM((k, tn), BF16)],
        compiler_params=_cp(("arbitrary", "arbitrary"), VMEM_LIMIT),
        name=name,
    )(x, w13, w13)


def _res_specs(tn, tpb, nct):
    res_spec = pl.BlockSpec((ROW_TILE, tn), lambda j, m: (m, j))
    mod_spec = pl.BlockSpec((None, None, 6, tn),
                            lambda j, m: (m // tpb, jnp.where(m % tpb >= nct, 1, 0), 0, j))
    return res_spec, mod_spec


def _mm_res(x, w, layer, res, modtab, g_row, tpb, nct, tn, name):
    r, k = x.shape
    n = w.shape[2]
    tm = ROW_TILE
    res_spec, mod_spec = _res_specs(tn, tpb, nct)
    return pl.pallas_call(
        functools.partial(_mm_res_kernel, g_row=g_row),
        out_shape=jax.ShapeDtypeStruct((r, n), F32),
        grid=(n // tn, r // tm),
        in_specs=[pl.BlockSpec((tm, k), lambda j, m: (m, 0)),
                  pl.BlockSpec((None, k, tn), lambda j, m: (layer, 0, j)),
                  res_spec, mod_spec],
        out_specs=pl.BlockSpec((tm, tn), lambda j, m: (m, j)),
        scratch_shapes=[pltpu.VMEM((k, tn), BF16)],
        compiler_params=_cp(("arbitrary", "arbitrary"), VMEM_LIMIT),
        name=name,
    )(x, w, res, modtab)


def _mm_glu_res(x, w, bias, layer, res, modtab, g_row, tpb, nct, tn, name):
    r, k = x.shape
    half = w.shape[2] // 2
    nb = half // tn
    tm = ROW_TILE
    res_spec, mod_spec = _res_specs(tn, tpb, nct)
    bias2 = bias.reshape(bias.shape[0], 1, 2 * half)
    return pl.pallas_call(
        functools.partial(_mm_glu_res_kernel, g_row=g_row),
        out_shape=jax.ShapeDtypeStruct((r, half), F32),
        grid=(nb, r // tm),
        in_specs=[pl.BlockSpec((tm, k), lambda j, m: (m, 0)),
                  pl.BlockSpec((None, k, tn), lambda j, m: (layer, 0, j)),
                  pl.BlockSpec((None, k, tn), lambda j, m: (layer, 0, nb + j)),
                  pl.BlockSpec((None, 1, tn), lambda j, m: (layer, 0, j)),
                  pl.BlockSpec((None, 1, tn), lambda j, m: (layer, 0, nb + j)),
                  res_spec, mod_spec],
        out_specs=pl.BlockSpec((tm, tn), lambda j, m: (m, j)),
        scratch_shapes=[pltpu.VMEM((k, tn), BF16), pltpu.VMEM((k, tn), BF16)],
        compiler_params=_cp(("arbitrary", "arbitrary"), VMEM_LIMIT),
        name=name,
    )(x, w, w, bias2, bias2, res, modtab)


def _gla_gate_kernel(x_ref, wgl_ref, wup_ref, b_ref, o_ref):
    wh, wl = _split(wgl_ref[...])
    x = x_ref[...]
    gl = _dot(x, wh) + _dot(x, wl)
    z = _dot3(gl, wup_ref[...]) + b_ref[...]
    log_sig = jnp.minimum(z, 0.0) - jnp.log(1.0 + jnp.exp(-jnp.abs(z)))
    o_ref[...] = log_sig * (1.0 / GLA_GATE_TAU)


def _gla_gate(u, w_gl, w_up_bd, b_gate):
    r, d = u.shape
    n = w_up_bd.shape[1]
    tm = 512
    return pl.pallas_call(
        _gla_gate_kernel,
        out_shape=jax.ShapeDtypeStruct((r, n), F32),
        grid=(r // tm,),
        in_specs=[pl.BlockSpec((tm, d), lambda m: (m, 0)),
                  pl.BlockSpec(w_gl.shape, lambda m: (0, 0)),
                  pl.BlockSpec(w_up_bd.shape, lambda m: (0, 0)),
                  pl.BlockSpec((1, n), lambda m: (0, 0))],
        out_specs=pl.BlockSpec((tm, n), lambda m: (m, 0)),
        compiler_params=_cp(("arbitrary",)),
        name="gla_gate",
    )(u, w_gl, w_up_bd, b_gate.reshape(1, n))


def _gla_scan_kernel(tri_ref, qf_ref, kf_ref, vf_ref, gf_ref, qb_ref, kb_ref, vb_ref, gb_ref, of_ref, ob_ref, s_ref,
                     *, hk, hv):
    @pl.when(pl.program_id(1) == 0)
    def _():
        s_ref[...] = jnp.zeros_like(s_ref)

    ones = jnp.ones((GLA_CHUNK, 128), BF16)
    scale = hk ** -0.5
    tn_dims = (((0,), (0,)), ((), ()))
    nt_dims = (((1,), (1,)), ((), ()))
    streams = ((qf_ref, kf_ref, vf_ref, gf_ref, of_ref), (qb_ref, kb_ref, vb_ref, gb_ref, ob_ref))
    chains = [(d, h) for d in range(2) for h in range(GLA_HEADS)]
    decays = []
    for d, h in chains:
        g = streams[d][3][:, h * hk:(h + 1) * hk]
        gh, gl = _split(g)
        tri = tri_ref[d]
        b = _dot(tri, gh) + _dot(tri, gl)
        tot = jnp.sum(g, axis=0, keepdims=True)
        tot_col = (lax.dot_general(gh, ones, tn_dims, preferred_element_type=F32)
                   + lax.dot_general(gl, ones, tn_dims, preferred_element_type=F32))
        decays.append((b, tot, tot_col))
    operands = []
    for (d, h), (b, tot, tot_col) in zip(chains, decays):
        q = streams[d][0][:, h * hk:(h + 1) * hk]
        k = streams[d][1][:, h * hk:(h + 1) * hk]
        qd = (q * scale * jnp.exp(b)).astype(BF16)
        kd = (k * jnp.exp(-b)).astype(BF16)
        kt = (k * jnp.exp(tot - b)).astype(BF16)
        scores = jnp.where(tri_ref[d] > 0, lax.dot_general(qd, kd, nt_dims, preferred_element_type=F32), 0.0)
        operands.append((qd, kt, scores.astype(BF16)))
    for (d, h), (qd, kt, scores) in zip(chains, operands):
        v = streams[d][2][:, h * hv:(h + 1) * hv].astype(BF16)
        streams[d][4][:, h * hv:(h + 1) * hv] = _dot(scores, v) + _dot(qd, s_ref[d, h].astype(BF16))
    for (d, h), (b, tot, tot_col), (qd, kt, scores) in zip(chains, decays, operands):
        v = streams[d][2][:, h * hv:(h + 1) * hv].astype(BF16)
        decay = jnp.concatenate([jnp.exp(tot_col)] * (hv // 128), axis=1)
        s_ref[d, h] = decay * s_ref[d, h] + lax.dot_general(kt, v, tn_dims, preferred_element_type=F32)


def _gla_scan(proj, log_a, nctx_chunks, dk, dv):
    bsz, length, _ = proj.shape
    nchunks = length // GLA_CHUNK
    hk, hv = dk // GLA_HEADS, dv // GLA_HEADS
    idx = jnp.arange(GLA_CHUNK)
    tri = jnp.stack([idx[:, None] >= idx[None, :], idx[:, None] <= idx[None, :]]).astype(BF16)

    def back(s):
        return jnp.where(s < nctx_chunks, nctx_chunks - 1 - s, nchunks - 1 - (s - nctx_chunks))

    def specs(chunk, direction):
        return [pl.BlockSpec((None, GLA_CHUNK, dk), lambda b, s: (b, chunk(s), 0)),
                pl.BlockSpec((None, GLA_CHUNK, dk), lambda b, s: (b, chunk(s), 1)),
                pl.BlockSpec((None, GLA_CHUNK, dv), lambda b, s: (b, chunk(s), (2 * dk) // dv)),
                pl.BlockSpec((None, GLA_CHUNK, dk), lambda b, s: (b, chunk(s), direction))]

    out = jax.ShapeDtypeStruct((bsz, length, dv), F32)
    return pl.pallas_call(
        functools.partial(_gla_scan_kernel, hk=hk, hv=hv),
        out_shape=(out, out),
        grid=(bsz, nchunks),
        in_specs=[pl.BlockSpec((2, GLA_CHUNK, GLA_CHUNK), lambda b, s: (0, 0, 0))]
        + specs(lambda s: s, 0) + specs(back, 1),
        out_specs=(pl.BlockSpec((None, GLA_CHUNK, dv), lambda b, s: (b, s, 0)),
                   pl.BlockSpec((None, GLA_CHUNK, dv), lambda b, s: (b, back(s), 0))),
        scratch_shapes=[pltpu.VMEM((2, GLA_HEADS, hk, hv), F32)],
        compiler_params=_cp(("arbitrary", "arbitrary"), VMEM_LIMIT),
        name="gla_scan",
    )(tri, proj, proj, proj, log_a, proj, proj, proj, log_a)


def _gla_finish_kernel(of_ref, ob_ref, r_ref, gh_ref, o_ref, *, hv):
    for h in range(GLA_HEADS):
        sl = slice(h * hv, (h + 1) * hv)
        o = of_ref[:, sl] + ob_ref[:, sl]
        ms = jnp.mean(o * o, axis=-1, keepdims=True)
        y = o * lax.rsqrt(ms + EPS) * gh_ref[...]
        o_ref[:, sl] = (y * _silu(r_ref[:, sl])).astype(o_ref.dtype)


def _gla_finish(o_fwd, o_bwd, proj, g_head, dk, dv):
    bsz, length, _ = o_fwd.shape
    hv = dv // GLA_HEADS
    tm = ROW_TILE
    o_spec = pl.BlockSpec((None, tm, dv), lambda b, t: (b, t, 0))
    return pl.pallas_call(
        functools.partial(_gla_finish_kernel, hv=hv),
        out_shape=jax.ShapeDtypeStruct((bsz, length, dv), BF16),
        grid=(bsz, length // tm),
        in_specs=[
            o_spec, o_spec,
            pl.BlockSpec((None, tm, dv), lambda b, t: (b, t, (2 * dk + dv) // dv)),
            pl.BlockSpec((1, hv), lambda b, t: (0, 0)),
        ],
        out_specs=o_spec,
        compiler_params=_cp(("arbitrary", "arbitrary")),
        name="gla_finish",
    )(o_fwd, o_bwd, proj, g_head.reshape(1, hv))


def _gla_layer(h, modtab, nw, w_in, layer, w_gate_up, b_gate, g_head, w_out, nctx):
    bsz, length, d = h.shape
    dk = w_gate_up.shape[2]
    dv = w_out.shape[1]
    nmain = 2 * dk + 2 * dv
    tpb = length // ROW_TILE
    nct = nctx // ROW_TILE
    u = _normmod(h, nw, modtab, SH1, SC1, nct, BF16).reshape(bsz * length, d)
    proj = _mm_plain(u, w_in, layer, nmain, 512, 1024, F32, "gla_in_proj")
    rank = GLA_GATE_RANK
    w_up_bd = jnp.zeros((2 * rank, 2 * dk), F32)
    w_up_bd = w_up_bd.at[:rank, :dk].set(w_gate_up[0]).at[rank:, dk:].set(w_gate_up[1])
    log_a = _gla_gate(u, w_in[layer, :, nmain:], w_up_bd, b_gate.reshape(2 * dk))
    proj3 = proj.reshape(bsz, length, nmain)
    o_fwd, o_bwd = _gla_scan(proj3, log_a.reshape(bsz, length, 2 * dk), nctx // GLA_CHUNK, dk, dv)
    gated = _gla_finish(o_fwd, o_bwd, proj3, g_head, dk, dv).reshape(bsz * length, dv)
    out = _mm_res(gated, w_out, layer, h.reshape(bsz * length, d), modtab, G1, tpb, nct, 512, "gla_out_proj")
    return out.reshape(bsz, length, d)


S5_LANES_PER_BATCH = 96
_AN1, _AN2, _AP1, _AP2, _BB1, _BB2, _CC1, _CC2, _PQ, _PK_ROWS = 0, 32, 64, 96, 128, 144, 160, 176, 192, 200


def _s5_params(a_re, a_im, log_dt, b_re, b_im, c_re, c_im):
    t_len = S5_CHUNK
    dt = jnp.exp(log_dt)[..., None]
    mag = jnp.exp(a_re * dt)
    ab_re = mag * jnp.cos(a_im * dt)
    ab_im = mag * jnp.sin(a_im * dt)
    den = a_re * a_re + a_im * a_im
    f_re = ((ab_re - 1.0) * a_re + ab_im * a_im) / den
    f_im = (ab_im * a_re - (ab_re - 1.0) * a_im) / den
    bb_re = f_re[..., None] * b_re - f_im[..., None] * b_im
    bb_im = f_re[..., None] * b_im + f_im[..., None] * b_re

    def cpow(n):
        nn = n.astype(F32)[:, None, :, None]
        m = jnp.exp((a_re * dt)[:, :, None, :] * nn)
        th = (a_im * dt)[:, :, None, :] * nn
        return m * jnp.cos(th), m * jnp.sin(th)

    steps = jnp.arange(t_len)
    order = jnp.stack([steps, t_len - 1 - steps])
    anr, ani = cpow(-order)
    apr, api = cpow(order)
    atr, ati = cpow(jnp.full((2, 1), t_len))
    cat = lambda u, v: jnp.concatenate([u, v], axis=-1)
    bbr_t, bbi_t = bb_re.transpose(0, 1, 3, 2), bb_im.transpose(0, 1, 3, 2)
    zero = jnp.zeros((2, a_re.shape[1], 5, 2 * a_re.shape[2]), F32)
    return jnp.concatenate([
        cat(anr, anr), cat(-ani, ani),
        cat(apr, api), cat(api, apr),
        cat(bbr_t, bbi_t), cat(bbi_t, bbr_t),
        cat(c_re, -c_re), cat(-c_im, -c_im),
        cat(atr, atr), cat(-ati, ati), cat(ati, -ati), zero], axis=2)


def _s5_in_kernel(hl_ref, hc_ref, nw_ref, mod_ref, o_ref, *, bsz, qb):
    nw = nw_ref[...]
    blocks = []
    for b in range(bsz):
        lat = _normmod_rows(hl_ref[b], nw, mod_ref[b, 1, SC1:SC1 + 1, :], mod_ref[b, 1, SH1:SH1 + 1, :])
        ctx = _normmod_rows(hc_ref[b], nw, mod_ref[b, 0, SC1:SC1 + 1, :], mod_ref[b, 0, SH1:SH1 + 1, :])
        pad = jnp.zeros((qb - lat.shape[0] - ctx.shape[0], lat.shape[1]), F32)
        blocks += [lat, ctx, pad]
    o_ref[...] = jnp.concatenate(blocks, axis=0).T.astype(o_ref.dtype)


def _s5_in(h, nw, modtab, nctx):
    bsz, length, d = h.shape
    t_len = S5_CHUNK
    qb = S5_LANES_PER_BATCH
    nctx_chunks = nctx // t_len
    return pl.pallas_call(
        functools.partial(_s5_in_kernel, bsz=bsz, qb=qb),
        out_shape=jax.ShapeDtypeStruct((d, t_len * bsz * qb), BF16),
        grid=(t_len,),
        in_specs=[
            pl.BlockSpec((bsz, None, GRID_W, d), lambda s: (0, nctx // GRID_W + s, 0, 0)),
            pl.BlockSpec((bsz, nctx_chunks, d), lambda s: (0, 0, s)),
            pl.BlockSpec((1, d), lambda s: (0, 0)),
            pl.BlockSpec(modtab.shape, lambda s: (0, 0, 0, 0)),
        ],
        out_specs=pl.BlockSpec((d, bsz * qb), lambda s: (0, s)),
        compiler_params=_cp(("arbitrary",), VMEM_LIMIT),
        name="s5_in",
    )(h.reshape(bsz, length // GRID_W, GRID_W, d), h[:, :nctx].reshape(bsz, nctx_chunks, t_len * d),
      nw.reshape(1, d), modtab)


def _tile_rows(x, reps):
    return jnp.concatenate([x] * reps, axis=0)


def _repeat_rows(x, reps):
    return jnp.concatenate([jnp.broadcast_to(x[i:i + 1, :], (reps, x.shape[1])) for i in range(x.shape[0])],
                           axis=0)


def _s5_kernel(ut_ref, pk_ref, y_ref, sv_ref, ss_ref, w_ref, r2_ref, acc_ref, *, bsz, qb, orders):
    gb = ut_ref.shape[0] // S5_GROUP
    t_len = S5_CHUNK
    k = S5_GROUP * t_len
    q = ut_ref.shape[1] // t_len
    row_s = lax.broadcasted_iota(jnp.int32, (k, k), 0) // S5_GROUP
    col_t = lax.broadcasted_iota(jnp.int32, (k, k), 1) // S5_GROUP
    masks = (row_s <= col_t, row_s >= col_t)
    w_ref[...] = jnp.zeros_like(w_ref)
    us = []
    for g in range(gb):
        x = ut_ref[g * S5_GROUP:(g + 1) * S5_GROUP, :]
        us.append(jnp.concatenate([x[:, s * q:(s + 1) * q] for s in range(t_len)], axis=0).T)
    pairs = [(g, d) for g in range(gb) for d in range(2)]
    l2s, raw = [], []
    for g, d in pairs:
        pk = pk_ref[d, g]
        l2 = (_repeat_rows(pk[_AN1:_AN1 + t_len], S5_GROUP) * _tile_rows(pk[_BB1:_BB1 + S5_GROUP], t_len)
              + _repeat_rows(pk[_AN2:_AN2 + t_len], S5_GROUP) * _tile_rows(pk[_BB2:_BB2 + S5_GROUP], t_len))
        r2t = (_repeat_rows(pk[_AP1:_AP1 + t_len], S5_GROUP) * _tile_rows(pk[_CC1:_CC1 + S5_GROUP], t_len)
               + _repeat_rows(pk[_AP2:_AP2 + t_len], S5_GROUP) * _tile_rows(pk[_CC2:_CC2 + S5_GROUP], t_len))
        r2 = r2t.T
        r2_ref[2 * g + d] = r2.astype(BF16)
        l2s.append(l2.astype(BF16))
        raw.append(_dot3(l2, r2))
    for (g, d), l2, toep_raw in zip(pairs, l2s, raw):
        i = 2 * g + d
        y_local = _dot(us[g], jnp.where(masks[d], toep_raw, 0.0).astype(BF16))
        if d == 0:
            acc_ref[g] = y_local
        else:
            acc_ref[g] = acc_ref[g] + y_local
        v = _dot(us[g], l2)
        vs = pltpu.roll(v, v.shape[1] // 2, axis=1)
        pk = pk_ref[d, g]
        pt, qt, qts = pk[_PQ:_PQ + 1], pk[_PQ + 1:_PQ + 2], pk[_PQ + 2:_PQ + 3]
        sv_ref[i] = pt * v + qt * vs
        ss_ref[i] = pt * vs + qts * v
    coef = []
    for g in range(gb):
        for d in range(2):
            pk = pk_ref[d, g]
            coef.append(tuple(jnp.broadcast_to(pk[_PQ + r:_PQ + r + 1], (bsz, pk.shape[1])) for r in range(3)))
    zero = jnp.zeros((bsz, pk_ref.shape[3]), F32)
    state = [(zero, zero)] * (2 * gb)
    nsteps = len(orders[0])
    for step in range(nsteps):
        for i in range(2 * gb):
            rows = pl.ds(orders[i % 2][step], bsz, stride=qb)
            w, ws = state[i]
            w_ref[i, rows, :] = w
            if step + 1 < nsteps:
                pt, qt, qts = coef[i]
                state[i] = (pt * w + qt * ws + sv_ref[i, rows, :], pt * ws + qts * w + ss_ref[i, rows, :])
    for g in range(gb):
        y = acc_ref[g]
        for d in range(2):
            i = 2 * g + d
            y = y + _dot(w_ref[i].astype(BF16), r2_ref[i])
        yt = y.T
        for t in range(t_len):
            y_ref[g * S5_GROUP:(g + 1) * S5_GROUP, t * q:(t + 1) * q] = yt[t * S5_GROUP:(t + 1) * S5_GROUP, :]


def _s5_scan(ut, pk, bsz, nctx_chunks, nlat_chunks):
    t_len = S5_CHUNK
    q = ut.shape[1] // t_len
    ngroups = ut.shape[0] // S5_GROUP
    np2 = pk.shape[3]
    gb = S5_GROUP_BLOCK
    qb = S5_LANES_PER_BATCH
    lat = list(range(nlat_chunks))
    ctx = list(range(nlat_chunks, nlat_chunks + nctx_chunks))
    orders = (ctx + lat, ctx[::-1] + lat[::-1])
    blk = (gb * S5_GROUP, t_len * q)
    return pl.pallas_call(
        functools.partial(_s5_kernel, bsz=bsz, qb=qb, orders=orders),
        out_shape=jax.ShapeDtypeStruct(ut.shape, F32),
        grid=(ngroups // gb,),
        in_specs=[pl.BlockSpec(blk, lambda g: (g, 0)),
                  pl.BlockSpec((2, gb, _PK_ROWS, np2), lambda g: (0, g, 0, 0))],
        out_specs=pl.BlockSpec(blk, lambda g: (g, 0)),
        scratch_shapes=[pltpu.VMEM((2 * gb, q, np2), F32), pltpu.VMEM((2 * gb, q, np2), F32),
                        pltpu.VMEM((2 * gb, q, np2), F32), pltpu.VMEM((2 * gb, np2, S5_GROUP * t_len), BF16),
                        pltpu.VMEM((gb, q, S5_GROUP * t_len), F32)],
        compiler_params=_cp(("arbitrary",), VMEM_LIMIT),
        name="s5_scan",
    )(ut, pk)


def _s5_out_kernel(yt_ref, hl_ref, hc_ref, nw_ref, mod_ref, d_ref, ol_ref, oc_ref, *, bsz, qb):
    y = yt_ref[...].T
    nw = nw_ref[...]
    nlat, nc = hl_ref.shape[1], hc_ref.shape[1]
    c0 = 0.7978845608028654

    def act(yv, hv, region, b):
        u = _normmod_rows(hv, nw, mod_ref[b, region, SC1:SC1 + 1, :], mod_ref[b, region, SH1:SH1 + 1, :])
        z = yv + u * d_ref[...]
        return (0.5 * z * (1.0 + jnp.tanh(c0 * (z + 0.044715 * (z * z * z))))).astype(ol_ref.dtype)

    for b in range(bsz):
        ol_ref[b] = act(y[b * qb:b * qb + nlat], hl_ref[b], 1, b)
        oc_ref[b] = act(y[b * qb + nlat:b * qb + nlat + nc], hc_ref[b], 0, b)


def _s5_out(yt, h, nw, modtab, d_skip, nctx):
    bsz, length, d = h.shape
    t_len = S5_CHUNK
    qb = S5_LANES_PER_BATCH
    nctx_chunks = nctx // t_len
    seq = length - nctx
    lat_spec = pl.BlockSpec((bsz, None, GRID_W, d), lambda s: (0, nctx // GRID_W + s, 0, 0))
    ctx_spec = pl.BlockSpec((bsz, nctx_chunks, d), lambda s: (0, 0, s))
    act_l, act_c = pl.pallas_call(
        functools.partial(_s5_out_kernel, bsz=bsz, qb=qb),
        out_shape=(jax.ShapeDtypeStruct((bsz, seq // GRID_W, GRID_W, d), BF16),
                   jax.ShapeDtypeStruct((bsz, nctx_chunks, t_len * d), BF16)),
        grid=(t_len,),
        in_specs=[pl.BlockSpec((d, bsz * qb), lambda s: (0, s)),
                  lat_spec, ctx_spec,
                  pl.BlockSpec((1, d), lambda s: (0, 0)),
                  pl.BlockSpec(modtab.shape, lambda s: (0, 0, 0, 0)),
                  pl.BlockSpec((1, d), lambda s: (0, 0))],
        out_specs=(pl.BlockSpec((bsz, None, GRID_W, d), lambda s: (0, s, 0, 0)), ctx_spec),
        compiler_params=_cp(("arbitrary",), VMEM_LIMIT),
        name="s5_out",
    )(yt, h.reshape(bsz, length // GRID_W, GRID_W, d), h[:, :nctx].reshape(bsz, nctx_chunks, t_len * d),
      nw.reshape(1, d), modtab, d_skip.reshape(1, d))
    return act_l.reshape(bsz, seq, d), act_c.reshape(bsz, nctx, d)


def _s5_layer(h, modtab, nw, a_re, a_im, log_dt, b_re, b_im, c_re, c_im, d_skip, w_glu, b_glu, layer, nctx,
              last):
    bsz, length, d = h.shape
    seq = length - nctx
    t_len = S5_CHUNK
    assert seq // GRID_W == t_len and nctx % t_len == 0 and nctx % GRID_W == 0
    assert seq // t_len + nctx // t_len <= S5_LANES_PER_BATCH
    ngroups = d // S5_GROUP
    ut = _s5_in(h, nw, modtab, nctx)
    pk = _s5_params(a_re, a_im, log_dt, b_re, b_im, c_re, c_im)
    yt = _s5_scan(ut, pk, bsz, nctx // t_len, GRID_W)
    act_l, act_c = _s5_out(yt, h, nw, modtab, d_skip, nctx)
    if last:
        act, h, nct = act_l, h[:, nctx:], 0
    else:
        act, nct = jnp.concatenate([act_c, act_l], axis=1), nctx // ROW_TILE
    rows = h.shape[0] * h.shape[1]
    out = _mm_glu_res(act.reshape(rows, d), w_glu, b_glu, layer, h.reshape(rows, d), modtab, G1,
                      h.shape[1] // ROW_TILE, nct, 512, "s5_glu")
    return out.reshape(h.shape)


def _dense_ffn(h, modtab, nw, w13, w2, layer, nct):
    bsz, length, d = h.shape
    rows = bsz * length
    v = _normmod(h, nw, modtab, SH2, SC2, nct, BF16).reshape(rows, d)
    act = _mm_swiglu(v, w13, layer, 512, 512, "ffn_w13")
    out = _mm_res(act, w2, layer, h.reshape(rows, d), modtab, G2, length // ROW_TILE, nct, 512, "ffn_w2")
    return out.reshape(h.shape)


def _moe_gather_kernel(src_ref, nxt_ref, v_hbm, o_ref, buf_ref, sem):
    t = pl.program_id(0)
    slot = t % 2

    def request(idx_ref, s):
        def issue(r8, carry):
            for i in range(DMA_ISSUE_UNROLL):
                r = r8 * DMA_ISSUE_UNROLL + i
                pltpu.make_async_copy(v_hbm.at[pl.ds(idx_ref[0, r], 1)], buf_ref.at[s, pl.ds(r, 1)],
                                      sem.at[s]).start(priority=i % 2)
            return carry
        lax.fori_loop(0, MOE_TILE // DMA_ISSUE_UNROLL, issue, 0)

    @pl.when(t == 0)
    def _():
        request(src_ref, 0)

    @pl.when(t + 1 < pl.num_programs(0))
    def _():
        request(nxt_ref, 1 - slot)

    pltpu.make_async_copy(v_hbm.at[pl.ds(0, MOE_TILE)], buf_ref.at[slot], sem.at[slot]).wait()
    o_ref[...] = buf_ref[slot].astype(o_ref.dtype)


def _moe_gather(src, v):
    ntiles = src.shape[0]
    d = v.shape[1]
    idx_block = (None, 1, MOE_TILE)
    return pl.pallas_call(
        _moe_gather_kernel,
        out_shape=jax.ShapeDtypeStruct((ntiles * MOE_TILE, d), BF16),
        grid=(ntiles,),
        in_specs=[pl.BlockSpec(idx_block, lambda t: (t, 0, 0), memory_space=pltpu.SMEM),
                  pl.BlockSpec(idx_block, lambda t: (jnp.minimum(t + 1, ntiles - 1), 0, 0),
                               memory_space=pltpu.SMEM),
                  pl.BlockSpec(memory_space=pl.ANY)],
        out_specs=pl.BlockSpec((MOE_TILE, d), lambda t: (t, 0)),
        scratch_shapes=[pltpu.VMEM((2, MOE_TILE, d), F32), pltpu.SemaphoreType.DMA((2,))],
        compiler_params=_cp(("arbitrary",)),
        name="moe_gather",
    )(src, src, v)


def _moe_fetch_weights(te_ref, first_ref, nxt_ref, w_hbm, stage_ref, sem, wbf_ref, col_blocks, layer):
    j, t = pl.program_id(0), pl.program_id(1)
    nsweeps = pl.num_programs(0)
    tn = stage_ref.shape[2]

    def copies(e, sweep):
        return [pltpu.make_async_copy(w_hbm.at[layer, e, :, pl.ds(pl.multiple_of(cb(sweep) * tn, 128), tn)],
                                      stage_ref.at[i], sem.at[i])
                for i, cb in enumerate(col_blocks)]

    @pl.when(jnp.logical_and(j == 0, t == 0))
    def _():
        for c in copies(te_ref[0], 0):
            c.start()

    @pl.when(first_ref[t] == 1)
    def _():
        for c in copies(te_ref[t], j):
            c.wait()
        for i in range(len(col_blocks)):
            wbf_ref[:, i * tn:(i + 1) * tn] = stage_ref[i].astype(BF16)
        sweep_done = nxt_ref[t] < 0
        next_e = jnp.where(sweep_done, te_ref[0], nxt_ref[t])
        next_sweep = jnp.where(sweep_done, j + 1, j)

        @pl.when(next_sweep < nsweeps)
        def _():
            for c in copies(next_e, next_sweep):
                c.start()


def _moe_swiglu_kernel(te_ref, nv_ref, first_ref, nxt_ref, x_ref, w_hbm, o_ref, wbf_ref, stage_ref, sem, *, layer):
    nb = pl.num_programs(0)
    _moe_fetch_weights(te_ref, first_ref, nxt_ref, w_hbm, stage_ref, sem, wbf_ref,
                       [lambda sweep: sweep, lambda sweep: nb + sweep], layer)

    in_use = pl.program_id(1) < nv_ref[0]

    @pl.when(in_use)
    def _():
        tn = o_ref.shape[1]
        z = _dot(x_ref[...], wbf_ref[...])
        o_ref[...] = (_silu(z[:, :tn]) * z[:, tn:]).astype(o_ref.dtype)

    @pl.when(jnp.logical_not(in_use))
    def _():
        o_ref[...] = jnp.zeros_like(o_ref)


def _moe_mm_kernel(te_ref, nv_ref, first_ref, nxt_ref, x_ref, w_hbm, o_ref, wbf_ref, stage_ref, sem, *, layer):
    _moe_fetch_weights(te_ref, first_ref, nxt_ref, w_hbm, stage_ref, sem, wbf_ref, [lambda sweep: sweep], layer)

    in_use = pl.program_id(1) < nv_ref[0]

    @pl.when(in_use)
    def _():
        o_ref[...] = _dot(x_ref[...], wbf_ref[...])

    @pl.when(jnp.logical_not(in_use))
    def _():
        o_ref[...] = jnp.zeros_like(o_ref)


def _moe_swiglu(route, xs, w13, layer, tn):
    rows, k = xs.shape
    half = w13.shape[3] // 2
    nb = half // tn
    tm = MOE_TILE
    return pl.pallas_call(
        functools.partial(_moe_swiglu_kernel, layer=layer),
        out_shape=jax.ShapeDtypeStruct((rows, half), BF16),
        grid_spec=pltpu.PrefetchScalarGridSpec(
            num_scalar_prefetch=4,
            grid=(nb, rows // tm),
            in_specs=[pl.BlockSpec((tm, k), lambda j, t, *_: (t, 0)),
                      pl.BlockSpec(memory_space=pl.ANY)],
            out_specs=pl.BlockSpec((tm, tn), lambda j, t, *_: (t, j)),
            scratch_shapes=[pltpu.VMEM((k, 2 * tn), BF16), pltpu.VMEM((2, k, tn), F32),
                            pltpu.SemaphoreType.DMA((2,))]),
        compiler_params=_cp(("arbitrary", "arbitrary"), VMEM_LIMIT),
        name="moe_w13",
    )(*route, xs, w13)


def _moe_mm(route, xs, w2, layer, tn):
    rows, k = xs.shape
    n = w2.shape[3]
    tm = MOE_TILE
    return pl.pallas_call(
        functools.partial(_moe_mm_kernel, layer=layer),
        out_shape=jax.ShapeDtypeStruct((rows, n), F32),
        grid_spec=pltpu.PrefetchScalarGridSpec(
            num_scalar_prefetch=4,
            grid=(n // tn, rows // tm),
            in_specs=[pl.BlockSpec((tm, k), lambda j, t, *_: (t, 0)),
                      pl.BlockSpec(memory_space=pl.ANY)],
            out_specs=pl.BlockSpec((tm, tn), lambda j, t, *_: (t, j)),
            scratch_shapes=[pltpu.VMEM((k, tn), BF16), pltpu.VMEM((1, k, tn), F32),
                            pltpu.SemaphoreType.DMA((1,))]),
        compiler_params=_cp(("arbitrary", "arbitrary"), VMEM_LIMIT),
        name="moe_w2",
    )(*route, xs, w2)


def _moe_combine_kernel(pos_ref, nxt_ref, gate_ref, y_hbm, res_ref, mod_ref, o_ref, buf_ref, sem, *, g_row):
    m = pl.program_id(0)
    slot = m % 2

    def request(idx_ref, s):
        for k in range(TOP_K):
            def issue(r8, carry, k=k):
                for i in range(DMA_ISSUE_UNROLL):
                    r = r8 * DMA_ISSUE_UNROLL + i
                    pltpu.make_async_copy(y_hbm.at[pl.ds(idx_ref[k, r], 1)], buf_ref.at[s, k, pl.ds(r, 1)],
                                          sem.at[s]).start(priority=i % 2)
                return carry
            lax.fori_loop(0, ROW_TILE // DMA_ISSUE_UNROLL, issue, 0)

    @pl.when(m == 0)
    def _():
        request(pos_ref, 0)

    @pl.when(m + 1 < pl.num_programs(0))
    def _():
        request(nxt_ref, 1 - slot)

    for k in range(TOP_K):
        pltpu.make_async_copy(y_hbm.at[pl.ds(0, ROW_TILE)], buf_ref.at[slot, k], sem.at[slot]).wait()
    p = gate_ref[...]
    mix = p[:, 0:1] * buf_ref[slot, 0] + p[:, 1:2] * buf_ref[slot, 1]
    o_ref[...] = res_ref[...] + mod_ref[g_row:g_row + 1, :] * mix


def _moe_combine(pos, gates, ys, res, modtab, g_row, tpb, nct):
    n, d = res.shape
    tm = ROW_TILE
    ntiles = n // tm
    return pl.pallas_call(
        functools.partial(_moe_combine_kernel, g_row=g_row),
        out_shape=jax.ShapeDtypeStruct((n, d), F32),
        grid=(ntiles,),
        in_specs=[pl.BlockSpec((None, TOP_K, tm), lambda m: (m, 0, 0), memory_space=pltpu.SMEM),
                  pl.BlockSpec((None, TOP_K, tm), lambda m: (jnp.minimum(m + 1, ntiles - 1), 0, 0),
                               memory_space=pltpu.SMEM),
                  pl.BlockSpec((tm, ROUTER_LANES), lambda m: (m, 0)),
                  pl.BlockSpec(memory_space=pl.ANY),
                  pl.BlockSpec((tm, d), lambda m: (m, 0)),
                  pl.BlockSpec((None, None, 6, d), lambda m: (m // tpb, jnp.where(m % tpb >= nct, 1, 0), 0, 0))],
        out_specs=pl.BlockSpec((tm, d), lambda m: (m, 0)),
        scratch_shapes=[pltpu.VMEM((2, TOP_K, tm, d), F32), pltpu.SemaphoreType.DMA((2,))],
        compiler_params=_cp(("arbitrary",), VMEM_LIMIT),
        name="moe_combine",
    )(pos, pos, gates, ys, res, modtab)


def _moe_route(idx2):
    n = idx2.shape[0]
    flat_e = idx2.reshape(-1)
    onehot = (flat_e[:, None] == jnp.arange(N_EXPERTS, dtype=jnp.int32)[None, :]).astype(jnp.int32)
    csum = jnp.cumsum(onehot, axis=0)
    rank = jnp.sum((csum - onehot) * onehot, axis=1)
    counts = csum[-1]
    ntile = (counts + MOE_TILE - 1) // MOE_TILE
    tile_end = jnp.cumsum(ntile)
    tile_start = tile_end - ntile
    pos = tile_start[flat_e] * MOE_TILE + rank
    tiles = (TOP_K * n) // MOE_TILE + N_EXPERTS
    src = jnp.zeros((tiles * MOE_TILE,), jnp.int32).at[pos].set(jnp.arange(TOP_K * n, dtype=jnp.int32) // TOP_K)
    nv = tile_end[-1:]
    tile_ids = jnp.arange(tiles, dtype=jnp.int32)
    tids = jnp.minimum(tile_ids, nv[0] - 1)
    te = jnp.sum((tile_end[None, :] <= tids[:, None]).astype(jnp.int32), axis=1)
    first = jnp.logical_and(tile_ids < nv[0], jnp.logical_or(tile_ids == 0, te != jnp.roll(te, 1)))
    run_end = tile_end[te]
    nxt = jnp.where(run_end < nv[0], te[jnp.minimum(run_end, tiles - 1)], -1)
    route = (te, nv.astype(jnp.int32), first.astype(jnp.int32), nxt.astype(jnp.int32))
    return pos.astype(jnp.int32), src.reshape(tiles, 1, MOE_TILE), route


def _moe_ffn(h, modtab, nw, w_router, w13, w2, layer, nct):
    bsz, length, d = h.shape
    n = bsz * length
    v, idx, gates = _normmod_router(h, nw, modtab, w_router, SH2, SC2, nct)
    idx2 = idx.reshape(n, ROUTER_LANES)[:, :TOP_K]
    pos, src, route = _moe_route(idx2)
    xs = _moe_gather(src, v.reshape(n, d))
    act = _moe_swiglu(route, xs, w13, layer, MOE_W13_COLS)
    ys = _moe_mm(route, act, w2, layer, MOE_W2_COLS)
    pos3 = pos.reshape(n // ROW_TILE, ROW_TILE, TOP_K).transpose(0, 2, 1)
    out = _moe_combine(pos3, gates.reshape(n, ROUTER_LANES), ys, h.reshape(n, d), modtab, G2,
                       length // ROW_TILE, nct)
    return out.reshape(h.shape)


def kernel(x, c, ctx, c_ctx, w_mod, b_mod, norm1, norm2, norm_f, gla_w_in, gla_w_gate_up, gla_b_gate,
           gla_g_head, gla_w_out, ffn_w13, ffn_w2, s5_a_re, s5_a_im, s5_log_dt, s5_b_re, s5_b_im, s5_c_re,
           s5_c_im, s5_d, s5_w_glu, s5_b_glu, moe_w_router, moe_w13, moe_w2):
    bsz, seq, d = x.shape
    nctx = ctx.shape[1]
    depth = w_mod.shape[0]
    assert bsz + 1 <= 8 and seq % ROW_TILE == 0 and nctx % ROW_TILE == 0 and seq % GRID_W == 0
    nct = nctx // ROW_TILE

    c8 = jnp.concatenate([c, c_ctx[None, :], jnp.zeros((8 - bsz - 1, d), F32)], axis=0)
    mods = _modulation(c8, w_mod, b_mod).reshape(depth, 8, 6, d)
    h = jnp.concatenate([ctx, x], axis=1)
    for i in range(depth):
        last = i == depth - 1
        j = i // 2
        modtab = jnp.stack([jnp.broadcast_to(mods[i, bsz], (bsz, 6, d)), mods[i, :bsz]], axis=1)
        if i % 2 == 0:
            h = _gla_layer(h, modtab, norm1[i], gla_w_in, j, gla_w_gate_up[j], gla_b_gate[j], gla_g_head[j],
                           gla_w_out, nctx)
        else:
            h = _s5_layer(h, modtab, norm1[i], s5_a_re[j], s5_a_im[j], s5_log_dt[j], s5_b_re[j], s5_b_im[j],
                          s5_c_re[j], s5_c_im[j], s5_d[j], s5_w_glu, s5_b_glu, j, nctx, last)
        ffn_nct = 0 if last else nct
        if i % 2 == 0:
            h = _dense_ffn(h, modtab, norm2[i], ffn_w13, ffn_w2, j, ffn_nct)
        else:
            h = _moe_ffn(h, modtab, norm2[i], moe_w_router[j], moe_w13, moe_w2, j, ffn_nct)
    if h.shape[1] != seq:
        h = h[:, nctx:]
    return _final_norm(h, norm_f)
```

```python
import functools

import jax
import jax.numpy as jnp
from jax import lax
from jax.experimental import pallas as pl
from jax.experimental.pallas import tpu as pltpu

F32 = jnp.float32
BF16 = jnp.bfloat16

EPS = 1e-6
GRID_W = 64

GLA_HEADS = 4
GLA_GATE_RANK = 16
GLA_GATE_TAU = 16.0
GLA_CHUNK = 64

S5_GROUP = 16
S5_CHUNK = 32
S5_GROUP_BLOCK = 4

N_EXPERTS = 8
TOP_K = 2
ROUTER_LANES = 128

ROW_TILE = 256
MOE_TILE = 256
DMA_ISSUE_UNROLL = 8
MOE_W13_COLS = 1408
MOE_W2_COLS = 1024
VMEM_LIMIT = 56 * 1024 * 1024

SH1, SC1, G1, SH2, SC2, G2 = range(6)


def _cp(sem, vmem=None):
    return pltpu.CompilerParams(dimension_semantics=sem, vmem_limit_bytes=vmem)


def _dot(a, b):
    return jnp.dot(a, b, preferred_element_type=F32)


def _split(x):
    hi = x.astype(BF16)
    lo = (x - hi.astype(F32)).astype(BF16)
    return hi, lo


def _dot3(a, b):
    ah, al = _split(a)
    bh, bl = _split(b)
    return _dot(ah, bh) + _dot(ah, bl) + _dot(al, bh)


def _sigmoid(x):
    return 1.0 / (1.0 + jnp.exp(-x))


def _silu(x):
    return x * _sigmoid(x)


def _mod_kernel(c_ref, w_ref, b_ref, o_ref):
    x = _silu(c_ref[...]).astype(BF16)
    o_ref[...] = _dot(x, w_ref[...].astype(BF16)) + b_ref[...]


def _modulation(c8, w_mod, b_mod):
    depth, d, n = w_mod.shape
    tn = 1024
    return pl.pallas_call(
        _mod_kernel,
        out_shape=jax.ShapeDtypeStruct((depth, 8, n), F32),
        grid=(depth, n // tn),
        in_specs=[
            pl.BlockSpec((8, d), lambda i, j: (0, 0)),
            pl.BlockSpec((None, d, tn), lambda i, j: (i, 0, j)),
            pl.BlockSpec((None, 1, tn), lambda i, j: (i, 0, j)),
        ],
        out_specs=pl.BlockSpec((None, 8, tn), lambda i, j: (i, 0, j)),
        compiler_params=_cp(("arbitrary", "arbitrary"), VMEM_LIMIT),
        name="modulation",
    )(c8, w_mod, b_mod.reshape(depth, 1, n))


def _normmod_rows(x, nw, scale, shift):
    ms = jnp.mean(x * x, axis=-1, keepdims=True)
    return x * lax.rsqrt(ms + EPS) * nw * (1.0 + scale) + shift


def _normmod_body(x_ref, nw_ref, mod_ref, sh_row, sc_row):
    return _normmod_rows(x_ref[...], nw_ref[...], mod_ref[sc_row:sc_row + 1, :], mod_ref[sh_row:sh_row + 1, :])


def _normmod_kernel(x_ref, nw_ref, mod_ref, o_ref, *, sh_row, sc_row):
    o_ref[...] = _normmod_body(x_ref, nw_ref, mod_ref, sh_row, sc_row).astype(o_ref.dtype)


def _normmod_router_kernel(x_ref, nw_ref, mod_ref, wr_ref, o_ref, idx_ref, gate_ref, *, sh_row, sc_row):
    y = _normmod_body(x_ref, nw_ref, mod_ref, sh_row, sc_row)
    o_ref[...] = y
    logits = _dot3(y, wr_ref[...])
    lane = lax.broadcasted_iota(jnp.int32, logits.shape, 1)
    lane_f = lane.astype(F32)
    neg = jnp.float32(-jnp.inf)
    l1 = jnp.where(lane < N_EXPERTS, logits, neg)
    m1 = jnp.max(l1, axis=-1, keepdims=True)
    i1 = jnp.min(jnp.where(l1 == m1, lane_f, float(ROUTER_LANES)), axis=-1, keepdims=True)
    l2 = jnp.where(lane_f == i1, neg, l1)
    m2 = jnp.max(l2, axis=-1, keepdims=True)
    i2 = jnp.min(jnp.where(l2 == m2, lane_f, float(ROUTER_LANES)), axis=-1, keepdims=True)
    e = jnp.exp(m2 - m1)
    g1 = 1.0 / (1.0 + e)
    g2 = e * g1
    idx_ref[...] = jnp.where(lane == 0, i1, jnp.where(lane == 1, i2, 0.0)).astype(jnp.int32)
    gate_ref[...] = jnp.where(lane == 0, g1, jnp.where(lane == 1, g2, 0.0))


def _row_specs(d, nct):
    x_spec = pl.BlockSpec((None, ROW_TILE, d), lambda b, t: (b, t, 0))
    mod_spec = pl.BlockSpec((None, None, 6, d), lambda b, t: (b, jnp.where(t >= nct, 1, 0), 0, 0))
    return x_spec, mod_spec


def _normmod(h, nw, modtab, sh_row, sc_row, nct, out_dtype):
    bsz, length, d = h.shape
    x_spec, mod_spec = _row_specs(d, nct)
    return pl.pallas_call(
        functools.partial(_normmod_kernel, sh_row=sh_row, sc_row=sc_row),
        out_shape=jax.ShapeDtypeStruct(h.shape, out_dtype),
        grid=(bsz, length // ROW_TILE),
        in_specs=[x_spec, pl.BlockSpec((1, d), lambda b, t: (0, 0)), mod_spec],
        out_specs=x_spec,
        compiler_params=_cp(("arbitrary", "arbitrary")),
        name="normmod",
    )(h, nw.reshape(1, d), modtab)


def _normmod_router(h, nw, modtab, w_router, sh_row, sc_row, nct):
    bsz, length, d = h.shape
    x_spec, mod_spec = _row_specs(d, nct)
    wr = jnp.pad(w_router, ((0, 0), (0, ROUTER_LANES - w_router.shape[1])))
    lane_spec = pl.BlockSpec((None, ROW_TILE, ROUTER_LANES), lambda b, t: (b, t, 0))
    return pl.pallas_call(
        functools.partial(_normmod_router_kernel, sh_row=sh_row, sc_row=sc_row),
        out_shape=(
            jax.ShapeDtypeStruct(h.shape, F32),
            jax.ShapeDtypeStruct((bsz, length, ROUTER_LANES), jnp.int32),
            jax.ShapeDtypeStruct((bsz, length, ROUTER_LANES), F32),
        ),
        grid=(bsz, length // ROW_TILE),
        in_specs=[x_spec, pl.BlockSpec((1, d), lambda b, t: (0, 0)), mod_spec,
                  pl.BlockSpec((d, ROUTER_LANES), lambda b, t: (0, 0))],
        out_specs=(x_spec, lane_spec, lane_spec),
        compiler_params=_cp(("arbitrary", "arbitrary")),
        name="normmod_router",
    )(h, nw.reshape(1, d), modtab, wr)


def _rms_kernel(x_ref, nw_ref, o_ref):
    x = x_ref[...]
    ms = jnp.mean(x * x, axis=-1, keepdims=True)
    o_ref[...] = x * lax.rsqrt(ms + EPS) * nw_ref[...]


def _final_norm(h, nw):
    bsz, length, d = h.shape
    x_spec = pl.BlockSpec((None, ROW_TILE, d), lambda b, t: (b, t, 0))
    return pl.pallas_call(
        _rms_kernel,
        out_shape=jax.ShapeDtypeStruct(h.shape, F32),
        grid=(bsz, length // ROW_TILE),
        in_specs=[x_spec, pl.BlockSpec((1, d), lambda b, t: (0, 0))],
        out_specs=x_spec,
        compiler_params=_cp(("arbitrary", "arbitrary")),
        name="final_norm",
    )(h, nw.reshape(1, d))


def _cast_weights(pairs):
    @pl.when(pl.program_id(1) == 0)
    def _():
        for w_ref, wbf_ref in pairs:
            wbf_ref[...] = w_ref[...].astype(BF16)


def _mm_plain_kernel(x_ref, w_ref, o_ref, wbf_ref):
    _cast_weights([(w_ref, wbf_ref)])
    o_ref[...] = _dot(x_ref[...], wbf_ref[...]).astype(o_ref.dtype)


def _mm_swiglu_kernel(x_ref, wa_ref, wb_ref, o_ref, wabf_ref, wbbf_ref):
    _cast_weights([(wa_ref, wabf_ref), (wb_ref, wbbf_ref)])
    x = x_ref[...]
    a = _dot(x, wabf_ref[...])
    b = _dot(x, wbbf_ref[...])
    o_ref[...] = (_silu(a) * b).astype(o_ref.dtype)


def _mm_res_kernel(x_ref, w_ref, res_ref, mod_ref, o_ref, wbf_ref, *, g_row):
    _cast_weights([(w_ref, wbf_ref)])
    acc = _dot(x_ref[...], wbf_ref[...])
    o_ref[...] = res_ref[...] + mod_ref[g_row:g_row + 1, :] * acc


def _mm_glu_res_kernel(x_ref, wa_ref, wb_ref, ba_ref, bb_ref, res_ref, mod_ref, o_ref, wabf_ref, wbbf_ref,
                       *, g_row):
    _cast_weights([(wa_ref, wabf_ref), (wb_ref, wbbf_ref)])
    x = x_ref[...]
    z1 = _dot(x, wabf_ref[...]) + ba_ref[...]
    z2 = _dot(x, wbbf_ref[...]) + bb_ref[...]
    o_ref[...] = res_ref[...] + mod_ref[g_row:g_row + 1, :] * (z1 * _sigmoid(z2))


def _mm_plain(x, w, layer, n, tm, tn, out_dtype, name):
    r, k = x.shape
    assert r % tm == 0 and n % tn == 0
    return pl.pallas_call(
        _mm_plain_kernel,
        out_shape=jax.ShapeDtypeStruct((r, n), out_dtype),
        grid=(n // tn, r // tm),
        in_specs=[pl.BlockSpec((tm, k), lambda j, m: (m, 0)),
                  pl.BlockSpec((None, k, tn), lambda j, m: (layer, 0, j))],
        out_specs=pl.BlockSpec((tm, tn), lambda j, m: (m, j)),
        scratch_shapes=[pltpu.VMEM((k, tn), BF16)],
        compiler_params=_cp(("arbitrary", "arbitrary"), VMEM_LIMIT),
        name=name,
    )(x, w)


def _mm_swiglu(x, w13, layer, tm, tn, name):
    r, k = x.shape
    half = w13.shape[2] // 2
    nb = half // tn
    assert r % tm == 0 and half % tn == 0
    return pl.pallas_call(
        _mm_swiglu_kernel,
        out_shape=jax.ShapeDtypeStruct((r, half), BF16),
        grid=(nb, r // tm),
        in_specs=[pl.BlockSpec((tm, k), lambda j, m: (m, 0)),
                  pl.BlockSpec((None, k, tn), lambda j, m: (layer, 0, j)),
                  pl.BlockSpec((None, k, tn), lambda j, m: (layer, 0, nb + j))],
        out_specs=pl.BlockSpec((tm, tn), lambda j, m: (m, j)),
        scratch_shapes=[pltpu.VMEM((k, tn), BF16), pltpu.VMEM((k, tn), BF16)],
        compiler_params=_cp(("arbitrary", "arbitrary"), VMEM_LIMIT),
        name=name,
    )(x, w13, w13)


def _res_specs(tn, tpb, nct):
    res_spec = pl.BlockSpec((ROW_TILE, tn), lambda j, m: (m, j))
    mod_spec = pl.BlockSpec((None, None, 6, tn),
                            lambda j, m: (m // tpb, jnp.where(m % tpb >= nct, 1, 0), 0, j))
    return res_spec, mod_spec


def _mm_res(x, w, layer, res, modtab, g_row, tpb, nct, tn, name):
    r, k = x.shape
    n = w.shape[2]
    tm = ROW_TILE
    res_spec, mod_spec = _res_specs(tn, tpb, nct)
    return pl.pallas_call(
        functools.partial(_mm_res_kernel, g_row=g_row),
        out_shape=jax.ShapeDtypeStruct((r, n), F32),
        grid=(n // tn, r // tm),
        in_specs=[pl.BlockSpec((tm, k), lambda j, m: (m, 0)),
                  pl.BlockSpec((None, k, tn), lambda j, m: (layer, 0, j)),
                  res_spec, mod_spec],
        out_specs=pl.BlockSpec((tm, tn), lambda j, m: (m, j)),
        scratch_shapes=[pltpu.VMEM((k, tn), BF16)],
        compiler_params=_cp(("arbitrary", "arbitrary"), VMEM_LIMIT),
        name=name,
    )(x, w, res, modtab)


def _mm_glu_res(x, w, bias, layer, res, modtab, g_row, tpb, nct, tn, name):
    r, k = x.shape
    half = w.shape[2] // 2
    nb = half // tn
    tm = ROW_TILE
    res_spec, mod_spec = _res_specs(tn, tpb, nct)
    bias2 = bias.reshape(bias.shape[0], 1, 2 * half)
    return pl.pallas_call(
        functools.partial(_mm_glu_res_kernel, g_row=g_row),
        out_shape=jax.ShapeDtypeStruct((r, half), F32),
        grid=(nb, r // tm),
        in_specs=[pl.BlockSpec((tm, k), lambda j, m: (m, 0)),
                  pl.BlockSpec((None, k, tn), lambda j, m: (layer, 0, j)),
                  pl.BlockSpec((None, k, tn), lambda j, m: (layer, 0, nb + j)),
                  pl.BlockSpec((None, 1, tn), lambda j, m: (layer, 0, j)),
                  pl.BlockSpec((None, 1, tn), lambda j, m: (layer, 0, nb + j)),
                  res_spec, mod_spec],
        out_specs=pl.BlockSpec((tm, tn), lambda j, m: (m, j)),
        scratch_shapes=[pltpu.VMEM((k, tn), BF16), pltpu.VMEM((k, tn), BF16)],
        compiler_params=_cp(("arbitrary", "arbitrary"), VMEM_LIMIT),
        name=name,
    )(x, w, w, bias2, bias2, res, modtab)


def _gla_gate_kernel(x_ref, wgl_ref, wup_ref, b_ref, o_ref):
    wh, wl = _split(wgl_ref[...])
    x = x_ref[...]
    gl = _dot(x, wh) + _dot(x, wl)
    z = _dot3(gl, wup_ref[...]) + b_ref[...]
    log_sig = jnp.minimum(z, 0.0) - jnp.log(1.0 + jnp.exp(-jnp.abs(z)))
    o_ref[...] = log_sig * (1.0 / GLA_GATE_TAU)


def _gla_gate(u, w_gl, w_up_bd, b_gate):
    r, d = u.shape
    n = w_up_bd.shape[1]
    tm = 512
    return pl.pallas_call(
        _gla_gate_kernel,
        out_shape=jax.ShapeDtypeStruct((r, n), F32),
        grid=(r // tm,),
        in_specs=[pl.BlockSpec((tm, d), lambda m: (m, 0)),
                  pl.BlockSpec(w_gl.shape, lambda m: (0, 0)),
                  pl.BlockSpec(w_up_bd.shape, lambda m: (0, 0)),
                  pl.BlockSpec((1, n), lambda m: (0, 0))],
        out_specs=pl.BlockSpec((tm, n), lambda m: (m, 0)),
        compiler_params=_cp(("arbitrary",)),
        name="gla_gate",
    )(u, w_gl, w_up_bd, b_gate.reshape(1, n))


def _gla_scan_kernel(tri_ref, qf_ref, kf_ref, vf_ref, gf_ref, qb_ref, kb_ref, vb_ref, gb_ref, of_ref, ob_ref, s_ref,
                     *, hk, hv):
    @pl.when(pl.program_id(1) == 0)
    def _():
        s_ref[...] = jnp.zeros_like(s_ref)

    ones = jnp.ones((GLA_CHUNK, 128), BF16)
    scale = hk ** -0.5
    tn_dims = (((0,), (0,)), ((), ()))
    nt_dims = (((1,), (1,)), ((), ()))
    streams = ((qf_ref, kf_ref, vf_ref, gf_ref, of_ref), (qb_ref, kb_ref, vb_ref, gb_ref, ob_ref))
    chains = [(d, h) for d in range(2) for h in range(GLA_HEADS)]
    decays = []
    for d, h in chains:
        g = streams[d][3][:, h * hk:(h + 1) * hk]
        gh, gl = _split(g)
        tri = tri_ref[d]
        b = _dot(tri, gh) + _dot(tri, gl)
        tot = jnp.sum(g, axis=0, keepdims=True)
        tot_col = (lax.dot_general(gh, ones, tn_dims, preferred_element_type=F32)
                   + lax.dot_general(gl, ones, tn_dims, preferred_element_type=F32))
        decays.append((b, tot, tot_col))
    operands = []
    for (d, h), (b, tot, tot_col) in zip(chains, decays):
        q = streams[d][0][:, h * hk:(h + 1) * hk]
        k = streams[d][1][:, h * hk:(h + 1) * hk]
        qd = (q * scale * jnp.exp(b)).astype(BF16)
        kd = (k * jnp.exp(-b)).astype(BF16)
        kt = (k * jnp.exp(tot - b)).astype(BF16)
        scores = jnp.where(tri_ref[d] > 0, lax.dot_general(qd, kd, nt_dims, preferred_element_type=F32), 0.0)
        operands.append((qd, kt, scores.astype(BF16)))
    for (d, h), (qd, kt, scores) in zip(chains, operands):
        v = streams[d][2][:, h * hv:(h + 1) * hv].astype(BF16)
        streams[d][4][:, h * hv:(h + 1) * hv] = _dot(scores, v) + _dot(qd, s_ref[d, h].astype(BF16))
    for (d, h), (b, tot, tot_col), (qd, kt, scores) in zip(chains, decays, operands):
        v = streams[d][2][:, h * hv:(h + 1) * hv].astype(BF16)
        decay = jnp.concatenate([jnp.exp(tot_col)] * (hv // 128), axis=1)
        s_ref[d, h] = decay * s_ref[d, h] + lax.dot_general(kt, v, tn_dims, preferred_element_type=F32)


def _gla_scan(proj, log_a, nctx_chunks, dk, dv):
    bsz, length, _ = proj.shape
    nchunks = length // GLA_CHUNK
    hk, hv = dk // GLA_HEADS, dv // GLA_HEADS
    idx = jnp.arange(GLA_CHUNK)
    tri = jnp.stack([idx[:, None] >= idx[None, :], idx[:, None] <= idx[None, :]]).astype(BF16)

    def back(s):
        return jnp.where(s < nctx_chunks, nctx_chunks - 1 - s, nchunks - 1 - (s - nctx_chunks))

    def specs(chunk, direction):
        return [pl.BlockSpec((None, GLA_CHUNK, dk), lambda b, s: (b, chunk(s), 0)),
                pl.BlockSpec((None, GLA_CHUNK, dk), lambda b, s: (b, chunk(s), 1)),
                pl.BlockSpec((None, GLA_CHUNK, dv), lambda b, s: (b, chunk(s), (2 * dk) // dv)),
                pl.BlockSpec((None, GLA_CHUNK, dk), lambda b, s: (b, chunk(s), direction))]

    out = jax.ShapeDtypeStruct((bsz, length, dv), F32)
    return pl.pallas_call(
        functools.partial(_gla_scan_kernel, hk=hk, hv=hv),
        out_shape=(out, out),
        grid=(bsz, nchunks),
        in_specs=[pl.BlockSpec((2, GLA_CHUNK, GLA_CHUNK), lambda b, s: (0, 0, 0))]
        + specs(lambda s: s, 0) + specs(back, 1),
        out_specs=(pl.BlockSpec((None, GLA_CHUNK, dv), lambda b, s: (b, s, 0)),
                   pl.BlockSpec((None, GLA_CHUNK, dv), lambda b, s: (b, back(s), 0))),
        scratch_shapes=[pltpu.VMEM((2, GLA_HEADS, hk, hv), F32)],
        compiler_params=_cp(("arbitrary", "arbitrary"), VMEM_LIMIT),
        name="gla_scan",
    )(tri, proj, proj, proj, log_a, proj, proj, proj, log_a)


def _gla_finish_kernel(of_ref, ob_ref, r_ref, gh_ref, o_ref, *, hv):
    for h in range(GLA_HEADS):
        sl = slice(h * hv, (h + 1) * hv)
        o = of_ref[:, sl] + ob_ref[:, sl]
        ms = jnp.mean(o * o, axis=-1, keepdims=True)
        y = o * lax.rsqrt(ms + EPS) * gh_ref[...]
        o_ref[:, sl] = (y * _silu(r_ref[:, sl])).astype(o_ref.dtype)


def _gla_finish(o_fwd, o_bwd, proj, g_head, dk, dv):
    bsz, length, _ = o_fwd.shape
    hv = dv // GLA_HEADS
    tm = ROW_TILE
    o_spec = pl.BlockSpec((None, tm, dv), lambda b, t: (b, t, 0))
    return pl.pallas_call(
        functools.partial(_gla_finish_kernel, hv=hv),
        out_shape=jax.ShapeDtypeStruct((bsz, length, dv), BF16),
        grid=(bsz, length // tm),
        in_specs=[
            o_spec, o_spec,
            pl.BlockSpec((None, tm, dv), lambda b, t: (b, t, (2 * dk + dv) // dv)),
            pl.BlockSpec((1, hv), lambda b, t: (0, 0)),
        ],
        out_specs=o_spec,
        compiler_params=_cp(("arbitrary", "arbitrary")),
        name="gla_finish",
    )(o_fwd, o_bwd, proj, g_head.reshape(1, hv))


def _gla_layer(h, modtab, nw, w_in, layer, w_gate_up, b_gate, g_head, w_out, nctx):
    bsz, length, d = h.shape
    dk = w_gate_up.shape[2]
    dv = w_out.shape[1]
    nmain = 2 * dk + 2 * dv
    tpb = length // ROW_TILE
    nct = nctx // ROW_TILE
    u = _normmod(h, nw, modtab, SH1, SC1, nct, BF16).reshape(bsz * length, d)
    proj = _mm_plain(u, w_in, layer, nmain, 512, 1024, F32, "gla_in_proj")
    rank = GLA_GATE_RANK
    w_up_bd = jnp.zeros((2 * rank, 2 * dk), F32)
    w_up_bd = w_up_bd.at[:rank, :dk].set(w_gate_up[0]).at[rank:, dk:].set(w_gate_up[1])
    log_a = _gla_gate(u, w_in[layer, :, nmain:], w_up_bd, b_gate.reshape(2 * dk))
    proj3 = proj.reshape(bsz, length, nmain)
    o_fwd, o_bwd = _gla_scan(proj3, log_a.reshape(bsz, length, 2 * dk), nctx // GLA_CHUNK, dk, dv)
    gated = _gla_finish(o_fwd, o_bwd, proj3, g_head, dk, dv).reshape(bsz * length, dv)
    out = _dense_mm_res(gated, w_out, layer, h.reshape(bsz * length, d), modtab, G1, tpb, nct, d, "gla_out_proj")
    return out.reshape(bsz, length, d)


S5_LANES_PER_BATCH = 96
_AN1, _AN2, _AP1, _AP2, _BB1, _BB2, _CC1, _CC2, _PQ, _PK_ROWS = 0, 32, 64, 96, 128, 144, 160, 176, 192, 200


def _s5_params(a_re, a_im, log_dt, b_re, b_im, c_re, c_im):
    t_len = S5_CHUNK
    dt = jnp.exp(log_dt)[..., None]
    mag = jnp.exp(a_re * dt)
    ab_re = mag * jnp.cos(a_im * dt)
    ab_im = mag * jnp.sin(a_im * dt)
    den = a_re * a_re + a_im * a_im
    f_re = ((ab_re - 1.0) * a_re + ab_im * a_im) / den
    f_im = (ab_im * a_re - (ab_re - 1.0) * a_im) / den
    bb_re = f_re[..., None] * b_re - f_im[..., None] * b_im
    bb_im = f_re[..., None] * b_im + f_im[..., None] * b_re

    def cpow(n):
        nn = n.astype(F32)[:, None, :, None]
        m = jnp.exp((a_re * dt)[:, :, None, :] * nn)
        th = (a_im * dt)[:, :, None, :] * nn
        return m * jnp.cos(th), m * jnp.sin(th)

    steps = jnp.arange(t_len)
    order = jnp.stack([steps, t_len - 1 - steps])
    anr, ani = cpow(-order)
    apr, api = cpow(order)
    atr, ati = cpow(jnp.full((2, 1), t_len))
    cat = lambda u, v: jnp.concatenate([u, v], axis=-1)
    bbr_t, bbi_t = bb_re.transpose(0, 1, 3, 2), bb_im.transpose(0, 1, 3, 2)
    zero = jnp.zeros((2, a_re.shape[1], 5, 2 * a_re.shape[2]), F32)
    return jnp.concatenate([
        cat(anr, anr), cat(-ani, ani),
        cat(apr, api), cat(api, apr),
        cat(bbr_t, bbi_t), cat(bbi_t, bbr_t),
        cat(c_re, -c_re), cat(-c_im, -c_im),
        cat(atr, atr), cat(-ati, ati), cat(ati, -ati), zero], axis=2)


def _s5_in_kernel(hl_ref, hc_ref, nw_ref, mod_ref, o_ref, *, bsz, qb):
    nw = nw_ref[...]
    blocks = []
    for b in range(bsz):
        lat = _normmod_rows(hl_ref[b], nw, mod_ref[b, 1, SC1:SC1 + 1, :], mod_ref[b, 1, SH1:SH1 + 1, :])
        ctx = _normmod_rows(hc_ref[b], nw, mod_ref[b, 0, SC1:SC1 + 1, :], mod_ref[b, 0, SH1:SH1 + 1, :])
        pad = jnp.zeros((qb - lat.shape[0] - ctx.shape[0], lat.shape[1]), F32)
        blocks += [lat, ctx, pad]
    o_ref[...] = jnp.concatenate(blocks, axis=0).T.astype(o_ref.dtype)


def _s5_in(h, nw, modtab, nctx):
    bsz, length, d = h.shape
    t_len = S5_CHUNK
    qb = S5_LANES_PER_BATCH
    nctx_chunks = nctx // t_len
    return pl.pallas_call(
        functools.partial(_s5_in_kernel, bsz=bsz, qb=qb),
        out_shape=jax.ShapeDtypeStruct((d, t_len * bsz * qb), BF16),
        grid=(t_len,),
        in_specs=[
            pl.BlockSpec((bsz, None, GRID_W, d), lambda s: (0, nctx // GRID_W + s, 0, 0)),
            pl.BlockSpec((bsz, nctx_chunks, d), lambda s: (0, 0, s)),
            pl.BlockSpec((1, d), lambda s: (0, 0)),
            pl.BlockSpec(modtab.shape, lambda s: (0, 0, 0, 0)),
        ],
        out_specs=pl.BlockSpec((d, bsz * qb), lambda s: (0, s)),
        compiler_params=_cp(("arbitrary",), VMEM_LIMIT),
        name="s5_in",
    )(h.reshape(bsz, length // GRID_W, GRID_W, d), h[:, :nctx].reshape(bsz, nctx_chunks, t_len * d),
      nw.reshape(1, d), modtab)


def _tile_rows(x, reps):
    return jnp.concatenate([x] * reps, axis=0)


def _repeat_rows(x, reps):
    return jnp.concatenate([jnp.broadcast_to(x[i:i + 1, :], (reps, x.shape[1])) for i in range(x.shape[0])],
                           axis=0)


def _s5_kernel(ut_ref, pk_ref, y_ref, sv_ref, ss_ref, w_ref, r2_ref, acc_ref, *, bsz, qb, orders):
    gb = ut_ref.shape[0] // S5_GROUP
    t_len = S5_CHUNK
    k = S5_GROUP * t_len
    q = ut_ref.shape[1] // t_len
    row_s = lax.broadcasted_iota(jnp.int32, (k, k), 0) // S5_GROUP
    col_t = lax.broadcasted_iota(jnp.int32, (k, k), 1) // S5_GROUP
    masks = (row_s <= col_t, row_s >= col_t)
    w_ref[...] = jnp.zeros_like(w_ref)
    us = []
    for g in range(gb):
        x = ut_ref[g * S5_GROUP:(g + 1) * S5_GROUP, :]
        us.append(jnp.concatenate([x[:, s * q:(s + 1) * q] for s in range(t_len)], axis=0).T)
    pairs = [(g, d) for g in range(gb) for d in range(2)]
    l2s, raw = [], []
    for g, d in pairs:
        pk = pk_ref[d, g]
        l2 = (_repeat_rows(pk[_AN1:_AN1 + t_len], S5_GROUP) * _tile_rows(pk[_BB1:_BB1 + S5_GROUP], t_len)
              + _repeat_rows(pk[_AN2:_AN2 + t_len], S5_GROUP) * _tile_rows(pk[_BB2:_BB2 + S5_GROUP], t_len))
        r2t = (_repeat_rows(pk[_AP1:_AP1 + t_len], S5_GROUP) * _tile_rows(pk[_CC1:_CC1 + S5_GROUP], t_len)
               + _repeat_rows(pk[_AP2:_AP2 + t_len], S5_GROUP) * _tile_rows(pk[_CC2:_CC2 + S5_GROUP], t_len))
        r2 = r2t.T
        r2_ref[2 * g + d] = r2.astype(BF16)
        l2s.append(l2.astype(BF16))
        raw.append(_dot3(l2, r2))
    for (g, d), l2, toep_raw in zip(pairs, l2s, raw):
        i = 2 * g + d
        y_local = _dot(us[g], jnp.where(masks[d], toep_raw, 0.0).astype(BF16))
        if d == 0:
            acc_ref[g] = y_local
        else:
            acc_ref[g] = acc_ref[g] + y_local
        v = _dot(us[g], l2)
        vs = pltpu.roll(v, v.shape[1] // 2, axis=1)
        pk = pk_ref[d, g]
        pt, qt, qts = pk[_PQ:_PQ + 1], pk[_PQ + 1:_PQ + 2], pk[_PQ + 2:_PQ + 3]
        sv_ref[i] = pt * v + qt * vs
        ss_ref[i] = pt * vs + qts * v
    coef = []
    for g in range(gb):
        for d in range(2):
            pk = pk_ref[d, g]
            coef.append(tuple(jnp.broadcast_to(pk[_PQ + r:_PQ + r + 1], (bsz, pk.shape[1])) for r in range(3)))
    zero = jnp.zeros((bsz, pk_ref.shape[3]), F32)
    state = [(zero, zero)] * (2 * gb)
    nsteps = len(orders[0])
    for step in range(nsteps):
        for i in range(2 * gb):
            rows = pl.ds(orders[i % 2][step], bsz, stride=qb)
            w, ws = state[i]
            w_ref[i, rows, :] = w
            if step + 1 < nsteps:
                pt, qt, qts = coef[i]
                state[i] = (pt * w + qt * ws + sv_ref[i, rows, :], pt * ws + qts * w + ss_ref[i, rows, :])
    for g in range(gb):
        y = acc_ref[g]
        for d in range(2):
            i = 2 * g + d
            y = y + _dot(w_ref[i].astype(BF16), r2_ref[i])
        yt = y.T
        for t in range(t_len):
            y_ref[g * S5_GROUP:(g + 1) * S5_GROUP, t * q:(t + 1) * q] = yt[t * S5_GROUP:(t + 1) * S5_GROUP, :]


def _s5_scan(ut, pk, bsz, nctx_chunks, nlat_chunks):
    t_len = S5_CHUNK
    q = ut.shape[1] // t_len
    ngroups = ut.shape[0] // S5_GROUP
    np2 = pk.shape[3]
    gb = S5_GROUP_BLOCK
    qb = S5_LANES_PER_BATCH
    lat = list(range(nlat_chunks))
    ctx = list(range(nlat_chunks, nlat_chunks + nctx_chunks))
    orders = (ctx + lat, ctx[::-1] + lat[::-1])
    blk = (gb * S5_GROUP, t_len * q)
    return pl.pallas_call(
        functools.partial(_s5_kernel, bsz=bsz, qb=qb, orders=orders),
        out_shape=jax.ShapeDtypeStruct(ut.shape, F32),
        grid=(ngroups // gb,),
        in_specs=[pl.BlockSpec(blk, lambda g: (g, 0)),
                  pl.BlockSpec((2, gb, _PK_ROWS, np2), lambda g: (0, g, 0, 0))],
        out_specs=pl.BlockSpec(blk, lambda g: (g, 0)),
        scratch_shapes=[pltpu.VMEM((2 * gb, q, np2), F32), pltpu.VMEM((2 * gb, q, np2), F32),
                        pltpu.VMEM((2 * gb, q, np2), F32), pltpu.VMEM((2 * gb, np2, S5_GROUP * t_len), BF16),
                        pltpu.VMEM((gb, q, S5_GROUP * t_len), F32)],
        compiler_params=_cp(("arbitrary",), VMEM_LIMIT),
        name="s5_scan",
    )(ut, pk)


def _s5_out_kernel(yt_ref, hl_ref, hc_ref, nw_ref, mod_ref, d_ref, ol_ref, oc_ref, *, bsz, qb):
    y = yt_ref[...].T
    nw = nw_ref[...]
    nlat, nc = hl_ref.shape[1], hc_ref.shape[1]
    c0 = 0.7978845608028654

    def act(yv, hv, region, b):
        u = _normmod_rows(hv, nw, mod_ref[b, region, SC1:SC1 + 1, :], mod_ref[b, region, SH1:SH1 + 1, :])
        z = yv + u * d_ref[...]
        return (0.5 * z * (1.0 + jnp.tanh(c0 * (z + 0.044715 * (z * z * z))))).astype(ol_ref.dtype)

    for b in range(bsz):
        ol_ref[b] = act(y[b * qb:b * qb + nlat], hl_ref[b], 1, b)
        oc_ref[b] = act(y[b * qb + nlat:b * qb + nlat + nc], hc_ref[b], 0, b)


def _s5_out(yt, h, nw, modtab, d_skip, nctx):
    bsz, length, d = h.shape
    t_len = S5_CHUNK
    qb = S5_LANES_PER_BATCH
    nctx_chunks = nctx // t_len
    seq = length - nctx
    lat_spec = pl.BlockSpec((bsz, None, GRID_W, d), lambda s: (0, nctx // GRID_W + s, 0, 0))
    ctx_spec = pl.BlockSpec((bsz, nctx_chunks, d), lambda s: (0, 0, s))
    act_l, act_c = pl.pallas_call(
        functools.partial(_s5_out_kernel, bsz=bsz, qb=qb),
        out_shape=(jax.ShapeDtypeStruct((bsz, seq // GRID_W, GRID_W, d), BF16),
                   jax.ShapeDtypeStruct((bsz, nctx_chunks, t_len * d), BF16)),
        grid=(t_len,),
        in_specs=[pl.BlockSpec((d, bsz * qb), lambda s: (0, s)),
                  lat_spec, ctx_spec,
                  pl.BlockSpec((1, d), lambda s: (0, 0)),
                  pl.BlockSpec(modtab.shape, lambda s: (0, 0, 0, 0)),
                  pl.BlockSpec((1, d), lambda s: (0, 0))],
        out_specs=(pl.BlockSpec((bsz, None, GRID_W, d), lambda s: (0, s, 0, 0)), ctx_spec),
        compiler_params=_cp(("arbitrary",), VMEM_LIMIT),
        name="s5_out",
    )(yt, h.reshape(bsz, length // GRID_W, GRID_W, d), h[:, :nctx].reshape(bsz, nctx_chunks, t_len * d),
      nw.reshape(1, d), modtab, d_skip.reshape(1, d))
    return act_l.reshape(bsz, seq, d), act_c.reshape(bsz, nctx, d)


def _s5_layer(h, modtab, nw, a_re, a_im, log_dt, b_re, b_im, c_re, c_im, d_skip, w_glu, b_glu, layer, nctx,
              last):
    bsz, length, d = h.shape
    seq = length - nctx
    t_len = S5_CHUNK
    assert seq // GRID_W == t_len and nctx % t_len == 0 and nctx % GRID_W == 0
    assert seq // t_len + nctx // t_len <= S5_LANES_PER_BATCH
    ngroups = d // S5_GROUP
    ut = _s5_in(h, nw, modtab, nctx)
    pk = _s5_params(a_re, a_im, log_dt, b_re, b_im, c_re, c_im)
    yt = _s5_scan(ut, pk, bsz, nctx // t_len, GRID_W)
    act_l, act_c = _s5_out(yt, h, nw, modtab, d_skip, nctx)
    if last:
        act, h, nct = act_l, h[:, nctx:], 0
    else:
        act, nct = jnp.concatenate([act_c, act_l], axis=1), nctx // ROW_TILE
    rows = h.shape[0] * h.shape[1]
    out = _dense_glu_res(act.reshape(rows, d), w_glu, b_glu, layer, h.reshape(rows, d), modtab, G1,
                         h.shape[1] // ROW_TILE, nct, 1024, "s5_glu")
    return out.reshape(h.shape)


def _dense_ffn(h, modtab, nw, w13, w2, layer, nct):
    bsz, length, d = h.shape
    rows = bsz * length
    v = _normmod(h, nw, modtab, SH2, SC2, nct, BF16).reshape(rows, d)
    w13_grouped = w13.reshape(w13.shape[0], 1, w13.shape[1], w13.shape[2])
    act = _moe_swiglu(_dense_route(rows // MOE_TILE), v, w13_grouped, layer, MOE_W13_COLS, "ffn_w13")
    out = _dense_mm_res(act, w2, layer, h.reshape(rows, d), modtab, G2, length // ROW_TILE, nct, MOE_W2_COLS,
                        "ffn_w2")
    return out.reshape(h.shape)


def _moe_gather_kernel(src_ref, nxt_ref, v_hbm, o_ref, buf_ref, sem):
    t = pl.program_id(0)
    slot = t % 2

    def request(idx_ref, s):
        def issue(r8, carry):
            for i in range(DMA_ISSUE_UNROLL):
                r = r8 * DMA_ISSUE_UNROLL + i
                pltpu.make_async_copy(v_hbm.at[pl.ds(idx_ref[0, r], 1)], buf_ref.at[s, pl.ds(r, 1)],
                                      sem.at[s]).start(priority=i % 2)
            return carry
        lax.fori_loop(0, MOE_TILE // DMA_ISSUE_UNROLL, issue, 0)

    @pl.when(t == 0)
    def _():
        request(src_ref, 0)

    @pl.when(t + 1 < pl.num_programs(0))
    def _():
        request(nxt_ref, 1 - slot)

    pltpu.make_async_copy(v_hbm.at[pl.ds(0, MOE_TILE)], buf_ref.at[slot], sem.at[slot]).wait()
    o_ref[...] = buf_ref[slot].astype(o_ref.dtype)


def _moe_gather(src, v):
    ntiles = src.shape[0]
    d = v.shape[1]
    idx_block = (None, 1, MOE_TILE)
    return pl.pallas_call(
        _moe_gather_kernel,
        out_shape=jax.ShapeDtypeStruct((ntiles * MOE_TILE, d), BF16),
        grid=(ntiles,),
        in_specs=[pl.BlockSpec(idx_block, lambda t: (t, 0, 0), memory_space=pltpu.SMEM),
                  pl.BlockSpec(idx_block, lambda t: (jnp.minimum(t + 1, ntiles - 1), 0, 0),
                               memory_space=pltpu.SMEM),
                  pl.BlockSpec(memory_space=pl.ANY)],
        out_specs=pl.BlockSpec((MOE_TILE, d), lambda t: (t, 0)),
        scratch_shapes=[pltpu.VMEM((2, MOE_TILE, d), F32), pltpu.SemaphoreType.DMA((2,))],
        compiler_params=_cp(("arbitrary",)),
        name="moe_gather",
    )(src, src, v)


def _moe_fetch_weights(te_ref, first_ref, nxt_ref, w_hbm, stage_ref, sem, wbf_ref, col_blocks, layer):
    j, t = pl.program_id(0), pl.program_id(1)
    nsweeps = pl.num_programs(0)
    tn = stage_ref.shape[2]

    def copies(e, sweep):
        return [pltpu.make_async_copy(w_hbm.at[layer, e, :, pl.ds(pl.multiple_of(cb(sweep) * tn, 128), tn)],
                                      stage_ref.at[i], sem.at[i])
                for i, cb in enumerate(col_blocks)]

    @pl.when(jnp.logical_and(j == 0, t == 0))
    def _():
        for c in copies(te_ref[0], 0):
            c.start()

    @pl.when(first_ref[t] == 1)
    def _():
        for c in copies(te_ref[t], j):
            c.wait()
        for i in range(len(col_blocks)):
            wbf_ref[:, i * tn:(i + 1) * tn] = stage_ref[i].astype(BF16)
        sweep_done = nxt_ref[t] < 0
        next_e = jnp.where(sweep_done, te_ref[0], nxt_ref[t])
        next_sweep = jnp.where(sweep_done, j + 1, j)

        @pl.when(next_sweep < nsweeps)
        def _():
            for c in copies(next_e, next_sweep):
                c.start()


def _moe_swiglu_kernel(te_ref, nv_ref, first_ref, nxt_ref, x_ref, w_hbm, o_ref, wbf_ref, stage_ref, sem, *, layer):
    nb = pl.num_programs(0)
    _moe_fetch_weights(te_ref, first_ref, nxt_ref, w_hbm, stage_ref, sem, wbf_ref,
                       [lambda sweep: sweep, lambda sweep: nb + sweep], layer)

    in_use = pl.program_id(1) < nv_ref[0]

    @pl.when(in_use)
    def _():
        tn = o_ref.shape[1]
        z = _dot(x_ref[...], wbf_ref[...])
        o_ref[...] = (_silu(z[:, :tn]) * z[:, tn:]).astype(o_ref.dtype)

    @pl.when(jnp.logical_not(in_use))
    def _():
        o_ref[...] = jnp.zeros_like(o_ref)


def _moe_mm_kernel(te_ref, nv_ref, first_ref, nxt_ref, x_ref, w_hbm, o_ref, wbf_ref, stage_ref, sem, *, layer):
    _moe_fetch_weights(te_ref, first_ref, nxt_ref, w_hbm, stage_ref, sem, wbf_ref, [lambda sweep: sweep], layer)

    in_use = pl.program_id(1) < nv_ref[0]

    @pl.when(in_use)
    def _():
        o_ref[...] = _dot(x_ref[...], wbf_ref[...])

    @pl.when(jnp.logical_not(in_use))
    def _():
        o_ref[...] = jnp.zeros_like(o_ref)


def _moe_swiglu(route, xs, w13, layer, tn, name="moe_w13"):
    rows, k = xs.shape
    half = w13.shape[3] // 2
    nb = half // tn
    tm = MOE_TILE
    return pl.pallas_call(
        functools.partial(_moe_swiglu_kernel, layer=layer),
        out_shape=jax.ShapeDtypeStruct((rows, half), BF16),
        grid_spec=pltpu.PrefetchScalarGridSpec(
            num_scalar_prefetch=4,
            grid=(nb, rows // tm),
            in_specs=[pl.BlockSpec((tm, k), lambda j, t, *_: (t, 0)),
                      pl.BlockSpec(memory_space=pl.ANY)],
            out_specs=pl.BlockSpec((tm, tn), lambda j, t, *_: (t, j)),
            scratch_shapes=[pltpu.VMEM((k, 2 * tn), BF16), pltpu.VMEM((2, k, tn), F32),
                            pltpu.SemaphoreType.DMA((2,))]),
        compiler_params=_cp(("arbitrary", "arbitrary"), VMEM_LIMIT),
        name=name,
    )(*route, xs, w13)


def _moe_mm(route, xs, w2, layer, tn):
    rows, k = xs.shape
    n = w2.shape[3]
    tm = MOE_TILE
    return pl.pallas_call(
        functools.partial(_moe_mm_kernel, layer=layer),
        out_shape=jax.ShapeDtypeStruct((rows, n), F32),
        grid_spec=pltpu.PrefetchScalarGridSpec(
            num_scalar_prefetch=4,
            grid=(n // tn, rows // tm),
            in_specs=[pl.BlockSpec((tm, k), lambda j, t, *_: (t, 0)),
                      pl.BlockSpec(memory_space=pl.ANY)],
            out_specs=pl.BlockSpec((tm, tn), lambda j, t, *_: (t, j)),
            scratch_shapes=[pltpu.VMEM((k, tn), BF16), pltpu.VMEM((1, k, tn), F32),
                            pltpu.SemaphoreType.DMA((1,))]),
        compiler_params=_cp(("arbitrary", "arbitrary"), VMEM_LIMIT),
        name="moe_w2",
    )(*route, xs, w2)


def _dense_route(ntiles):
    zeros = jnp.zeros((ntiles,), jnp.int32)
    return (zeros, jnp.full((1,), ntiles, jnp.int32), zeros.at[0].set(1), zeros - 1)


def _dense_mm_res_kernel(te_ref, nv_ref, first_ref, nxt_ref, x_ref, w_hbm, res_ref, mod_ref, o_ref, wbf_ref,
                         stage_ref, sem, *, layer, g_row):
    _moe_fetch_weights(te_ref, first_ref, nxt_ref, w_hbm, stage_ref, sem, wbf_ref, [lambda sweep: sweep], layer)
    o_ref[...] = res_ref[...] + mod_ref[g_row:g_row + 1, :] * _dot(x_ref[...], wbf_ref[...])


def _dense_glu_res_kernel(te_ref, nv_ref, first_ref, nxt_ref, x_ref, w_hbm, ba_ref, bb_ref, res_ref, mod_ref, o_ref,
                          wbf_ref, stage_ref, sem, *, layer, g_row):
    nb = pl.num_programs(0)
    _moe_fetch_weights(te_ref, first_ref, nxt_ref, w_hbm, stage_ref, sem, wbf_ref,
                       [lambda sweep: sweep, lambda sweep: nb + sweep], layer)
    tn = o_ref.shape[1]
    z = _dot(x_ref[...], wbf_ref[...])
    z1 = z[:, :tn] + ba_ref[...]
    z2 = z[:, tn:] + bb_ref[...]
    o_ref[...] = res_ref[...] + mod_ref[g_row:g_row + 1, :] * (z1 * _sigmoid(z2))


def _dense_res_specs(tn, tpb, nct):
    res_spec = pl.BlockSpec((ROW_TILE, tn), lambda j, t, *_: (t, j))
    mod_spec = pl.BlockSpec((None, None, 6, tn),
                            lambda j, t, *_: (t // tpb, jnp.where(t % tpb >= nct, 1, 0), 0, j))
    return res_spec, mod_spec


def _dense_mm_res(x, w, layer, res, modtab, g_row, tpb, nct, tn, name):
    r, k = x.shape
    n = w.shape[2]
    tm = ROW_TILE
    res_spec, mod_spec = _dense_res_specs(tn, tpb, nct)
    return pl.pallas_call(
        functools.partial(_dense_mm_res_kernel, layer=layer, g_row=g_row),
        out_shape=jax.ShapeDtypeStruct((r, n), F32),
        grid_spec=pltpu.PrefetchScalarGridSpec(
            num_scalar_prefetch=4,
            grid=(n // tn, r // tm),
            in_specs=[pl.BlockSpec((tm, k), lambda j, t, *_: (t, 0)),
                      pl.BlockSpec(memory_space=pl.ANY), res_spec, mod_spec],
            out_specs=pl.BlockSpec((tm, tn), lambda j, t, *_: (t, j)),
            scratch_shapes=[pltpu.VMEM((k, tn), BF16), pltpu.VMEM((1, k, tn), F32),
                            pltpu.SemaphoreType.DMA((1,))]),
        compiler_params=_cp(("arbitrary", "arbitrary"), VMEM_LIMIT),
        name=name,
    )(*_dense_route(r // tm), x, w.reshape(w.shape[0], 1, k, n), res, modtab)


def _dense_glu_res(x, w, bias, layer, res, modtab, g_row, tpb, nct, tn, name):
    r, k = x.shape
    half = w.shape[2] // 2
    nb = half // tn
    tm = ROW_TILE
    res_spec, mod_spec = _dense_res_specs(tn, tpb, nct)
    bias3 = bias.reshape(bias.shape[0], 1, 2 * half)
    return pl.pallas_call(
        functools.partial(_dense_glu_res_kernel, layer=layer, g_row=g_row),
        out_shape=jax.ShapeDtypeStruct((r, half), F32),
        grid_spec=pltpu.PrefetchScalarGridSpec(
            num_scalar_prefetch=4,
            grid=(nb, r // tm),
            in_specs=[pl.BlockSpec((tm, k), lambda j, t, *_: (t, 0)),
                      pl.BlockSpec(memory_space=pl.ANY),
                      pl.BlockSpec((None, 1, tn), lambda j, t, *_: (layer, 0, j)),
                      pl.BlockSpec((None, 1, tn), lambda j, t, *_: (layer, 0, nb + j)),
                      res_spec, mod_spec],
            out_specs=pl.BlockSpec((tm, tn), lambda j, t, *_: (t, j)),
            scratch_shapes=[pltpu.VMEM((k, 2 * tn), BF16), pltpu.VMEM((2, k, tn), F32),
                            pltpu.SemaphoreType.DMA((2,))]),
        compiler_params=_cp(("arbitrary", "arbitrary"), VMEM_LIMIT),
        name=name,
    )(*_dense_route(r // tm), x, w.reshape(w.shape[0], 1, k, 2 * half), bias3, bias3, res, modtab)


def _moe_combine_kernel(pos_ref, nxt_ref, gate_ref, y_hbm, res_ref, mod_ref, o_ref, buf_ref, sem, *, g_row):
    m = pl.program_id(0)
    slot = m % 2

    def request(idx_ref, s):
        for k in range(TOP_K):
            def issue(r8, carry, k=k):
                for i in range(DMA_ISSUE_UNROLL):
                    r = r8 * DMA_ISSUE_UNROLL + i
                    pltpu.make_async_copy(y_hbm.at[pl.ds(idx_ref[k, r], 1)], buf_ref.at[s, k, pl.ds(r, 1)],
                                          sem.at[s]).start(priority=i % 2)
                return carry
            lax.fori_loop(0, ROW_TILE // DMA_ISSUE_UNROLL, issue, 0)

    @pl.when(m == 0)
    def _():
        request(pos_ref, 0)

    @pl.when(m + 1 < pl.num_programs(0))
    def _():
        request(nxt_ref, 1 - slot)

    for k in range(TOP_K):
        pltpu.make_async_copy(y_hbm.at[pl.ds(0, ROW_TILE)], buf_ref.at[slot, k], sem.at[slot]).wait()
    p = gate_ref[...]
    mix = p[:, 0:1] * buf_ref[slot, 0] + p[:, 1:2] * buf_ref[slot, 1]
    o_ref[...] = res_ref[...] + mod_ref[g_row:g_row + 1, :] * mix


def _moe_combine(pos, gates, ys, res, modtab, g_row, tpb, nct):
    n, d = res.shape
    tm = ROW_TILE
    ntiles = n // tm
    return pl.pallas_call(
        functools.partial(_moe_combine_kernel, g_row=g_row),
        out_shape=jax.ShapeDtypeStruct((n, d), F32),
        grid=(ntiles,),
        in_specs=[pl.BlockSpec((None, TOP_K, tm), lambda m: (m, 0, 0), memory_space=pltpu.SMEM),
                  pl.BlockSpec((None, TOP_K, tm), lambda m: (jnp.minimum(m + 1, ntiles - 1), 0, 0),
                               memory_space=pltpu.SMEM),
                  pl.BlockSpec((tm, ROUTER_LANES), lambda m: (m, 0)),
                  pl.BlockSpec(memory_space=pl.ANY),
                  pl.BlockSpec((tm, d), lambda m: (m, 0)),
                  pl.BlockSpec((None, None, 6, d), lambda m: (m // tpb, jnp.where(m % tpb >= nct, 1, 0), 0, 0))],
        out_specs=pl.BlockSpec((tm, d), lambda m: (m, 0)),
        scratch_shapes=[pltpu.VMEM((2, TOP_K, tm, d), F32), pltpu.SemaphoreType.DMA((2,))],
        compiler_params=_cp(("arbitrary",), VMEM_LIMIT),
        name="moe_combine",
    )(pos, pos, gates, ys, res, modtab)


def _moe_route(idx2):
    n = idx2.shape[0]
    flat_e = idx2.reshape(-1)
    onehot = (flat_e[:, None] == jnp.arange(N_EXPERTS, dtype=jnp.int32)[None, :]).astype(jnp.int32)
    csum = jnp.cumsum(onehot, axis=0)
    rank = jnp.sum((csum - onehot) * onehot, axis=1)
    counts = csum[-1]
    ntile = (counts + MOE_TILE - 1) // MOE_TILE
    tile_end = jnp.cumsum(ntile)
    tile_start = tile_end - ntile
    pos = tile_start[flat_e] * MOE_TILE + rank
    tiles = (TOP_K * n) // MOE_TILE + N_EXPERTS
    src = jnp.zeros((tiles * MOE_TILE,), jnp.int32).at[pos].set(jnp.arange(TOP_K * n, dtype=jnp.int32) // TOP_K)
    nv = tile_end[-1:]
    tile_ids = jnp.arange(tiles, dtype=jnp.int32)
    tids = jnp.minimum(tile_ids, nv[0] - 1)
    te = jnp.sum((tile_end[None, :] <= tids[:, None]).astype(jnp.int32), axis=1)
    first = jnp.logical_and(tile_ids < nv[0], jnp.logical_or(tile_ids == 0, te != jnp.roll(te, 1)))
    run_end = tile_end[te]
    nxt = jnp.where(run_end < nv[0], te[jnp.minimum(run_end, tiles - 1)], -1)
    route = (te, nv.astype(jnp.int32), first.astype(jnp.int32), nxt.astype(jnp.int32))
    return pos.astype(jnp.int32), src.reshape(tiles, 1, MOE_TILE), route


def _moe_ffn(h, modtab, nw, w_router, w13, w2, layer, nct):
    bsz, length, d = h.shape
    n = bsz * length
    v, idx, gates = _normmod_router(h, nw, modtab, w_router, SH2, SC2, nct)
    idx2 = idx.reshape(n, ROUTER_LANES)[:, :TOP_K]
    pos, src, route = _moe_route(idx2)
    xs = _moe_gather(src, v.reshape(n, d))
    act = _moe_swiglu(route, xs, w13, layer, MOE_W13_COLS)
    ys = _moe_mm(route, act, w2, layer, MOE_W2_COLS)
    pos3 = pos.reshape(n // ROW_TILE, ROW_TILE, TOP_K).transpose(0, 2, 1)
    out = _moe_combine(pos3, gates.reshape(n, ROUTER_LANES), ys, h.reshape(n, d), modtab, G2,
                       length // ROW_TILE, nct)
    return out.reshape(h.shape)


def kernel(x, c, ctx, c_ctx, w_mod, b_mod, norm1, norm2, norm_f, gla_w_in, gla_w_gate_up, gla_b_gate,
           gla_g_head, gla_w_out, ffn_w13, ffn_w2, s5_a_re, s5_a_im, s5_log_dt, s5_b_re, s5_b_im, s5_c_re,
           s5_c_im, s5_d, s5_w_glu, s5_b_glu, moe_w_router, moe_w13, moe_w2):
    bsz, seq, d = x.shape
    nctx = ctx.shape[1]
    depth = w_mod.shape[0]
    assert bsz + 1 <= 8 and seq % ROW_TILE == 0 and nctx % ROW_TILE == 0 and seq % GRID_W == 0
    nct = nctx // ROW_TILE

    c8 = jnp.concatenate([c, c_ctx[None, :], jnp.zeros((8 - bsz - 1, d), F32)], axis=0)
    mods = _modulation(c8, w_mod, b_mod).reshape(depth, 8, 6, d)
    h = jnp.concatenate([ctx, x], axis=1)
    for i in range(depth):
        last = i == depth - 1
        j = i // 2
        modtab = jnp.stack([jnp.broadcast_to(mods[i, bsz], (bsz, 6, d)), mods[i, :bsz]], axis=1)
        if i % 2 == 0:
            h = _gla_layer(h, modtab, norm1[i], gla_w_in, j, gla_w_gate_up[j], gla_b_gate[j], gla_g_head[j],
                           gla_w_out, nctx)
        else:
            h = _s5_layer(h, modtab, norm1[i], s5_a_re[j], s5_a_im[j], s5_log_dt[j], s5_b_re[j], s5_b_im[j],
                          s5_c_re[j], s5_c_im[j], s5_d[j], s5_w_glu, s5_b_glu, j, nctx, last)
        ffn_nct = 0 if last else nct
        if i % 2 == 0:
            h = _dense_ffn(h, modtab, norm2[i], ffn_w13, ffn_w2, j, ffn_nct)
        else:
            h = _moe_ffn(h, modtab, norm2[i], moe_w_router[j], moe_w13, moe_w2, j, ffn_nct)
    if h.shape[1] != seq:
        h = h[:, nctx:]
    return _final_norm(h, norm_f)
```

```python
import functools

import jax
import jax.numpy as jnp
from jax import lax
from jax.experimental import pallas as pl
from jax.experimental.pallas import tpu as pltpu

F32 = jnp.float32
BF16 = jnp.bfloat16

EPS = 1e-6
GRID_W = 64

GLA_HEADS = 4
GLA_GATE_RANK = 16
GLA_GATE_TAU = 16.0
GLA_CHUNK = 64

S5_GROUP = 16
S5_CHUNK = 32
S5_GROUP_BLOCK = 4

N_EXPERTS = 8
TOP_K = 2
ROUTER_LANES = 128

ROW_TILE = 256
MOE_TILE = 256
DMA_ISSUE_UNROLL = 8
MOE_W13_COLS = 1408
MOE_W2_COLS = 1024
VMEM_LIMIT = 56 * 1024 * 1024

SH1, SC1, G1, SH2, SC2, G2 = range(6)


def _cp(sem, vmem=None):
    return pltpu.CompilerParams(dimension_semantics=sem, vmem_limit_bytes=vmem)


def _dot(a, b):
    return jnp.dot(a, b, preferred_element_type=F32)


def _split(x):
    hi = x.astype(BF16)
    lo = (x - hi.astype(F32)).astype(BF16)
    return hi, lo


def _dot3(a, b):
    ah, al = _split(a)
    bh, bl = _split(b)
    return _dot(ah, bh) + _dot(ah, bl) + _dot(al, bh)


def _sigmoid(x):
    return 1.0 / (1.0 + jnp.exp(-x))


def _silu(x):
    return x * _sigmoid(x)


def _mod_kernel(c_ref, w_ref, b_ref, o_ref):
    x = _silu(c_ref[...]).astype(BF16)
    o_ref[...] = _dot(x, w_ref[...].astype(BF16)) + b_ref[...]


def _modulation(c8, w_mod, b_mod):
    depth, d, n = w_mod.shape
    tn = 1024
    return pl.pallas_call(
        _mod_kernel,
        out_shape=jax.ShapeDtypeStruct((depth, 8, n), F32),
        grid=(depth, n // tn),
        in_specs=[
            pl.BlockSpec((8, d), lambda i, j: (0, 0)),
            pl.BlockSpec((None, d, tn), lambda i, j: (i, 0, j)),
            pl.BlockSpec((None, 1, tn), lambda i, j: (i, 0, j)),
        ],
        out_specs=pl.BlockSpec((None, 8, tn), lambda i, j: (i, 0, j)),
        compiler_params=_cp(("arbitrary", "arbitrary"), VMEM_LIMIT),
        name="modulation",
    )(c8, w_mod, b_mod.reshape(depth, 1, n))


def _normmod_rows(x, nw, scale, shift):
    ms = jnp.mean(x * x, axis=-1, keepdims=True)
    return x * lax.rsqrt(ms + EPS) * nw * (1.0 + scale) + shift


def _normmod_body(x_ref, nw_ref, mod_ref, sh_row, sc_row):
    return _normmod_rows(x_ref[...], nw_ref[...], mod_ref[sc_row:sc_row + 1, :], mod_ref[sh_row:sh_row + 1, :])


def _normmod_kernel(x_ref, nw_ref, mod_ref, o_ref, *, sh_row, sc_row):
    o_ref[...] = _normmod_body(x_ref, nw_ref, mod_ref, sh_row, sc_row).astype(o_ref.dtype)


def _normmod_router_kernel(x_ref, nw_ref, mod_ref, wr_ref, o_ref, idx_ref, gate_ref, *, sh_row, sc_row):
    y = _normmod_body(x_ref, nw_ref, mod_ref, sh_row, sc_row)
    o_ref[...] = y
    logits = _dot3(y, wr_ref[...])
    lane = lax.broadcasted_iota(jnp.int32, logits.shape, 1)
    lane_f = lane.astype(F32)
    neg = jnp.float32(-jnp.inf)
    l1 = jnp.where(lane < N_EXPERTS, logits, neg)
    m1 = jnp.max(l1, axis=-1, keepdims=True)
    i1 = jnp.min(jnp.where(l1 == m1, lane_f, float(ROUTER_LANES)), axis=-1, keepdims=True)
    l2 = jnp.where(lane_f == i1, neg, l1)
    m2 = jnp.max(l2, axis=-1, keepdims=True)
    i2 = jnp.min(jnp.where(l2 == m2, lane_f, float(ROUTER_LANES)), axis=-1, keepdims=True)
    e = jnp.exp(m2 - m1)
    g1 = 1.0 / (1.0 + e)
    g2 = e * g1
    idx_ref[...] = jnp.where(lane == 0, i1, jnp.where(lane == 1, i2, 0.0)).astype(jnp.int32)
    gate_ref[...] = jnp.where(lane == 0, g1, jnp.where(lane == 1, g2, 0.0))


def _row_specs(d, nct):
    x_spec = pl.BlockSpec((None, ROW_TILE, d), lambda b, t: (b, t, 0))
    mod_spec = pl.BlockSpec((None, None, 6, d), lambda b, t: (b, jnp.where(t >= nct, 1, 0), 0, 0))
    return x_spec, mod_spec


def _normmod(h, nw, modtab, sh_row, sc_row, nct, out_dtype):
    bsz, length, d = h.shape
    x_spec, mod_spec = _row_specs(d, nct)
    return pl.pallas_call(
        functools.partial(_normmod_kernel, sh_row=sh_row, sc_row=sc_row),
        out_shape=jax.ShapeDtypeStruct(h.shape, out_dtype),
        grid=(bsz, length // ROW_TILE),
        in_specs=[x_spec, pl.BlockSpec((1, d), lambda b, t: (0, 0)), mod_spec],
        out_specs=x_spec,
        compiler_params=_cp(("arbitrary", "arbitrary")),
        name="normmod",
    )(h, nw.reshape(1, d), modtab)


def _normmod_router(h, nw, modtab, w_router, sh_row, sc_row, nct):
    bsz, length, d = h.shape
    x_spec, mod_spec = _row_specs(d, nct)
    wr = jnp.pad(w_router, ((0, 0), (0, ROUTER_LANES - w_router.shape[1])))
    lane_spec = pl.BlockSpec((None, ROW_TILE, ROUTER_LANES), lambda b, t: (b, t, 0))
    return pl.pallas_call(
        functools.partial(_normmod_router_kernel, sh_row=sh_row, sc_row=sc_row),
        out_shape=(
            jax.ShapeDtypeStruct(h.shape, F32),
            jax.ShapeDtypeStruct((bsz, length, ROUTER_LANES), jnp.int32),
            jax.ShapeDtypeStruct((bsz, length, ROUTER_LANES), F32),
        ),
        grid=(bsz, length // ROW_TILE),
        in_specs=[x_spec, pl.BlockSpec((1, d), lambda b, t: (0, 0)), mod_spec,
                  pl.BlockSpec((d, ROUTER_LANES), lambda b, t: (0, 0))],
        out_specs=(x_spec, lane_spec, lane_spec),
        compiler_params=_cp(("arbitrary", "arbitrary")),
        name="normmod_router",
    )(h, nw.reshape(1, d), modtab, wr)


def _rms_kernel(x_ref, nw_ref, o_ref):
    x = x_ref[...]
    ms = jnp.mean(x * x, axis=-1, keepdims=True)
    o_ref[...] = x * lax.rsqrt(ms + EPS) * nw_ref[...]


def _final_norm(h, nw):
    bsz, length, d = h.shape
    x_spec = pl.BlockSpec((None, ROW_TILE, d), lambda b, t: (b, t, 0))
    return pl.pallas_call(
        _rms_kernel,
        out_shape=jax.ShapeDtypeStruct(h.shape, F32),
        grid=(bsz, length // ROW_TILE),
        in_specs=[x_spec, pl.BlockSpec((1, d), lambda b, t: (0, 0))],
        out_specs=x_spec,
        compiler_params=_cp(("arbitrary", "arbitrary")),
        name="final_norm",
    )(h, nw.reshape(1, d))


def _cast_weights(pairs):
    @pl.when(pl.program_id(1) == 0)
    def _():
        for w_ref, wbf_ref in pairs:
            wbf_ref[...] = w_ref[...].astype(BF16)


def _mm_plain_kernel(x_ref, w_ref, o_ref, wbf_ref):
    _cast_weights([(w_ref, wbf_ref)])
    o_ref[...] = _dot(x_ref[...], wbf_ref[...]).astype(o_ref.dtype)


def _mm_swiglu_kernel(x_ref, wa_ref, wb_ref, o_ref, wabf_ref, wbbf_ref):
    _cast_weights([(wa_ref, wabf_ref), (wb_ref, wbbf_ref)])
    x = x_ref[...]
    a = _dot(x, wabf_ref[...])
    b = _dot(x, wbbf_ref[...])
    o_ref[...] = (_silu(a) * b).astype(o_ref.dtype)


def _mm_res_kernel(x_ref, w_ref, res_ref, mod_ref, o_ref, wbf_ref, *, g_row):
    _cast_weights([(w_ref, wbf_ref)])
    acc = _dot(x_ref[...], wbf_ref[...])
    o_ref[...] = res_ref[...] + mod_ref[g_row:g_row + 1, :] * acc


def _mm_glu_res_kernel(x_ref, wa_ref, wb_ref, ba_ref, bb_ref, res_ref, mod_ref, o_ref, wabf_ref, wbbf_ref,
                       *, g_row):
    _cast_weights([(wa_ref, wabf_ref), (wb_ref, wbbf_ref)])
    x = x_ref[...]
    z1 = _dot(x, wabf_ref[...]) + ba_ref[...]
    z2 = _dot(x, wbbf_ref[...]) + bb_ref[...]
    o_ref[...] = res_ref[...] + mod_ref[g_row:g_row + 1, :] * (z1 * _sigmoid(z2))


def _mm_plain(x, w, layer, n, tm, tn, out_dtype, name):
    r, k = x.shape
    assert r % tm == 0 and n % tn == 0
    return pl.pallas_call(
        _mm_plain_kernel,
        out_shape=jax.ShapeDtypeStruct((r, n), out_dtype),
        grid=(n // tn, r // tm),
        in_specs=[pl.BlockSpec((tm, k), lambda j, m: (m, 0)),
                  pl.BlockSpec((None, k, tn), lambda j, m: (layer, 0, j))],
        out_specs=pl.BlockSpec((tm, tn), lambda j, m: (m, j)),
        scratch_shapes=[pltpu.VMEM((k, tn), BF16)],
        compiler_params=_cp(("arbitrary", "arbitrary"), VMEM_LIMIT),
        name=name,
    )(x, w)


def _mm_swiglu(x, w13, layer, tm, tn, name):
    r, k = x.shape
    half = w13.shape[2] // 2
    nb = half // tn
    assert r % tm == 0 and half % tn == 0
    return pl.pallas_call(
        _mm_swiglu_kernel,
        out_shape=jax.ShapeDtypeStruct((r, half), BF16),
        grid=(nb, r // tm),
        in_specs=[pl.BlockSpec((tm, k), lambda j, m: (m, 0)),
                  pl.BlockSpec((None, k, tn), lambda j, m: (layer, 0, j)),
                  pl.BlockSpec((None, k, tn), lambda j, m: (layer, 0, nb + j))],
        out_specs=pl.BlockSpec((tm, tn), lambda j, m: (m, j)),
        scratch_shapes=[pltpu.VMEM((k, tn), BF16), pltpu.VMEM((k, tn), BF16)],
        compiler_params=_cp(("arbitrary", "arbitrary"), VMEM_LIMIT),
        name=name,
    )(x, w13, w13)


def _res_specs(tn, tpb, nct):
    res_spec = pl.BlockSpec((ROW_TILE, tn), lambda j, m: (m, j))
    mod_spec = pl.BlockSpec((None, None, 6, tn),
                            lambda j, m: (m // tpb, jnp.where(m % tpb >= nct, 1, 0), 0, j))
    return res_spec, mod_spec


def _mm_res(x, w, layer, res, modtab, g_row, tpb, nct, tn, name):
    r, k = x.shape
    n = w.shape[2]
    tm = ROW_TILE
    res_spec, mod_spec = _res_specs(tn, tpb, nct)
    return pl.pallas_call(
        functools.partial(_mm_res_kernel, g_row=g_row),
        out_shape=jax.ShapeDtypeStruct((r, n), F32),
        grid=(n // tn, r // tm),
        in_specs=[pl.BlockSpec((tm, k), lambda j, m: (m, 0)),
                  pl.BlockSpec((None, k, tn), lambda j, m: (layer, 0, j)),
                  res_spec, mod_spec],
        out_specs=pl.BlockSpec((tm, tn), lambda j, m: (m, j)),
        scratch_shapes=[pltpu.VMEM((k, tn), BF16)],
        compiler_params=_cp(("arbitrary", "arbitrary"), VMEM_LIMIT),
        name=name,
    )(x, w, res, modtab)


def _mm_glu_res(x, w, bias, layer, res, modtab, g_row, tpb, nct, tn, name):
    r, k = x.shape
    half = w.shape[2] // 2
    nb = half // tn
    tm = ROW_TILE
    res_spec, mod_spec = _res_specs(tn, tpb, nct)
    bias2 = bias.reshape(bias.shape[0], 1, 2 * half)
    return pl.pallas_call(
        functools.partial(_mm_glu_res_kernel, g_row=g_row),
        out_shape=jax.ShapeDtypeStruct((r, half), F32),
        grid=(nb, r // tm),
        in_specs=[pl.BlockSpec((tm, k), lambda j, m: (m, 0)),
                  pl.BlockSpec((None, k, tn), lambda j, m: (layer, 0, j)),
                  pl.BlockSpec((None, k, tn), lambda j, m: (layer, 0, nb + j)),
                  pl.BlockSpec((None, 1, tn), lambda j, m: (layer, 0, j)),
                  pl.BlockSpec((None, 1, tn), lambda j, m: (layer, 0, nb + j)),
                  res_spec, mod_spec],
        out_specs=pl.BlockSpec((tm, tn), lambda j, m: (m, j)),
        scratch_shapes=[pltpu.VMEM((k, tn), BF16), pltpu.VMEM((k, tn), BF16)],
        compiler_params=_cp(("arbitrary", "arbitrary"), VMEM_LIMIT),
        name=name,
    )(x, w, w, bias2, bias2, res, modtab)


def _gla_gate_kernel(x_ref, wgl_ref, wup_ref, b_ref, o_ref):
    wh, wl = _split(wgl_ref[...])
    x = x_ref[...]
    gl = _dot(x, wh) + _dot(x, wl)
    z = _dot3(gl, wup_ref[...]) + b_ref[...]
    log_sig = jnp.minimum(z, 0.0) - jnp.log(1.0 + jnp.exp(-jnp.abs(z)))
    o_ref[...] = log_sig * (1.0 / GLA_GATE_TAU)


def _gla_gate(u, w_gl, w_up_bd, b_gate):
    r, d = u.shape
    n = w_up_bd.shape[1]
    tm = 512
    return pl.pallas_call(
        _gla_gate_kernel,
        out_shape=jax.ShapeDtypeStruct((r, n), F32),
        grid=(r // tm,),
        in_specs=[pl.BlockSpec((tm, d), lambda m: (m, 0)),
                  pl.BlockSpec(w_gl.shape, lambda m: (0, 0)),
                  pl.BlockSpec(w_up_bd.shape, lambda m: (0, 0)),
                  pl.BlockSpec((1, n), lambda m: (0, 0))],
        out_specs=pl.BlockSpec((tm, n), lambda m: (m, 0)),
        compiler_params=_cp(("arbitrary",)),
        name="gla_gate",
    )(u, w_gl, w_up_bd, b_gate.reshape(1, n))


def _gla_scan_kernel(tri_ref, qf_ref, kf_ref, vf_ref, gf_ref, qb_ref, kb_ref, vb_ref, gb_ref, of_ref, ob_ref, s_ref,
                     *, hk, hv):
    @pl.when(pl.program_id(1) == 0)
    def _():
        s_ref[...] = jnp.zeros_like(s_ref)

    ones = jnp.ones((GLA_CHUNK, 128), BF16)
    scale = hk ** -0.5
    tn_dims = (((0,), (0,)), ((), ()))
    nt_dims = (((1,), (1,)), ((), ()))
    streams = ((qf_ref, kf_ref, vf_ref, gf_ref, of_ref), (qb_ref, kb_ref, vb_ref, gb_ref, ob_ref))
    chains = [(d, h) for d in range(2) for h in range(GLA_HEADS)]
    decays = []
    for d, h in chains:
        g = streams[d][3][:, h * hk:(h + 1) * hk]
        gh, gl = _split(g)
        tri = tri_ref[d]
        b = _dot(tri, gh) + _dot(tri, gl)
        tot = jnp.sum(g, axis=0, keepdims=True)
        tot_col = (lax.dot_general(gh, ones, tn_dims, preferred_element_type=F32)
                   + lax.dot_general(gl, ones, tn_dims, preferred_element_type=F32))
        decays.append((b, tot, tot_col))
    operands = []
    for (d, h), (b, tot, tot_col) in zip(chains, decays):
        q = streams[d][0][:, h * hk:(h + 1) * hk]
        k = streams[d][1][:, h * hk:(h + 1) * hk]
        qd = (q * scale * jnp.exp(b)).astype(BF16)
        kd = (k * jnp.exp(-b)).astype(BF16)
        kt = (k * jnp.exp(tot - b)).astype(BF16)
        scores = jnp.where(tri_ref[d] > 0, lax.dot_general(qd, kd, nt_dims, preferred_element_type=F32), 0.0)
        operands.append((qd, kt, scores.astype(BF16)))
    for (d, h), (qd, kt, scores) in zip(chains, operands):
        v = streams[d][2][:, h * hv:(h + 1) * hv].astype(BF16)
        streams[d][4][:, h * hv:(h + 1) * hv] = _dot(scores, v) + _dot(qd, s_ref[d, h].astype(BF16))
    for (d, h), (b, tot, tot_col), (qd, kt, scores) in zip(chains, decays, operands):
        v = streams[d][2][:, h * hv:(h + 1) * hv].astype(BF16)
        decay = jnp.concatenate([jnp.exp(tot_col)] * (hv // 128), axis=1)
        s_ref[d, h] = decay * s_ref[d, h] + lax.dot_general(kt, v, tn_dims, preferred_element_type=F32)


def _gla_scan(proj, log_a, nctx_chunks, dk, dv):
    bsz, length, _ = proj.shape
    nchunks = length // GLA_CHUNK
    hk, hv = dk // GLA_HEADS, dv // GLA_HEADS
    idx = jnp.arange(GLA_CHUNK)
    tri = jnp.stack([idx[:, None] >= idx[None, :], idx[:, None] <= idx[None, :]]).astype(BF16)

    def back(s):
        return jnp.where(s < nctx_chunks, nctx_chunks - 1 - s, nchunks - 1 - (s - nctx_chunks))

    def specs(chunk, direction):
        return [pl.BlockSpec((None, GLA_CHUNK, dk), lambda b, s: (b, chunk(s), 0)),
                pl.BlockSpec((None, GLA_CHUNK, dk), lambda b, s: (b, chunk(s), 1)),
                pl.BlockSpec((None, GLA_CHUNK, dv), lambda b, s: (b, chunk(s), (2 * dk) // dv)),
                pl.BlockSpec((None, GLA_CHUNK, dk), lambda b, s: (b, chunk(s), direction))]

    out = jax.ShapeDtypeStruct((bsz, length, dv), F32)
    return pl.pallas_call(
        functools.partial(_gla_scan_kernel, hk=hk, hv=hv),
        out_shape=(out, out),
        grid=(bsz, nchunks),
        in_specs=[pl.BlockSpec((2, GLA_CHUNK, GLA_CHUNK), lambda b, s: (0, 0, 0))]
        + specs(lambda s: s, 0) + specs(back, 1),
        out_specs=(pl.BlockSpec((None, GLA_CHUNK, dv), lambda b, s: (b, s, 0)),
                   pl.BlockSpec((None, GLA_CHUNK, dv), lambda b, s: (b, back(s), 0))),
        scratch_shapes=[pltpu.VMEM((2, GLA_HEADS, hk, hv), F32)],
        compiler_params=_cp(("arbitrary", "arbitrary"), VMEM_LIMIT),
        name="gla_scan",
    )(tri, proj, proj, proj, log_a, proj, proj, proj, log_a)


def _gla_finish_kernel(of_ref, ob_ref, r_ref, gh_ref, o_ref, *, hv):
    for h in range(GLA_HEADS):
        sl = slice(h * hv, (h + 1) * hv)
        o = of_ref[:, sl] + ob_ref[:, sl]
        ms = jnp.mean(o * o, axis=-1, keepdims=True)
        y = o * lax.rsqrt(ms + EPS) * gh_ref[...]
        o_ref[:, sl] = (y * _silu(r_ref[:, sl])).astype(o_ref.dtype)


def _gla_finish(o_fwd, o_bwd, proj, g_head, dk, dv):
    bsz, length, _ = o_fwd.shape
    hv = dv // GLA_HEADS
    tm = ROW_TILE
    o_spec = pl.BlockSpec((None, tm, dv), lambda b, t: (b, t, 0))
    return pl.pallas_call(
        functools.partial(_gla_finish_kernel, hv=hv),
        out_shape=jax.ShapeDtypeStruct((bsz, length, dv), BF16),
        grid=(bsz, length // tm),
        in_specs=[
            o_spec, o_spec,
            pl.BlockSpec((None, tm, dv), lambda b, t: (b, t, (2 * dk + dv) // dv)),
            pl.BlockSpec((1, hv), lambda b, t: (0, 0)),
        ],
        out_specs=o_spec,
        compiler_params=_cp(("arbitrary", "arbitrary")),
        name="gla_finish",
    )(o_fwd, o_bwd, proj, g_head.reshape(1, hv))


def _gla_layer(h, u, modtab, nw, nw_ffn, w_in, layer, w_gate_up, b_gate, g_head, w_out, nctx):
    bsz, length, d = h.shape
    dk = w_gate_up.shape[2]
    dv = w_out.shape[1]
    nmain = 2 * dk + 2 * dv
    tpb = length // ROW_TILE
    nct = nctx // ROW_TILE
    if u is None:
        u = _normmod(h, nw, modtab, SH1, SC1, nct, BF16).reshape(bsz * length, d)
    proj = _mm_plain(u, w_in, layer, nmain, 512, 1024, F32, "gla_in_proj")
    rank = GLA_GATE_RANK
    w_up_bd = jnp.zeros((2 * rank, 2 * dk), F32)
    w_up_bd = w_up_bd.at[:rank, :dk].set(w_gate_up[0]).at[rank:, dk:].set(w_gate_up[1])
    log_a = _gla_gate(u, w_in[layer, :, nmain:], w_up_bd, b_gate.reshape(2 * dk))
    proj3 = proj.reshape(bsz, length, nmain)
    o_fwd, o_bwd = _gla_scan(proj3, log_a.reshape(bsz, length, 2 * dk), nctx // GLA_CHUNK, dk, dv)
    gated = _gla_finish(o_fwd, o_bwd, proj3, g_head, dk, dv).reshape(bsz * length, dv)
    out, v = _dense_mm_res(gated, w_out, layer, h.reshape(bsz * length, d), modtab, G1, tpb, nct, d, "gla_out_proj",
                           next_norm=(nw_ffn, SH2, SC2))
    return out.reshape(bsz, length, d), v


S5_LANES_PER_BATCH = 96
_AN1, _AN2, _AP1, _AP2, _BB1, _BB2, _CC1, _CC2, _PQ, _PK_ROWS = 0, 32, 64, 96, 128, 144, 160, 176, 192, 200


def _s5_params(a_re, a_im, log_dt, b_re, b_im, c_re, c_im):
    t_len = S5_CHUNK
    dt = jnp.exp(log_dt)[..., None]
    mag = jnp.exp(a_re * dt)
    ab_re = mag * jnp.cos(a_im * dt)
    ab_im = mag * jnp.sin(a_im * dt)
    den = a_re * a_re + a_im * a_im
    f_re = ((ab_re - 1.0) * a_re + ab_im * a_im) / den
    f_im = (ab_im * a_re - (ab_re - 1.0) * a_im) / den
    bb_re = f_re[..., None] * b_re - f_im[..., None] * b_im
    bb_im = f_re[..., None] * b_im + f_im[..., None] * b_re

    def cpow(n):
        nn = n.astype(F32)[:, None, :, None]
        m = jnp.exp((a_re * dt)[:, :, None, :] * nn)
        th = (a_im * dt)[:, :, None, :] * nn
        return m * jnp.cos(th), m * jnp.sin(th)

    steps = jnp.arange(t_len)
    order = jnp.stack([steps, t_len - 1 - steps])
    apr, api = cpow(order)
    inv_sq = jnp.exp(-2.0 * (a_re * dt)[:, :, None, :] * order.astype(F32)[:, None, :, None])
    anr, ani = apr * inv_sq, -api * inv_sq
    atr, ati = cpow(jnp.full((2, 1), t_len))
    cat = lambda u, v: jnp.concatenate([u, v], axis=-1)
    bbr_t, bbi_t = bb_re.transpose(0, 1, 3, 2), bb_im.transpose(0, 1, 3, 2)
    zero = jnp.zeros((2, a_re.shape[1], 5, 2 * a_re.shape[2]), F32)
    return jnp.concatenate([
        cat(anr, anr), cat(-ani, ani),
        cat(apr, api), cat(api, apr),
        cat(bbr_t, bbi_t), cat(bbi_t, bbr_t),
        cat(c_re, -c_re), cat(-c_im, -c_im),
        cat(atr, atr), cat(-ati, ati), cat(ati, -ati), zero], axis=2)


def _s5_in_kernel(hl_ref, hc_ref, nw_ref, mod_ref, o_ref, *, bsz, qb):
    nw = nw_ref[...]
    blocks = []
    for b in range(bsz):
        lat = _normmod_rows(hl_ref[b], nw, mod_ref[b, 1, SC1:SC1 + 1, :], mod_ref[b, 1, SH1:SH1 + 1, :])
        ctx = _normmod_rows(hc_ref[b], nw, mod_ref[b, 0, SC1:SC1 + 1, :], mod_ref[b, 0, SH1:SH1 + 1, :])
        pad = jnp.zeros((qb - lat.shape[0] - ctx.shape[0], lat.shape[1]), F32)
        blocks += [lat, ctx, pad]
    o_ref[...] = jnp.concatenate(blocks, axis=0).T.astype(o_ref.dtype)


def _s5_in(h, nw, modtab, nctx):
    bsz, length, d = h.shape
    t_len = S5_CHUNK
    qb = S5_LANES_PER_BATCH
    nctx_chunks = nctx // t_len
    return pl.pallas_call(
        functools.partial(_s5_in_kernel, bsz=bsz, qb=qb),
        out_shape=jax.ShapeDtypeStruct((d, t_len * bsz * qb), BF16),
        grid=(t_len,),
        in_specs=[
            pl.BlockSpec((bsz, None, GRID_W, d), lambda s: (0, nctx // GRID_W + s, 0, 0)),
            pl.BlockSpec((bsz, nctx_chunks, d), lambda s: (0, 0, s)),
            pl.BlockSpec((1, d), lambda s: (0, 0)),
            pl.BlockSpec(modtab.shape, lambda s: (0, 0, 0, 0)),
        ],
        out_specs=pl.BlockSpec((d, bsz * qb), lambda s: (0, s)),
        compiler_params=_cp(("arbitrary",), VMEM_LIMIT),
        name="s5_in",
    )(h.reshape(bsz, length // GRID_W, GRID_W, d), h[:, :nctx].reshape(bsz, nctx_chunks, t_len * d),
      nw.reshape(1, d), modtab)


def _tile_rows(x, reps):
    return jnp.concatenate([x] * reps, axis=0)


def _repeat_rows(x, reps):
    return jnp.concatenate([jnp.broadcast_to(x[i:i + 1, :], (reps, x.shape[1])) for i in range(x.shape[0])],
                           axis=0)


def _s5_kernel(ut_ref, pk_ref, y_ref, sv_ref, ss_ref, w_ref, r2_ref, acc_ref, *, bsz, qb, orders):
    gb = ut_ref.shape[0] // S5_GROUP
    t_len = S5_CHUNK
    k = S5_GROUP * t_len
    q = ut_ref.shape[1] // t_len
    row_s = lax.broadcasted_iota(jnp.int32, (k, k), 0) // S5_GROUP
    col_t = lax.broadcasted_iota(jnp.int32, (k, k), 1) // S5_GROUP
    masks = (row_s <= col_t, row_s >= col_t)
    w_ref[...] = jnp.zeros_like(w_ref)
    us = []
    for g in range(gb):
        x = ut_ref[g * S5_GROUP:(g + 1) * S5_GROUP, :]
        us.append(jnp.concatenate([x[:, s * q:(s + 1) * q] for s in range(t_len)], axis=0).T)
    pairs = [(g, d) for g in range(gb) for d in range(2)]
    l2s, raw = [], []
    for g, d in pairs:
        pk = pk_ref[d, g]
        l2 = (_repeat_rows(pk[_AN1:_AN1 + t_len], S5_GROUP) * _tile_rows(pk[_BB1:_BB1 + S5_GROUP], t_len)
              + _repeat_rows(pk[_AN2:_AN2 + t_len], S5_GROUP) * _tile_rows(pk[_BB2:_BB2 + S5_GROUP], t_len))
        r2t = (_repeat_rows(pk[_AP1:_AP1 + t_len], S5_GROUP) * _tile_rows(pk[_CC1:_CC1 + S5_GROUP], t_len)
               + _repeat_rows(pk[_AP2:_AP2 + t_len], S5_GROUP) * _tile_rows(pk[_CC2:_CC2 + S5_GROUP], t_len))
        r2 = r2t.T
        r2_ref[2 * g + d] = r2.astype(BF16)
        l2s.append(l2.astype(BF16))
        raw.append(_dot3(l2, r2))
    for (g, d), l2, toep_raw in zip(pairs, l2s, raw):
        i = 2 * g + d
        y_local = _dot(us[g], jnp.where(masks[d], toep_raw, 0.0).astype(BF16))
        if d == 0:
            acc_ref[g] = y_local
        else:
            acc_ref[g] = acc_ref[g] + y_local
        v = _dot(us[g], l2)
        vs = pltpu.roll(v, v.shape[1] // 2, axis=1)
        pk = pk_ref[d, g]
        pt, qt, qts = pk[_PQ:_PQ + 1], pk[_PQ + 1:_PQ + 2], pk[_PQ + 2:_PQ + 3]
        sv_ref[i] = pt * v + qt * vs
        ss_ref[i] = pt * vs + qts * v
    coef = []
    for g in range(gb):
        for d in range(2):
            pk = pk_ref[d, g]
            coef.append(tuple(jnp.broadcast_to(pk[_PQ + r:_PQ + r + 1], (bsz, pk.shape[1])) for r in range(3)))
    zero = jnp.zeros((bsz, pk_ref.shape[3]), F32)
    state = [(zero, zero)] * (2 * gb)
    nsteps = len(orders[0])
    for step in range(nsteps):
        for i in range(2 * gb):
            rows = pl.ds(orders[i % 2][step], bsz, stride=qb)
            w, ws = state[i]
            w_ref[i, rows, :] = w
            if step + 1 < nsteps:
                pt, qt, qts = coef[i]
                state[i] = (pt * w + qt * ws + sv_ref[i, rows, :], pt * ws + qts * w + ss_ref[i, rows, :])
    for g in range(gb):
        y = acc_ref[g]
        for d in range(2):
            i = 2 * g + d
            y = y + _dot(w_ref[i].astype(BF16), r2_ref[i])
        yt = y.T
        for t in range(t_len):
            y_ref[g * S5_GROUP:(g + 1) * S5_GROUP, t * q:(t + 1) * q] = yt[t * S5_GROUP:(t + 1) * S5_GROUP, :]


def _s5_scan(ut, pk, bsz, nctx_chunks, nlat_chunks):
    t_len = S5_CHUNK
    q = ut.shape[1] // t_len
    ngroups = ut.shape[0] // S5_GROUP
    np2 = pk.shape[3]
    gb = S5_GROUP_BLOCK
    qb = S5_LANES_PER_BATCH
    lat = list(range(nlat_chunks))
    ctx = list(range(nlat_chunks, nlat_chunks + nctx_chunks))
    orders = (ctx + lat, ctx[::-1] + lat[::-1])
    blk = (gb * S5_GROUP, t_len * q)
    return pl.pallas_call(
        functools.partial(_s5_kernel, bsz=bsz, qb=qb, orders=orders),
        out_shape=jax.ShapeDtypeStruct(ut.shape, F32),
        grid=(ngroups // gb,),
        in_specs=[pl.BlockSpec(blk, lambda g: (g, 0)),
                  pl.BlockSpec((2, gb, _PK_ROWS, np2), lambda g: (0, g, 0, 0))],
        out_specs=pl.BlockSpec(blk, lambda g: (g, 0)),
        scratch_shapes=[pltpu.VMEM((2 * gb, q, np2), F32), pltpu.VMEM((2 * gb, q, np2), F32),
                        pltpu.VMEM((2 * gb, q, np2), F32), pltpu.VMEM((2 * gb, np2, S5_GROUP * t_len), BF16),
                        pltpu.VMEM((gb, q, S5_GROUP * t_len), F32)],
        compiler_params=_cp(("arbitrary",), VMEM_LIMIT),
        name="s5_scan",
    )(ut, pk)


def _s5_out_kernel(yt_ref, hl_ref, hc_ref, nw_ref, mod_ref, d_ref, ol_ref, oc_ref, *, bsz, qb):
    y = yt_ref[...].T
    nw = nw_ref[...]
    nlat, nc = hl_ref.shape[1], hc_ref.shape[1]
    c0 = 0.7978845608028654

    def act(yv, hv, region, b):
        u = _normmod_rows(hv, nw, mod_ref[b, region, SC1:SC1 + 1, :], mod_ref[b, region, SH1:SH1 + 1, :])
        z = yv + u * d_ref[...]
        return (0.5 * z * (1.0 + jnp.tanh(c0 * (z + 0.044715 * (z * z * z))))).astype(ol_ref.dtype)

    for b in range(bsz):
        ol_ref[b] = act(y[b * qb:b * qb + nlat], hl_ref[b], 1, b)
        oc_ref[b] = act(y[b * qb + nlat:b * qb + nlat + nc], hc_ref[b], 0, b)


def _s5_out(yt, h, nw, modtab, d_skip, nctx):
    bsz, length, d = h.shape
    t_len = S5_CHUNK
    qb = S5_LANES_PER_BATCH
    nctx_chunks = nctx // t_len
    seq = length - nctx
    lat_spec = pl.BlockSpec((bsz, None, GRID_W, d), lambda s: (0, nctx // GRID_W + s, 0, 0))
    ctx_spec = pl.BlockSpec((bsz, nctx_chunks, d), lambda s: (0, 0, s))
    act_l, act_c = pl.pallas_call(
        functools.partial(_s5_out_kernel, bsz=bsz, qb=qb),
        out_shape=(jax.ShapeDtypeStruct((bsz, seq // GRID_W, GRID_W, d), BF16),
                   jax.ShapeDtypeStruct((bsz, nctx_chunks, t_len * d), BF16)),
        grid=(t_len,),
        in_specs=[pl.BlockSpec((d, bsz * qb), lambda s: (0, s)),
                  lat_spec, ctx_spec,
                  pl.BlockSpec((1, d), lambda s: (0, 0)),
                  pl.BlockSpec(modtab.shape, lambda s: (0, 0, 0, 0)),
                  pl.BlockSpec((1, d), lambda s: (0, 0))],
        out_specs=(pl.BlockSpec((bsz, None, GRID_W, d), lambda s: (0, s, 0, 0)), ctx_spec),
        compiler_params=_cp(("arbitrary",), VMEM_LIMIT),
        name="s5_out",
    )(yt, h.reshape(bsz, length // GRID_W, GRID_W, d), h[:, :nctx].reshape(bsz, nctx_chunks, t_len * d),
      nw.reshape(1, d), modtab, d_skip.reshape(1, d))
    return act_l.reshape(bsz, seq, d), act_c.reshape(bsz, nctx, d)


def _s5_layer(h, modtab, nw, a_re, a_im, log_dt, b_re, b_im, c_re, c_im, d_skip, w_glu, b_glu, layer, nctx,
              last):
    bsz, length, d = h.shape
    seq = length - nctx
    t_len = S5_CHUNK
    assert seq // GRID_W == t_len and nctx % t_len == 0 and nctx % GRID_W == 0
    assert seq // t_len + nctx // t_len <= S5_LANES_PER_BATCH
    ngroups = d // S5_GROUP
    ut = _s5_in(h, nw, modtab, nctx)
    pk = _s5_params(a_re, a_im, log_dt, b_re, b_im, c_re, c_im)
    yt = _s5_scan(ut, pk, bsz, nctx // t_len, GRID_W)
    act_l, act_c = _s5_out(yt, h, nw, modtab, d_skip, nctx)
    if last:
        act, h, nct = act_l, h[:, nctx:], 0
    else:
        act, nct = jnp.concatenate([act_c, act_l], axis=1), nctx // ROW_TILE
    rows = h.shape[0] * h.shape[1]
    out = _dense_glu_res(act.reshape(rows, d), w_glu, b_glu, layer, h.reshape(rows, d), modtab, G1,
                         h.shape[1] // ROW_TILE, nct, 1024, "s5_glu")
    return out.reshape(h.shape)


def _dense_ffn(h, v, modtab, w13, w2, layer, nct):
    bsz, length, d = h.shape
    rows = bsz * length
    w13_grouped = w13.reshape(w13.shape[0], 1, w13.shape[1], w13.shape[2])
    act = _moe_swiglu(_dense_route(rows // MOE_TILE), v, w13_grouped, layer, MOE_W13_COLS, "ffn_w13")
    out = _dense_mm_res(act, w2, layer, h.reshape(rows, d), modtab, G2, length // ROW_TILE, nct, MOE_W2_COLS,
                        "ffn_w2")
    return out.reshape(h.shape)


def _moe_gather_kernel(src_ref, nxt_ref, v_hbm, o_ref, buf_ref, sem):
    t = pl.program_id(0)
    slot = t % 2

    def request(idx_ref, s):
        def issue(r8, carry):
            for i in range(DMA_ISSUE_UNROLL):
                r = r8 * DMA_ISSUE_UNROLL + i
                pltpu.make_async_copy(v_hbm.at[pl.ds(idx_ref[0, r], 1)], buf_ref.at[s, pl.ds(r, 1)],
                                      sem.at[s]).start(priority=i % 2)
            return carry
        lax.fori_loop(0, MOE_TILE // DMA_ISSUE_UNROLL, issue, 0)

    @pl.when(t == 0)
    def _():
        request(src_ref, 0)

    @pl.when(t + 1 < pl.num_programs(0))
    def _():
        request(nxt_ref, 1 - slot)

    pltpu.make_async_copy(v_hbm.at[pl.ds(0, MOE_TILE)], buf_ref.at[slot], sem.at[slot]).wait()
    o_ref[...] = buf_ref[slot].astype(o_ref.dtype)


def _moe_gather(src, v):
    ntiles = src.shape[0]
    d = v.shape[1]
    idx_block = (None, 1, MOE_TILE)
    return pl.pallas_call(
        _moe_gather_kernel,
        out_shape=jax.ShapeDtypeStruct((ntiles * MOE_TILE, d), BF16),
        grid=(ntiles,),
        in_specs=[pl.BlockSpec(idx_block, lambda t: (t, 0, 0), memory_space=pltpu.SMEM),
                  pl.BlockSpec(idx_block, lambda t: (jnp.minimum(t + 1, ntiles - 1), 0, 0),
                               memory_space=pltpu.SMEM),
                  pl.BlockSpec(memory_space=pl.ANY)],
        out_specs=pl.BlockSpec((MOE_TILE, d), lambda t: (t, 0)),
        scratch_shapes=[pltpu.VMEM((2, MOE_TILE, d), F32), pltpu.SemaphoreType.DMA((2,))],
        compiler_params=_cp(("arbitrary",)),
        name="moe_gather",
    )(src, src, v)


def _moe_fetch_weights(te_ref, first_ref, nxt_ref, w_hbm, stage_ref, sem, wbf_ref, col_blocks, layer):
    j, t = pl.program_id(0), pl.program_id(1)
    nsweeps = pl.num_programs(0)
    tn = stage_ref.shape[2]

    def copies(e, sweep):
        return [pltpu.make_async_copy(w_hbm.at[layer, e, :, pl.ds(pl.multiple_of(cb(sweep) * tn, 128), tn)],
                                      stage_ref.at[i], sem.at[i])
                for i, cb in enumerate(col_blocks)]

    @pl.when(jnp.logical_and(j == 0, t == 0))
    def _():
        for c in copies(te_ref[0], 0):
            c.start()

    @pl.when(first_ref[t] == 1)
    def _():
        for c in copies(te_ref[t], j):
            c.wait()
        for i in range(len(col_blocks)):
            wbf_ref[:, i * tn:(i + 1) * tn] = stage_ref[i].astype(BF16)
        sweep_done = nxt_ref[t] < 0
        next_e = jnp.where(sweep_done, te_ref[0], nxt_ref[t])
        next_sweep = jnp.where(sweep_done, j + 1, j)

        @pl.when(next_sweep < nsweeps)
        def _():
            for c in copies(next_e, next_sweep):
                c.start()


def _moe_swiglu_kernel(te_ref, nv_ref, first_ref, nxt_ref, x_ref, w_hbm, o_ref, wbf_ref, stage_ref, sem, *, layer):
    nb = pl.num_programs(0)
    _moe_fetch_weights(te_ref, first_ref, nxt_ref, w_hbm, stage_ref, sem, wbf_ref,
                       [lambda sweep: sweep, lambda sweep: nb + sweep], layer)

    in_use = pl.program_id(1) < nv_ref[0]

    @pl.when(in_use)
    def _():
        tn = o_ref.shape[1]
        z = _dot(x_ref[...], wbf_ref[...])
        o_ref[...] = (_silu(z[:, :tn]) * z[:, tn:]).astype(o_ref.dtype)

    @pl.when(jnp.logical_not(in_use))
    def _():
        o_ref[...] = jnp.zeros_like(o_ref)


def _moe_mm_kernel(te_ref, nv_ref, first_ref, nxt_ref, x_ref, w_hbm, o_ref, wbf_ref, stage_ref, sem, *, layer):
    _moe_fetch_weights(te_ref, first_ref, nxt_ref, w_hbm, stage_ref, sem, wbf_ref, [lambda sweep: sweep], layer)

    in_use = pl.program_id(1) < nv_ref[0]

    @pl.when(in_use)
    def _():
        o_ref[...] = _dot(x_ref[...], wbf_ref[...])

    @pl.when(jnp.logical_not(in_use))
    def _():
        o_ref[...] = jnp.zeros_like(o_ref)


def _moe_swiglu(route, xs, w13, layer, tn, name="moe_w13"):
    rows, k = xs.shape
    half = w13.shape[3] // 2
    nb = half // tn
    tm = MOE_TILE
    return pl.pallas_call(
        functools.partial(_moe_swiglu_kernel, layer=layer),
        out_shape=jax.ShapeDtypeStruct((rows, half), BF16),
        grid_spec=pltpu.PrefetchScalarGridSpec(
            num_scalar_prefetch=4,
            grid=(nb, rows // tm),
            in_specs=[pl.BlockSpec((tm, k), lambda j, t, *_: (t, 0)),
                      pl.BlockSpec(memory_space=pl.ANY)],
            out_specs=pl.BlockSpec((tm, tn), lambda j, t, *_: (t, j)),
            scratch_shapes=[pltpu.VMEM((k, 2 * tn), BF16), pltpu.VMEM((2, k, tn), F32),
                            pltpu.SemaphoreType.DMA((2,))]),
        compiler_params=_cp(("arbitrary", "arbitrary"), VMEM_LIMIT),
        name=name,
    )(*route, xs, w13)


def _moe_mm(route, xs, w2, layer, tn):
    rows, k = xs.shape
    n = w2.shape[3]
    tm = MOE_TILE
    return pl.pallas_call(
        functools.partial(_moe_mm_kernel, layer=layer),
        out_shape=jax.ShapeDtypeStruct((rows, n), F32),
        grid_spec=pltpu.PrefetchScalarGridSpec(
            num_scalar_prefetch=4,
            grid=(n // tn, rows // tm),
            in_specs=[pl.BlockSpec((tm, k), lambda j, t, *_: (t, 0)),
                      pl.BlockSpec(memory_space=pl.ANY)],
            out_specs=pl.BlockSpec((tm, tn), lambda j, t, *_: (t, j)),
            scratch_shapes=[pltpu.VMEM((k, tn), BF16), pltpu.VMEM((1, k, tn), F32),
                            pltpu.SemaphoreType.DMA((1,))]),
        compiler_params=_cp(("arbitrary", "arbitrary"), VMEM_LIMIT),
        name="moe_w2",
    )(*route, xs, w2)


def _dense_route(ntiles):
    zeros = jnp.zeros((ntiles,), jnp.int32)
    return (zeros, jnp.full((1,), ntiles, jnp.int32), zeros.at[0].set(1), zeros - 1)


def _dense_mm_res_kernel(te_ref, nv_ref, first_ref, nxt_ref, x_ref, w_hbm, res_ref, mod_ref, o_ref, wbf_ref,
                         stage_ref, sem, *, layer, g_row):
    _moe_fetch_weights(te_ref, first_ref, nxt_ref, w_hbm, stage_ref, sem, wbf_ref, [lambda sweep: sweep], layer)
    o_ref[...] = res_ref[...] + mod_ref[g_row:g_row + 1, :] * _dot(x_ref[...], wbf_ref[...])


def _dense_mm_res_norm_kernel(te_ref, nv_ref, first_ref, nxt_ref, x_ref, w_hbm, res_ref, mod_ref, nw_ref, o_ref,
                              v_ref, wbf_ref, stage_ref, sem, *, layer, g_row, sh_row, sc_row):
    _moe_fetch_weights(te_ref, first_ref, nxt_ref, w_hbm, stage_ref, sem, wbf_ref, [lambda sweep: sweep], layer)
    h_new = res_ref[...] + mod_ref[g_row:g_row + 1, :] * _dot(x_ref[...], wbf_ref[...])
    o_ref[...] = h_new
    v_ref[...] = _normmod_rows(h_new, nw_ref[...], mod_ref[sc_row:sc_row + 1, :],
                               mod_ref[sh_row:sh_row + 1, :]).astype(v_ref.dtype)


def _dense_glu_res_kernel(te_ref, nv_ref, first_ref, nxt_ref, x_ref, w_hbm, ba_ref, bb_ref, res_ref, mod_ref, o_ref,
                          wbf_ref, stage_ref, sem, *, layer, g_row):
    nb = pl.num_programs(0)
    _moe_fetch_weights(te_ref, first_ref, nxt_ref, w_hbm, stage_ref, sem, wbf_ref,
                       [lambda sweep: sweep, lambda sweep: nb + sweep], layer)
    tn = o_ref.shape[1]
    z = _dot(x_ref[...], wbf_ref[...])
    z1 = z[:, :tn] + ba_ref[...]
    z2 = z[:, tn:] + bb_ref[...]
    o_ref[...] = res_ref[...] + mod_ref[g_row:g_row + 1, :] * (z1 * _sigmoid(z2))


def _dense_res_specs(tn, tpb, nct):
    res_spec = pl.BlockSpec((ROW_TILE, tn), lambda j, t, *_: (t, j))
    mod_spec = pl.BlockSpec((None, None, 6, tn),
                            lambda j, t, *_: (t // tpb, jnp.where(t % tpb >= nct, 1, 0), 0, j))
    return res_spec, mod_spec


def _dense_mm_res(x, w, layer, res, modtab, g_row, tpb, nct, tn, name, next_norm=None):
    r, k = x.shape
    n = w.shape[2]
    tm = ROW_TILE
    res_spec, mod_spec = _dense_res_specs(tn, tpb, nct)
    out_spec = pl.BlockSpec((tm, tn), lambda j, t, *_: (t, j))
    in_specs = [pl.BlockSpec((tm, k), lambda j, t, *_: (t, 0)), pl.BlockSpec(memory_space=pl.ANY), res_spec, mod_spec]
    operands = [x, w.reshape(w.shape[0], 1, k, n), res, modtab]
    if next_norm is None:
        body = functools.partial(_dense_mm_res_kernel, layer=layer, g_row=g_row)
        out_shape, out_specs = jax.ShapeDtypeStruct((r, n), F32), out_spec
    else:
        assert tn == n
        nw, sh_row, sc_row = next_norm
        body = functools.partial(_dense_mm_res_norm_kernel, layer=layer, g_row=g_row, sh_row=sh_row, sc_row=sc_row)
        in_specs.append(pl.BlockSpec((1, n), lambda j, t, *_: (0, 0)))
        operands.append(nw.reshape(1, n))
        out_shape = (jax.ShapeDtypeStruct((r, n), F32), jax.ShapeDtypeStruct((r, n), BF16))
        out_specs = (out_spec, out_spec)
    return pl.pallas_call(
        body,
        out_shape=out_shape,
        grid_spec=pltpu.PrefetchScalarGridSpec(
            num_scalar_prefetch=4,
            grid=(n // tn, r // tm),
            in_specs=in_specs,
            out_specs=out_specs,
            scratch_shapes=[pltpu.VMEM((k, tn), BF16), pltpu.VMEM((1, k, tn), F32),
                            pltpu.SemaphoreType.DMA((1,))]),
        compiler_params=_cp(("arbitrary", "arbitrary"), VMEM_LIMIT),
        name=name,
    )(*_dense_route(r // tm), *operands)


def _dense_glu_res(x, w, bias, layer, res, modtab, g_row, tpb, nct, tn, name):
    r, k = x.shape
    half = w.shape[2] // 2
    nb = half // tn
    tm = ROW_TILE
    res_spec, mod_spec = _dense_res_specs(tn, tpb, nct)
    bias3 = bias.reshape(bias.shape[0], 1, 2 * half)
    return pl.pallas_call(
        functools.partial(_dense_glu_res_kernel, layer=layer, g_row=g_row),
        out_shape=jax.ShapeDtypeStruct((r, half), F32),
        grid_spec=pltpu.PrefetchScalarGridSpec(
            num_scalar_prefetch=4,
            grid=(nb, r // tm),
            in_specs=[pl.BlockSpec((tm, k), lambda j, t, *_: (t, 0)),
                      pl.BlockSpec(memory_space=pl.ANY),
                      pl.BlockSpec((None, 1, tn), lambda j, t, *_: (layer, 0, j)),
                      pl.BlockSpec((None, 1, tn), lambda j, t, *_: (layer, 0, nb + j)),
                      res_spec, mod_spec],
            out_specs=pl.BlockSpec((tm, tn), lambda j, t, *_: (t, j)),
            scratch_shapes=[pltpu.VMEM((k, 2 * tn), BF16), pltpu.VMEM((2, k, tn), F32),
                            pltpu.SemaphoreType.DMA((2,))]),
        compiler_params=_cp(("arbitrary", "arbitrary"), VMEM_LIMIT),
        name=name,
    )(*_dense_route(r // tm), x, w.reshape(w.shape[0], 1, k, 2 * half), bias3, bias3, res, modtab)


def _moe_combine_kernel(pos_ref, nxt_ref, gate_ref, y_hbm, res_ref, mod_ref, *rest, g_row, tail):
    if tail == "final":
        nw_ref, o_ref, buf_ref, sem = rest
    elif tail == "next":
        nw_ref, mod2_ref, o_ref, u_ref, buf_ref, sem = rest
    else:
        o_ref, buf_ref, sem = rest
    m = pl.program_id(0)
    slot = m % 2

    def request(idx_ref, s):
        for k in range(TOP_K):
            def issue(r8, carry, k=k):
                for i in range(DMA_ISSUE_UNROLL):
                    r = r8 * DMA_ISSUE_UNROLL + i
                    pltpu.make_async_copy(y_hbm.at[pl.ds(idx_ref[k, r], 1)], buf_ref.at[s, k, pl.ds(r, 1)],
                                          sem.at[s]).start(priority=i % 2)
                return carry
            lax.fori_loop(0, ROW_TILE // DMA_ISSUE_UNROLL, issue, 0)

    @pl.when(m == 0)
    def _():
        request(pos_ref, 0)

    @pl.when(m + 1 < pl.num_programs(0))
    def _():
        request(nxt_ref, 1 - slot)

    for k in range(TOP_K):
        pltpu.make_async_copy(y_hbm.at[pl.ds(0, ROW_TILE)], buf_ref.at[slot, k], sem.at[slot]).wait()
    p = gate_ref[...]
    mix = p[:, 0:1] * buf_ref[slot, 0] + p[:, 1:2] * buf_ref[slot, 1]
    h_new = res_ref[...] + mod_ref[g_row:g_row + 1, :] * mix
    if tail == "final":
        ms = jnp.mean(h_new * h_new, axis=-1, keepdims=True)
        o_ref[...] = h_new * lax.rsqrt(ms + EPS) * nw_ref[...]
    else:
        o_ref[...] = h_new
    if tail == "next":
        u_ref[...] = _normmod_rows(h_new, nw_ref[...], mod2_ref[SC1:SC1 + 1, :],
                                   mod2_ref[SH1:SH1 + 1, :]).astype(u_ref.dtype)


def _moe_combine(pos, gates, ys, res, modtab, g_row, tpb, nct, tail=None, nw=None, modtab_next=None):
    n, d = res.shape
    tm = ROW_TILE
    ntiles = n // tm
    row_spec = pl.BlockSpec((tm, d), lambda m: (m, 0))
    mod_spec = pl.BlockSpec((None, None, 6, d), lambda m: (m // tpb, jnp.where(m % tpb >= nct, 1, 0), 0, 0))
    in_specs = [pl.BlockSpec((None, TOP_K, tm), lambda m: (m, 0, 0), memory_space=pltpu.SMEM),
                pl.BlockSpec((None, TOP_K, tm), lambda m: (jnp.minimum(m + 1, ntiles - 1), 0, 0),
                             memory_space=pltpu.SMEM),
                pl.BlockSpec((tm, ROUTER_LANES), lambda m: (m, 0)),
                pl.BlockSpec(memory_space=pl.ANY), row_spec, mod_spec]
    operands = [pos, pos, gates, ys, res, modtab]
    out_shape, out_specs = jax.ShapeDtypeStruct((n, d), F32), row_spec
    if tail is not None:
        in_specs.append(pl.BlockSpec((1, d), lambda m: (0, 0)))
        operands.append(nw.reshape(1, d))
    if tail == "next":
        in_specs.append(mod_spec)
        operands.append(modtab_next)
        out_shape, out_specs = (out_shape, jax.ShapeDtypeStruct((n, d), BF16)), (row_spec, row_spec)
    return pl.pallas_call(
        functools.partial(_moe_combine_kernel, g_row=g_row, tail=tail),
        out_shape=out_shape,
        grid=(ntiles,),
        in_specs=in_specs,
        out_specs=out_specs,
        scratch_shapes=[pltpu.VMEM((2, TOP_K, tm, d), F32), pltpu.SemaphoreType.DMA((2,))],
        compiler_params=_cp(("arbitrary",), VMEM_LIMIT),
        name="moe_combine",
    )(*operands)


def _moe_route(idx2):
    n = idx2.shape[0]
    flat_e = idx2.reshape(-1)
    onehot = (flat_e[:, None] == jnp.arange(N_EXPERTS, dtype=jnp.int32)[None, :]).astype(jnp.int32)
    csum = jnp.cumsum(onehot, axis=0)
    rank = jnp.sum((csum - onehot) * onehot, axis=1)
    counts = csum[-1]
    ntile = (counts + MOE_TILE - 1) // MOE_TILE
    tile_end = jnp.cumsum(ntile)
    tile_start = tile_end - ntile
    pos = tile_start[flat_e] * MOE_TILE + rank
    tiles = (TOP_K * n) // MOE_TILE + N_EXPERTS
    src = jnp.zeros((tiles * MOE_TILE,), jnp.int32).at[pos].set(jnp.arange(TOP_K * n, dtype=jnp.int32) // TOP_K)
    nv = tile_end[-1:]
    tile_ids = jnp.arange(tiles, dtype=jnp.int32)
    tids = jnp.minimum(tile_ids, nv[0] - 1)
    te = jnp.sum((tile_end[None, :] <= tids[:, None]).astype(jnp.int32), axis=1)
    first = jnp.logical_and(tile_ids < nv[0], jnp.logical_or(tile_ids == 0, te != jnp.roll(te, 1)))
    run_end = tile_end[te]
    nxt = jnp.where(run_end < nv[0], te[jnp.minimum(run_end, tiles - 1)], -1)
    route = (te, nv.astype(jnp.int32), first.astype(jnp.int32), nxt.astype(jnp.int32))
    return pos.astype(jnp.int32), src.reshape(tiles, 1, MOE_TILE), route


def _moe_ffn(h, modtab, nw, w_router, w13, w2, layer, nct, tail=None, nw_tail=None, modtab_next=None):
    bsz, length, d = h.shape
    n = bsz * length
    v, idx, gates = _normmod_router(h, nw, modtab, w_router, SH2, SC2, nct)
    idx2 = idx.reshape(n, ROUTER_LANES)[:, :TOP_K]
    pos, src, route = _moe_route(idx2)
    xs = _moe_gather(src, v.reshape(n, d))
    act = _moe_swiglu(route, xs, w13, layer, MOE_W13_COLS)
    ys = _moe_mm(route, act, w2, layer, MOE_W2_COLS)
    pos3 = pos.reshape(n // ROW_TILE, ROW_TILE, TOP_K).transpose(0, 2, 1)
    out = _moe_combine(pos3, gates.reshape(n, ROUTER_LANES), ys, h.reshape(n, d), modtab, G2,
                       length // ROW_TILE, nct, tail, nw_tail, modtab_next)
    if tail == "next":
        return out[0].reshape(h.shape), out[1]
    return out.reshape(h.shape)


def kernel(x, c, ctx, c_ctx, w_mod, b_mod, norm1, norm2, norm_f, gla_w_in, gla_w_gate_up, gla_b_gate,
           gla_g_head, gla_w_out, ffn_w13, ffn_w2, s5_a_re, s5_a_im, s5_log_dt, s5_b_re, s5_b_im, s5_c_re,
           s5_c_im, s5_d, s5_w_glu, s5_b_glu, moe_w_router, moe_w13, moe_w2):
    bsz, seq, d = x.shape
    nctx = ctx.shape[1]
    depth = w_mod.shape[0]
    assert bsz + 1 <= 8 and seq % ROW_TILE == 0 and nctx % ROW_TILE == 0 and seq % GRID_W == 0
    nct = nctx // ROW_TILE

    c8 = jnp.concatenate([c, c_ctx[None, :], jnp.zeros((8 - bsz - 1, d), F32)], axis=0)
    mods = _modulation(c8, w_mod, b_mod).reshape(depth, 8, 6, d)
    assert depth % 2 == 0
    modtabs = [jnp.stack([jnp.broadcast_to(mods[i, bsz], (bsz, 6, d)), mods[i, :bsz]], axis=1) for i in range(depth)]
    h = jnp.concatenate([ctx, x], axis=1)
    u = None
    for i in range(depth):
        last = i == depth - 1
        j = i // 2
        modtab = modtabs[i]
        if i % 2 == 0:
            h, v = _gla_layer(h, u, modtab, norm1[i], norm2[i], gla_w_in, j, gla_w_gate_up[j], gla_b_gate[j],
                              gla_g_head[j], gla_w_out, nctx)
            h = _dense_ffn(h, v, modtab, ffn_w13, ffn_w2, j, nct)
        else:
            h = _s5_layer(h, modtab, norm1[i], s5_a_re[j], s5_a_im[j], s5_log_dt[j], s5_b_re[j], s5_b_im[j],
                          s5_c_re[j], s5_c_im[j], s5_d[j], s5_w_glu, s5_b_glu, j, nctx, last)
            if last:
                h = _moe_ffn(h, modtab, norm2[i], moe_w_router[j], moe_w13, moe_w2, j, 0, "final", norm_f)
            else:
                h, u = _moe_ffn(h, modtab, norm2[i], moe_w_router[j], moe_w13, moe_w2, j, nct, "next",
                                norm1[i + 1], modtabs[i + 1])
    return h
```

```python
import functools

import jax
import jax.numpy as jnp
from jax import lax
from jax.experimental import pallas as pl
from jax.experimental.pallas import tpu as pltpu

F32 = jnp.float32
BF16 = jnp.bfloat16

EPS = 1e-6
GRID_W = 64

GLA_HEADS = 4
GLA_GATE_RANK = 16
GLA_GATE_TAU = 16.0
GLA_CHUNK = 64

S5_GROUP = 16
S5_CHUNK = 32
S5_GROUP_BLOCK = 4

N_EXPERTS = 8
TOP_K = 2
ROUTER_LANES = 128

ROW_TILE = 256
MOE_TILE = 256
DMA_ISSUE_UNROLL = 8
MOE_W13_COLS = 1408
MOE_W2_COLS = 1024
GLA_IN_COLS = 2048
VMEM_LIMIT = 56 * 1024 * 1024

SH1, SC1, G1, SH2, SC2, G2 = range(6)


def _cp(sem, vmem=None):
    return pltpu.CompilerParams(dimension_semantics=sem, vmem_limit_bytes=vmem)


def _dot(a, b):
    return jnp.dot(a, b, preferred_element_type=F32)


def _split(x):
    hi = x.astype(BF16)
    lo = (x - hi.astype(F32)).astype(BF16)
    return hi, lo


def _dot3(a, b):
    ah, al = _split(a)
    bh, bl = _split(b)
    return _dot(ah, bh) + _dot(ah, bl) + _dot(al, bh)


def _sigmoid(x):
    return 1.0 / (1.0 + jnp.exp(-x))


def _silu(x):
    return x * _sigmoid(x)


def _mod_kernel(c_ref, w_ref, b_ref, o_ref):
    x = _silu(c_ref[...]).astype(BF16)
    o_ref[...] = _dot(x, w_ref[...].astype(BF16)) + b_ref[...]


def _modulation(c8, w_mod, b_mod):
    depth, d, n = w_mod.shape
    tn = 1024
    return pl.pallas_call(
        _mod_kernel,
        out_shape=jax.ShapeDtypeStruct((depth, 8, n), F32),
        grid=(depth, n // tn),
        in_specs=[
            pl.BlockSpec((8, d), lambda i, j: (0, 0)),
            pl.BlockSpec((None, d, tn), lambda i, j: (i, 0, j)),
            pl.BlockSpec((None, 1, tn), lambda i, j: (i, 0, j)),
        ],
        out_specs=pl.BlockSpec((None, 8, tn), lambda i, j: (i, 0, j)),
        compiler_params=_cp(("arbitrary", "arbitrary"), VMEM_LIMIT),
        name="modulation",
    )(c8, w_mod, b_mod.reshape(depth, 1, n))


def _normmod_rows(x, nw, scale, shift):
    ms = jnp.mean(x * x, axis=-1, keepdims=True)
    return x * lax.rsqrt(ms + EPS) * nw * (1.0 + scale) + shift


def _normmod_body(x_ref, nw_ref, mod_ref, sh_row, sc_row):
    return _normmod_rows(x_ref[...], nw_ref[...], mod_ref[sc_row:sc_row + 1, :], mod_ref[sh_row:sh_row + 1, :])


def _normmod_kernel(x_ref, nw_ref, mod_ref, o_ref, *, sh_row, sc_row):
    o_ref[...] = _normmod_body(x_ref, nw_ref, mod_ref, sh_row, sc_row).astype(o_ref.dtype)


def _normmod_router_kernel(x_ref, nw_ref, mod_ref, wr_ref, o_ref, idx_ref, gate_ref, *, sh_row, sc_row):
    y = _normmod_body(x_ref, nw_ref, mod_ref, sh_row, sc_row)
    o_ref[...] = y
    logits = _dot3(y, wr_ref[...])
    lane = lax.broadcasted_iota(jnp.int32, logits.shape, 1)
    lane_f = lane.astype(F32)
    neg = jnp.float32(-jnp.inf)
    l1 = jnp.where(lane < N_EXPERTS, logits, neg)
    m1 = jnp.max(l1, axis=-1, keepdims=True)
    i1 = jnp.min(jnp.where(l1 == m1, lane_f, float(ROUTER_LANES)), axis=-1, keepdims=True)
    l2 = jnp.where(lane_f == i1, neg, l1)
    m2 = jnp.max(l2, axis=-1, keepdims=True)
    i2 = jnp.min(jnp.where(l2 == m2, lane_f, float(ROUTER_LANES)), axis=-1, keepdims=True)
    e = jnp.exp(m2 - m1)
    g1 = 1.0 / (1.0 + e)
    g2 = e * g1
    idx_ref[...] = jnp.where(lane == 0, i1, jnp.where(lane == 1, i2, 0.0)).astype(jnp.int32)
    gate_ref[...] = jnp.where(lane == 0, g1, jnp.where(lane == 1, g2, 0.0))


def _row_specs(d, nct):
    x_spec = pl.BlockSpec((None, ROW_TILE, d), lambda b, t: (b, t, 0))
    mod_spec = pl.BlockSpec((None, None, 6, d), lambda b, t: (b, jnp.where(t >= nct, 1, 0), 0, 0))
    return x_spec, mod_spec


def _normmod(h, nw, modtab, sh_row, sc_row, nct, out_dtype):
    bsz, length, d = h.shape
    x_spec, mod_spec = _row_specs(d, nct)
    return pl.pallas_call(
        functools.partial(_normmod_kernel, sh_row=sh_row, sc_row=sc_row),
        out_shape=jax.ShapeDtypeStruct(h.shape, out_dtype),
        grid=(bsz, length // ROW_TILE),
        in_specs=[x_spec, pl.BlockSpec((1, d), lambda b, t: (0, 0)), mod_spec],
        out_specs=x_spec,
        compiler_params=_cp(("arbitrary", "arbitrary")),
        name="normmod",
    )(h, nw.reshape(1, d), modtab)


def _normmod_router(h, nw, modtab, w_router, sh_row, sc_row, nct):
    bsz, length, d = h.shape
    x_spec, mod_spec = _row_specs(d, nct)
    wr = jnp.pad(w_router, ((0, 0), (0, ROUTER_LANES - w_router.shape[1])))
    lane_spec = pl.BlockSpec((None, ROW_TILE, ROUTER_LANES), lambda b, t: (b, t, 0))
    return pl.pallas_call(
        functools.partial(_normmod_router_kernel, sh_row=sh_row, sc_row=sc_row),
        out_shape=(
            jax.ShapeDtypeStruct(h.shape, F32),
            jax.ShapeDtypeStruct((bsz, length, ROUTER_LANES), jnp.int32),
            jax.ShapeDtypeStruct((bsz, length, ROUTER_LANES), F32),
        ),
        grid=(bsz, length // ROW_TILE),
        in_specs=[x_spec, pl.BlockSpec((1, d), lambda b, t: (0, 0)), mod_spec,
                  pl.BlockSpec((d, ROUTER_LANES), lambda b, t: (0, 0))],
        out_specs=(x_spec, lane_spec, lane_spec),
        compiler_params=_cp(("arbitrary", "arbitrary")),
        name="normmod_router",
    )(h, nw.reshape(1, d), modtab, wr)


def _gla_gate_kernel(x_ref, wgl_ref, wup_ref, b_ref, o_ref):
    wh, wl = _split(wgl_ref[...])
    x = x_ref[...]
    gl = _dot(x, wh) + _dot(x, wl)
    z = _dot3(gl, wup_ref[...]) + b_ref[...]
    log_sig = jnp.minimum(z, 0.0) - jnp.log(1.0 + jnp.exp(-jnp.abs(z)))
    o_ref[...] = log_sig * (1.0 / GLA_GATE_TAU)


def _gla_gate(u, w_gl, w_up_bd, b_gate):
    r, d = u.shape
    n = w_up_bd.shape[1]
    tm = 512
    return pl.pallas_call(
        _gla_gate_kernel,
        out_shape=jax.ShapeDtypeStruct((r, n), F32),
        grid=(r // tm,),
        in_specs=[pl.BlockSpec((tm, d), lambda m: (m, 0)),
                  pl.BlockSpec(w_gl.shape, lambda m: (0, 0)),
                  pl.BlockSpec(w_up_bd.shape, lambda m: (0, 0)),
                  pl.BlockSpec((1, n), lambda m: (0, 0))],
        out_specs=pl.BlockSpec((tm, n), lambda m: (m, 0)),
        compiler_params=_cp(("arbitrary",)),
        name="gla_gate",
    )(u, w_gl, w_up_bd, b_gate.reshape(1, n))


def _gla_scan_kernel(tri_ref, qf_ref, kf_ref, vf_ref, gf_ref, qb_ref, kb_ref, vb_ref, gb_ref, of_ref, ob_ref, s_ref,
                     *, hk, hv):
    @pl.when(pl.program_id(1) == 0)
    def _():
        s_ref[...] = jnp.zeros_like(s_ref)

    ones = jnp.ones((GLA_CHUNK, 128), BF16)
    scale = hk ** -0.5
    tn_dims = (((0,), (0,)), ((), ()))
    nt_dims = (((1,), (1,)), ((), ()))
    streams = ((qf_ref, kf_ref, vf_ref, gf_ref, of_ref), (qb_ref, kb_ref, vb_ref, gb_ref, ob_ref))
    chains = [(d, h) for d in range(2) for h in range(GLA_HEADS)]
    decays = []
    for d, h in chains:
        g = streams[d][3][:, h * hk:(h + 1) * hk]
        gh, gl = _split(g)
        tri = tri_ref[d]
        b = _dot(tri, gh) + _dot(tri, gl)
        tot = jnp.sum(g, axis=0, keepdims=True)
        tot_col = (lax.dot_general(gh, ones, tn_dims, preferred_element_type=F32)
                   + lax.dot_general(gl, ones, tn_dims, preferred_element_type=F32))
        decays.append((b, tot, tot_col))
    operands = []
    for (d, h), (b, tot, tot_col) in zip(chains, decays):
        q = streams[d][0][:, h * hk:(h + 1) * hk]
        k = streams[d][1][:, h * hk:(h + 1) * hk]
        qd = (q * scale * jnp.exp(b)).astype(BF16)
        kd = (k * jnp.exp(-b)).astype(BF16)
        kt = (k * jnp.exp(tot - b)).astype(BF16)
        scores = jnp.where(tri_ref[d] > 0, lax.dot_general(qd, kd, nt_dims, preferred_element_type=F32), 0.0)
        operands.append((qd, kt, scores.astype(BF16)))
    for (d, h), (qd, kt, scores) in zip(chains, operands):
        v = streams[d][2][:, h * hv:(h + 1) * hv].astype(BF16)
        streams[d][4][:, h * hv:(h + 1) * hv] = _dot(scores, v) + _dot(qd, s_ref[d, h].astype(BF16))
    for (d, h), (b, tot, tot_col), (qd, kt, scores) in zip(chains, decays, operands):
        v = streams[d][2][:, h * hv:(h + 1) * hv].astype(BF16)
        decay = jnp.concatenate([jnp.exp(tot_col)] * (hv // 128), axis=1)
        s_ref[d, h] = decay * s_ref[d, h] + lax.dot_general(kt, v, tn_dims, preferred_element_type=F32)


def _gla_scan(proj, log_a, nctx_chunks, dk, dv):
    bsz, length, _ = proj.shape
    nchunks = length // GLA_CHUNK
    hk, hv = dk // GLA_HEADS, dv // GLA_HEADS
    idx = jnp.arange(GLA_CHUNK)
    tri = jnp.stack([idx[:, None] >= idx[None, :], idx[:, None] <= idx[None, :]]).astype(BF16)

    def back(s):
        return jnp.where(s < nctx_chunks, nctx_chunks - 1 - s, nchunks - 1 - (s - nctx_chunks))

    def specs(chunk, direction):
        return [pl.BlockSpec((None, GLA_CHUNK, dk), lambda b, s: (b, chunk(s), 0)),
                pl.BlockSpec((None, GLA_CHUNK, dk), lambda b, s: (b, chunk(s), 1)),
                pl.BlockSpec((None, GLA_CHUNK, dv), lambda b, s: (b, chunk(s), (2 * dk) // dv)),
                pl.BlockSpec((None, GLA_CHUNK, dk), lambda b, s: (b, chunk(s), direction))]

    out = jax.ShapeDtypeStruct((bsz, length, dv), F32)
    return pl.pallas_call(
        functools.partial(_gla_scan_kernel, hk=hk, hv=hv),
        out_shape=(out, out),
        grid=(bsz, nchunks),
        in_specs=[pl.BlockSpec((2, GLA_CHUNK, GLA_CHUNK), lambda b, s: (0, 0, 0))]
        + specs(lambda s: s, 0) + specs(back, 1),
        out_specs=(pl.BlockSpec((None, GLA_CHUNK, dv), lambda b, s: (b, s, 0)),
                   pl.BlockSpec((None, GLA_CHUNK, dv), lambda b, s: (b, back(s), 0))),
        scratch_shapes=[pltpu.VMEM((2, GLA_HEADS, hk, hv), F32)],
        compiler_params=_cp(("arbitrary", "arbitrary"), VMEM_LIMIT),
        name="gla_scan",
    )(tri, proj, proj, proj, log_a, proj, proj, proj, log_a)


def _gla_finish_kernel(of_ref, ob_ref, r_ref, gh_ref, o_ref, *, hv):
    for h in range(GLA_HEADS):
        sl = slice(h * hv, (h + 1) * hv)
        o = of_ref[:, sl] + ob_ref[:, sl]
        ms = jnp.mean(o * o, axis=-1, keepdims=True)
        y = o * lax.rsqrt(ms + EPS) * gh_ref[...]
        o_ref[:, sl] = (y * _silu(r_ref[:, sl])).astype(o_ref.dtype)


def _gla_finish(o_fwd, o_bwd, proj, g_head, dk, dv):
    bsz, length, _ = o_fwd.shape
    hv = dv // GLA_HEADS
    tm = ROW_TILE
    o_spec = pl.BlockSpec((None, tm, dv), lambda b, t: (b, t, 0))
    return pl.pallas_call(
        functools.partial(_gla_finish_kernel, hv=hv),
        out_shape=jax.ShapeDtypeStruct((bsz, length, dv), BF16),
        grid=(bsz, length // tm),
        in_specs=[
            o_spec, o_spec,
            pl.BlockSpec((None, tm, dv), lambda b, t: (b, t, (2 * dk + dv) // dv)),
            pl.BlockSpec((1, hv), lambda b, t: (0, 0)),
        ],
        out_specs=o_spec,
        compiler_params=_cp(("arbitrary", "arbitrary")),
        name="gla_finish",
    )(o_fwd, o_bwd, proj, g_head.reshape(1, hv))


def _gla_layer(h, u, modtab, nw, nw_ffn, w_in, layer, w_gate_up, b_gate, g_head, w_out, nctx):
    bsz, length, d = h.shape
    dk = w_gate_up.shape[2]
    dv = w_out.shape[1]
    nmain = 2 * dk + 2 * dv
    tpb = length // ROW_TILE
    nct = nctx // ROW_TILE
    if u is None:
        u = _normmod(h, nw, modtab, SH1, SC1, nct, BF16).reshape(bsz * length, d)
    w_in_grouped = w_in.reshape(w_in.shape[0], 1, d, w_in.shape[2])
    proj = _moe_mm(_dense_route(bsz * length // MOE_TILE), u, w_in_grouped, layer, GLA_IN_COLS, nmain, "gla_in_proj")
    rank = GLA_GATE_RANK
    w_up_bd = jnp.zeros((2 * rank, 2 * dk), F32)
    w_up_bd = w_up_bd.at[:rank, :dk].set(w_gate_up[0]).at[rank:, dk:].set(w_gate_up[1])
    log_a = _gla_gate(u, w_in[layer, :, nmain:], w_up_bd, b_gate.reshape(2 * dk))
    proj3 = proj.reshape(bsz, length, nmain)
    o_fwd, o_bwd = _gla_scan(proj3, log_a.reshape(bsz, length, 2 * dk), nctx // GLA_CHUNK, dk, dv)
    gated = _gla_finish(o_fwd, o_bwd, proj3, g_head, dk, dv).reshape(bsz * length, dv)
    out, v = _dense_mm_res(gated, w_out, layer, h.reshape(bsz * length, d), modtab, G1, tpb, nct, d, "gla_out_proj",
                           next_norm=(nw_ffn, SH2, SC2))
    return out.reshape(bsz, length, d), v


S5_LANES_PER_BATCH = 96
_AN1, _AN2, _AP1, _AP2, _BB1, _BB2, _CC1, _CC2, _PQ, _PK_ROWS = 0, 32, 64, 96, 128, 144, 160, 176, 192, 200


def _s5_params(a_re, a_im, log_dt, b_re, b_im, c_re, c_im):
    t_len = S5_CHUNK
    dt = jnp.exp(log_dt)[..., None]
    mag = jnp.exp(a_re * dt)
    ab_re = mag * jnp.cos(a_im * dt)
    ab_im = mag * jnp.sin(a_im * dt)
    den = a_re * a_re + a_im * a_im
    f_re = ((ab_re - 1.0) * a_re + ab_im * a_im) / den
    f_im = (ab_im * a_re - (ab_re - 1.0) * a_im) / den
    bb_re = f_re[..., None] * b_re - f_im[..., None] * b_im
    bb_im = f_re[..., None] * b_im + f_im[..., None] * b_re

    def cpow(n):
        nn = n.astype(F32)[:, None, :, None]
        m = jnp.exp((a_re * dt)[:, :, None, :] * nn)
        th = (a_im * dt)[:, :, None, :] * nn
        return m * jnp.cos(th), m * jnp.sin(th)

    steps = jnp.arange(t_len)
    order = jnp.stack([steps, t_len - 1 - steps])
    apr, api = cpow(order)
    inv_sq = jnp.exp(-2.0 * (a_re * dt)[:, :, None, :] * order.astype(F32)[:, None, :, None])
    anr, ani = apr * inv_sq, -api * inv_sq
    atr, ati = cpow(jnp.full((2, 1), t_len))
    cat = lambda u, v: jnp.concatenate([u, v], axis=-1)
    bbr_t, bbi_t = bb_re.transpose(0, 1, 3, 2), bb_im.transpose(0, 1, 3, 2)
    zero = jnp.zeros((2, a_re.shape[1], 5, 2 * a_re.shape[2]), F32)
    return jnp.concatenate([
        cat(anr, anr), cat(-ani, ani),
        cat(apr, api), cat(api, apr),
        cat(bbr_t, bbi_t), cat(bbi_t, bbr_t),
        cat(c_re, -c_re), cat(-c_im, -c_im),
        cat(atr, atr), cat(-ati, ati), cat(ati, -ati), zero], axis=2)


def _s5_in_kernel(hl_ref, hc_ref, nw_ref, mod_ref, o_ref, *, bsz, qb):
    nw = nw_ref[...]
    blocks = []
    for b in range(bsz):
        lat = _normmod_rows(hl_ref[b], nw, mod_ref[b, 1, SC1:SC1 + 1, :], mod_ref[b, 1, SH1:SH1 + 1, :])
        ctx = _normmod_rows(hc_ref[b], nw, mod_ref[b, 0, SC1:SC1 + 1, :], mod_ref[b, 0, SH1:SH1 + 1, :])
        pad = jnp.zeros((qb - lat.shape[0] - ctx.shape[0], lat.shape[1]), F32)
        blocks += [lat, ctx, pad]
    o_ref[...] = jnp.concatenate(blocks, axis=0).T.astype(o_ref.dtype)


def _s5_in(h, nw, modtab, nctx):
    bsz, length, d = h.shape
    t_len = S5_CHUNK
    qb = S5_LANES_PER_BATCH
    nctx_chunks = nctx // t_len
    return pl.pallas_call(
        functools.partial(_s5_in_kernel, bsz=bsz, qb=qb),
        out_shape=jax.ShapeDtypeStruct((d, t_len * bsz * qb), BF16),
        grid=(t_len,),
        in_specs=[
            pl.BlockSpec((bsz, None, GRID_W, d), lambda s: (0, nctx // GRID_W + s, 0, 0)),
            pl.BlockSpec((bsz, nctx_chunks, d), lambda s: (0, 0, s)),
            pl.BlockSpec((1, d), lambda s: (0, 0)),
            pl.BlockSpec(modtab.shape, lambda s: (0, 0, 0, 0)),
        ],
        out_specs=pl.BlockSpec((d, bsz * qb), lambda s: (0, s)),
        compiler_params=_cp(("arbitrary",), VMEM_LIMIT),
        name="s5_in",
    )(h.reshape(bsz, length // GRID_W, GRID_W, d), h[:, :nctx].reshape(bsz, nctx_chunks, t_len * d),
      nw.reshape(1, d), modtab)


def _tile_rows(x, reps):
    return jnp.concatenate([x] * reps, axis=0)


def _repeat_rows(x, reps):
    return jnp.concatenate([jnp.broadcast_to(x[i:i + 1, :], (reps, x.shape[1])) for i in range(x.shape[0])],
                           axis=0)


def _s5_kernel(ut_ref, pk_ref, y_ref, sv_ref, ss_ref, w_ref, r2_ref, acc_ref, *, bsz, qb, orders):
    gb = ut_ref.shape[0] // S5_GROUP
    t_len = S5_CHUNK
    k = S5_GROUP * t_len
    q = ut_ref.shape[1] // t_len
    row_s = lax.broadcasted_iota(jnp.int32, (k, k), 0) // S5_GROUP
    col_t = lax.broadcasted_iota(jnp.int32, (k, k), 1) // S5_GROUP
    masks = (row_s <= col_t, row_s >= col_t)
    w_ref[...] = jnp.zeros_like(w_ref)
    us = []
    for g in range(gb):
        x = ut_ref[g * S5_GROUP:(g + 1) * S5_GROUP, :]
        us.append(jnp.concatenate([x[:, s * q:(s + 1) * q] for s in range(t_len)], axis=0).T)
    pairs = [(g, d) for g in range(gb) for d in range(2)]
    l2s, raw = [], []
    for g, d in pairs:
        pk = pk_ref[d, g]
        l2 = (_repeat_rows(pk[_AN1:_AN1 + t_len], S5_GROUP) * _tile_rows(pk[_BB1:_BB1 + S5_GROUP], t_len)
              + _repeat_rows(pk[_AN2:_AN2 + t_len], S5_GROUP) * _tile_rows(pk[_BB2:_BB2 + S5_GROUP], t_len))
        r2t = (_repeat_rows(pk[_AP1:_AP1 + t_len], S5_GROUP) * _tile_rows(pk[_CC1:_CC1 + S5_GROUP], t_len)
               + _repeat_rows(pk[_AP2:_AP2 + t_len], S5_GROUP) * _tile_rows(pk[_CC2:_CC2 + S5_GROUP], t_len))
        r2 = r2t.T
        r2_ref[2 * g + d] = r2.astype(BF16)
        l2s.append(l2.astype(BF16))
        raw.append(_dot3(l2, r2))
    for (g, d), l2, toep_raw in zip(pairs, l2s, raw):
        i = 2 * g + d
        y_local = _dot(us[g], jnp.where(masks[d], toep_raw, 0.0).astype(BF16))
        if d == 0:
            acc_ref[g] = y_local
        else:
            acc_ref[g] = acc_ref[g] + y_local
        v = _dot(us[g], l2)
        vs = pltpu.roll(v, v.shape[1] // 2, axis=1)
        pk = pk_ref[d, g]
        pt, qt, qts = pk[_PQ:_PQ + 1], pk[_PQ + 1:_PQ + 2], pk[_PQ + 2:_PQ + 3]
        sv_ref[i] = pt * v + qt * vs
        ss_ref[i] = pt * vs + qts * v
    coef = []
    for g in range(gb):
        for d in range(2):
            pk = pk_ref[d, g]
            coef.append(tuple(jnp.broadcast_to(pk[_PQ + r:_PQ + r + 1], (bsz, pk.shape[1])) for r in range(3)))
    zero = jnp.zeros((bsz, pk_ref.shape[3]), F32)
    state = [(zero, zero)] * (2 * gb)
    nsteps = len(orders[0])
    for step in range(nsteps):
        for i in range(2 * gb):
            rows = pl.ds(orders[i % 2][step], bsz, stride=qb)
            w, ws = state[i]
            w_ref[i, rows, :] = w
            if step + 1 < nsteps:
                pt, qt, qts = coef[i]
                state[i] = (pt * w + qt * ws + sv_ref[i, rows, :], pt * ws + qts * w + ss_ref[i, rows, :])
    for g in range(gb):
        y = acc_ref[g]
        for d in range(2):
            i = 2 * g + d
            y = y + _dot(w_ref[i].astype(BF16), r2_ref[i])
        yt = y.T
        for t in range(t_len):
            y_ref[g * S5_GROUP:(g + 1) * S5_GROUP, t * q:(t + 1) * q] = yt[t * S5_GROUP:(t + 1) * S5_GROUP, :]


def _s5_scan(ut, pk, bsz, nctx_chunks, nlat_chunks):
    t_len = S5_CHUNK
    q = ut.shape[1] // t_len
    ngroups = ut.shape[0] // S5_GROUP
    np2 = pk.shape[3]
    gb = S5_GROUP_BLOCK
    qb = S5_LANES_PER_BATCH
    lat = list(range(nlat_chunks))
    ctx = list(range(nlat_chunks, nlat_chunks + nctx_chunks))
    orders = (ctx + lat, ctx[::-1] + lat[::-1])
    blk = (gb * S5_GROUP, t_len * q)
    return pl.pallas_call(
        functools.partial(_s5_kernel, bsz=bsz, qb=qb, orders=orders),
        out_shape=jax.ShapeDtypeStruct(ut.shape, F32),
        grid=(ngroups // gb,),
        in_specs=[pl.BlockSpec(blk, lambda g: (g, 0)),
                  pl.BlockSpec((2, gb, _PK_ROWS, np2), lambda g: (0, g, 0, 0))],
        out_specs=pl.BlockSpec(blk, lambda g: (g, 0)),
        scratch_shapes=[pltpu.VMEM((2 * gb, q, np2), F32), pltpu.VMEM((2 * gb, q, np2), F32),
                        pltpu.VMEM((2 * gb, q, np2), F32), pltpu.VMEM((2 * gb, np2, S5_GROUP * t_len), BF16),
                        pltpu.VMEM((gb, q, S5_GROUP * t_len), F32)],
        compiler_params=_cp(("arbitrary",), VMEM_LIMIT),
        name="s5_scan",
    )(ut, pk)


def _s5_out_kernel(yt_ref, hl_ref, hc_ref, nw_ref, mod_ref, d_ref, ol_ref, oc_ref, *, bsz, qb):
    y = yt_ref[...].T
    nw = nw_ref[...]
    nlat, nc = hl_ref.shape[1], hc_ref.shape[1]
    c0 = 0.7978845608028654

    def act(yv, hv, region, b):
        u = _normmod_rows(hv, nw, mod_ref[b, region, SC1:SC1 + 1, :], mod_ref[b, region, SH1:SH1 + 1, :])
        z = yv + u * d_ref[...]
        return (0.5 * z * (1.0 + jnp.tanh(c0 * (z + 0.044715 * (z * z * z))))).astype(ol_ref.dtype)

    for b in range(bsz):
        ol_ref[b] = act(y[b * qb:b * qb + nlat], hl_ref[b], 1, b)
        oc_ref[b] = act(y[b * qb + nlat:b * qb + nlat + nc], hc_ref[b], 0, b)


def _s5_out(yt, h, nw, modtab, d_skip, nctx):
    bsz, length, d = h.shape
    t_len = S5_CHUNK
    qb = S5_LANES_PER_BATCH
    nctx_chunks = nctx // t_len
    seq = length - nctx
    lat_spec = pl.BlockSpec((bsz, None, GRID_W, d), lambda s: (0, nctx // GRID_W + s, 0, 0))
    ctx_spec = pl.BlockSpec((bsz, nctx_chunks, d), lambda s: (0, 0, s))
    act_l, act_c = pl.pallas_call(
        functools.partial(_s5_out_kernel, bsz=bsz, qb=qb),
        out_shape=(jax.ShapeDtypeStruct((bsz, seq // GRID_W, GRID_W, d), BF16),
                   jax.ShapeDtypeStruct((bsz, nctx_chunks, t_len * d), BF16)),
        grid=(t_len,),
        in_specs=[pl.BlockSpec((d, bsz * qb), lambda s: (0, s)),
                  lat_spec, ctx_spec,
                  pl.BlockSpec((1, d), lambda s: (0, 0)),
                  pl.BlockSpec(modtab.shape, lambda s: (0, 0, 0, 0)),
                  pl.BlockSpec((1, d), lambda s: (0, 0))],
        out_specs=(pl.BlockSpec((bsz, None, GRID_W, d), lambda s: (0, s, 0, 0)), ctx_spec),
        compiler_params=_cp(("arbitrary",), VMEM_LIMIT),
        name="s5_out",
    )(yt, h.reshape(bsz, length // GRID_W, GRID_W, d), h[:, :nctx].reshape(bsz, nctx_chunks, t_len * d),
      nw.reshape(1, d), modtab, d_skip.reshape(1, d))
    return act_l.reshape(bsz, seq, d), act_c.reshape(bsz, nctx, d)


def _s5_layer(h, modtab, nw, a_re, a_im, log_dt, b_re, b_im, c_re, c_im, d_skip, w_glu, b_glu, layer, nctx,
              last):
    bsz, length, d = h.shape
    seq = length - nctx
    t_len = S5_CHUNK
    assert seq // GRID_W == t_len and nctx % t_len == 0 and nctx % GRID_W == 0
    assert seq // t_len + nctx // t_len <= S5_LANES_PER_BATCH
    ngroups = d // S5_GROUP
    ut = _s5_in(h, nw, modtab, nctx)
    pk = _s5_params(a_re, a_im, log_dt, b_re, b_im, c_re, c_im)
    yt = _s5_scan(ut, pk, bsz, nctx // t_len, GRID_W)
    act_l, act_c = _s5_out(yt, h, nw, modtab, d_skip, nctx)
    if last:
        act, h, nct = act_l, h[:, nctx:], 0
    else:
        act, nct = jnp.concatenate([act_c, act_l], axis=1), nctx // ROW_TILE
    rows = h.shape[0] * h.shape[1]
    out = _dense_glu_res(act.reshape(rows, d), w_glu, b_glu, layer, h.reshape(rows, d), modtab, G1,
                         h.shape[1] // ROW_TILE, nct, 1024, "s5_glu")
    return out.reshape(h.shape)


def _dense_ffn(h, v, modtab, w13, w2, layer, nct):
    bsz, length, d = h.shape
    rows = bsz * length
    w13_grouped = w13.reshape(w13.shape[0], 1, w13.shape[1], w13.shape[2])
    act = _moe_swiglu(_dense_route(rows // MOE_TILE), v, w13_grouped, layer, MOE_W13_COLS, "ffn_w13")
    out = _dense_mm_res(act, w2, layer, h.reshape(rows, d), modtab, G2, length // ROW_TILE, nct, MOE_W2_COLS,
                        "ffn_w2")
    return out.reshape(h.shape)


def _moe_gather_kernel(src_ref, nxt_ref, v_hbm, o_ref, buf_ref, sem):
    t = pl.program_id(0)
    slot = t % 2

    def request(idx_ref, s):
        def issue(r8, carry):
            for i in range(DMA_ISSUE_UNROLL):
                r = r8 * DMA_ISSUE_UNROLL + i
                pltpu.make_async_copy(v_hbm.at[pl.ds(idx_ref[0, r], 1)], buf_ref.at[s, pl.ds(r, 1)],
                                      sem.at[s]).start(priority=i % 2)
            return carry
        lax.fori_loop(0, MOE_TILE // DMA_ISSUE_UNROLL, issue, 0)

    @pl.when(t == 0)
    def _():
        request(src_ref, 0)

    @pl.when(t + 1 < pl.num_programs(0))
    def _():
        request(nxt_ref, 1 - slot)

    pltpu.make_async_copy(v_hbm.at[pl.ds(0, MOE_TILE)], buf_ref.at[slot], sem.at[slot]).wait()
    o_ref[...] = buf_ref[slot].astype(o_ref.dtype)


def _moe_gather(src, v):
    ntiles = src.shape[0]
    d = v.shape[1]
    idx_block = (None, 1, MOE_TILE)
    return pl.pallas_call(
        _moe_gather_kernel,
        out_shape=jax.ShapeDtypeStruct((ntiles * MOE_TILE, d), BF16),
        grid=(ntiles,),
        in_specs=[pl.BlockSpec(idx_block, lambda t: (t, 0, 0), memory_space=pltpu.SMEM),
                  pl.BlockSpec(idx_block, lambda t: (jnp.minimum(t + 1, ntiles - 1), 0, 0),
                               memory_space=pltpu.SMEM),
                  pl.BlockSpec(memory_space=pl.ANY)],
        out_specs=pl.BlockSpec((MOE_TILE, d), lambda t: (t, 0)),
        scratch_shapes=[pltpu.VMEM((2, MOE_TILE, d), F32), pltpu.SemaphoreType.DMA((2,))],
        compiler_params=_cp(("arbitrary",)),
        name="moe_gather",
    )(src, src, v)


def _moe_fetch_weights(te_ref, first_ref, nxt_ref, w_hbm, stage_ref, sem, wbf_ref, col_blocks, layer):
    j, t = pl.program_id(0), pl.program_id(1)
    nsweeps = pl.num_programs(0)
    tn = stage_ref.shape[2]

    def copies(e, sweep):
        return [pltpu.make_async_copy(w_hbm.at[layer, e, :, pl.ds(pl.multiple_of(cb(sweep) * tn, 128), tn)],
                                      stage_ref.at[i], sem.at[i])
                for i, cb in enumerate(col_blocks)]

    @pl.when(jnp.logical_and(j == 0, t == 0))
    def _():
        for c in copies(te_ref[0], 0):
            c.start()

    @pl.when(first_ref[t] == 1)
    def _():
        for c in copies(te_ref[t], j):
            c.wait()
        for i in range(len(col_blocks)):
            wbf_ref[:, i * tn:(i + 1) * tn] = stage_ref[i].astype(BF16)
        sweep_done = nxt_ref[t] < 0
        next_e = jnp.where(sweep_done, te_ref[0], nxt_ref[t])
        next_sweep = jnp.where(sweep_done, j + 1, j)

        @pl.when(next_sweep < nsweeps)
        def _():
            for c in copies(next_e, next_sweep):
                c.start()


def _moe_swiglu_kernel(te_ref, nv_ref, first_ref, nxt_ref, x_ref, w_hbm, o_ref, wbf_ref, stage_ref, sem, *, layer):
    nb = pl.num_programs(0)
    _moe_fetch_weights(te_ref, first_ref, nxt_ref, w_hbm, stage_ref, sem, wbf_ref,
                       [lambda sweep: sweep, lambda sweep: nb + sweep], layer)

    in_use = pl.program_id(1) < nv_ref[0]

    @pl.when(in_use)
    def _():
        tn = o_ref.shape[1]
        z = _dot(x_ref[...], wbf_ref[...])
        o_ref[...] = (_silu(z[:, :tn]) * z[:, tn:]).astype(o_ref.dtype)

    @pl.when(jnp.logical_not(in_use))
    def _():
        o_ref[...] = jnp.zeros_like(o_ref)


def _moe_mm_kernel(te_ref, nv_ref, first_ref, nxt_ref, x_ref, w_hbm, o_ref, wbf_ref, stage_ref, sem, *, layer):
    _moe_fetch_weights(te_ref, first_ref, nxt_ref, w_hbm, stage_ref, sem, wbf_ref, [lambda sweep: sweep], layer)

    in_use = pl.program_id(1) < nv_ref[0]

    @pl.when(in_use)
    def _():
        o_ref[...] = _dot(x_ref[...], wbf_ref[...])

    @pl.when(jnp.logical_not(in_use))
    def _():
        o_ref[...] = jnp.zeros_like(o_ref)


def _moe_swiglu(route, xs, w13, layer, tn, name="moe_w13"):
    rows, k = xs.shape
    half = w13.shape[3] // 2
    nb = half // tn
    tm = MOE_TILE
    return pl.pallas_call(
        functools.partial(_moe_swiglu_kernel, layer=layer),
        out_shape=jax.ShapeDtypeStruct((rows, half), BF16),
        grid_spec=pltpu.PrefetchScalarGridSpec(
            num_scalar_prefetch=4,
            grid=(nb, rows // tm),
            in_specs=[pl.BlockSpec((tm, k), lambda j, t, *_: (t, 0)),
                      pl.BlockSpec(memory_space=pl.ANY)],
            out_specs=pl.BlockSpec((tm, tn), lambda j, t, *_: (t, j)),
            scratch_shapes=[pltpu.VMEM((k, 2 * tn), BF16), pltpu.VMEM((2, k, tn), F32),
                            pltpu.SemaphoreType.DMA((2,))]),
        compiler_params=_cp(("arbitrary", "arbitrary"), VMEM_LIMIT),
        name=name,
    )(*route, xs, w13)


def _moe_mm(route, xs, w2, layer, tn, n=None, name="moe_w2"):
    rows, k = xs.shape
    n = w2.shape[3] if n is None else n
    tm = MOE_TILE
    return pl.pallas_call(
        functools.partial(_moe_mm_kernel, layer=layer),
        out_shape=jax.ShapeDtypeStruct((rows, n), F32),
        grid_spec=pltpu.PrefetchScalarGridSpec(
            num_scalar_prefetch=4,
            grid=(n // tn, rows // tm),
            in_specs=[pl.BlockSpec((tm, k), lambda j, t, *_: (t, 0)),
                      pl.BlockSpec(memory_space=pl.ANY)],
            out_specs=pl.BlockSpec((tm, tn), lambda j, t, *_: (t, j)),
            scratch_shapes=[pltpu.VMEM((k, tn), BF16), pltpu.VMEM((1, k, tn), F32),
                            pltpu.SemaphoreType.DMA((1,))]),
        compiler_params=_cp(("arbitrary", "arbitrary"), VMEM_LIMIT),
        name=name,
    )(*route, xs, w2)


def _dense_route(ntiles):
    zeros = jnp.zeros((ntiles,), jnp.int32)
    return (zeros, jnp.full((1,), ntiles, jnp.int32), zeros.at[0].set(1), zeros - 1)


def _dense_mm_res_kernel(te_ref, nv_ref, first_ref, nxt_ref, x_ref, w_hbm, res_ref, mod_ref, o_ref, wbf_ref,
                         stage_ref, sem, *, layer, g_row):
    _moe_fetch_weights(te_ref, first_ref, nxt_ref, w_hbm, stage_ref, sem, wbf_ref, [lambda sweep: sweep], layer)
    o_ref[...] = res_ref[...] + mod_ref[g_row:g_row + 1, :] * _dot(x_ref[...], wbf_ref[...])


def _dense_mm_res_norm_kernel(te_ref, nv_ref, first_ref, nxt_ref, x_ref, w_hbm, res_ref, mod_ref, nw_ref, o_ref,
                              v_ref, wbf_ref, stage_ref, sem, *, layer, g_row, sh_row, sc_row):
    _moe_fetch_weights(te_ref, first_ref, nxt_ref, w_hbm, stage_ref, sem, wbf_ref, [lambda sweep: sweep], layer)
    h_new = res_ref[...] + mod_ref[g_row:g_row + 1, :] * _dot(x_ref[...], wbf_ref[...])
    o_ref[...] = h_new
    v_ref[...] = _normmod_rows(h_new, nw_ref[...], mod_ref[sc_row:sc_row + 1, :],
                               mod_ref[sh_row:sh_row + 1, :]).astype(v_ref.dtype)


def _dense_glu_res_kernel(te_ref, nv_ref, first_ref, nxt_ref, x_ref, w_hbm, ba_ref, bb_ref, res_ref, mod_ref, o_ref,
                          wbf_ref, stage_ref, sem, *, layer, g_row):
    nb = pl.num_programs(0)
    _moe_fetch_weights(te_ref, first_ref, nxt_ref, w_hbm, stage_ref, sem, wbf_ref,
                       [lambda sweep: sweep, lambda sweep: nb + sweep], layer)
    tn = o_ref.shape[1]
    z = _dot(x_ref[...], wbf_ref[...])
    z1 = z[:, :tn] + ba_ref[...]
    z2 = z[:, tn:] + bb_ref[...]
    o_ref[...] = res_ref[...] + mod_ref[g_row:g_row + 1, :] * (z1 * _sigmoid(z2))


def _dense_res_specs(tn, tpb, nct):
    res_spec = pl.BlockSpec((ROW_TILE, tn), lambda j, t, *_: (t, j))
    mod_spec = pl.BlockSpec((None, None, 6, tn),
                            lambda j, t, *_: (t // tpb, jnp.where(t % tpb >= nct, 1, 0), 0, j))
    return res_spec, mod_spec


def _dense_mm_res(x, w, layer, res, modtab, g_row, tpb, nct, tn, name, next_norm=None):
    r, k = x.shape
    n = w.shape[2]
    tm = ROW_TILE
    res_spec, mod_spec = _dense_res_specs(tn, tpb, nct)
    out_spec = pl.BlockSpec((tm, tn), lambda j, t, *_: (t, j))
    in_specs = [pl.BlockSpec((tm, k), lambda j, t, *_: (t, 0)), pl.BlockSpec(memory_space=pl.ANY), res_spec, mod_spec]
    operands = [x, w.reshape(w.shape[0], 1, k, n), res, modtab]
    if next_norm is None:
        body = functools.partial(_dense_mm_res_kernel, layer=layer, g_row=g_row)
        out_shape, out_specs = jax.ShapeDtypeStruct((r, n), F32), out_spec
    else:
        assert tn == n
        nw, sh_row, sc_row = next_norm
        body = functools.partial(_dense_mm_res_norm_kernel, layer=layer, g_row=g_row, sh_row=sh_row, sc_row=sc_row)
        in_specs.append(pl.BlockSpec((1, n), lambda j, t, *_: (0, 0)))
        operands.append(nw.reshape(1, n))
        out_shape = (jax.ShapeDtypeStruct((r, n), F32), jax.ShapeDtypeStruct((r, n), BF16))
        out_specs = (out_spec, out_spec)
    return pl.pallas_call(
        body,
        out_shape=out_shape,
        grid_spec=pltpu.PrefetchScalarGridSpec(
            num_scalar_prefetch=4,
            grid=(n // tn, r // tm),
            in_specs=in_specs,
            out_specs=out_specs,
            scratch_shapes=[pltpu.VMEM((k, tn), BF16), pltpu.VMEM((1, k, tn), F32),
                            pltpu.SemaphoreType.DMA((1,))]),
        compiler_params=_cp(("arbitrary", "arbitrary"), VMEM_LIMIT),
        name=name,
    )(*_dense_route(r // tm), *operands)


def _dense_glu_res(x, w, bias, layer, res, modtab, g_row, tpb, nct, tn, name):
    r, k = x.shape
    half = w.shape[2] // 2
    nb = half // tn
    tm = ROW_TILE
    res_spec, mod_spec = _dense_res_specs(tn, tpb, nct)
    bias3 = bias.reshape(bias.shape[0], 1, 2 * half)
    return pl.pallas_call(
        functools.partial(_dense_glu_res_kernel, layer=layer, g_row=g_row),
        out_shape=jax.ShapeDtypeStruct((r, half), F32),
        grid_spec=pltpu.PrefetchScalarGridSpec(
            num_scalar_prefetch=4,
            grid=(nb, r // tm),
            in_specs=[pl.BlockSpec((tm, k), lambda j, t, *_: (t, 0)),
                      pl.BlockSpec(memory_space=pl.ANY),
                      pl.BlockSpec((None, 1, tn), lambda j, t, *_: (layer, 0, j)),
                      pl.BlockSpec((None, 1, tn), lambda j, t, *_: (layer, 0, nb + j)),
                      res_spec, mod_spec],
            out_specs=pl.BlockSpec((tm, tn), lambda j, t, *_: (t, j)),
            scratch_shapes=[pltpu.VMEM((k, 2 * tn), BF16), pltpu.VMEM((2, k, tn), F32),
                            pltpu.SemaphoreType.DMA((2,))]),
        compiler_params=_cp(("arbitrary", "arbitrary"), VMEM_LIMIT),
        name=name,
    )(*_dense_route(r // tm), x, w.reshape(w.shape[0], 1, k, 2 * half), bias3, bias3, res, modtab)


def _moe_combine_kernel(pos_ref, nxt_ref, gate_ref, y_hbm, res_ref, mod_ref, *rest, g_row, tail):
    if tail == "final":
        nw_ref, o_ref, buf_ref, sem = rest
    elif tail == "next":
        nw_ref, mod2_ref, o_ref, u_ref, buf_ref, sem = rest
    else:
        o_ref, buf_ref, sem = rest
    m = pl.program_id(0)
    slot = m % 2

    def request(idx_ref, s):
        for k in range(TOP_K):
            def issue(r8, carry, k=k):
                for i in range(DMA_ISSUE_UNROLL):
                    r = r8 * DMA_ISSUE_UNROLL + i
                    pltpu.make_async_copy(y_hbm.at[pl.ds(idx_ref[k, r], 1)], buf_ref.at[s, k, pl.ds(r, 1)],
                                          sem.at[s]).start(priority=i % 2)
                return carry
            lax.fori_loop(0, ROW_TILE // DMA_ISSUE_UNROLL, issue, 0)

    @pl.when(m == 0)
    def _():
        request(pos_ref, 0)

    @pl.when(m + 1 < pl.num_programs(0))
    def _():
        request(nxt_ref, 1 - slot)

    for k in range(TOP_K):
        pltpu.make_async_copy(y_hbm.at[pl.ds(0, ROW_TILE)], buf_ref.at[slot, k], sem.at[slot]).wait()
    p = gate_ref[...]
    mix = p[:, 0:1] * buf_ref[slot, 0] + p[:, 1:2] * buf_ref[slot, 1]
    h_new = res_ref[...] + mod_ref[g_row:g_row + 1, :] * mix
    if tail == "final":
        ms = jnp.mean(h_new * h_new, axis=-1, keepdims=True)
        o_ref[...] = h_new * lax.rsqrt(ms + EPS) * nw_ref[...]
    else:
        o_ref[...] = h_new
    if tail == "next":
        u_ref[...] = _normmod_rows(h_new, nw_ref[...], mod2_ref[SC1:SC1 + 1, :],
                                   mod2_ref[SH1:SH1 + 1, :]).astype(u_ref.dtype)


def _moe_combine(pos, gates, ys, res, modtab, g_row, tpb, nct, tail=None, nw=None, modtab_next=None):
    n, d = res.shape
    tm = ROW_TILE
    ntiles = n // tm
    row_spec = pl.BlockSpec((tm, d), lambda m: (m, 0))
    mod_spec = pl.BlockSpec((None, None, 6, d), lambda m: (m // tpb, jnp.where(m % tpb >= nct, 1, 0), 0, 0))
    in_specs = [pl.BlockSpec((None, TOP_K, tm), lambda m: (m, 0, 0), memory_space=pltpu.SMEM),
                pl.BlockSpec((None, TOP_K, tm), lambda m: (jnp.minimum(m + 1, ntiles - 1), 0, 0),
                             memory_space=pltpu.SMEM),
                pl.BlockSpec((tm, ROUTER_LANES), lambda m: (m, 0)),
                pl.BlockSpec(memory_space=pl.ANY), row_spec, mod_spec]
    operands = [pos, pos, gates, ys, res, modtab]
    out_shape, out_specs = jax.ShapeDtypeStruct((n, d), F32), row_spec
    if tail is not None:
        in_specs.append(pl.BlockSpec((1, d), lambda m: (0, 0)))
        operands.append(nw.reshape(1, d))
    if tail == "next":
        in_specs.append(mod_spec)
        operands.append(modtab_next)
        out_shape, out_specs = (out_shape, jax.ShapeDtypeStruct((n, d), BF16)), (row_spec, row_spec)
    return pl.pallas_call(
        functools.partial(_moe_combine_kernel, g_row=g_row, tail=tail),
        out_shape=out_shape,
        grid=(ntiles,),
        in_specs=in_specs,
        out_specs=out_specs,
        scratch_shapes=[pltpu.VMEM((2, TOP_K, tm, d), F32), pltpu.SemaphoreType.DMA((2,))],
        compiler_params=_cp(("arbitrary",), VMEM_LIMIT),
        name="moe_combine",
    )(*operands)


def _moe_route(idx2):
    n = idx2.shape[0]
    flat_e = idx2.reshape(-1)
    onehot = (flat_e[:, None] == jnp.arange(N_EXPERTS, dtype=jnp.int32)[None, :]).astype(jnp.int32)
    csum = jnp.cumsum(onehot, axis=0)
    rank = jnp.sum((csum - onehot) * onehot, axis=1)
    counts = csum[-1]
    ntile = (counts + MOE_TILE - 1) // MOE_TILE
    tile_end = jnp.cumsum(ntile)
    tile_start = tile_end - ntile
    pos = tile_start[flat_e] * MOE_TILE + rank
    tiles = (TOP_K * n) // MOE_TILE + N_EXPERTS
    src = jnp.zeros((tiles * MOE_TILE,), jnp.int32).at[pos].set(jnp.arange(TOP_K * n, dtype=jnp.int32) // TOP_K)
    nv = tile_end[-1:]
    tile_ids = jnp.arange(tiles, dtype=jnp.int32)
    tids = jnp.minimum(tile_ids, nv[0] - 1)
    te = jnp.sum((tile_end[None, :] <= tids[:, None]).astype(jnp.int32), axis=1)
    first = jnp.logical_and(tile_ids < nv[0], jnp.logical_or(tile_ids == 0, te != jnp.roll(te, 1)))
    run_end = tile_end[te]
    nxt = jnp.where(run_end < nv[0], te[jnp.minimum(run_end, tiles - 1)], -1)
    route = (te, nv.astype(jnp.int32), first.astype(jnp.int32), nxt.astype(jnp.int32))
    return pos.astype(jnp.int32), src.reshape(tiles, 1, MOE_TILE), route


def _moe_ffn(h, modtab, nw, w_router, w13, w2, layer, nct, tail=None, nw_tail=None, modtab_next=None):
    bsz, length, d = h.shape
    n = bsz * length
    v, idx, gates = _normmod_router(h, nw, modtab, w_router, SH2, SC2, nct)
    idx2 = idx.reshape(n, ROUTER_LANES)[:, :TOP_K]
    pos, src, route = _moe_route(idx2)
    xs = _moe_gather(src, v.reshape(n, d))
    act = _moe_swiglu(route, xs, w13, layer, MOE_W13_COLS)
    ys = _moe_mm(route, act, w2, layer, MOE_W2_COLS)
    pos3 = pos.reshape(n // ROW_TILE, ROW_TILE, TOP_K).transpose(0, 2, 1)
    out = _moe_combine(pos3, gates.reshape(n, ROUTER_LANES), ys, h.reshape(n, d), modtab, G2,
                       length // ROW_TILE, nct, tail, nw_tail, modtab_next)
    if tail == "next":
        return out[0].reshape(h.shape), out[1]
    return out.reshape(h.shape)


def kernel(x, c, ctx, c_ctx, w_mod, b_mod, norm1, norm2, norm_f, gla_w_in, gla_w_gate_up, gla_b_gate,
           gla_g_head, gla_w_out, ffn_w13, ffn_w2, s5_a_re, s5_a_im, s5_log_dt, s5_b_re, s5_b_im, s5_c_re,
           s5_c_im, s5_d, s5_w_glu, s5_b_glu, moe_w_router, moe_w13, moe_w2):
    bsz, seq, d = x.shape
    nctx = ctx.shape[1]
    depth = w_mod.shape[0]
    assert bsz + 1 <= 8 and seq % ROW_TILE == 0 and nctx % ROW_TILE == 0 and seq % GRID_W == 0
    nct = nctx // ROW_TILE

    c8 = jnp.concatenate([c, c_ctx[None, :], jnp.zeros((8 - bsz - 1, d), F32)], axis=0)
    mods = _modulation(c8, w_mod, b_mod).reshape(depth, 8, 6, d)
    assert depth % 2 == 0
    modtabs = [jnp.stack([jnp.broadcast_to(mods[i, bsz], (bsz, 6, d)), mods[i, :bsz]], axis=1) for i in range(depth)]
    h = jnp.concatenate([ctx, x], axis=1)
    u = None
    for i in range(depth):
        last = i == depth - 1
        j = i // 2
        modtab = modtabs[i]
        if i % 2 == 0:
            h, v = _gla_layer(h, u, modtab, norm1[i], norm2[i], gla_w_in, j, gla_w_gate_up[j], gla_b_gate[j],
                              gla_g_head[j], gla_w_out, nctx)
            h = _dense_ffn(h, v, modtab, ffn_w13, ffn_w2, j, nct)
        else:
            h = _s5_layer(h, modtab, norm1[i], s5_a_re[j], s5_a_im[j], s5_log_dt[j], s5_b_re[j], s5_b_im[j],
                          s5_c_re[j], s5_c_im[j], s5_d[j], s5_w_glu, s5_b_glu, j, nctx, last)
            if last:
                h = _moe_ffn(h, modtab, norm2[i], moe_w_router[j], moe_w13, moe_w2, j, 0, "final", norm_f)
            else:
                h, u = _moe_ffn(h, modtab, norm2[i], moe_w_router[j], moe_w13, moe_w2, j, nct, "next",
                                norm1[i + 1], modtabs[i + 1])
    return h
```

```python
import functools

import jax
import jax.numpy as jnp
from jax import lax
from jax.experimental import pallas as pl
from jax.experimental.pallas import tpu as pltpu

F32 = jnp.float32
BF16 = jnp.bfloat16

EPS = 1e-6
GRID_W = 64

GLA_HEADS = 4
GLA_GATE_RANK = 16
GLA_GATE_TAU = 16.0
GLA_CHUNK = 64

S5_GROUP = 16
S5_CHUNK = 32
S5_GROUP_BLOCK = 4

N_EXPERTS = 8
TOP_K = 2
ROUTER_LANES = 128

ROW_TILE = 256
MOE_TILE = 256
DMA_ISSUE_UNROLL = 8
MOE_W13_COLS = 1408
MOE_W2_COLS = 1024
GLA_IN_COLS = 2048
VMEM_LIMIT = 56 * 1024 * 1024

SH1, SC1, G1, SH2, SC2, G2 = range(6)


def _cp(sem, vmem=None):
    return pltpu.CompilerParams(dimension_semantics=sem, vmem_limit_bytes=vmem)


def _dot(a, b):
    return jnp.dot(a, b, preferred_element_type=F32)


def _split(x):
    hi = x.astype(BF16)
    lo = (x - hi.astype(F32)).astype(BF16)
    return hi, lo


def _dot3(a, b):
    ah, al = _split(a)
    bh, bl = _split(b)
    return _dot(ah, bh) + _dot(ah, bl) + _dot(al, bh)


def _sigmoid(x):
    return 1.0 / (1.0 + jnp.exp(-x))


def _silu(x):
    return x * _sigmoid(x)


def _mod_kernel(c_ref, w_ref, b_ref, o_ref):
    x = _silu(c_ref[...]).astype(BF16)
    o_ref[...] = _dot(x, w_ref[...].astype(BF16)) + b_ref[...]


def _modulation(c8, w_mod, b_mod):
    depth, d, n = w_mod.shape
    tn = 1024
    return pl.pallas_call(
        _mod_kernel,
        out_shape=jax.ShapeDtypeStruct((depth, 8, n), F32),
        grid=(depth, n // tn),
        in_specs=[
            pl.BlockSpec((8, d), lambda i, j: (0, 0)),
            pl.BlockSpec((None, d, tn), lambda i, j: (i, 0, j)),
            pl.BlockSpec((None, 1, tn), lambda i, j: (i, 0, j)),
        ],
        out_specs=pl.BlockSpec((None, 8, tn), lambda i, j: (i, 0, j)),
        compiler_params=_cp(("arbitrary", "arbitrary"), VMEM_LIMIT),
        name="modulation",
    )(c8, w_mod, b_mod.reshape(depth, 1, n))


def _normmod_rows(x, nw, scale, shift):
    ms = jnp.mean(x * x, axis=-1, keepdims=True)
    return x * lax.rsqrt(ms + EPS) * nw * (1.0 + scale) + shift


def _normmod_body(x_ref, nw_ref, mod_ref, sh_row, sc_row):
    return _normmod_rows(x_ref[...], nw_ref[...], mod_ref[sc_row:sc_row + 1, :], mod_ref[sh_row:sh_row + 1, :])


def _normmod_kernel(x_ref, nw_ref, mod_ref, o_ref, *, sh_row, sc_row):
    o_ref[...] = _normmod_body(x_ref, nw_ref, mod_ref, sh_row, sc_row).astype(o_ref.dtype)


def _normmod_router_kernel(x_ref, nw_ref, mod_ref, wr_ref, o_ref, idx_ref, gate_ref, *, sh_row, sc_row):
    y = _normmod_body(x_ref, nw_ref, mod_ref, sh_row, sc_row)
    o_ref[...] = y
    logits = _dot3(y, wr_ref[...])
    lane = lax.broadcasted_iota(jnp.int32, logits.shape, 1)
    lane_f = lane.astype(F32)
    neg = jnp.float32(-jnp.inf)
    l1 = jnp.where(lane < N_EXPERTS, logits, neg)
    m1 = jnp.max(l1, axis=-1, keepdims=True)
    i1 = jnp.min(jnp.where(l1 == m1, lane_f, float(ROUTER_LANES)), axis=-1, keepdims=True)
    l2 = jnp.where(lane_f == i1, neg, l1)
    m2 = jnp.max(l2, axis=-1, keepdims=True)
    i2 = jnp.min(jnp.where(l2 == m2, lane_f, float(ROUTER_LANES)), axis=-1, keepdims=True)
    e = jnp.exp(m2 - m1)
    g1 = 1.0 / (1.0 + e)
    g2 = e * g1
    idx_ref[...] = jnp.where(lane == 0, i1, jnp.where(lane == 1, i2, 0.0)).astype(jnp.int32)
    gate_ref[...] = jnp.where(lane == 0, g1, jnp.where(lane == 1, g2, 0.0))


def _row_specs(d, nct):
    x_spec = pl.BlockSpec((None, ROW_TILE, d), lambda b, t: (b, t, 0))
    mod_spec = pl.BlockSpec((None, None, 6, d), lambda b, t: (b, jnp.where(t >= nct, 1, 0), 0, 0))
    return x_spec, mod_spec


def _normmod(h, nw, modtab, sh_row, sc_row, nct, out_dtype):
    bsz, length, d = h.shape
    x_spec, mod_spec = _row_specs(d, nct)
    return pl.pallas_call(
        functools.partial(_normmod_kernel, sh_row=sh_row, sc_row=sc_row),
        out_shape=jax.ShapeDtypeStruct(h.shape, out_dtype),
        grid=(bsz, length // ROW_TILE),
        in_specs=[x_spec, pl.BlockSpec((1, d), lambda b, t: (0, 0)), mod_spec],
        out_specs=x_spec,
        compiler_params=_cp(("arbitrary", "arbitrary")),
        name="normmod",
    )(h, nw.reshape(1, d), modtab)


def _normmod_router(h, nw, modtab, w_router, sh_row, sc_row, nct):
    bsz, length, d = h.shape
    x_spec, mod_spec = _row_specs(d, nct)
    wr = jnp.pad(w_router, ((0, 0), (0, ROUTER_LANES - w_router.shape[1])))
    lane_spec = pl.BlockSpec((None, ROW_TILE, ROUTER_LANES), lambda b, t: (b, t, 0))
    return pl.pallas_call(
        functools.partial(_normmod_router_kernel, sh_row=sh_row, sc_row=sc_row),
        out_shape=(
            jax.ShapeDtypeStruct(h.shape, F32),
            jax.ShapeDtypeStruct((bsz, length, ROUTER_LANES), jnp.int32),
            jax.ShapeDtypeStruct((bsz, length, ROUTER_LANES), F32),
        ),
        grid=(bsz, length // ROW_TILE),
        in_specs=[x_spec, pl.BlockSpec((1, d), lambda b, t: (0, 0)), mod_spec,
                  pl.BlockSpec((d, ROUTER_LANES), lambda b, t: (0, 0))],
        out_specs=(x_spec, lane_spec, lane_spec),
        compiler_params=_cp(("arbitrary", "arbitrary")),
        name="normmod_router",
    )(h, nw.reshape(1, d), modtab, wr)


def _gla_gate_kernel(x_ref, wgl_ref, wup_ref, b_ref, o_ref):
    wh, wl = _split(wgl_ref[...])
    x = x_ref[...]
    gl = _dot(x, wh) + _dot(x, wl)
    z = _dot3(gl, wup_ref[...]) + b_ref[...]
    log_sig = jnp.minimum(z, 0.0) - jnp.log(1.0 + jnp.exp(-jnp.abs(z)))
    o_ref[...] = log_sig * (1.0 / GLA_GATE_TAU)


def _gla_gate(u, w_gl, w_up_bd, b_gate):
    r, d = u.shape
    n = w_up_bd.shape[1]
    tm = 512
    return pl.pallas_call(
        _gla_gate_kernel,
        out_shape=jax.ShapeDtypeStruct((r, n), F32),
        grid=(r // tm,),
        in_specs=[pl.BlockSpec((tm, d), lambda m: (m, 0)),
                  pl.BlockSpec(w_gl.shape, lambda m: (0, 0)),
                  pl.BlockSpec(w_up_bd.shape, lambda m: (0, 0)),
                  pl.BlockSpec((1, n), lambda m: (0, 0))],
        out_specs=pl.BlockSpec((tm, n), lambda m: (m, 0)),
        compiler_params=_cp(("arbitrary",)),
        name="gla_gate",
    )(u, w_gl, w_up_bd, b_gate.reshape(1, n))


def _gla_scan_kernel(tri_ref, qf_ref, kf_ref, vf_ref, gf_ref, qb_ref, kb_ref, vb_ref, gb_ref, of_ref, ob_ref, s_ref,
                     *, hk, hv):
    @pl.when(pl.program_id(1) == 0)
    def _():
        s_ref[...] = jnp.zeros_like(s_ref)

    ones = jnp.ones((GLA_CHUNK, 128), BF16)
    scale = hk ** -0.5
    tn_dims = (((0,), (0,)), ((), ()))
    nt_dims = (((1,), (1,)), ((), ()))
    streams = ((qf_ref, kf_ref, vf_ref, gf_ref, of_ref), (qb_ref, kb_ref, vb_ref, gb_ref, ob_ref))
    chains = [(d, h) for d in range(2) for h in range(GLA_HEADS)]
    decays = []
    for d, h in chains:
        g = streams[d][3][:, h * hk:(h + 1) * hk]
        gh, gl = _split(g)
        tri = tri_ref[d]
        b = _dot(tri, gh) + _dot(tri, gl)
        tot = jnp.sum(g, axis=0, keepdims=True)
        tot_col = (lax.dot_general(gh, ones, tn_dims, preferred_element_type=F32)
                   + lax.dot_general(gl, ones, tn_dims, preferred_element_type=F32))
        decays.append((b, tot, tot_col))
    operands = []
    for (d, h), (b, tot, tot_col) in zip(chains, decays):
        q = streams[d][0][:, h * hk:(h + 1) * hk]
        k = streams[d][1][:, h * hk:(h + 1) * hk]
        qd = (q * scale * jnp.exp(b)).astype(BF16)
        kd = (k * jnp.exp(-b)).astype(BF16)
        kt = (k * jnp.exp(tot - b)).astype(BF16)
        scores = jnp.where(tri_ref[d] > 0, lax.dot_general(qd, kd, nt_dims, preferred_element_type=F32), 0.0)
        operands.append((qd, kt, scores.astype(BF16)))
    for (d, h), (qd, kt, scores) in zip(chains, operands):
        v = streams[d][2][:, h * hv:(h + 1) * hv].astype(BF16)
        streams[d][4][:, h * hv:(h + 1) * hv] = _dot(scores, v) + _dot(qd, s_ref[d, h].astype(BF16))
    for (d, h), (b, tot, tot_col), (qd, kt, scores) in zip(chains, decays, operands):
        v = streams[d][2][:, h * hv:(h + 1) * hv].astype(BF16)
        decay = jnp.concatenate([jnp.exp(tot_col)] * (hv // 128), axis=1)
        s_ref[d, h] = decay * s_ref[d, h] + lax.dot_general(kt, v, tn_dims, preferred_element_type=F32)


def _gla_scan(proj, log_a, nctx_chunks, dk, dv):
    bsz, length, _ = proj.shape
    nchunks = length // GLA_CHUNK
    hk, hv = dk // GLA_HEADS, dv // GLA_HEADS
    idx = jnp.arange(GLA_CHUNK)
    tri = jnp.stack([idx[:, None] >= idx[None, :], idx[:, None] <= idx[None, :]]).astype(BF16)

    def back(s):
        return jnp.where(s < nctx_chunks, nctx_chunks - 1 - s, nchunks - 1 - (s - nctx_chunks))

    def specs(chunk, direction):
        return [pl.BlockSpec((None, GLA_CHUNK, dk), lambda b, s: (b, chunk(s), 0)),
                pl.BlockSpec((None, GLA_CHUNK, dk), lambda b, s: (b, chunk(s), 1)),
                pl.BlockSpec((None, GLA_CHUNK, dv), lambda b, s: (b, chunk(s), (2 * dk) // dv)),
                pl.BlockSpec((None, GLA_CHUNK, dk), lambda b, s: (b, chunk(s), direction))]

    out = jax.ShapeDtypeStruct((bsz, length, dv), F32)
    return pl.pallas_call(
        functools.partial(_gla_scan_kernel, hk=hk, hv=hv),
        out_shape=(out, out),
        grid=(bsz, nchunks),
        in_specs=[pl.BlockSpec((2, GLA_CHUNK, GLA_CHUNK), lambda b, s: (0, 0, 0))]
        + specs(lambda s: s, 0) + specs(back, 1),
        out_specs=(pl.BlockSpec((None, GLA_CHUNK, dv), lambda b, s: (b, s, 0)),
                   pl.BlockSpec((None, GLA_CHUNK, dv), lambda b, s: (b, back(s), 0))),
        scratch_shapes=[pltpu.VMEM((2, GLA_HEADS, hk, hv), F32)],
        compiler_params=_cp(("arbitrary", "arbitrary"), VMEM_LIMIT),
        name="gla_scan",
    )(tri, proj, proj, proj, log_a, proj, proj, proj, log_a)


def _gla_finish_kernel(of_ref, ob_ref, r_ref, gh_ref, o_ref, *, hv):
    for h in range(GLA_HEADS):
        sl = slice(h * hv, (h + 1) * hv)
        o = of_ref[:, sl] + ob_ref[:, sl]
        ms = jnp.mean(o * o, axis=-1, keepdims=True)
        y = o * lax.rsqrt(ms + EPS) * gh_ref[...]
        o_ref[:, sl] = (y * _silu(r_ref[:, sl])).astype(o_ref.dtype)


def _gla_finish(o_fwd, o_bwd, proj, g_head, dk, dv):
    bsz, length, _ = o_fwd.shape
    hv = dv // GLA_HEADS
    tm = ROW_TILE
    o_spec = pl.BlockSpec((None, tm, dv), lambda b, t: (b, t, 0))
    return pl.pallas_call(
        functools.partial(_gla_finish_kernel, hv=hv),
        out_shape=jax.ShapeDtypeStruct((bsz, length, dv), BF16),
        grid=(bsz, length // tm),
        in_specs=[
            o_spec, o_spec,
            pl.BlockSpec((None, tm, dv), lambda b, t: (b, t, (2 * dk + dv) // dv)),
            pl.BlockSpec((1, hv), lambda b, t: (0, 0)),
        ],
        out_specs=o_spec,
        compiler_params=_cp(("arbitrary", "arbitrary")),
        name="gla_finish",
    )(o_fwd, o_bwd, proj, g_head.reshape(1, hv))


def _gla_layer(h, u, modtab, nw, nw_ffn, w_in, layer, w_gate_up, b_gate, g_head, w_out, nctx):
    bsz, length, d = h.shape
    dk = w_gate_up.shape[2]
    dv = w_out.shape[1]
    nmain = 2 * dk + 2 * dv
    tpb = length // ROW_TILE
    nct = nctx // ROW_TILE
    if u is None:
        u = _normmod(h, nw, modtab, SH1, SC1, nct, BF16).reshape(bsz * length, d)
    w_in_grouped = w_in.reshape(w_in.shape[0], 1, d, w_in.shape[2])
    proj = _moe_mm(_dense_route(bsz * length // MOE_TILE), u, w_in_grouped, layer, GLA_IN_COLS, nmain, "gla_in_proj")
    rank = GLA_GATE_RANK
    w_up_bd = jnp.zeros((2 * rank, 2 * dk), F32)
    w_up_bd = w_up_bd.at[:rank, :dk].set(w_gate_up[0]).at[rank:, dk:].set(w_gate_up[1])
    log_a = _gla_gate(u, w_in[layer, :, nmain:], w_up_bd, b_gate.reshape(2 * dk))
    proj3 = proj.reshape(bsz, length, nmain)
    o_fwd, o_bwd = _gla_scan(proj3, log_a.reshape(bsz, length, 2 * dk), nctx // GLA_CHUNK, dk, dv)
    gated = _gla_finish(o_fwd, o_bwd, proj3, g_head, dk, dv).reshape(bsz * length, dv)
    out, v = _dense_mm_res(gated, w_out, layer, h.reshape(bsz * length, d), modtab, G1, tpb, nct, d, "gla_out_proj",
                           next_norm=(nw_ffn, SH2, SC2))
    return out.reshape(bsz, length, d), v


S5_LANES_PER_BATCH = 96
_AN1, _AN2, _AP1, _AP2, _BB1, _BB2, _CC1, _CC2, _PQ, _PK_ROWS = 0, 32, 64, 96, 128, 144, 160, 176, 192, 200


def _s5_params(a_re, a_im, log_dt, b_re, b_im, c_re, c_im):
    t_len = S5_CHUNK
    dt = jnp.exp(log_dt)[..., None]
    mag = jnp.exp(a_re * dt)
    ab_re = mag * jnp.cos(a_im * dt)
    ab_im = mag * jnp.sin(a_im * dt)
    den = a_re * a_re + a_im * a_im
    f_re = ((ab_re - 1.0) * a_re + ab_im * a_im) / den
    f_im = (ab_im * a_re - (ab_re - 1.0) * a_im) / den
    bb_re = f_re[..., None] * b_re - f_im[..., None] * b_im
    bb_im = f_re[..., None] * b_im + f_im[..., None] * b_re

    def cpow(n):
        nn = n.astype(F32)[:, None, :, None]
        m = jnp.exp((a_re * dt)[:, :, None, :] * nn)
        th = (a_im * dt)[:, :, None, :] * nn
        return m * jnp.cos(th), m * jnp.sin(th)

    steps = jnp.arange(t_len)
    order = jnp.stack([steps, t_len - 1 - steps])
    apr, api = cpow(order)
    inv_sq = jnp.exp(-2.0 * (a_re * dt)[:, :, None, :] * order.astype(F32)[:, None, :, None])
    anr, ani = apr * inv_sq, -api * inv_sq
    atr, ati = cpow(jnp.full((2, 1), t_len))
    cat = lambda u, v: jnp.concatenate([u, v], axis=-1)
    bbr_t, bbi_t = bb_re.transpose(0, 1, 3, 2), bb_im.transpose(0, 1, 3, 2)
    zero = jnp.zeros((2, a_re.shape[1], 5, 2 * a_re.shape[2]), F32)
    return jnp.concatenate([
        cat(anr, anr), cat(-ani, ani),
        cat(apr, api), cat(api, apr),
        cat(bbr_t, bbi_t), cat(bbi_t, bbr_t),
        cat(c_re, -c_re), cat(-c_im, -c_im),
        cat(atr, atr), cat(-ati, ati), cat(ati, -ati), zero], axis=2)


def _s5_in_kernel(hl_ref, hc_ref, nw_ref, mod_ref, o_ref, *, bsz, qb):
    nw = nw_ref[...]
    blocks = []
    for b in range(bsz):
        lat = _normmod_rows(hl_ref[b], nw, mod_ref[b, 1, SC1:SC1 + 1, :], mod_ref[b, 1, SH1:SH1 + 1, :])
        ctx = _normmod_rows(hc_ref[b], nw, mod_ref[b, 0, SC1:SC1 + 1, :], mod_ref[b, 0, SH1:SH1 + 1, :])
        pad = jnp.zeros((qb - lat.shape[0] - ctx.shape[0], lat.shape[1]), F32)
        blocks += [lat, ctx, pad]
    o_ref[...] = jnp.concatenate(blocks, axis=0).T.astype(o_ref.dtype)


def _s5_in(h, nw, modtab, nctx):
    bsz, length, d = h.shape
    t_len = S5_CHUNK
    qb = S5_LANES_PER_BATCH
    nctx_chunks = nctx // t_len
    return pl.pallas_call(
        functools.partial(_s5_in_kernel, bsz=bsz, qb=qb),
        out_shape=jax.ShapeDtypeStruct((d, t_len * bsz * qb), BF16),
        grid=(t_len,),
        in_specs=[
            pl.BlockSpec((bsz, None, GRID_W, d), lambda s: (0, nctx // GRID_W + s, 0, 0)),
            pl.BlockSpec((bsz, nctx_chunks, d), lambda s: (0, 0, s)),
            pl.BlockSpec((1, d), lambda s: (0, 0)),
            pl.BlockSpec(modtab.shape, lambda s: (0, 0, 0, 0)),
        ],
        out_specs=pl.BlockSpec((d, bsz * qb), lambda s: (0, s)),
        compiler_params=_cp(("arbitrary",), VMEM_LIMIT),
        name="s5_in",
    )(h.reshape(bsz, length // GRID_W, GRID_W, d), h[:, :nctx].reshape(bsz, nctx_chunks, t_len * d),
      nw.reshape(1, d), modtab)


def _tile_rows(x, reps):
    return jnp.concatenate([x] * reps, axis=0)


def _repeat_rows(x, reps):
    return jnp.concatenate([jnp.broadcast_to(x[i:i + 1, :], (reps, x.shape[1])) for i in range(x.shape[0])],
                           axis=0)


def _s5_kernel(ut_ref, pk_ref, y_ref, sv_ref, ss_ref, w_ref, r2_ref, acc_ref, *, bsz, qb, orders):
    gb = ut_ref.shape[0] // S5_GROUP
    t_len = S5_CHUNK
    k = S5_GROUP * t_len
    q = ut_ref.shape[1] // t_len
    row_s = lax.broadcasted_iota(jnp.int32, (k, k), 0) // S5_GROUP
    col_t = lax.broadcasted_iota(jnp.int32, (k, k), 1) // S5_GROUP
    masks = (row_s <= col_t, row_s >= col_t)
    w_ref[...] = jnp.zeros_like(w_ref)
    us = []
    for g in range(gb):
        x = ut_ref[g * S5_GROUP:(g + 1) * S5_GROUP, :]
        us.append(jnp.concatenate([x[:, s * q:(s + 1) * q] for s in range(t_len)], axis=0).T)
    pairs = [(g, d) for g in range(gb) for d in range(2)]
    l2s, raw = [], []
    for g, d in pairs:
        pk = pk_ref[d, g]
        l2 = (_repeat_rows(pk[_AN1:_AN1 + t_len], S5_GROUP) * _tile_rows(pk[_BB1:_BB1 + S5_GROUP], t_len)
              + _repeat_rows(pk[_AN2:_AN2 + t_len], S5_GROUP) * _tile_rows(pk[_BB2:_BB2 + S5_GROUP], t_len))
        r2t = (_repeat_rows(pk[_AP1:_AP1 + t_len], S5_GROUP) * _tile_rows(pk[_CC1:_CC1 + S5_GROUP], t_len)
               + _repeat_rows(pk[_AP2:_AP2 + t_len], S5_GROUP) * _tile_rows(pk[_CC2:_CC2 + S5_GROUP], t_len))
        r2 = r2t.T
        r2_ref[2 * g + d] = r2.astype(BF16)
        l2s.append(l2.astype(BF16))
        raw.append(_dot3(l2, r2))
    for (g, d), l2, toep_raw in zip(pairs, l2s, raw):
        i = 2 * g + d
        y_local = _dot(us[g], jnp.where(masks[d], toep_raw, 0.0).astype(BF16))
        if d == 0:
            acc_ref[g] = y_local
        else:
            acc_ref[g] = acc_ref[g] + y_local
        v = _dot(us[g], l2)
        vs = pltpu.roll(v, v.shape[1] // 2, axis=1)
        pk = pk_ref[d, g]
        pt, qt, qts = pk[_PQ:_PQ + 1], pk[_PQ + 1:_PQ + 2], pk[_PQ + 2:_PQ + 3]
        sv_ref[i] = pt * v + qt * vs
        ss_ref[i] = pt * vs + qts * v
    coef = []
    for g in range(gb):
        for d in range(2):
            pk = pk_ref[d, g]
            coef.append(tuple(jnp.broadcast_to(pk[_PQ + r:_PQ + r + 1], (bsz, pk.shape[1])) for r in range(3)))
    zero = jnp.zeros((bsz, pk_ref.shape[3]), F32)
    state = [(zero, zero)] * (2 * gb)
    nsteps = len(orders[0])
    for step in range(nsteps):
        for i in range(2 * gb):
            rows = pl.ds(orders[i % 2][step], bsz, stride=qb)
            w, ws = state[i]
            w_ref[i, rows, :] = w
            if step + 1 < nsteps:
                pt, qt, qts = coef[i]
                state[i] = (pt * w + qt * ws + sv_ref[i, rows, :], pt * ws + qts * w + ss_ref[i, rows, :])
    for g in range(gb):
        y = acc_ref[g]
        for d in range(2):
            i = 2 * g + d
            y = y + _dot(w_ref[i].astype(BF16), r2_ref[i])
        yt = y.T
        for t in range(t_len):
            y_ref[g * S5_GROUP:(g + 1) * S5_GROUP, t * q:(t + 1) * q] = yt[t * S5_GROUP:(t + 1) * S5_GROUP, :]


def _s5_scan(ut, pk, bsz, nctx_chunks, nlat_chunks):
    t_len = S5_CHUNK
    q = ut.shape[1] // t_len
    ngroups = ut.shape[0] // S5_GROUP
    np2 = pk.shape[3]
    gb = S5_GROUP_BLOCK
    qb = S5_LANES_PER_BATCH
    lat = list(range(nlat_chunks))
    ctx = list(range(nlat_chunks, nlat_chunks + nctx_chunks))
    orders = (ctx + lat, ctx[::-1] + lat[::-1])
    blk = (gb * S5_GROUP, t_len * q)
    return pl.pallas_call(
        functools.partial(_s5_kernel, bsz=bsz, qb=qb, orders=orders),
        out_shape=jax.ShapeDtypeStruct(ut.shape, F32),
        grid=(ngroups // gb,),
        in_specs=[pl.BlockSpec(blk, lambda g: (g, 0)),
                  pl.BlockSpec((2, gb, _PK_ROWS, np2), lambda g: (0, g, 0, 0))],
        out_specs=pl.BlockSpec(blk, lambda g: (g, 0)),
        scratch_shapes=[pltpu.VMEM((2 * gb, q, np2), F32), pltpu.VMEM((2 * gb, q, np2), F32),
                        pltpu.VMEM((2 * gb, q, np2), F32), pltpu.VMEM((2 * gb, np2, S5_GROUP * t_len), BF16),
                        pltpu.VMEM((gb, q, S5_GROUP * t_len), F32)],
        compiler_params=_cp(("arbitrary",), VMEM_LIMIT),
        name="s5_scan",
    )(ut, pk)


def _s5_out_kernel(yt_ref, hl_ref, hc_ref, nw_ref, mod_ref, d_ref, ol_ref, oc_ref, *, bsz, qb):
    y = yt_ref[...].T
    nw = nw_ref[...]
    nlat, nc = hl_ref.shape[1], hc_ref.shape[1]
    c0 = 0.7978845608028654

    def act(yv, hv, region, b):
        u = _normmod_rows(hv, nw, mod_ref[b, region, SC1:SC1 + 1, :], mod_ref[b, region, SH1:SH1 + 1, :])
        z = yv + u * d_ref[...]
        return (0.5 * z * (1.0 + jnp.tanh(c0 * (z + 0.044715 * (z * z * z))))).astype(ol_ref.dtype)

    for b in range(bsz):
        ol_ref[b] = act(y[b * qb:b * qb + nlat], hl_ref[b], 1, b)
        oc_ref[b] = act(y[b * qb + nlat:b * qb + nlat + nc], hc_ref[b], 0, b)


def _s5_out(yt, h, nw, modtab, d_skip, nctx):
    bsz, length, d = h.shape
    t_len = S5_CHUNK
    qb = S5_LANES_PER_BATCH
    nctx_chunks = nctx // t_len
    seq = length - nctx
    lat_spec = pl.BlockSpec((bsz, None, GRID_W, d), lambda s: (0, nctx // GRID_W + s, 0, 0))
    ctx_spec = pl.BlockSpec((bsz, nctx_chunks, d), lambda s: (0, 0, s))
    act_l, act_c = pl.pallas_call(
        functools.partial(_s5_out_kernel, bsz=bsz, qb=qb),
        out_shape=(jax.ShapeDtypeStruct((bsz, seq // GRID_W, GRID_W, d), BF16),
                   jax.ShapeDtypeStruct((bsz, nctx_chunks, t_len * d), BF16)),
        grid=(t_len,),
        in_specs=[pl.BlockSpec((d, bsz * qb), lambda s: (0, s)),
                  lat_spec, ctx_spec,
                  pl.BlockSpec((1, d), lambda s: (0, 0)),
                  pl.BlockSpec(modtab.shape, lambda s: (0, 0, 0, 0)),
                  pl.BlockSpec((1, d), lambda s: (0, 0))],
        out_specs=(pl.BlockSpec((bsz, None, GRID_W, d), lambda s: (0, s, 0, 0)), ctx_spec),
        compiler_params=_cp(("arbitrary",), VMEM_LIMIT),
        name="s5_out",
    )(yt, h.reshape(bsz, length // GRID_W, GRID_W, d), h[:, :nctx].reshape(bsz, nctx_chunks, t_len * d),
      nw.reshape(1, d), modtab, d_skip.reshape(1, d))
    return act_l.reshape(bsz, seq, d), act_c.reshape(bsz, nctx, d)


def _s5_layer(h, modtab, nw, a_re, a_im, log_dt, b_re, b_im, c_re, c_im, d_skip, w_glu, b_glu, layer, nctx,
              last):
    bsz, length, d = h.shape
    seq = length - nctx
    t_len = S5_CHUNK
    assert seq // GRID_W == t_len and nctx % t_len == 0 and nctx % GRID_W == 0
    assert seq // t_len + nctx // t_len <= S5_LANES_PER_BATCH
    ngroups = d // S5_GROUP
    ut = _s5_in(h, nw, modtab, nctx)
    pk = _s5_params(a_re, a_im, log_dt, b_re, b_im, c_re, c_im)
    yt = _s5_scan(ut, pk, bsz, nctx // t_len, GRID_W)
    act_l, act_c = _s5_out(yt, h, nw, modtab, d_skip, nctx)
    if last:
        h, nct = h[:, nctx:], 0
    else:
        nct = nctx // ROW_TILE
    rows = h.shape[0] * h.shape[1]
    out = _dense_glu_res(act_l.reshape(bsz * seq, d), act_c.reshape(bsz * nctx, d), w_glu, b_glu, layer,
                         h.reshape(rows, d), modtab, G1, h.shape[1] // ROW_TILE, nct, 1024, "s5_glu")
    return out.reshape(h.shape)


def _dense_ffn(h, v, modtab, w13, w2, layer, nct):
    bsz, length, d = h.shape
    rows = bsz * length
    w13_grouped = w13.reshape(w13.shape[0], 1, w13.shape[1], w13.shape[2])
    act = _moe_swiglu(_dense_route(rows // MOE_TILE), v, w13_grouped, layer, MOE_W13_COLS, "ffn_w13")
    out = _dense_mm_res(act, w2, layer, h.reshape(rows, d), modtab, G2, length // ROW_TILE, nct, MOE_W2_COLS,
                        "ffn_w2")
    return out.reshape(h.shape)


def _moe_gather_kernel(src_ref, nxt_ref, v_hbm, o_ref, buf_ref, sem):
    t = pl.program_id(0)
    slot = t % 2

    def request(idx_ref, s):
        def issue(r8, carry):
            for i in range(DMA_ISSUE_UNROLL):
                r = r8 * DMA_ISSUE_UNROLL + i
                pltpu.make_async_copy(v_hbm.at[pl.ds(idx_ref[0, r], 1)], buf_ref.at[s, pl.ds(r, 1)],
                                      sem.at[s]).start(priority=i % 2)
            return carry
        lax.fori_loop(0, MOE_TILE // DMA_ISSUE_UNROLL, issue, 0)

    @pl.when(t == 0)
    def _():
        request(src_ref, 0)

    @pl.when(t + 1 < pl.num_programs(0))
    def _():
        request(nxt_ref, 1 - slot)

    pltpu.make_async_copy(v_hbm.at[pl.ds(0, MOE_TILE)], buf_ref.at[slot], sem.at[slot]).wait()
    o_ref[...] = buf_ref[slot].astype(o_ref.dtype)


def _moe_gather(src, v):
    ntiles = src.shape[0]
    d = v.shape[1]
    idx_block = (None, 1, MOE_TILE)
    return pl.pallas_call(
        _moe_gather_kernel,
        out_shape=jax.ShapeDtypeStruct((ntiles * MOE_TILE, d), BF16),
        grid=(ntiles,),
        in_specs=[pl.BlockSpec(idx_block, lambda t: (t, 0, 0), memory_space=pltpu.SMEM),
                  pl.BlockSpec(idx_block, lambda t: (jnp.minimum(t + 1, ntiles - 1), 0, 0),
                               memory_space=pltpu.SMEM),
                  pl.BlockSpec(memory_space=pl.ANY)],
        out_specs=pl.BlockSpec((MOE_TILE, d), lambda t: (t, 0)),
        scratch_shapes=[pltpu.VMEM((2, MOE_TILE, d), F32), pltpu.SemaphoreType.DMA((2,))],
        compiler_params=_cp(("arbitrary",)),
        name="moe_gather",
    )(src, src, v)


def _moe_fetch_weights(te_ref, first_ref, nxt_ref, w_hbm, stage_ref, sem, wbf_ref, col_blocks, layer):
    j, t = pl.program_id(0), pl.program_id(1)
    nsweeps = pl.num_programs(0)
    tn = stage_ref.shape[2]

    def copies(e, sweep):
        return [pltpu.make_async_copy(w_hbm.at[layer, e, :, pl.ds(pl.multiple_of(cb(sweep) * tn, 128), tn)],
                                      stage_ref.at[i], sem.at[i])
                for i, cb in enumerate(col_blocks)]

    @pl.when(jnp.logical_and(j == 0, t == 0))
    def _():
        for c in copies(te_ref[0], 0):
            c.start()

    @pl.when(first_ref[t] == 1)
    def _():
        for c in copies(te_ref[t], j):
            c.wait()
        for i in range(len(col_blocks)):
            wbf_ref[:, i * tn:(i + 1) * tn] = stage_ref[i].astype(BF16)
        sweep_done = nxt_ref[t] < 0
        next_e = jnp.where(sweep_done, te_ref[0], nxt_ref[t])
        next_sweep = jnp.where(sweep_done, j + 1, j)

        @pl.when(next_sweep < nsweeps)
        def _():
            for c in copies(next_e, next_sweep):
                c.start()


def _moe_swiglu_kernel(te_ref, nv_ref, first_ref, nxt_ref, x_ref, w_hbm, o_ref, wbf_ref, stage_ref, sem, *, layer):
    nb = pl.num_programs(0)
    _moe_fetch_weights(te_ref, first_ref, nxt_ref, w_hbm, stage_ref, sem, wbf_ref,
                       [lambda sweep: sweep, lambda sweep: nb + sweep], layer)

    in_use = pl.program_id(1) < nv_ref[0]

    @pl.when(in_use)
    def _():
        tn = o_ref.shape[1]
        z = _dot(x_ref[...], wbf_ref[...])
        o_ref[...] = (_silu(z[:, :tn]) * z[:, tn:]).astype(o_ref.dtype)

    @pl.when(jnp.logical_not(in_use))
    def _():
        o_ref[...] = jnp.zeros_like(o_ref)


def _moe_mm_kernel(te_ref, nv_ref, first_ref, nxt_ref, x_ref, w_hbm, o_ref, wbf_ref, stage_ref, sem, *, layer):
    _moe_fetch_weights(te_ref, first_ref, nxt_ref, w_hbm, stage_ref, sem, wbf_ref, [lambda sweep: sweep], layer)

    in_use = pl.program_id(1) < nv_ref[0]

    @pl.when(in_use)
    def _():
        o_ref[...] = _dot(x_ref[...], wbf_ref[...])

    @pl.when(jnp.logical_not(in_use))
    def _():
        o_ref[...] = jnp.zeros_like(o_ref)


def _moe_swiglu(route, xs, w13, layer, tn, name="moe_w13"):
    rows, k = xs.shape
    half = w13.shape[3] // 2
    nb = half // tn
    tm = MOE_TILE
    return pl.pallas_call(
        functools.partial(_moe_swiglu_kernel, layer=layer),
        out_shape=jax.ShapeDtypeStruct((rows, half), BF16),
        grid_spec=pltpu.PrefetchScalarGridSpec(
            num_scalar_prefetch=4,
            grid=(nb, rows // tm),
            in_specs=[pl.BlockSpec((tm, k), lambda j, t, *_: (t, 0)),
                      pl.BlockSpec(memory_space=pl.ANY)],
            out_specs=pl.BlockSpec((tm, tn), lambda j, t, *_: (t, j)),
            scratch_shapes=[pltpu.VMEM((k, 2 * tn), BF16), pltpu.VMEM((2, k, tn), F32),
                            pltpu.SemaphoreType.DMA((2,))]),
        compiler_params=_cp(("arbitrary", "arbitrary"), VMEM_LIMIT),
        name=name,
    )(*route, xs, w13)


def _moe_mm(route, xs, w2, layer, tn, n=None, name="moe_w2"):
    rows, k = xs.shape
    n = w2.shape[3] if n is None else n
    tm = MOE_TILE
    return pl.pallas_call(
        functools.partial(_moe_mm_kernel, layer=layer),
        out_shape=jax.ShapeDtypeStruct((rows, n), F32),
        grid_spec=pltpu.PrefetchScalarGridSpec(
            num_scalar_prefetch=4,
            grid=(n // tn, rows // tm),
            in_specs=[pl.BlockSpec((tm, k), lambda j, t, *_: (t, 0)),
                      pl.BlockSpec(memory_space=pl.ANY)],
            out_specs=pl.BlockSpec((tm, tn), lambda j, t, *_: (t, j)),
            scratch_shapes=[pltpu.VMEM((k, tn), BF16), pltpu.VMEM((1, k, tn), F32),
                            pltpu.SemaphoreType.DMA((1,))]),
        compiler_params=_cp(("arbitrary", "arbitrary"), VMEM_LIMIT),
        name=name,
    )(*route, xs, w2)


def _dense_route(ntiles):
    zeros = jnp.zeros((ntiles,), jnp.int32)
    return (zeros, jnp.full((1,), ntiles, jnp.int32), zeros.at[0].set(1), zeros - 1)


def _dense_mm_res_kernel(te_ref, nv_ref, first_ref, nxt_ref, x_ref, w_hbm, res_ref, mod_ref, o_ref, wbf_ref,
                         stage_ref, sem, *, layer, g_row):
    _moe_fetch_weights(te_ref, first_ref, nxt_ref, w_hbm, stage_ref, sem, wbf_ref, [lambda sweep: sweep], layer)
    o_ref[...] = res_ref[...] + mod_ref[g_row:g_row + 1, :] * _dot(x_ref[...], wbf_ref[...])


def _dense_mm_res_norm_kernel(te_ref, nv_ref, first_ref, nxt_ref, x_ref, w_hbm, res_ref, mod_ref, nw_ref, o_ref,
                              v_ref, wbf_ref, stage_ref, sem, *, layer, g_row, sh_row, sc_row):
    _moe_fetch_weights(te_ref, first_ref, nxt_ref, w_hbm, stage_ref, sem, wbf_ref, [lambda sweep: sweep], layer)
    h_new = res_ref[...] + mod_ref[g_row:g_row + 1, :] * _dot(x_ref[...], wbf_ref[...])
    o_ref[...] = h_new
    v_ref[...] = _normmod_rows(h_new, nw_ref[...], mod_ref[sc_row:sc_row + 1, :],
                               mod_ref[sh_row:sh_row + 1, :]).astype(v_ref.dtype)


def _dense_glu_res_kernel(te_ref, nv_ref, first_ref, nxt_ref, x_ref, xc_ref, w_hbm, ba_ref, bb_ref, res_ref, mod_ref,
                          o_ref, wbf_ref, stage_ref, sem, *, layer, g_row, tpb, nct):
    nb = pl.num_programs(0)
    _moe_fetch_weights(te_ref, first_ref, nxt_ref, w_hbm, stage_ref, sem, wbf_ref,
                       [lambda sweep: sweep, lambda sweep: nb + sweep], layer)
    tn = o_ref.shape[1]
    x = x_ref[...] if nct == 0 else jnp.where(pl.program_id(1) % tpb < nct, xc_ref[...], x_ref[...])
    z = _dot(x, wbf_ref[...])
    z1 = z[:, :tn] + ba_ref[...]
    z2 = z[:, tn:] + bb_ref[...]
    o_ref[...] = res_ref[...] + mod_ref[g_row:g_row + 1, :] * (z1 * _sigmoid(z2))


def _dense_res_specs(tn, tpb, nct):
    res_spec = pl.BlockSpec((ROW_TILE, tn), lambda j, t, *_: (t, j))
    mod_spec = pl.BlockSpec((None, None, 6, tn),
                            lambda j, t, *_: (t // tpb, jnp.where(t % tpb >= nct, 1, 0), 0, j))
    return res_spec, mod_spec


def _dense_mm_res(x, w, layer, res, modtab, g_row, tpb, nct, tn, name, next_norm=None):
    r, k = x.shape
    n = w.shape[2]
    tm = ROW_TILE
    res_spec, mod_spec = _dense_res_specs(tn, tpb, nct)
    out_spec = pl.BlockSpec((tm, tn), lambda j, t, *_: (t, j))
    in_specs = [pl.BlockSpec((tm, k), lambda j, t, *_: (t, 0)), pl.BlockSpec(memory_space=pl.ANY), res_spec, mod_spec]
    operands = [x, w.reshape(w.shape[0], 1, k, n), res, modtab]
    if next_norm is None:
        body = functools.partial(_dense_mm_res_kernel, layer=layer, g_row=g_row)
        out_shape, out_specs = jax.ShapeDtypeStruct((r, n), F32), out_spec
    else:
        assert tn == n
        nw, sh_row, sc_row = next_norm
        body = functools.partial(_dense_mm_res_norm_kernel, layer=layer, g_row=g_row, sh_row=sh_row, sc_row=sc_row)
        in_specs.append(pl.BlockSpec((1, n), lambda j, t, *_: (0, 0)))
        operands.append(nw.reshape(1, n))
        out_shape = (jax.ShapeDtypeStruct((r, n), F32), jax.ShapeDtypeStruct((r, n), BF16))
        out_specs = (out_spec, out_spec)
    return pl.pallas_call(
        body,
        out_shape=out_shape,
        grid_spec=pltpu.PrefetchScalarGridSpec(
            num_scalar_prefetch=4,
            grid=(n // tn, r // tm),
            in_specs=in_specs,
            out_specs=out_specs,
            scratch_shapes=[pltpu.VMEM((k, tn), BF16), pltpu.VMEM((1, k, tn), F32),
                            pltpu.SemaphoreType.DMA((1,))]),
        compiler_params=_cp(("arbitrary", "arbitrary"), VMEM_LIMIT),
        name=name,
    )(*_dense_route(r // tm), *operands)


def _dense_glu_res(x, x_ctx, w, bias, layer, res, modtab, g_row, tpb, nct, tn, name):
    r, k = res.shape[0], x.shape[1]
    half = w.shape[2] // 2
    nb = half // tn
    tm = ROW_TILE
    nlat = tpb - nct
    res_spec, mod_spec = _dense_res_specs(tn, tpb, nct)
    bias3 = bias.reshape(bias.shape[0], 1, 2 * half)
    return pl.pallas_call(
        functools.partial(_dense_glu_res_kernel, layer=layer, g_row=g_row, tpb=tpb, nct=nct),
        out_shape=jax.ShapeDtypeStruct((r, half), F32),
        grid_spec=pltpu.PrefetchScalarGridSpec(
            num_scalar_prefetch=4,
            grid=(nb, r // tm),
            in_specs=[pl.BlockSpec((tm, k), lambda j, t, *_: ((t // tpb) * nlat + jnp.maximum(t % tpb - nct, 0), 0)),
                      pl.BlockSpec((tm, k), lambda j, t, *_: ((t // tpb) * max(nct, 1)
                                                              + jnp.minimum(t % tpb, max(nct, 1) - 1), 0)),
                      pl.BlockSpec(memory_space=pl.ANY),
                      pl.BlockSpec((None, 1, tn), lambda j, t, *_: (layer, 0, j)),
                      pl.BlockSpec((None, 1, tn), lambda j, t, *_: (layer, 0, nb + j)),
                      res_spec, mod_spec],
            out_specs=pl.BlockSpec((tm, tn), lambda j, t, *_: (t, j)),
            scratch_shapes=[pltpu.VMEM((k, 2 * tn), BF16), pltpu.VMEM((2, k, tn), F32),
                            pltpu.SemaphoreType.DMA((2,))]),
        compiler_params=_cp(("arbitrary", "arbitrary"), VMEM_LIMIT),
        name=name,
    )(*_dense_route(r // tm), x, x if nct == 0 else x_ctx, w.reshape(w.shape[0], 1, k, 2 * half), bias3, bias3,
      res, modtab)


def _moe_combine_kernel(pos_ref, nxt_ref, gate_ref, y_hbm, res_ref, mod_ref, *rest, g_row, tail):
    if tail == "final":
        nw_ref, o_ref, buf_ref, sem = rest
    elif tail == "next":
        nw_ref, mod2_ref, o_ref, u_ref, buf_ref, sem = rest
    else:
        o_ref, buf_ref, sem = rest
    m = pl.program_id(0)
    slot = m % 2

    def request(idx_ref, s):
        for k in range(TOP_K):
            def issue(r8, carry, k=k):
                for i in range(DMA_ISSUE_UNROLL):
                    r = r8 * DMA_ISSUE_UNROLL + i
                    pltpu.make_async_copy(y_hbm.at[pl.ds(idx_ref[k, r], 1)], buf_ref.at[s, k, pl.ds(r, 1)],
                                          sem.at[s]).start(priority=i % 2)
                return carry
            lax.fori_loop(0, ROW_TILE // DMA_ISSUE_UNROLL, issue, 0)

    @pl.when(m == 0)
    def _():
        request(pos_ref, 0)

    @pl.when(m + 1 < pl.num_programs(0))
    def _():
        request(nxt_ref, 1 - slot)

    for k in range(TOP_K):
        pltpu.make_async_copy(y_hbm.at[pl.ds(0, ROW_TILE)], buf_ref.at[slot, k], sem.at[slot]).wait()
    p = gate_ref[...]
    mix = p[:, 0:1] * buf_ref[slot, 0] + p[:, 1:2] * buf_ref[slot, 1]
    h_new = res_ref[...] + mod_ref[g_row:g_row + 1, :] * mix
    if tail == "final":
        ms = jnp.mean(h_new * h_new, axis=-1, keepdims=True)
        o_ref[...] = h_new * lax.rsqrt(ms + EPS) * nw_ref[...]
    else:
        o_ref[...] = h_new
    if tail == "next":
        u_ref[...] = _normmod_rows(h_new, nw_ref[...], mod2_ref[SC1:SC1 + 1, :],
                                   mod2_ref[SH1:SH1 + 1, :]).astype(u_ref.dtype)


def _moe_combine(pos, gates, ys, res, modtab, g_row, tpb, nct, tail=None, nw=None, modtab_next=None):
    n, d = res.shape
    tm = ROW_TILE
    ntiles = n // tm
    row_spec = pl.BlockSpec((tm, d), lambda m: (m, 0))
    mod_spec = pl.BlockSpec((None, None, 6, d), lambda m: (m // tpb, jnp.where(m % tpb >= nct, 1, 0), 0, 0))
    in_specs = [pl.BlockSpec((None, TOP_K, tm), lambda m: (m, 0, 0), memory_space=pltpu.SMEM),
                pl.BlockSpec((None, TOP_K, tm), lambda m: (jnp.minimum(m + 1, ntiles - 1), 0, 0),
                             memory_space=pltpu.SMEM),
                pl.BlockSpec((tm, ROUTER_LANES), lambda m: (m, 0)),
                pl.BlockSpec(memory_space=pl.ANY), row_spec, mod_spec]
    operands = [pos, pos, gates, ys, res, modtab]
    out_shape, out_specs = jax.ShapeDtypeStruct((n, d), F32), row_spec
    if tail is not None:
        in_specs.append(pl.BlockSpec((1, d), lambda m: (0, 0)))
        operands.append(nw.reshape(1, d))
    if tail == "next":
        in_specs.append(mod_spec)
        operands.append(modtab_next)
        out_shape, out_specs = (out_shape, jax.ShapeDtypeStruct((n, d), BF16)), (row_spec, row_spec)
    return pl.pallas_call(
        functools.partial(_moe_combine_kernel, g_row=g_row, tail=tail),
        out_shape=out_shape,
        grid=(ntiles,),
        in_specs=in_specs,
        out_specs=out_specs,
        scratch_shapes=[pltpu.VMEM((2, TOP_K, tm, d), F32), pltpu.SemaphoreType.DMA((2,))],
        compiler_params=_cp(("arbitrary",), VMEM_LIMIT),
        name="moe_combine",
    )(*operands)


def _moe_route(idx2):
    n = idx2.shape[0]
    flat_e = idx2.reshape(-1)
    onehot = (flat_e[:, None] == jnp.arange(N_EXPERTS, dtype=jnp.int32)[None, :]).astype(jnp.int32)
    csum = jnp.cumsum(onehot, axis=0)
    rank = jnp.sum((csum - onehot) * onehot, axis=1)
    counts = csum[-1]
    ntile = (counts + MOE_TILE - 1) // MOE_TILE
    tile_end = jnp.cumsum(ntile)
    tile_start = tile_end - ntile
    pos = tile_start[flat_e] * MOE_TILE + rank
    tiles = (TOP_K * n) // MOE_TILE + N_EXPERTS
    src = jnp.zeros((tiles * MOE_TILE,), jnp.int32).at[pos].set(jnp.arange(TOP_K * n, dtype=jnp.int32) // TOP_K)
    nv = tile_end[-1:]
    tile_ids = jnp.arange(tiles, dtype=jnp.int32)
    tids = jnp.minimum(tile_ids, nv[0] - 1)
    te = jnp.sum((tile_end[None, :] <= tids[:, None]).astype(jnp.int32), axis=1)
    first = jnp.logical_and(tile_ids < nv[0], jnp.logical_or(tile_ids == 0, te != jnp.roll(te, 1)))
    run_end = tile_end[te]
    nxt = jnp.where(run_end < nv[0], te[jnp.minimum(run_end, tiles - 1)], -1)
    route = (te, nv.astype(jnp.int32), first.astype(jnp.int32), nxt.astype(jnp.int32))
    return pos.astype(jnp.int32), src.reshape(tiles, 1, MOE_TILE), route


def _moe_ffn(h, modtab, nw, w_router, w13, w2, layer, nct, tail=None, nw_tail=None, modtab_next=None):
    bsz, length, d = h.shape
    n = bsz * length
    v, idx, gates = _normmod_router(h, nw, modtab, w_router, SH2, SC2, nct)
    idx2 = idx.reshape(n, ROUTER_LANES)[:, :TOP_K]
    pos, src, route = _moe_route(idx2)
    xs = _moe_gather(src, v.reshape(n, d))
    act = _moe_swiglu(route, xs, w13, layer, MOE_W13_COLS)
    ys = _moe_mm(route, act, w2, layer, MOE_W2_COLS)
    pos3 = pos.reshape(n // ROW_TILE, ROW_TILE, TOP_K).transpose(0, 2, 1)
    out = _moe_combine(pos3, gates.reshape(n, ROUTER_LANES), ys, h.reshape(n, d), modtab, G2,
                       length // ROW_TILE, nct, tail, nw_tail, modtab_next)
    if tail == "next":
        return out[0].reshape(h.shape), out[1]
    return out.reshape(h.shape)


def kernel(x, c, ctx, c_ctx, w_mod, b_mod, norm1, norm2, norm_f, gla_w_in, gla_w_gate_up, gla_b_gate,
           gla_g_head, gla_w_out, ffn_w13, ffn_w2, s5_a_re, s5_a_im, s5_log_dt, s5_b_re, s5_b_im, s5_c_re,
           s5_c_im, s5_d, s5_w_glu, s5_b_glu, moe_w_router, moe_w13, moe_w2):
    bsz, seq, d = x.shape
    nctx = ctx.shape[1]
    depth = w_mod.shape[0]
    assert bsz + 1 <= 8 and seq % ROW_TILE == 0 and nctx % ROW_TILE == 0 and seq % GRID_W == 0
    nct = nctx // ROW_TILE

    c8 = jnp.concatenate([c, c_ctx[None, :], jnp.zeros((8 - bsz - 1, d), F32)], axis=0)
    mods = _modulation(c8, w_mod, b_mod).reshape(depth, 8, 6, d)
    assert depth % 2 == 0
    modtabs = [jnp.stack([jnp.broadcast_to(mods[i, bsz], (bsz, 6, d)), mods[i, :bsz]], axis=1) for i in range(depth)]
    h = jnp.concatenate([ctx, x], axis=1)
    u = None
    for i in range(depth):
        last = i == depth - 1
        j = i // 2
        modtab = modtabs[i]
        if i % 2 == 0:
            h, v = _gla_layer(h, u, modtab, norm1[i], norm2[i], gla_w_in, j, gla_w_gate_up[j], gla_b_gate[j],
                              gla_g_head[j], gla_w_out, nctx)
            h = _dense_ffn(h, v, modtab, ffn_w13, ffn_w2, j, nct)
        else:
            h = _s5_layer(h, modtab, norm1[i], s5_a_re[j], s5_a_im[j], s5_log_dt[j], s5_b_re[j], s5_b_im[j],
                          s5_c_re[j], s5_c_im[j], s5_d[j], s5_w_glu, s5_b_glu, j, nctx, last)
            if last:
                h = _moe_ffn(h, modtab, norm2[i], moe_w_router[j], moe_w13, moe_w2, j, 0, "final", norm_f)
            else:
                h, u = _moe_ffn(h, modtab, norm2[i], moe_w_router[j], moe_w13, moe_w2, j, nct, "next",
                                norm1[i + 1], modtabs[i + 1])
    return h
```

```python
import functools

import jax
import jax.numpy as jnp
from jax import lax
from jax.experimental import pallas as pl
from jax.experimental.pallas import tpu as pltpu

F32 = jnp.float32
BF16 = jnp.bfloat16

EPS = 1e-6
GRID_W = 64

GLA_HEADS = 4
GLA_GATE_RANK = 16
GLA_GATE_TAU = 16.0
GLA_CHUNK = 64

S5_GROUP = 16
S5_CHUNK = 32
S5_GROUP_BLOCK = 4

N_EXPERTS = 8
TOP_K = 2
ROUTER_LANES = 128

ROW_TILE = 256
MOE_TILE = 256
DMA_ISSUE_UNROLL = 8
MOE_W13_COLS = 1408
MOE_W2_COLS = 1024
GLA_IN_COLS = 2048
VMEM_LIMIT = 56 * 1024 * 1024

SH1, SC1, G1, SH2, SC2, G2 = range(6)


def _cp(sem, vmem=None):
    return pltpu.CompilerParams(dimension_semantics=sem, vmem_limit_bytes=vmem)


def _dot(a, b):
    return jnp.dot(a, b, preferred_element_type=F32)


def _split(x):
    hi = x.astype(BF16)
    lo = (x - hi.astype(F32)).astype(BF16)
    return hi, lo


def _dot3(a, b):
    ah, al = _split(a)
    bh, bl = _split(b)
    return _dot(ah, bh) + _dot(ah, bl) + _dot(al, bh)


def _sigmoid(x):
    return 1.0 / (1.0 + jnp.exp(-x))


def _silu(x):
    return x * _sigmoid(x)


def _mod_kernel(c_ref, w_ref, b_ref, o_ref):
    x = _silu(c_ref[...]).astype(BF16)
    o_ref[...] = _dot(x, w_ref[...].astype(BF16)) + b_ref[...]


def _modulation(c8, w_mod, b_mod):
    depth, d, n = w_mod.shape
    tn = 1024
    return pl.pallas_call(
        _mod_kernel,
        out_shape=jax.ShapeDtypeStruct((depth, 8, n), F32),
        grid=(depth, n // tn),
        in_specs=[
            pl.BlockSpec((8, d), lambda i, j: (0, 0)),
            pl.BlockSpec((None, d, tn), lambda i, j: (i, 0, j)),
            pl.BlockSpec((None, 1, tn), lambda i, j: (i, 0, j)),
        ],
        out_specs=pl.BlockSpec((None, 8, tn), lambda i, j: (i, 0, j)),
        compiler_params=_cp(("arbitrary", "arbitrary"), VMEM_LIMIT),
        name="modulation",
    )(c8, w_mod, b_mod.reshape(depth, 1, n))


def _normmod_rows(x, nw, scale, shift):
    ms = jnp.mean(x * x, axis=-1, keepdims=True)
    return x * lax.rsqrt(ms + EPS) * nw * (1.0 + scale) + shift


def _normmod_body(x_ref, nw_ref, mod_ref, sh_row, sc_row):
    return _normmod_rows(x_ref[...], nw_ref[...], mod_ref[sc_row:sc_row + 1, :], mod_ref[sh_row:sh_row + 1, :])


def _normmod_kernel(x_ref, nw_ref, mod_ref, o_ref, *, sh_row, sc_row):
    o_ref[...] = _normmod_body(x_ref, nw_ref, mod_ref, sh_row, sc_row).astype(o_ref.dtype)


def _normmod_router_kernel(x_ref, nw_ref, mod_ref, wr_ref, o_ref, idx_ref, gate_ref, *, sh_row, sc_row):
    y = _normmod_body(x_ref, nw_ref, mod_ref, sh_row, sc_row)
    o_ref[...] = y
    logits = _dot3(y, wr_ref[...])
    lane = lax.broadcasted_iota(jnp.int32, logits.shape, 1)
    lane_f = lane.astype(F32)
    neg = jnp.float32(-jnp.inf)
    l1 = jnp.where(lane < N_EXPERTS, logits, neg)
    m1 = jnp.max(l1, axis=-1, keepdims=True)
    i1 = jnp.min(jnp.where(l1 == m1, lane_f, float(ROUTER_LANES)), axis=-1, keepdims=True)
    l2 = jnp.where(lane_f == i1, neg, l1)
    m2 = jnp.max(l2, axis=-1, keepdims=True)
    i2 = jnp.min(jnp.where(l2 == m2, lane_f, float(ROUTER_LANES)), axis=-1, keepdims=True)
    e = jnp.exp(m2 - m1)
    g1 = 1.0 / (1.0 + e)
    g2 = e * g1
    idx_ref[...] = jnp.where(lane == 0, i1, jnp.where(lane == 1, i2, 0.0)).astype(jnp.int32)
    gate_ref[...] = jnp.where(lane == 0, g1, jnp.where(lane == 1, g2, 0.0))


def _row_specs(d, nct):
    x_spec = pl.BlockSpec((None, ROW_TILE, d), lambda b, t: (b, t, 0))
    mod_spec = pl.BlockSpec((None, None, 6, d), lambda b, t: (b, jnp.where(t >= nct, 1, 0), 0, 0))
    return x_spec, mod_spec


def _normmod(h, nw, modtab, sh_row, sc_row, nct, out_dtype):
    bsz, length, d = h.shape
    x_spec, mod_spec = _row_specs(d, nct)
    return pl.pallas_call(
        functools.partial(_normmod_kernel, sh_row=sh_row, sc_row=sc_row),
        out_shape=jax.ShapeDtypeStruct(h.shape, out_dtype),
        grid=(bsz, length // ROW_TILE),
        in_specs=[x_spec, pl.BlockSpec((1, d), lambda b, t: (0, 0)), mod_spec],
        out_specs=x_spec,
        compiler_params=_cp(("arbitrary", "arbitrary")),
        name="normmod",
    )(h, nw.reshape(1, d), modtab)


def _normmod_router(h, nw, modtab, w_router, sh_row, sc_row, nct):
    bsz, length, d = h.shape
    x_spec, mod_spec = _row_specs(d, nct)
    wr = jnp.pad(w_router, ((0, 0), (0, ROUTER_LANES - w_router.shape[1])))
    lane_spec = pl.BlockSpec((None, ROW_TILE, ROUTER_LANES), lambda b, t: (b, t, 0))
    return pl.pallas_call(
        functools.partial(_normmod_router_kernel, sh_row=sh_row, sc_row=sc_row),
        out_shape=(
            jax.ShapeDtypeStruct(h.shape, F32),
            jax.ShapeDtypeStruct((bsz, length, ROUTER_LANES), jnp.int32),
            jax.ShapeDtypeStruct((bsz, length, ROUTER_LANES), F32),
        ),
        grid=(bsz, length // ROW_TILE),
        in_specs=[x_spec, pl.BlockSpec((1, d), lambda b, t: (0, 0)), mod_spec,
                  pl.BlockSpec((d, ROUTER_LANES), lambda b, t: (0, 0))],
        out_specs=(x_spec, lane_spec, lane_spec),
        compiler_params=_cp(("arbitrary", "arbitrary")),
        name="normmod_router",
    )(h, nw.reshape(1, d), modtab, wr)


def _gla_gate_kernel(x_ref, wgl_ref, wup_ref, b_ref, o_ref):
    wh, wl = _split(wgl_ref[...])
    x = x_ref[...]
    gl = _dot(x, wh) + _dot(x, wl)
    z = _dot3(gl, wup_ref[...]) + b_ref[...]
    log_sig = jnp.minimum(z, 0.0) - jnp.log(1.0 + jnp.exp(-jnp.abs(z)))
    o_ref[...] = log_sig * (1.0 / GLA_GATE_TAU)


def _gla_gate(u, w_gl, w_up_bd, b_gate):
    r, d = u.shape
    n = w_up_bd.shape[1]
    tm = 512
    return pl.pallas_call(
        _gla_gate_kernel,
        out_shape=jax.ShapeDtypeStruct((r, n), F32),
        grid=(r // tm,),
        in_specs=[pl.BlockSpec((tm, d), lambda m: (m, 0)),
                  pl.BlockSpec(w_gl.shape, lambda m: (0, 0)),
                  pl.BlockSpec(w_up_bd.shape, lambda m: (0, 0)),
                  pl.BlockSpec((1, n), lambda m: (0, 0))],
        out_specs=pl.BlockSpec((tm, n), lambda m: (m, 0)),
        compiler_params=_cp(("arbitrary",)),
        name="gla_gate",
    )(u, w_gl, w_up_bd, b_gate.reshape(1, n))


def _gla_scan_kernel(tri_ref, qf_ref, kf_ref, vf_ref, gf_ref, qb_ref, kb_ref, vb_ref, gb_ref, of_ref, ob_ref, s_ref,
                     *, hk, hv):
    @pl.when(pl.program_id(1) == 0)
    def _():
        s_ref[...] = jnp.zeros_like(s_ref)

    ones = jnp.ones((GLA_CHUNK, 128), BF16)
    scale = hk ** -0.5
    tn_dims = (((0,), (0,)), ((), ()))
    nt_dims = (((1,), (1,)), ((), ()))
    streams = ((qf_ref, kf_ref, vf_ref, gf_ref, of_ref), (qb_ref, kb_ref, vb_ref, gb_ref, ob_ref))
    chains = [(d, h) for d in range(2) for h in range(GLA_HEADS)]
    decays = []
    for d, h in chains:
        g = streams[d][3][:, h * hk:(h + 1) * hk]
        gh, gl = _split(g)
        tri = tri_ref[d]
        b = _dot(tri, gh) + _dot(tri, gl)
        tot = jnp.sum(g, axis=0, keepdims=True)
        tot_col = (lax.dot_general(gh, ones, tn_dims, preferred_element_type=F32)
                   + lax.dot_general(gl, ones, tn_dims, preferred_element_type=F32))
        decays.append((b, tot, tot_col))
    operands = []
    for (d, h), (b, tot, tot_col) in zip(chains, decays):
        q = streams[d][0][:, h * hk:(h + 1) * hk]
        k = streams[d][1][:, h * hk:(h + 1) * hk]
        qd = (q * scale * jnp.exp(b)).astype(BF16)
        kd = (k * jnp.exp(-b)).astype(BF16)
        kt = (k * jnp.exp(tot - b)).astype(BF16)
        scores = jnp.where(tri_ref[d] > 0, lax.dot_general(qd, kd, nt_dims, preferred_element_type=F32), 0.0)
        operands.append((qd, kt, scores.astype(BF16)))
    for (d, h), (qd, kt, scores) in zip(chains, operands):
        v = streams[d][2][:, h * hv:(h + 1) * hv].astype(BF16)
        o = _dot(scores, v) + _dot(qd, s_ref[d, h].astype(BF16))
        streams[d][4][:, h * hv:(h + 1) * hv] = o.astype(streams[d][4].dtype)
    for (d, h), (b, tot, tot_col), (qd, kt, scores) in zip(chains, decays, operands):
        v = streams[d][2][:, h * hv:(h + 1) * hv].astype(BF16)
        decay = jnp.concatenate([jnp.exp(tot_col)] * (hv // 128), axis=1)
        s_ref[d, h] = decay * s_ref[d, h] + lax.dot_general(kt, v, tn_dims, preferred_element_type=F32)


def _gla_scan(proj, log_a, nctx_chunks, dk, dv):
    bsz, length, _ = proj.shape
    nchunks = length // GLA_CHUNK
    hk, hv = dk // GLA_HEADS, dv // GLA_HEADS
    idx = jnp.arange(GLA_CHUNK)
    tri = jnp.stack([idx[:, None] >= idx[None, :], idx[:, None] <= idx[None, :]]).astype(BF16)

    def back(s):
        return jnp.where(s < nctx_chunks, nctx_chunks - 1 - s, nchunks - 1 - (s - nctx_chunks))

    def specs(chunk, direction):
        return [pl.BlockSpec((None, GLA_CHUNK, dk), lambda b, s: (b, chunk(s), 0)),
                pl.BlockSpec((None, GLA_CHUNK, dk), lambda b, s: (b, chunk(s), 1)),
                pl.BlockSpec((None, GLA_CHUNK, dv), lambda b, s: (b, chunk(s), (2 * dk) // dv)),
                pl.BlockSpec((None, GLA_CHUNK, dk), lambda b, s: (b, chunk(s), direction))]

    out = jax.ShapeDtypeStruct((bsz, length, dv), BF16)
    return pl.pallas_call(
        functools.partial(_gla_scan_kernel, hk=hk, hv=hv),
        out_shape=(out, out),
        grid=(bsz, nchunks),
        in_specs=[pl.BlockSpec((2, GLA_CHUNK, GLA_CHUNK), lambda b, s: (0, 0, 0))]
        + specs(lambda s: s, 0) + specs(back, 1),
        out_specs=(pl.BlockSpec((None, GLA_CHUNK, dv), lambda b, s: (b, s, 0)),
                   pl.BlockSpec((None, GLA_CHUNK, dv), lambda b, s: (b, back(s), 0))),
        scratch_shapes=[pltpu.VMEM((2, GLA_HEADS, hk, hv), F32)],
        compiler_params=_cp(("arbitrary", "arbitrary"), VMEM_LIMIT),
        name="gla_scan",
    )(tri, proj, proj, proj, log_a, proj, proj, proj, log_a)


def _gla_finish_kernel(of_ref, ob_ref, r_ref, gh_ref, o_ref, *, hv):
    for h in range(GLA_HEADS):
        sl = slice(h * hv, (h + 1) * hv)
        o = of_ref[:, sl].astype(F32) + ob_ref[:, sl].astype(F32)
        ms = jnp.mean(o * o, axis=-1, keepdims=True)
        y = o * lax.rsqrt(ms + EPS) * gh_ref[...]
        o_ref[:, sl] = (y * _silu(r_ref[:, sl])).astype(o_ref.dtype)


def _gla_finish(o_fwd, o_bwd, proj, g_head, dk, dv):
    bsz, length, _ = o_fwd.shape
    hv = dv // GLA_HEADS
    tm = ROW_TILE
    o_spec = pl.BlockSpec((None, tm, dv), lambda b, t: (b, t, 0))
    return pl.pallas_call(
        functools.partial(_gla_finish_kernel, hv=hv),
        out_shape=jax.ShapeDtypeStruct((bsz, length, dv), BF16),
        grid=(bsz, length // tm),
        in_specs=[
            o_spec, o_spec,
            pl.BlockSpec((None, tm, dv), lambda b, t: (b, t, (2 * dk + dv) // dv)),
            pl.BlockSpec((1, hv), lambda b, t: (0, 0)),
        ],
        out_specs=o_spec,
        compiler_params=_cp(("arbitrary", "arbitrary")),
        name="gla_finish",
    )(o_fwd, o_bwd, proj, g_head.reshape(1, hv))


def _gla_layer(h, u, modtab, nw, nw_ffn, w_in, layer, w_gate_up, b_gate, g_head, w_out, nctx):
    bsz, length, d = h.shape
    dk = w_gate_up.shape[2]
    dv = w_out.shape[1]
    nmain = 2 * dk + 2 * dv
    tpb = length // ROW_TILE
    nct = nctx // ROW_TILE
    if u is None:
        u = _normmod(h, nw, modtab, SH1, SC1, nct, BF16).reshape(bsz * length, d)
    w_in_grouped = w_in.reshape(w_in.shape[0], 1, d, w_in.shape[2])
    proj = _moe_mm(_dense_route(bsz * length // MOE_TILE), u, w_in_grouped, layer, GLA_IN_COLS, nmain, "gla_in_proj")
    rank = GLA_GATE_RANK
    w_up_bd = jnp.zeros((2 * rank, 2 * dk), F32)
    w_up_bd = w_up_bd.at[:rank, :dk].set(w_gate_up[0]).at[rank:, dk:].set(w_gate_up[1])
    log_a = _gla_gate(u, w_in[layer, :, nmain:], w_up_bd, b_gate.reshape(2 * dk))
    proj3 = proj.reshape(bsz, length, nmain)
    o_fwd, o_bwd = _gla_scan(proj3, log_a.reshape(bsz, length, 2 * dk), nctx // GLA_CHUNK, dk, dv)
    gated = _gla_finish(o_fwd, o_bwd, proj3, g_head, dk, dv).reshape(bsz * length, dv)
    out, v = _dense_mm_res(gated, w_out, layer, h.reshape(bsz * length, d), modtab, G1, tpb, nct, d, "gla_out_proj",
                           next_norm=(nw_ffn, SH2, SC2))
    return out.reshape(bsz, length, d), v


S5_LANES_PER_BATCH = 96
_AN1, _AN2, _AP1, _AP2, _BB1, _BB2, _CC1, _CC2, _PQ, _PK_ROWS = 0, 32, 64, 96, 128, 144, 160, 176, 192, 200


def _s5_params(a_re, a_im, log_dt, b_re, b_im, c_re, c_im):
    t_len = S5_CHUNK
    dt = jnp.exp(log_dt)[..., None]
    mag = jnp.exp(a_re * dt)
    ab_re = mag * jnp.cos(a_im * dt)
    ab_im = mag * jnp.sin(a_im * dt)
    den = a_re * a_re + a_im * a_im
    f_re = ((ab_re - 1.0) * a_re + ab_im * a_im) / den
    f_im = (ab_im * a_re - (ab_re - 1.0) * a_im) / den
    bb_re = f_re[..., None] * b_re - f_im[..., None] * b_im
    bb_im = f_re[..., None] * b_im + f_im[..., None] * b_re

    def cpow(n):
        nn = n.astype(F32)[:, None, :, None]
        m = jnp.exp((a_re * dt)[:, :, None, :] * nn)
        th = (a_im * dt)[:, :, None, :] * nn
        return m * jnp.cos(th), m * jnp.sin(th)

    steps = jnp.arange(t_len)
    order = jnp.stack([steps, t_len - 1 - steps])
    apr, api = cpow(order)
    inv_sq = jnp.exp(-2.0 * (a_re * dt)[:, :, None, :] * order.astype(F32)[:, None, :, None])
    anr, ani = apr * inv_sq, -api * inv_sq
    atr, ati = cpow(jnp.full((2, 1), t_len))
    cat = lambda u, v: jnp.concatenate([u, v], axis=-1)
    bbr_t, bbi_t = bb_re.transpose(0, 1, 3, 2), bb_im.transpose(0, 1, 3, 2)
    zero = jnp.zeros((2, a_re.shape[1], 5, 2 * a_re.shape[2]), F32)
    return jnp.concatenate([
        cat(anr, anr), cat(-ani, ani),
        cat(apr, api), cat(api, apr),
        cat(bbr_t, bbi_t), cat(bbi_t, bbr_t),
        cat(c_re, -c_re), cat(-c_im, -c_im),
        cat(atr, atr), cat(-ati, ati), cat(ati, -ati), zero], axis=2)


def _s5_in_kernel(hl_ref, hc_ref, nw_ref, mod_ref, o_ref, *, bsz, qb):
    nw = nw_ref[...]
    blocks = []
    for b in range(bsz):
        lat = _normmod_rows(hl_ref[b], nw, mod_ref[b, 1, SC1:SC1 + 1, :], mod_ref[b, 1, SH1:SH1 + 1, :])
        ctx = _normmod_rows(hc_ref[b], nw, mod_ref[b, 0, SC1:SC1 + 1, :], mod_ref[b, 0, SH1:SH1 + 1, :])
        pad = jnp.zeros((qb - lat.shape[0] - ctx.shape[0], lat.shape[1]), F32)
        blocks += [lat, ctx, pad]
    o_ref[...] = jnp.concatenate(blocks, axis=0).T.astype(o_ref.dtype)


def _s5_in(h, nw, modtab, nctx):
    bsz, length, d = h.shape
    t_len = S5_CHUNK
    qb = S5_LANES_PER_BATCH
    nctx_chunks = nctx // t_len
    return pl.pallas_call(
        functools.partial(_s5_in_kernel, bsz=bsz, qb=qb),
        out_shape=jax.ShapeDtypeStruct((d, t_len * bsz * qb), BF16),
        grid=(t_len,),
        in_specs=[
            pl.BlockSpec((bsz, None, GRID_W, d), lambda s: (0, nctx // GRID_W + s, 0, 0)),
            pl.BlockSpec((bsz, nctx_chunks, d), lambda s: (0, 0, s)),
            pl.BlockSpec((1, d), lambda s: (0, 0)),
            pl.BlockSpec(modtab.shape, lambda s: (0, 0, 0, 0)),
        ],
        out_specs=pl.BlockSpec((d, bsz * qb), lambda s: (0, s)),
        compiler_params=_cp(("arbitrary",), VMEM_LIMIT),
        name="s5_in",
    )(h.reshape(bsz, length // GRID_W, GRID_W, d), h[:, :nctx].reshape(bsz, nctx_chunks, t_len * d),
      nw.reshape(1, d), modtab)


def _tile_rows(x, reps):
    return jnp.concatenate([x] * reps, axis=0)


def _repeat_rows(x, reps):
    return jnp.concatenate([jnp.broadcast_to(x[i:i + 1, :], (reps, x.shape[1])) for i in range(x.shape[0])],
                           axis=0)


def _s5_kernel(ut_ref, pk_ref, y_ref, sv_ref, ss_ref, w_ref, r2_ref, acc_ref, *, bsz, qb, orders):
    gb = ut_ref.shape[0] // S5_GROUP
    t_len = S5_CHUNK
    k = S5_GROUP * t_len
    q = ut_ref.shape[1] // t_len
    row_s = lax.broadcasted_iota(jnp.int32, (k, k), 0) // S5_GROUP
    col_t = lax.broadcasted_iota(jnp.int32, (k, k), 1) // S5_GROUP
    masks = (row_s <= col_t, row_s >= col_t)
    w_ref[...] = jnp.zeros_like(w_ref)
    us = []
    for g in range(gb):
        x = ut_ref[g * S5_GROUP:(g + 1) * S5_GROUP, :]
        us.append(jnp.concatenate([x[:, s * q:(s + 1) * q] for s in range(t_len)], axis=0).T)
    pairs = [(g, d) for g in range(gb) for d in range(2)]
    l2s, raw = [], []
    for g, d in pairs:
        pk = pk_ref[d, g]
        l2 = (_repeat_rows(pk[_AN1:_AN1 + t_len], S5_GROUP) * _tile_rows(pk[_BB1:_BB1 + S5_GROUP], t_len)
              + _repeat_rows(pk[_AN2:_AN2 + t_len], S5_GROUP) * _tile_rows(pk[_BB2:_BB2 + S5_GROUP], t_len))
        r2t = (_repeat_rows(pk[_AP1:_AP1 + t_len], S5_GROUP) * _tile_rows(pk[_CC1:_CC1 + S5_GROUP], t_len)
               + _repeat_rows(pk[_AP2:_AP2 + t_len], S5_GROUP) * _tile_rows(pk[_CC2:_CC2 + S5_GROUP], t_len))
        r2 = r2t.T
        r2_ref[2 * g + d] = r2.astype(BF16)
        l2s.append(l2.astype(BF16))
        raw.append(_dot3(l2, r2))
    for (g, d), l2, toep_raw in zip(pairs, l2s, raw):
        i = 2 * g + d
        y_local = _dot(us[g], jnp.where(masks[d], toep_raw, 0.0).astype(BF16))
        if d == 0:
            acc_ref[g] = y_local
        else:
            acc_ref[g] = acc_ref[g] + y_local
        v = _dot(us[g], l2)
        vs = pltpu.roll(v, v.shape[1] // 2, axis=1)
        pk = pk_ref[d, g]
        pt, qt, qts = pk[_PQ:_PQ + 1], pk[_PQ + 1:_PQ + 2], pk[_PQ + 2:_PQ + 3]
        sv_ref[i] = pt * v + qt * vs
        ss_ref[i] = pt * vs + qts * v
    coef = []
    for g in range(gb):
        for d in range(2):
            pk = pk_ref[d, g]
            coef.append(tuple(jnp.broadcast_to(pk[_PQ + r:_PQ + r + 1], (bsz, pk.shape[1])) for r in range(3)))
    zero = jnp.zeros((bsz, pk_ref.shape[3]), F32)
    state = [(zero, zero)] * (2 * gb)
    nsteps = len(orders[0])
    for step in range(nsteps):
        for i in range(2 * gb):
            rows = pl.ds(orders[i % 2][step], bsz, stride=qb)
            w, ws = state[i]
            w_ref[i, rows, :] = w
            if step + 1 < nsteps:
                pt, qt, qts = coef[i]
                state[i] = (pt * w + qt * ws + sv_ref[i, rows, :], pt * ws + qts * w + ss_ref[i, rows, :])
    for g in range(gb):
        y = acc_ref[g]
        for d in range(2):
            i = 2 * g + d
            y = y + _dot(w_ref[i].astype(BF16), r2_ref[i])
        yt = y.T
        for t in range(t_len):
            y_ref[g * S5_GROUP:(g + 1) * S5_GROUP, t * q:(t + 1) * q] = yt[t * S5_GROUP:(t + 1) * S5_GROUP, :]


def _s5_scan(ut, pk, bsz, nctx_chunks, nlat_chunks):
    t_len = S5_CHUNK
    q = ut.shape[1] // t_len
    ngroups = ut.shape[0] // S5_GROUP
    np2 = pk.shape[3]
    gb = S5_GROUP_BLOCK
    qb = S5_LANES_PER_BATCH
    lat = list(range(nlat_chunks))
    ctx = list(range(nlat_chunks, nlat_chunks + nctx_chunks))
    orders = (ctx + lat, ctx[::-1] + lat[::-1])
    blk = (gb * S5_GROUP, t_len * q)
    return pl.pallas_call(
        functools.partial(_s5_kernel, bsz=bsz, qb=qb, orders=orders),
        out_shape=jax.ShapeDtypeStruct(ut.shape, F32),
        grid=(ngroups // gb,),
        in_specs=[pl.BlockSpec(blk, lambda g: (g, 0)),
                  pl.BlockSpec((2, gb, _PK_ROWS, np2), lambda g: (0, g, 0, 0))],
        out_specs=pl.BlockSpec(blk, lambda g: (g, 0)),
        scratch_shapes=[pltpu.VMEM((2 * gb, q, np2), F32), pltpu.VMEM((2 * gb, q, np2), F32),
                        pltpu.VMEM((2 * gb, q, np2), F32), pltpu.VMEM((2 * gb, np2, S5_GROUP * t_len), BF16),
                        pltpu.VMEM((gb, q, S5_GROUP * t_len), F32)],
        compiler_params=_cp(("arbitrary",), VMEM_LIMIT),
        name="s5_scan",
    )(ut, pk)


def _s5_out_kernel(yt_ref, hl_ref, hc_ref, nw_ref, mod_ref, d_ref, ol_ref, oc_ref, *, bsz, qb):
    y = yt_ref[...].T
    nw = nw_ref[...]
    nlat, nc = hl_ref.shape[1], hc_ref.shape[1]
    c0 = 0.7978845608028654

    def act(yv, hv, region, b):
        u = _normmod_rows(hv, nw, mod_ref[b, region, SC1:SC1 + 1, :], mod_ref[b, region, SH1:SH1 + 1, :])
        z = yv + u * d_ref[...]
        return (0.5 * z * (1.0 + jnp.tanh(c0 * (z + 0.044715 * (z * z * z))))).astype(ol_ref.dtype)

    for b in range(bsz):
        ol_ref[b] = act(y[b * qb:b * qb + nlat], hl_ref[b], 1, b)
        oc_ref[b] = act(y[b * qb + nlat:b * qb + nlat + nc], hc_ref[b], 0, b)


def _s5_out(yt, h, nw, modtab, d_skip, nctx):
    bsz, length, d = h.shape
    t_len = S5_CHUNK
    qb = S5_LANES_PER_BATCH
    nctx_chunks = nctx // t_len
    seq = length - nctx
    lat_spec = pl.BlockSpec((bsz, None, GRID_W, d), lambda s: (0, nctx // GRID_W + s, 0, 0))
    ctx_spec = pl.BlockSpec((bsz, nctx_chunks, d), lambda s: (0, 0, s))
    act_l, act_c = pl.pallas_call(
        functools.partial(_s5_out_kernel, bsz=bsz, qb=qb),
        out_shape=(jax.ShapeDtypeStruct((bsz, seq // GRID_W, GRID_W, d), BF16),
                   jax.ShapeDtypeStruct((bsz, nctx_chunks, t_len * d), BF16)),
        grid=(t_len,),
        in_specs=[pl.BlockSpec((d, bsz * qb), lambda s: (0, s)),
                  lat_spec, ctx_spec,
                  pl.BlockSpec((1, d), lambda s: (0, 0)),
                  pl.BlockSpec(modtab.shape, lambda s: (0, 0, 0, 0)),
                  pl.BlockSpec((1, d), lambda s: (0, 0))],
        out_specs=(pl.BlockSpec((bsz, None, GRID_W, d), lambda s: (0, s, 0, 0)), ctx_spec),
        compiler_params=_cp(("arbitrary",), VMEM_LIMIT),
        name="s5_out",
    )(yt, h.reshape(bsz, length // GRID_W, GRID_W, d), h[:, :nctx].reshape(bsz, nctx_chunks, t_len * d),
      nw.reshape(1, d), modtab, d_skip.reshape(1, d))
    return act_l.reshape(bsz, seq, d), act_c.reshape(bsz, nctx, d)


def _s5_layer(h, modtab, nw, a_re, a_im, log_dt, b_re, b_im, c_re, c_im, d_skip, w_glu, b_glu, layer, nctx,
              last):
    bsz, length, d = h.shape
    seq = length - nctx
    t_len = S5_CHUNK
    assert seq // GRID_W == t_len and nctx % t_len == 0 and nctx % GRID_W == 0
    assert seq // t_len + nctx // t_len <= S5_LANES_PER_BATCH
    ngroups = d // S5_GROUP
    ut = _s5_in(h, nw, modtab, nctx)
    pk = _s5_params(a_re, a_im, log_dt, b_re, b_im, c_re, c_im)
    yt = _s5_scan(ut, pk, bsz, nctx // t_len, GRID_W)
    act_l, act_c = _s5_out(yt, h, nw, modtab, d_skip, nctx)
    if last:
        h, nct = h[:, nctx:], 0
    else:
        nct = nctx // ROW_TILE
    rows = h.shape[0] * h.shape[1]
    out = _dense_glu_res(act_l.reshape(bsz * seq, d), act_c.reshape(bsz * nctx, d), w_glu, b_glu, layer,
                         h.reshape(rows, d), modtab, G1, h.shape[1] // ROW_TILE, nct, 1024, "s5_glu")
    return out.reshape(h.shape)


def _dense_ffn(h, v, modtab, w13, w2, layer, nct):
    bsz, length, d = h.shape
    rows = bsz * length
    w13_grouped = w13.reshape(w13.shape[0], 1, w13.shape[1], w13.shape[2])
    act = _moe_swiglu(_dense_route(rows // MOE_TILE), v, w13_grouped, layer, MOE_W13_COLS, "ffn_w13")
    out = _dense_mm_res(act, w2, layer, h.reshape(rows, d), modtab, G2, length // ROW_TILE, nct, MOE_W2_COLS,
                        "ffn_w2")
    return out.reshape(h.shape)


def _moe_gather_kernel(src_ref, nxt_ref, v_hbm, o_ref, buf_ref, sem):
    t = pl.program_id(0)
    slot = t % 2

    def request(idx_ref, s):
        def issue(r8, carry):
            for i in range(DMA_ISSUE_UNROLL):
                r = r8 * DMA_ISSUE_UNROLL + i
                pltpu.make_async_copy(v_hbm.at[pl.ds(idx_ref[0, r], 1)], buf_ref.at[s, pl.ds(r, 1)],
                                      sem.at[s]).start(priority=i % 2)
            return carry
        lax.fori_loop(0, MOE_TILE // DMA_ISSUE_UNROLL, issue, 0)

    @pl.when(t == 0)
    def _():
        request(src_ref, 0)

    @pl.when(t + 1 < pl.num_programs(0))
    def _():
        request(nxt_ref, 1 - slot)

    pltpu.make_async_copy(v_hbm.at[pl.ds(0, MOE_TILE)], buf_ref.at[slot], sem.at[slot]).wait()
    o_ref[...] = buf_ref[slot].astype(o_ref.dtype)


def _moe_gather(src, v):
    ntiles = src.shape[0]
    d = v.shape[1]
    idx_block = (None, 1, MOE_TILE)
    return pl.pallas_call(
        _moe_gather_kernel,
        out_shape=jax.ShapeDtypeStruct((ntiles * MOE_TILE, d), BF16),
        grid=(ntiles,),
        in_specs=[pl.BlockSpec(idx_block, lambda t: (t, 0, 0), memory_space=pltpu.SMEM),
                  pl.BlockSpec(idx_block, lambda t: (jnp.minimum(t + 1, ntiles - 1), 0, 0),
                               memory_space=pltpu.SMEM),
                  pl.BlockSpec(memory_space=pl.ANY)],
        out_specs=pl.BlockSpec((MOE_TILE, d), lambda t: (t, 0)),
        scratch_shapes=[pltpu.VMEM((2, MOE_TILE, d), F32), pltpu.SemaphoreType.DMA((2,))],
        compiler_params=_cp(("arbitrary",)),
        name="moe_gather",
    )(src, src, v)


def _moe_fetch_weights(te_ref, first_ref, nxt_ref, w_hbm, stage_ref, sem, wbf_ref, col_blocks, layer):
    j, t = pl.program_id(0), pl.program_id(1)
    nsweeps = pl.num_programs(0)
    tn = stage_ref.shape[2]

    def copies(e, sweep):
        return [pltpu.make_async_copy(w_hbm.at[layer, e, :, pl.ds(pl.multiple_of(cb(sweep) * tn, 128), tn)],
                                      stage_ref.at[i], sem.at[i])
                for i, cb in enumerate(col_blocks)]

    @pl.when(jnp.logical_and(j == 0, t == 0))
    def _():
        for c in copies(te_ref[0], 0):
            c.start()

    @pl.when(first_ref[t] == 1)
    def _():
        for c in copies(te_ref[t], j):
            c.wait()
        for i in range(len(col_blocks)):
            wbf_ref[:, i * tn:(i + 1) * tn] = stage_ref[i].astype(BF16)
        sweep_done = nxt_ref[t] < 0
        next_e = jnp.where(sweep_done, te_ref[0], nxt_ref[t])
        next_sweep = jnp.where(sweep_done, j + 1, j)

        @pl.when(next_sweep < nsweeps)
        def _():
            for c in copies(next_e, next_sweep):
                c.start()


def _moe_swiglu_kernel(te_ref, nv_ref, first_ref, nxt_ref, x_ref, w_hbm, o_ref, wbf_ref, stage_ref, sem, *, layer):
    nb = pl.num_programs(0)
    _moe_fetch_weights(te_ref, first_ref, nxt_ref, w_hbm, stage_ref, sem, wbf_ref,
                       [lambda sweep: sweep, lambda sweep: nb + sweep], layer)

    in_use = pl.program_id(1) < nv_ref[0]

    @pl.when(in_use)
    def _():
        tn = o_ref.shape[1]
        z = _dot(x_ref[...], wbf_ref[...])
        o_ref[...] = (_silu(z[:, :tn]) * z[:, tn:]).astype(o_ref.dtype)

    @pl.when(jnp.logical_not(in_use))
    def _():
        o_ref[...] = jnp.zeros_like(o_ref)


def _moe_mm_kernel(te_ref, nv_ref, first_ref, nxt_ref, x_ref, w_hbm, o_ref, wbf_ref, stage_ref, sem, *, layer):
    _moe_fetch_weights(te_ref, first_ref, nxt_ref, w_hbm, stage_ref, sem, wbf_ref, [lambda sweep: sweep], layer)

    in_use = pl.program_id(1) < nv_ref[0]

    @pl.when(in_use)
    def _():
        o_ref[...] = _dot(x_ref[...], wbf_ref[...])

    @pl.when(jnp.logical_not(in_use))
    def _():
        o_ref[...] = jnp.zeros_like(o_ref)


def _moe_swiglu(route, xs, w13, layer, tn, name="moe_w13"):
    rows, k = xs.shape
    half = w13.shape[3] // 2
    nb = half // tn
    tm = MOE_TILE
    return pl.pallas_call(
        functools.partial(_moe_swiglu_kernel, layer=layer),
        out_shape=jax.ShapeDtypeStruct((rows, half), BF16),
        grid_spec=pltpu.PrefetchScalarGridSpec(
            num_scalar_prefetch=4,
            grid=(nb, rows // tm),
            in_specs=[pl.BlockSpec((tm, k), lambda j, t, *_: (t, 0)),
                      pl.BlockSpec(memory_space=pl.ANY)],
            out_specs=pl.BlockSpec((tm, tn), lambda j, t, *_: (t, j)),
            scratch_shapes=[pltpu.VMEM((k, 2 * tn), BF16), pltpu.VMEM((2, k, tn), F32),
                            pltpu.SemaphoreType.DMA((2,))]),
        compiler_params=_cp(("arbitrary", "arbitrary"), VMEM_LIMIT),
        name=name,
    )(*route, xs, w13)


def _moe_mm(route, xs, w2, layer, tn, n=None, name="moe_w2"):
    rows, k = xs.shape
    n = w2.shape[3] if n is None else n
    tm = MOE_TILE
    return pl.pallas_call(
        functools.partial(_moe_mm_kernel, layer=layer),
        out_shape=jax.ShapeDtypeStruct((rows, n), F32),
        grid_spec=pltpu.PrefetchScalarGridSpec(
            num_scalar_prefetch=4,
            grid=(n // tn, rows // tm),
            in_specs=[pl.BlockSpec((tm, k), lambda j, t, *_: (t, 0)),
                      pl.BlockSpec(memory_space=pl.ANY)],
            out_specs=pl.BlockSpec((tm, tn), lambda j, t, *_: (t, j)),
            scratch_shapes=[pltpu.VMEM((k, tn), BF16), pltpu.VMEM((1, k, tn), F32),
                            pltpu.SemaphoreType.DMA((1,))]),
        compiler_params=_cp(("arbitrary", "arbitrary"), VMEM_LIMIT),
        name=name,
    )(*route, xs, w2)


def _dense_route(ntiles):
    zeros = jnp.zeros((ntiles,), jnp.int32)
    return (zeros, jnp.full((1,), ntiles, jnp.int32), zeros.at[0].set(1), zeros - 1)


def _dense_mm_res_kernel(te_ref, nv_ref, first_ref, nxt_ref, x_ref, w_hbm, res_ref, mod_ref, o_ref, wbf_ref,
                         stage_ref, sem, *, layer, g_row):
    _moe_fetch_weights(te_ref, first_ref, nxt_ref, w_hbm, stage_ref, sem, wbf_ref, [lambda sweep: sweep], layer)
    o_ref[...] = res_ref[...] + mod_ref[g_row:g_row + 1, :] * _dot(x_ref[...], wbf_ref[...])


def _dense_mm_res_norm_kernel(te_ref, nv_ref, first_ref, nxt_ref, x_ref, w_hbm, res_ref, mod_ref, nw_ref, o_ref,
                              v_ref, wbf_ref, stage_ref, sem, *, layer, g_row, sh_row, sc_row):
    _moe_fetch_weights(te_ref, first_ref, nxt_ref, w_hbm, stage_ref, sem, wbf_ref, [lambda sweep: sweep], layer)
    h_new = res_ref[...] + mod_ref[g_row:g_row + 1, :] * _dot(x_ref[...], wbf_ref[...])
    o_ref[...] = h_new
    v_ref[...] = _normmod_rows(h_new, nw_ref[...], mod_ref[sc_row:sc_row + 1, :],
                               mod_ref[sh_row:sh_row + 1, :]).astype(v_ref.dtype)


def _dense_glu_res_kernel(te_ref, nv_ref, first_ref, nxt_ref, x_ref, xc_ref, w_hbm, ba_ref, bb_ref, res_ref, mod_ref,
                          o_ref, wbf_ref, stage_ref, sem, *, layer, g_row, tpb, nct):
    nb = pl.num_programs(0)
    _moe_fetch_weights(te_ref, first_ref, nxt_ref, w_hbm, stage_ref, sem, wbf_ref,
                       [lambda sweep: sweep, lambda sweep: nb + sweep], layer)
    tn = o_ref.shape[1]
    x = x_ref[...] if nct == 0 else jnp.where(pl.program_id(1) % tpb < nct, xc_ref[...], x_ref[...])
    z = _dot(x, wbf_ref[...])
    z1 = z[:, :tn] + ba_ref[...]
    z2 = z[:, tn:] + bb_ref[...]
    o_ref[...] = res_ref[...] + mod_ref[g_row:g_row + 1, :] * (z1 * _sigmoid(z2))


def _dense_res_specs(tn, tpb, nct):
    res_spec = pl.BlockSpec((ROW_TILE, tn), lambda j, t, *_: (t, j))
    mod_spec = pl.BlockSpec((None, None, 6, tn),
                            lambda j, t, *_: (t // tpb, jnp.where(t % tpb >= nct, 1, 0), 0, j))
    return res_spec, mod_spec


def _dense_mm_res(x, w, layer, res, modtab, g_row, tpb, nct, tn, name, next_norm=None):
    r, k = x.shape
    n = w.shape[2]
    tm = ROW_TILE
    res_spec, mod_spec = _dense_res_specs(tn, tpb, nct)
    out_spec = pl.BlockSpec((tm, tn), lambda j, t, *_: (t, j))
    in_specs = [pl.BlockSpec((tm, k), lambda j, t, *_: (t, 0)), pl.BlockSpec(memory_space=pl.ANY), res_spec, mod_spec]
    operands = [x, w.reshape(w.shape[0], 1, k, n), res, modtab]
    if next_norm is None:
        body = functools.partial(_dense_mm_res_kernel, layer=layer, g_row=g_row)
        out_shape, out_specs = jax.ShapeDtypeStruct((r, n), F32), out_spec
    else:
        assert tn == n
        nw, sh_row, sc_row = next_norm
        body = functools.partial(_dense_mm_res_norm_kernel, layer=layer, g_row=g_row, sh_row=sh_row, sc_row=sc_row)
        in_specs.append(pl.BlockSpec((1, n), lambda j, t, *_: (0, 0)))
        operands.append(nw.reshape(1, n))
        out_shape = (jax.ShapeDtypeStruct((r, n), F32), jax.ShapeDtypeStruct((r, n), BF16))
        out_specs = (out_spec, out_spec)
    return pl.pallas_call(
        body,
        out_shape=out_shape,
        grid_spec=pltpu.PrefetchScalarGridSpec(
            num_scalar_prefetch=4,
            grid=(n // tn, r // tm),
            in_specs=in_specs,
            out_specs=out_specs,
            scratch_shapes=[pltpu.VMEM((k, tn), BF16), pltpu.VMEM((1, k, tn), F32),
                            pltpu.SemaphoreType.DMA((1,))]),
        compiler_params=_cp(("arbitrary", "arbitrary"), VMEM_LIMIT),
        name=name,
    )(*_dense_route(r // tm), *operands)


def _dense_glu_res(x, x_ctx, w, bias, layer, res, modtab, g_row, tpb, nct, tn, name):
    r, k = res.shape[0], x.shape[1]
    half = w.shape[2] // 2
    nb = half // tn
    tm = ROW_TILE
    nlat = tpb - nct
    res_spec, mod_spec = _dense_res_specs(tn, tpb, nct)
    bias3 = bias.reshape(bias.shape[0], 1, 2 * half)
    return pl.pallas_call(
        functools.partial(_dense_glu_res_kernel, layer=layer, g_row=g_row, tpb=tpb, nct=nct),
        out_shape=jax.ShapeDtypeStruct((r, half), F32),
        grid_spec=pltpu.PrefetchScalarGridSpec(
            num_scalar_prefetch=4,
            grid=(nb, r // tm),
            in_specs=[pl.BlockSpec((tm, k), lambda j, t, *_: ((t // tpb) * nlat + jnp.maximum(t % tpb - nct, 0), 0)),
                      pl.BlockSpec((tm, k), lambda j, t, *_: ((t // tpb) * max(nct, 1)
                                                              + jnp.minimum(t % tpb, max(nct, 1) - 1), 0)),
                      pl.BlockSpec(memory_space=pl.ANY),
                      pl.BlockSpec((None, 1, tn), lambda j, t, *_: (layer, 0, j)),
                      pl.BlockSpec((None, 1, tn), lambda j, t, *_: (layer, 0, nb + j)),
                      res_spec, mod_spec],
            out_specs=pl.BlockSpec((tm, tn), lambda j, t, *_: (t, j)),
            scratch_shapes=[pltpu.VMEM((k, 2 * tn), BF16), pltpu.VMEM((2, k, tn), F32),
                            pltpu.SemaphoreType.DMA((2,))]),
        compiler_params=_cp(("arbitrary", "arbitrary"), VMEM_LIMIT),
        name=name,
    )(*_dense_route(r // tm), x, x if nct == 0 else x_ctx, w.reshape(w.shape[0], 1, k, 2 * half), bias3, bias3,
      res, modtab)


def _moe_combine_kernel(pos_ref, nxt_ref, gate_ref, y_hbm, res_ref, mod_ref, *rest, g_row, tail):
    if tail == "final":
        nw_ref, o_ref, buf_ref, sem = rest
    elif tail == "next":
        nw_ref, mod2_ref, o_ref, u_ref, buf_ref, sem = rest
    else:
        o_ref, buf_ref, sem = rest
    m = pl.program_id(0)
    slot = m % 2

    def request(idx_ref, s):
        for k in range(TOP_K):
            def issue(r8, carry, k=k):
                for i in range(DMA_ISSUE_UNROLL):
                    r = r8 * DMA_ISSUE_UNROLL + i
                    pltpu.make_async_copy(y_hbm.at[pl.ds(idx_ref[k, r], 1)], buf_ref.at[s, k, pl.ds(r, 1)],
                                          sem.at[s]).start(priority=i % 2)
                return carry
            lax.fori_loop(0, ROW_TILE // DMA_ISSUE_UNROLL, issue, 0)

    @pl.when(m == 0)
    def _():
        request(pos_ref, 0)

    @pl.when(m + 1 < pl.num_programs(0))
    def _():
        request(nxt_ref, 1 - slot)

    for k in range(TOP_K):
        pltpu.make_async_copy(y_hbm.at[pl.ds(0, ROW_TILE)], buf_ref.at[slot, k], sem.at[slot]).wait()
    p = gate_ref[...]
    mix = p[:, 0:1] * buf_ref[slot, 0] + p[:, 1:2] * buf_ref[slot, 1]
    h_new = res_ref[...] + mod_ref[g_row:g_row + 1, :] * mix
    if tail == "final":
        ms = jnp.mean(h_new * h_new, axis=-1, keepdims=True)
        o_ref[...] = h_new * lax.rsqrt(ms + EPS) * nw_ref[...]
    else:
        o_ref[...] = h_new
    if tail == "next":
        u_ref[...] = _normmod_rows(h_new, nw_ref[...], mod2_ref[SC1:SC1 + 1, :],
                                   mod2_ref[SH1:SH1 + 1, :]).astype(u_ref.dtype)


def _moe_combine(pos, gates, ys, res, modtab, g_row, tpb, nct, tail=None, nw=None, modtab_next=None):
    n, d = res.shape
    tm = ROW_TILE
    ntiles = n // tm
    row_spec = pl.BlockSpec((tm, d), lambda m: (m, 0))
    mod_spec = pl.BlockSpec((None, None, 6, d), lambda m: (m // tpb, jnp.where(m % tpb >= nct, 1, 0), 0, 0))
    in_specs = [pl.BlockSpec((None, TOP_K, tm), lambda m: (m, 0, 0), memory_space=pltpu.SMEM),
                pl.BlockSpec((None, TOP_K, tm), lambda m: (jnp.minimum(m + 1, ntiles - 1), 0, 0),
                             memory_space=pltpu.SMEM),
                pl.BlockSpec((tm, ROUTER_LANES), lambda m: (m, 0)),
                pl.BlockSpec(memory_space=pl.ANY), row_spec, mod_spec]
    operands = [pos, pos, gates, ys, res, modtab]
    out_shape, out_specs = jax.ShapeDtypeStruct((n, d), F32), row_spec
    if tail is not None:
        in_specs.append(pl.BlockSpec((1, d), lambda m: (0, 0)))
        operands.append(nw.reshape(1, d))
    if tail == "next":
        in_specs.append(mod_spec)
        operands.append(modtab_next)
        out_shape, out_specs = (out_shape, jax.ShapeDtypeStruct((n, d), BF16)), (row_spec, row_spec)
    return pl.pallas_call(
        functools.partial(_moe_combine_kernel, g_row=g_row, tail=tail),
        out_shape=out_shape,
        grid=(ntiles,),
        in_specs=in_specs,
        out_specs=out_specs,
        scratch_shapes=[pltpu.VMEM((2, TOP_K, tm, d), F32), pltpu.SemaphoreType.DMA((2,))],
        compiler_params=_cp(("arbitrary",), VMEM_LIMIT),
        name="moe_combine",
    )(*operands)


def _moe_route(idx2):
    n = idx2.shape[0]
    flat_e = idx2.reshape(-1)
    onehot = (flat_e[:, None] == jnp.arange(N_EXPERTS, dtype=jnp.int32)[None, :]).astype(jnp.int32)
    csum = jnp.cumsum(onehot, axis=0)
    rank = jnp.sum((csum - onehot) * onehot, axis=1)
    counts = csum[-1]
    ntile = (counts + MOE_TILE - 1) // MOE_TILE
    tile_end = jnp.cumsum(ntile)
    tile_start = tile_end - ntile
    pos = tile_start[flat_e] * MOE_TILE + rank
    tiles = (TOP_K * n) // MOE_TILE + N_EXPERTS
    src = jnp.zeros((tiles * MOE_TILE,), jnp.int32).at[pos].set(jnp.arange(TOP_K * n, dtype=jnp.int32) // TOP_K)
    nv = tile_end[-1:]
    tile_ids = jnp.arange(tiles, dtype=jnp.int32)
    tids = jnp.minimum(tile_ids, nv[0] - 1)
    te = jnp.sum((tile_end[None, :] <= tids[:, None]).astype(jnp.int32), axis=1)
    first = jnp.logical_and(tile_ids < nv[0], jnp.logical_or(tile_ids == 0, te != jnp.roll(te, 1)))
    run_end = tile_end[te]
    nxt = jnp.where(run_end < nv[0], te[jnp.minimum(run_end, tiles - 1)], -1)
    route = (te, nv.astype(jnp.int32), first.astype(jnp.int32), nxt.astype(jnp.int32))
    return pos.astype(jnp.int32), src.reshape(tiles, 1, MOE_TILE), route


def _moe_ffn(h, modtab, nw, w_router, w13, w2, layer, nct, tail=None, nw_tail=None, modtab_next=None):
    bsz, length, d = h.shape
    n = bsz * length
    v, idx, gates = _normmod_router(h, nw, modtab, w_router, SH2, SC2, nct)
    idx2 = idx.reshape(n, ROUTER_LANES)[:, :TOP_K]
    pos, src, route = _moe_route(idx2)
    xs = _moe_gather(src, v.reshape(n, d))
    act = _moe_swiglu(route, xs, w13, layer, MOE_W13_COLS)
    ys = _moe_mm(route, act, w2, layer, MOE_W2_COLS)
    pos3 = pos.reshape(n // ROW_TILE, ROW_TILE, TOP_K).transpose(0, 2, 1)
    out = _moe_combine(pos3, gates.reshape(n, ROUTER_LANES), ys, h.reshape(n, d), modtab, G2,
                       length // ROW_TILE, nct, tail, nw_tail, modtab_next)
    if tail == "next":
        return out[0].reshape(h.shape), out[1]
    return out.reshape(h.shape)


def kernel(x, c, ctx, c_ctx, w_mod, b_mod, norm1, norm2, norm_f, gla_w_in, gla_w_gate_up, gla_b_gate,
           gla_g_head, gla_w_out, ffn_w13, ffn_w2, s5_a_re, s5_a_im, s5_log_dt, s5_b_re, s5_b_im, s5_c_re,
           s5_c_im, s5_d, s5_w_glu, s5_b_glu, moe_w_router, moe_w13, moe_w2):
    bsz, seq, d = x.shape
    nctx = ctx.shape[1]
    depth = w_mod.shape[0]
    assert bsz + 1 <= 8 and seq % ROW_TILE == 0 and nctx % ROW_TILE == 0 and seq % GRID_W == 0
    nct = nctx // ROW_TILE

    c8 = jnp.concatenate([c, c_ctx[None, :], jnp.zeros((8 - bsz - 1, d), F32)], axis=0)
    mods = _modulation(c8, w_mod, b_mod).reshape(depth, 8, 6, d)
    assert depth % 2 == 0
    modtabs = [jnp.stack([jnp.broadcast_to(mods[i, bsz], (bsz, 6, d)), mods[i, :bsz]], axis=1) for i in range(depth)]
    h = jnp.concatenate([ctx, x], axis=1)
    u = None
    for i in range(depth):
        last = i == depth - 1
        j = i // 2
        modtab = modtabs[i]
        if i % 2 == 0:
            h, v = _gla_layer(h, u, modtab, norm1[i], norm2[i], gla_w_in, j, gla_w_gate_up[j], gla_b_gate[j],
                              gla_g_head[j], gla_w_out, nctx)
            h = _dense_ffn(h, v, modtab, ffn_w13, ffn_w2, j, nct)
        else:
            h = _s5_layer(h, modtab, norm1[i], s5_a_re[j], s5_a_im[j], s5_log_dt[j], s5_b_re[j], s5_b_im[j],
                          s5_c_re[j], s5_c_im[j], s5_d[j], s5_w_glu, s5_b_glu, j, nctx, last)
            if last:
                h = _moe_ffn(h, modtab, norm2[i], moe_w_router[j], moe_w13, moe_w2, j, 0, "final", norm_f)
            else:
                h, u = _moe_ffn(h, modtab, norm2[i], moe_w_router[j], moe_w13, moe_w2, j, nct, "next",
                                norm1[i + 1], modtabs[i + 1])
    return h
```
